```python
import math
import jax
import jax.numpy as jnp
from jax import lax
import numpy as np

D_MODEL = 2048
BATCH = 2
SEQ = 8192
DEPTH = 1
DEC_BATCH = 32
DEC_SEQ = 8
PAST_LEN = 16384
PAGE_SIZE = 128

HEAD_DIM = 128
A_HEADS = 8
A_KV_HEADS = 2
A_GROUP = A_HEADS // A_KV_HEADS
IDX_HEADS = 8
IDX_DIM = 64
TOPK_MAX = 256
B_HEADS = 8
B_KV_HEADS = 2
B_GROUP = B_HEADS // B_KV_HEADS
CMP_BLOCK = 32
CMP_STRIDE = 16
CMP_RATIO = CMP_BLOCK // CMP_STRIDE
CMP_HID = 128
SEL_BLOCK = 64
SEL_TOPN = 16
WINDOW = 512
N_MEM = 256
MEM_HEADS = 4
MEM_HEAD_DIM = 256
N_BUCKETS = 32
MAX_DISTANCE = 1024
D_FF = 4 * D_MODEL
Q_BLOCK = 128
LN_EPS = 1e-5
DN_ALPHA = (2 * DEPTH) ** 0.25
DN_BETA = (8 * DEPTH) ** -0.25

_SPLIT = (
    ('a_q', A_HEADS * HEAD_DIM),
    ('a_kv', A_KV_HEADS * 2 * HEAD_DIM),
    ('i_q', IDX_HEADS * IDX_DIM),
    ('i_k', IDX_DIM),
    ('i_w', IDX_HEADS),
    ('b_q', B_HEADS * HEAD_DIM),
    ('b_cmp', B_KV_HEADS * 2 * HEAD_DIM),
    ('b_sel', B_KV_HEADS * 2 * HEAD_DIM),
    ('b_win', B_KV_HEADS * 2 * HEAD_DIM),
    ('b_gate', B_HEADS * 3),
    ('m_q', MEM_HEADS * MEM_HEAD_DIM),
    ('g_merge', 3 * D_MODEL),
)
D_IN = sum(s for _, s in _SPLIT)
_OFFSETS = [sum(s for _, s in _SPLIT[:i + 1]) for i in range(len(_SPLIT) - 1)]

kernel_name = 'dsa_nsa_memory_hybrid_step'


def layer_norm(x, g, b):
    xf = x.astype(jnp.float32)
    xc = xf - jnp.mean(xf, axis=-1, keepdims=True)
    var = jnp.mean(xc * xc, axis=-1, keepdims=True)
    y = xc * lax.rsqrt(var + LN_EPS) * g.astype(jnp.float32) + b.astype(jnp.float32)
    return y.astype(x.dtype)


def masked_softmax(logits, mask):
    logits = jnp.where(mask, logits, -jnp.inf)
    m = jnp.max(logits, axis=-1, keepdims=True)
    m = jnp.where(jnp.isfinite(m), m, 0.0)
    e = jnp.exp(logits - m)
    return e / jnp.maximum(jnp.sum(e, axis=-1, keepdims=True), 1e-30)


def rel_bucket(dist):
    d = jnp.maximum(dist, 0)
    exact = N_BUCKETS // 2
    df = jnp.maximum(d, 1).astype(jnp.float32)
    large = exact + (jnp.log(df / exact) / math.log(MAX_DISTANCE / exact) * (N_BUCKETS - exact)).astype(jnp.int32)
    return jnp.where(d < exact, d, jnp.minimum(large, N_BUCKETS - 1))


def project(x, w_in):
    b, t, _ = x.shape
    p = dict(zip([n for n, _ in _SPLIT], jnp.split(x @ w_in, _OFFSETS, axis=-1)))
    return dict(
        a_q=p['a_q'].reshape(b, t, A_KV_HEADS, A_GROUP, HEAD_DIM),
        a_kv=p['a_kv'].reshape(b, t, A_KV_HEADS, 2, HEAD_DIM),
        i_q=p['i_q'].reshape(b, t, IDX_HEADS, IDX_DIM),
        i_k=p['i_k'],
        i_w=p['i_w'],
        b_q=p['b_q'].reshape(b, t, B_KV_HEADS, B_GROUP, HEAD_DIM),
        b_cmp=p['b_cmp'].reshape(b, t, B_KV_HEADS, 2, HEAD_DIM),
        b_sel=p['b_sel'].reshape(b, t, B_KV_HEADS, 2, HEAD_DIM),
        b_win=p['b_win'].reshape(b, t, B_KV_HEADS, 2, HEAD_DIM),
        b_gate=p['b_gate'].reshape(b, t, B_KV_HEADS, B_GROUP, 3),
        m_q=p['m_q'].reshape(b, t, MEM_HEADS, MEM_HEAD_DIM),
        g_merge=p['g_merge'].reshape(b, t, 3, D_MODEL),
    )


def compress(x, pe, w1, w2):
    b, l, g, _ = x.shape
    nc = (l - CMP_BLOCK) // CMP_STRIDE + 1
    chunks = x[:, :(nc + CMP_RATIO - 1) * CMP_STRIDE].reshape(b, nc + CMP_RATIO - 1, CMP_STRIDE, g, HEAD_DIM)
    chunks = jnp.swapaxes(chunks, 2, 3).reshape(b, nc + CMP_RATIO - 1, g, CMP_STRIDE * HEAD_DIM)
    z = jnp.einsum('bcgk,rkh->bcgrh', chunks, w1.reshape(CMP_RATIO, CMP_STRIDE * HEAD_DIM, CMP_HID))
    pre = sum(z[:, r:r + nc, :, r] for r in range(CMP_RATIO)) + pe.reshape(-1) @ w1
    return jax.nn.gelu(pre) @ w2


def block_overlap(l):
    nc = (l - CMP_BLOCK) // CMP_STRIDE + 1
    ns = -(-l // SEL_BLOCK)
    cs = jnp.arange(nc) * CMP_STRIDE
    ss = jnp.arange(ns) * SEL_BLOCK
    ov = jnp.minimum(cs[:, None] + CMP_BLOCK, ss[None, :] + SEL_BLOCK) - jnp.maximum(cs[:, None], ss[None, :])
    return jnp.clip(ov, 0, None).astype(jnp.float32) / CMP_BLOCK


def token_mixers(q, q_pos, idx_keys, cmp_k, cmp_v, get_a_rows, get_b_rows, win_kv, win_pos, rel_table):
    a_q, i_q, i_w, b_q, b_gate = q
    bsz, nq = a_q.shape[:2]
    l = idx_keys.shape[1]
    f32 = jnp.float32
    scale = HEAD_DIM ** -0.5
    table_a = rel_table[:, :A_HEADS].reshape(N_BUCKETS, A_KV_HEADS, A_GROUP)
    table_b = rel_table[:, A_HEADS:].reshape(N_BUCKETS, B_KV_HEADS, B_GROUP)

    s = jnp.einsum('bqhd,bkd->bqhk', i_q, idx_keys).astype(f32) * IDX_DIM ** -0.5
    score = jnp.einsum('bqh,bqhk->bqk', i_w.astype(f32) * IDX_HEADS ** -0.5, jax.nn.relu(s))
    score = jnp.where(jnp.arange(l)[None, None, :] <= q_pos[None, :, None], score, -jnp.inf)
    top_val, top_pos = lax.top_k(score, min(TOPK_MAX, l // 4))
    kv_a = get_a_rows(top_pos)
    bias_a = table_a[rel_bucket(q_pos[None, :, None] - top_pos)].transpose(0, 1, 3, 4, 2)
    logits = jnp.einsum('bqgrd,bqkgd->bqgrk', a_q, kv_a[..., 0, :]).astype(f32) * scale + bias_a
    p = masked_softmax(logits, jnp.isfinite(top_val)[:, :, None, None, :])
    o_a = jnp.einsum('bqgrk,bqkgd->bqgrd', p.astype(kv_a.dtype), kv_a[..., 1, :]).reshape(bsz, nq, A_HEADS * HEAD_DIM)

    nc = cmp_k.shape[1]
    cmp_end = jnp.arange(nc) * CMP_STRIDE + CMP_BLOCK - 1
    logits = jnp.einsum('bqgrd,bngd->bqgrn', b_q, cmp_k).astype(f32) * scale
    p_cmp = masked_softmax(logits, (cmp_end[None, :] <= q_pos[:, None])[None, :, None, None, :])
    o_cmp = jnp.einsum('bqgrn,bngd->bqgrd', p_cmp.astype(cmp_v.dtype), cmp_v)

    overlap = block_overlap(l)
    ns = overlap.shape[1]
    imp = jnp.einsum('bqgrn,nj->bqgj', p_cmp, overlap)
    blk = jnp.arange(ns)[None, :]
    cur = (q_pos // SEL_BLOCK)[:, None]
    eligible = (blk <= cur)[None, :, None, :]
    forced = ((blk == 0) | (blk == cur) | (blk == cur - 1))[None, :, None, :]
    sel_score = jnp.where(eligible, jnp.where(forced, jnp.inf, imp), -jnp.inf)
    n_sel = min(SEL_TOPN, ns)
    sel_val, sel_blk = lax.top_k(sel_score, n_sel)
    sel_pos = sel_blk[..., None] * SEL_BLOCK + jnp.arange(SEL_BLOCK)
    sel_ok = (sel_val > -jnp.inf)[..., None] & (sel_pos <= q_pos[None, :, None, None, None])
    sel_pos = sel_pos.reshape(bsz, nq, B_KV_HEADS, n_sel * SEL_BLOCK)
    sel_ok = sel_ok.reshape(bsz, nq, B_KV_HEADS, n_sel * SEL_BLOCK)
    kv_s = get_b_rows(sel_pos)
    gidx = jnp.arange(B_KV_HEADS)[None, None, :, None]
    bias_s = table_b[rel_bucket(q_pos[None, :, None, None] - sel_pos), gidx].transpose(0, 1, 2, 4, 3)
    logits = jnp.einsum('bqgrd,bqgnd->bqgrn', b_q, kv_s[..., 0, :]).astype(f32) * scale + bias_s
    p = masked_softmax(logits, sel_ok[:, :, :, None, :])
    o_sel = jnp.einsum('bqgrn,bqgnd->bqgrd', p.astype(kv_s.dtype), kv_s[..., 1, :])

    dist = q_pos[:, None] - win_pos[None, :]
    win_ok = (dist >= 0) & (dist <= WINDOW) & (win_pos[None, :] >= 0)
    bias_w = table_b[rel_bucket(dist)].transpose(0, 2, 3, 1)
    logits = jnp.einsum('bqgrd,bkgd->bqgrk', b_q, win_kv[..., 0, :]).astype(f32) * scale + bias_w[None]
    p = masked_softmax(logits, win_ok[None, :, None, None, :])
    o_win = jnp.einsum('bqgrk,bkgd->bqgrd', p.astype(win_kv.dtype), win_kv[..., 1, :])

    gate = jax.nn.sigmoid(b_gate.astype(f32)).astype(o_cmp.dtype)
    o_b = gate[..., 0:1] * o_cmp + gate[..., 1:2] * o_sel + gate[..., 2:3] * o_win
    return o_a, o_b.reshape(bsz, nq, B_HEADS * HEAD_DIM)


def mem_attend(m_q, mem_kv):
    logits = jnp.einsum('bthd,bmhd->bhtm', m_q, mem_kv[..., 0, :]).astype(jnp.float32) * MEM_HEAD_DIM ** -0.5
    p = jax.nn.softmax(logits, axis=-1)
    o = jnp.einsum('bhtm,bmhd->bthd', p.astype(mem_kv.dtype), mem_kv[..., 1, :])
    return o.reshape(m_q.shape[0], m_q.shape[1], MEM_HEADS * MEM_HEAD_DIM)


def merge_and_ffn(x, g_merge, o_a, o_b, o_m, lw):
    g = jax.nn.sigmoid(g_merge.astype(jnp.float32)).astype(x.dtype)
    merged = g[..., 0, :] * (o_a @ lw['w_pa']) + g[..., 1, :] * (o_b @ lw['w_pb']) + g[..., 2, :] * (o_m @ lw['w_pm'])
    x = layer_norm(DN_ALPHA * x + merged @ lw['w_o'], lw['ln1_g'], lw['ln1_b'])
    u = jnp.square(jax.nn.relu(x @ lw['w_up'] + lw['b_up']))
    return layer_norm(DN_ALPHA * x + u @ lw['w_down'] + lw['b_down'], lw['ln2_g'], lw['ln2_b'])


def gather_pages(pool, page_table):
    g = pool[page_table]
    return g.reshape((g.shape[0], g.shape[1] * g.shape[2]) + g.shape[3:])


def paged_rows(pool, page_table, new_rows, pos, bidx, gidx=None):
    past_len = page_table.shape[1] * PAGE_SIZE
    pp = jnp.clip(pos, 0, past_len - 1)
    phys = page_table[bidx, pp // PAGE_SIZE]
    npos = jnp.clip(pos - past_len, 0, new_rows.shape[1] - 1)
    if gidx is None:
        old, new = pool[phys, pp % PAGE_SIZE], new_rows[bidx, npos]
    else:
        old, new = pool[phys, pp % PAGE_SIZE, gidx], new_rows[bidx, npos, gidx]
    is_past = (pos < past_len).reshape(pos.shape + (1,) * (old.ndim - pos.ndim))
    return jnp.where(is_past, old, new)


def prompt_layer(x, mem, rel_table, lw):
    bsz, t, _ = x.shape
    pr = project(x, lw['w_in'])
    q = (pr['a_q'], pr['i_q'], pr['i_w'], pr['b_q'], pr['b_gate'])
    cmp_k = compress(pr['b_cmp'][..., 0, :], lw['cmp_pe_k'], lw['cmp_w1_k'], lw['cmp_w2_k'])
    cmp_v = compress(pr['b_cmp'][..., 1, :], lw['cmp_pe_v'], lw['cmp_w1_v'], lw['cmp_w2_v'])
    win_pad = jnp.pad(pr['b_win'], ((0, 0), (WINDOW, 0), (0, 0), (0, 0), (0, 0)))
    bidx = jnp.arange(bsz)
    gidx = jnp.arange(B_KV_HEADS)[None, None, :, None]

    def get_a_rows(pos):
        return pr['a_kv'][bidx[:, None, None], pos]

    def get_b_rows(pos):
        return pr['b_sel'][bidx[:, None, None, None], jnp.minimum(pos, t - 1), gidx]

    def query_block(c):
        s0 = c * Q_BLOCK
        qb = jax.tree_util.tree_map(lambda a: lax.dynamic_slice_in_dim(a, s0, Q_BLOCK, axis=1), q)
        win = lax.dynamic_slice_in_dim(win_pad, s0, WINDOW + Q_BLOCK, axis=1)
        return token_mixers(qb, s0 + jnp.arange(Q_BLOCK), pr['i_k'], cmp_k, cmp_v, get_a_rows, get_b_rows,
                            win, s0 - WINDOW + jnp.arange(WINDOW + Q_BLOCK), rel_table)

    o_a, o_b = lax.map(query_block, jnp.arange(t // Q_BLOCK))
    o_a = jnp.moveaxis(o_a, 0, 1).reshape(bsz, t, A_HEADS * HEAD_DIM)
    o_b = jnp.moveaxis(o_b, 0, 1).reshape(bsz, t, B_HEADS * HEAD_DIM)
    mem_kv = (mem @ lw['w_mem_kv']).reshape(bsz, N_MEM, MEM_HEADS, 2, MEM_HEAD_DIM)
    o_m = mem_attend(pr['m_q'], mem_kv)
    y = merge_and_ffn(x, pr['g_merge'], o_a, o_b, o_m, lw)
    wb = min(WINDOW, t)
    return y, (pr['a_kv'], pr['i_k'], pr['b_cmp'], pr['b_sel'], pr['b_win'][:, t - wb:], mem_kv)


def sample_layer(x, pool_a_kv, pool_a_idx, pool_b_cmp, pool_b_sel, win_state, mem_kv, page_table, rel_table, lw):
    dbsz, ds, _ = x.shape
    past_len = page_table.shape[1] * PAGE_SIZE
    pr = project(x, lw['w_in'])
    q = (pr['a_q'], pr['i_q'], pr['i_w'], pr['b_q'], pr['b_gate'])
    idx_keys = jnp.concatenate([gather_pages(pool_a_idx, page_table), pr['i_k']], axis=1)
    cmp_full = jnp.concatenate([gather_pages(pool_b_cmp, page_table), pr['b_cmp']], axis=1)
    cmp_k = compress(cmp_full[..., 0, :], lw['cmp_pe_k'], lw['cmp_w1_k'], lw['cmp_w2_k'])
    cmp_v = compress(cmp_full[..., 1, :], lw['cmp_pe_v'], lw['cmp_w1_v'], lw['cmp_w2_v'])
    wb = win_state.shape[1]
    win_kv = jnp.concatenate([win_state, pr['b_win']], axis=1)
    win_pos = past_len - wb + jnp.arange(wb + ds)
    bidx = jnp.arange(dbsz)
    gidx = jnp.arange(B_KV_HEADS)[None, None, :, None]

    def get_a_rows(pos):
        return paged_rows(pool_a_kv, page_table, pr['a_kv'], pos, bidx[:, None, None])

    def get_b_rows(pos):
        return paged_rows(pool_b_sel, page_table, pr['b_sel'], pos, bidx[:, None, None, None], gidx)

    o_a, o_b = token_mixers(q, past_len + jnp.arange(ds), idx_keys, cmp_k, cmp_v, get_a_rows, get_b_rows,
                            win_kv, win_pos, rel_table)
    o_m = mem_attend(pr['m_q'], mem_kv)
    y = merge_and_ffn(x, pr['g_merge'], o_a, o_b, o_m, lw)
    return y, (pr['a_kv'], pr['i_k'], pr['b_cmp'], pr['b_sel'], win_kv[:, ds:])


def setup_inputs(seed: int = 0) -> dict:
    key = jax.random.key(seed)
    ks = iter(jax.random.split(key, 40))

    def nrm(shape, scale=1.0):
        return jax.random.normal(next(ks), shape, jnp.float32) * scale

    d = D_MODEL
    n_pages = PAST_LEN // PAGE_SIZE
    n_used = DEC_BATCH * n_pages
    n_pool = n_used + max(1, n_used // 4)
    wb = min(WINDOW, PAST_LEN)
    page_table = jax.random.permutation(next(ks), n_pool)[:n_used].reshape(DEC_BATCH, n_pages).astype(jnp.int32)
    a_w = A_HEADS * HEAD_DIM
    b_w = B_HEADS * HEAD_DIM
    m_w = MEM_HEADS * MEM_HEAD_DIM
    cmp_in = CMP_BLOCK * HEAD_DIM
    return {
        'x_prompt': nrm((BATCH, SEQ, d)),
        'x_sample': nrm((DEC_BATCH, DEC_SEQ, d)),
        'mem_prompt': nrm((BATCH, N_MEM, d)),
        'cache_a_kv': nrm((DEPTH, n_pool, PAGE_SIZE, A_KV_HEADS, 2, HEAD_DIM)),
        'cache_a_idx': nrm((DEPTH, n_pool, PAGE_SIZE, IDX_DIM)),
        'cache_b_cmp': nrm((DEPTH, n_pool, PAGE_SIZE, B_KV_HEADS, 2, HEAD_DIM)),
        'cache_b_sel': nrm((DEPTH, n_pool, PAGE_SIZE, B_KV_HEADS, 2, HEAD_DIM)),
        'state_b_win': nrm((DEPTH, DEC_BATCH, wb, B_KV_HEADS, 2, HEAD_DIM)),
        'cache_mem': nrm((DEPTH, DEC_BATCH, N_MEM, MEM_HEADS, 2, MEM_HEAD_DIM)),
        'page_table': page_table,
        'rel_table': nrm((N_BUCKETS, A_HEADS + B_HEADS), 0.5),
        'w_in': nrm((DEPTH, d, D_IN), d ** -0.5),
        'w_mem_kv': nrm((DEPTH, d, 2 * m_w), d ** -0.5),
        'cmp_pe_k': nrm((DEPTH, CMP_BLOCK, HEAD_DIM), 0.1),
        'cmp_w1_k': nrm((DEPTH, cmp_in, CMP_HID), cmp_in ** -0.5),
        'cmp_w2_k': nrm((DEPTH, CMP_HID, HEAD_DIM), CMP_HID ** -0.5),
        'cmp_pe_v': nrm((DEPTH, CMP_BLOCK, HEAD_DIM), 0.1),
        'cmp_w1_v': nrm((DEPTH, cmp_in, CMP_HID), cmp_in ** -0.5),
        'cmp_w2_v': nrm((DEPTH, CMP_HID, HEAD_DIM), CMP_HID ** -0.5),
        'w_pa': nrm((DEPTH, a_w, d), a_w ** -0.5),
        'w_pb': nrm((DEPTH, b_w, d), b_w ** -0.5),
        'w_pm': nrm((DEPTH, m_w, d), m_w ** -0.5),
        'w_o': nrm((DEPTH, d, d), DN_BETA * d ** -0.5),
        'ln1_g': 1.0 + nrm((DEPTH, d), 0.02),
        'ln1_b': nrm((DEPTH, d), 0.02),
        'w_up': nrm((DEPTH, d, D_FF), d ** -0.5),
        'b_up': nrm((DEPTH, D_FF), 0.02),
        'w_down': nrm((DEPTH, D_FF, d), DN_BETA * D_FF ** -0.5),
        'b_down': nrm((DEPTH, d), 0.02),
        'ln2_g': 1.0 + nrm((DEPTH, d), 0.02),
        'ln2_b': nrm((DEPTH, d), 0.02),
    }


def reference(x_prompt, x_sample, mem_prompt, cache_a_kv, cache_a_idx, cache_b_cmp, cache_b_sel, state_b_win,
              cache_mem, page_table, rel_table, w_in, w_mem_kv, cmp_pe_k, cmp_w1_k, cmp_w2_k, cmp_pe_v, cmp_w1_v,
              cmp_w2_v, w_pa, w_pb, w_pm, w_o, ln1_g, ln1_b, w_up, b_up, w_down, b_down, ln2_g, ln2_b):
    hp, hs = x_prompt, x_sample
    p_states, s_states = [], []
    for l in range(DEPTH):
        lw = dict(w_in=w_in[l], w_mem_kv=w_mem_kv[l], cmp_pe_k=cmp_pe_k[l], cmp_w1_k=cmp_w1_k[l],
                  cmp_w2_k=cmp_w2_k[l], cmp_pe_v=cmp_pe_v[l], cmp_w1_v=cmp_w1_v[l], cmp_w2_v=cmp_w2_v[l],
                  w_pa=w_pa[l], w_pb=w_pb[l], w_pm=w_pm[l], w_o=w_o[l], ln1_g=ln1_g[l], ln1_b=ln1_b[l],
                  w_up=w_up[l], b_up=b_up[l], w_down=w_down[l], b_down=b_down[l], ln2_g=ln2_g[l], ln2_b=ln2_b[l])
        hp, st_p = prompt_layer(hp, mem_prompt, rel_table, lw)
        hs, st_s = sample_layer(hs, cache_a_kv[l], cache_a_idx[l], cache_b_cmp[l], cache_b_sel[l], state_b_win[l],
                                cache_mem[l], page_table, rel_table, lw)
        p_states.append(st_p)
        s_states.append(st_s)
    p_a_kv, p_a_idx, p_b_cmp, p_b_sel, p_b_win, p_mem = [jnp.stack(z) for z in zip(*p_states)]
    s_a_kv, s_a_idx, s_b_cmp, s_b_sel, s_b_win = [jnp.stack(z) for z in zip(*s_states)]
    return (hp, hs, p_a_kv, p_a_idx, p_b_cmp, p_b_sel, p_b_win, p_mem, s_a_kv, s_a_idx, s_b_cmp, s_b_sel, s_b_win)
```

```python
import functools
import math

import numpy as np
import jax
import jax.numpy as jnp
from jax import lax
from jax.experimental import pallas as pl
from jax.experimental.pallas import tpu as pltpu

F32 = jnp.float32
BF16 = jnp.bfloat16
I32 = jnp.int32

HEAD_DIM = 128
A_HEADS = 8
A_KV_HEADS = 2
A_GROUP = A_HEADS // A_KV_HEADS
IDX_HEADS = 8
IDX_DIM = 64
TOPK_MAX = 256
B_HEADS = 8
B_KV_HEADS = 2
B_GROUP = B_HEADS // B_KV_HEADS
CMP_BLOCK = 32
CMP_STRIDE = 16
CMP_RATIO = CMP_BLOCK // CMP_STRIDE
CMP_HID = 128
SEL_BLOCK = 64
SEL_TOPN = 16
WINDOW = 512
MEM_HEADS = 4
MEM_HEAD_DIM = 256
N_BUCKETS = 32
MAX_DISTANCE = 1024
LN_EPS = 1e-5
PAGE_SIZE = 128

LANES = 128
VMEM_LIMIT = 56 * 1024 * 1024

TILE = 128
NEG = -1e30
INT_MIN = -2 ** 31
PAGES_PER_STEP = 16

_GROUPS = ('a_q', 'a_kv', 'i_q', 'i_k', 'i_w', 'b_q', 'b_cmp', 'b_sel', 'b_win', 'b_gate', 'm_q', 'g_merge')


def _split_sizes(d_model):
    return dict(
        a_q=A_HEADS * HEAD_DIM, a_kv=A_KV_HEADS * 2 * HEAD_DIM, i_q=IDX_HEADS * IDX_DIM, i_k=IDX_DIM,
        i_w=IDX_HEADS, b_q=B_HEADS * HEAD_DIM, b_cmp=B_KV_HEADS * 2 * HEAD_DIM, b_sel=B_KV_HEADS * 2 * HEAD_DIM,
        b_win=B_KV_HEADS * 2 * HEAD_DIM, b_gate=B_HEADS * 3, m_q=MEM_HEADS * MEM_HEAD_DIM, g_merge=3 * d_model)


def _params(n_grid, vmem=VMEM_LIMIT):
    return pltpu.CompilerParams(dimension_semantics=('arbitrary',) * n_grid, vmem_limit_bytes=vmem)


def _resident(block, index_map):
    return pl.BlockSpec(block, index_map, pipeline_mode=pl.Buffered(1))


def _mm_kernel(x_ref, w_ref, o_ref):
    o_ref[...] = jnp.dot(x_ref[...].astype(BF16), w_ref[...], preferred_element_type=F32).astype(o_ref.dtype)


def _matmul(x, w, out_dtype, tn, tm=1024):
    m, k = x.shape
    n = w.shape[1]
    tm = min(tm, m)
    assert m % tm == 0 and n % tn == 0
    return pl.pallas_call(
        _mm_kernel,
        grid=(m // tm, n // tn),
        in_specs=[pl.BlockSpec((tm, k), lambda i, j: (i, 0)), pl.BlockSpec((k, tn), lambda i, j: (0, j))],
        out_specs=pl.BlockSpec((tm, tn), lambda i, j: (i, j)),
        out_shape=jax.ShapeDtypeStruct((m, n), out_dtype),
        compiler_params=_params(2),
    )(x, w)


def _rel_bucket(dist):
    d = jnp.maximum(dist, 0)
    exact = N_BUCKETS // 2
    df = jnp.maximum(d, 1).astype(F32)
    large = exact + (jnp.log(df / exact) / math.log(MAX_DISTANCE / exact) * (N_BUCKETS - exact)).astype(I32)
    return jnp.where(d < exact, d, jnp.minimum(large, N_BUCKETS - 1))


def _num_near_tiles():
    exact = N_BUCKETS // 2
    d = np.arange(1, 4 * MAX_DISTANCE, dtype=np.float64)
    large = exact + np.floor(np.log(d / exact) / math.log(MAX_DISTANCE / exact) * (N_BUCKETS - exact))
    bucket = np.where(d < exact, d, np.minimum(large, N_BUCKETS - 1))
    d_const = int(d[np.argmax(bucket == N_BUCKETS - 1)])
    return -(-(d_const + TILE // 2 + TILE - 1) // TILE)


def _bias_kernel(u_ref, o_ref, *, n_tiles, d_top):
    for dt in range(n_tiles):
        start = d_top - dt * TILE - (TILE - 1)
        row = u_ref[0, :, start:start + 2 * TILE]
        x = jnp.broadcast_to(row, (TILE, 2 * TILE))
        x = pltpu.roll(x, TILE + 1, 1, stride=1, stride_axis=0)
        o_ref[dt] = x[:, :TILE]


def _bias_tiles(rel_table, nd):
    n_tiles = nd + 1
    d_top = n_tiles * TILE
    ul = d_top + 2 * TILE
    n_heads = rel_table.shape[1]
    dist = d_top - jnp.arange(ul)
    u = rel_table[_rel_bucket(dist)].T.reshape(n_heads, 1, ul)
    out = pl.pallas_call(
        functools.partial(_bias_kernel, n_tiles=n_tiles, d_top=d_top),
        grid=(n_heads,),
        in_specs=[pl.BlockSpec((1, 1, ul), lambda h: (h, 0, 0))],
        out_specs=pl.BlockSpec((None, n_tiles, None, None, TILE, TILE),
                               lambda h: (h // 8, 0, (h % 8) // 4, h % 4, 0, 0)),
        out_shape=jax.ShapeDtypeStruct((2, n_tiles, 2, 4, TILE, TILE), F32),
        compiler_params=_params(1),
    )(u)
    return out.reshape(2, n_tiles, 2, 4 * TILE, TILE)


def _dot_nt(a, b):
    return lax.dot_general(a, b, (((1,), (1,)), ((), ())), preferred_element_type=F32)


def _flash_step(s, m, l, acc, v):
    m_new = jnp.maximum(m, jnp.max(s, axis=1, keepdims=True))
    alpha = jnp.exp(m - m_new)
    p = jnp.exp(s - m_new)
    l = alpha * l + jnp.sum(p, axis=1, keepdims=True)
    acc = alpha * acc + jnp.dot(p.astype(BF16), v, preferred_element_type=F32)
    return m_new, l, acc


def _flash_init(rows, width):
    return (jnp.full((rows, 1), NEG, F32), jnp.zeros((rows, 1), F32), jnp.zeros((rows, width), F32))


def _flash_out(m, l, acc):
    return jnp.where(m > 0.5 * NEG, acc / jnp.maximum(l, 1e-30), 0.0)


def _sortable_key(x):
    bits = pltpu.bitcast(x, I32)
    bits = jnp.where(bits == INT_MIN, 0, bits)
    return jnp.where(bits < 0, bits ^ 0x7FFFFFFF, bits)


def _kth_largest_key(count_ge, rows, k):
    def bit_body(i, t):
        cand = t + lax.shift_left(jnp.int32(1), 31 - i)
        return jnp.where(count_ge(cand) >= k, cand, t)
    return lax.fori_loop(0, 32, bit_body, jnp.full((rows, 1), INT_MIN, I32))


def _topn_mask(score, n):
    colf = lax.broadcasted_iota(I32, score.shape, 1).astype(F32)

    def body(_, carry):
        sc, selm = carry
        mx = jnp.max(sc, axis=1, keepdims=True)
        first = jnp.min(jnp.where(sc == mx, colf, 1e9), axis=1, keepdims=True)
        hit = colf == first
        selm = jnp.maximum(selm, jnp.where(hit, jnp.where(mx > -jnp.inf, 1.0, 0.0), 0.0))
        return jnp.where(hit, -jnp.inf, sc), selm

    return lax.fori_loop(0, n, body, (score, jnp.zeros(score.shape, F32)))[1]


def _softmax_rows(s, ok):
    s = jnp.where(ok, s, NEG)
    m = jnp.max(s, axis=1, keepdims=True)
    e = jnp.where(ok, jnp.exp(s - m), 0.0)
    return e / jnp.maximum(jnp.sum(e, axis=1, keepdims=True), 1e-30)


def _block_expand(blk0, n_blk):
    rb = lax.broadcasted_iota(I32, (n_blk, TILE), 0)
    cj = lax.broadcasted_iota(I32, (n_blk, TILE), 1)
    target = blk0 + jnp.where(cj >= SEL_BLOCK, 1, 0)
    return jnp.where(rb == target, 1.0, 0.0).astype(BF16)


def _gelu(x):
    return 0.5 * x * (1.0 + jnp.tanh(math.sqrt(2.0 / math.pi) * (x + 0.044715 * (x * x * x))))


def _layer_norm(x, g, b):
    xc = x - jnp.mean(x, axis=1, keepdims=True)
    var = jnp.mean(xc * xc, axis=1, keepdims=True)
    return xc * lax.rsqrt(var + LN_EPS) * g + b


def _cmpz_compute(x_ref, w_ref, o_ref):
    rows = x_ref.shape[0]
    n_col = B_KV_HEADS * 2 * HEAD_DIM
    for kv in range(2):
        xs = []
        for g in range(B_KV_HEADS):
            c0 = (g * 2 + kv) * HEAD_DIM
            xs.append(jnp.concatenate(
                [x_ref[:, p * n_col + c0:p * n_col + c0 + HEAD_DIM] for p in range(CMP_STRIDE)], axis=1))
        z = jnp.dot(jnp.concatenate(xs, axis=0).astype(BF16), w_ref[kv], preferred_element_type=F32)
        for g in range(B_KV_HEADS):
            c = (g * 2 + kv) * CMP_RATIO * CMP_HID
            o_ref[:, c:c + CMP_RATIO * CMP_HID] = z[g * rows:(g + 1) * rows]


def _cmpz_kernel(x_ref, w_ref, o_ref):
    _cmpz_compute(x_ref, w_ref, o_ref)


def _cmpz_dense(x2d, w1cat, tc=256):
    n = x2d.shape[0]
    tc = min(tc, n)
    assert n % tc == 0
    return pl.pallas_call(
        _cmpz_kernel,
        grid=(n // tc,),
        in_specs=[pl.BlockSpec((tc, x2d.shape[1]), lambda i: (i, 0)),
                  _resident(w1cat.shape, lambda i: (0, 0, 0))],
        out_specs=pl.BlockSpec((tc, 4 * CMP_RATIO * CMP_HID), lambda i: (i, 0)),
        out_shape=jax.ShapeDtypeStruct((n, 4 * CMP_RATIO * CMP_HID), F32),
        compiler_params=_params(1),
    )(x2d, w1cat)


def _page_fetch(pt_ref, pool_ref, buf_ref, sem_ref, step, slot):
    for k in range(PAGES_PER_STEP):
        pid = pt_ref[step * PAGES_PER_STEP + k]
        pltpu.make_async_copy(pool_ref.at[pid], buf_ref.at[slot, k], sem_ref.at[slot]).start()


def _page_wait(pool_ref, buf_ref, sem_ref, slot):
    for k in range(PAGES_PER_STEP):
        pltpu.make_async_copy(pool_ref.at[0], buf_ref.at[slot, k], sem_ref.at[slot]).wait()


def _page_pipeline(pt_ref, pool_ref, buf_ref, sem_ref):
    step = pl.program_id(0) * pl.num_programs(1) + pl.program_id(1)
    total = pl.num_programs(0) * pl.num_programs(1)
    slot = lax.rem(step, 2)

    @pl.when(step == 0)
    def _():
        _page_fetch(pt_ref, pool_ref, buf_ref, sem_ref, step, slot)

    @pl.when(step + 1 < total)
    def _():
        _page_fetch(pt_ref, pool_ref, buf_ref, sem_ref, step + 1, 1 - slot)

    _page_wait(pool_ref, buf_ref, sem_ref, slot)
    return slot


def _cmpz_paged_kernel(pt_ref, pool_ref, w_ref, o_ref, buf_ref, sem_ref, x_ref):
    slot = _page_pipeline(pt_ref, pool_ref, buf_ref, sem_ref)
    chunks = PAGE_SIZE // CMP_STRIDE
    for k in range(PAGES_PER_STEP):
        x_ref[k * chunks:(k + 1) * chunks, :] = buf_ref[slot, k]
    _cmpz_compute(x_ref, w_ref, o_ref)


def _cmpz_paged(pool, page_table, w1cat):
    n_pool = pool.shape[0]
    db, n_pages = page_table.shape
    chunks = PAGE_SIZE // CMP_STRIDE
    width = CMP_STRIDE * B_KV_HEADS * 2 * HEAD_DIM
    pool3 = pool.reshape(n_pool, chunks, width)
    npg = n_pages // PAGES_PER_STEP
    rows = PAGES_PER_STEP * chunks
    return pl.pallas_call(
        _cmpz_paged_kernel,
        grid_spec=pltpu.PrefetchScalarGridSpec(
            num_scalar_prefetch=1,
            grid=(db, npg),
            in_specs=[pl.BlockSpec(memory_space=pl.ANY),
                      _resident(w1cat.shape, lambda b, g, pt: (0, 0, 0))],
            out_specs=pl.BlockSpec((rows, 4 * CMP_RATIO * CMP_HID), lambda b, g, pt: (b * npg + g, 0)),
            scratch_shapes=[pltpu.VMEM((2, PAGES_PER_STEP, chunks, width), F32),
                            pltpu.SemaphoreType.DMA((2,)),
                            pltpu.VMEM((rows, width), F32)]),
        out_shape=jax.ShapeDtypeStruct((db * n_pages * chunks, 4 * CMP_RATIO * CMP_HID), F32),
        compiler_params=_params(2),
    )(page_table.reshape(-1), pool3, w1cat)


def _cmp_finish_kernel(z_ref, pe_ref, w1_ref, w2_ref, k_ref, v_ref):
    n = z_ref.shape[1]
    for kv, o_ref in ((0, k_ref), (1, v_ref)):
        pew = jnp.dot(pe_ref[kv], w1_ref[kv], preferred_element_type=F32)[0:1]
        for g in range(B_KV_HEADS):
            c = (g * 2 + kv) * CMP_RATIO * CMP_HID
            z0 = z_ref[0, :, c:c + CMP_HID]
            z1 = z_ref[0, :, c + CMP_HID:c + 2 * CMP_HID]
            pre = z0 + pltpu.roll(z1, n - 1, 0) + pew
            out = jnp.dot(_gelu(pre).astype(BF16), w2_ref[kv], preferred_element_type=F32)
            o_ref[0, :, g * HEAD_DIM:(g + 1) * HEAD_DIM] = out.astype(o_ref.dtype)


def _cmp_finish(z3, pe8, w1, w2):
    nb, n, zc = z3.shape
    out = jax.ShapeDtypeStruct((nb, n, B_KV_HEADS * HEAD_DIM), BF16)
    return pl.pallas_call(
        _cmp_finish_kernel,
        grid=(nb,),
        in_specs=[pl.BlockSpec((1, n, zc), lambda b: (b, 0, 0)),
                  _resident(pe8.shape, lambda b: (0, 0, 0)),
                  _resident(w1.shape, lambda b: (0, 0, 0)),
                  _resident(w2.shape, lambda b: (0, 0, 0))],
        out_specs=[pl.BlockSpec((1, n, B_KV_HEADS * HEAD_DIM), lambda b: (b, 0, 0))] * 2,
        out_shape=[out, out],
        compiler_params=_params(1),
    )(z3, pe8, w1, w2)


def _ka_kernel(iq_ref, ikw_ref, ikwq_ref, aq_ref, akv_ref, bias_ref, o_ref, keys_ref, *, topk, nd):
    qt = pl.program_id(1)
    row = lax.broadcasted_iota(I32, (TILE, TILE), 0)
    col = lax.broadcasted_iota(I32, (TILE, TILE), 1)
    w = ikwq_ref[:, IDX_DIM:IDX_DIM + IDX_HEADS] * (IDX_HEADS ** -0.5 * IDX_DIM ** -0.5)

    def score_tile(kt, diag):
        kb = ikw_ref[pl.ds(pl.multiple_of(kt * TILE, TILE), TILE), :].astype(BF16)
        acc = jnp.zeros((TILE, TILE), F32)
        for h in range(IDX_HEADS):
            s = _dot_nt(iq_ref[:, h * LANES:(h + 1) * LANES], kb)
            acc = acc + jnp.maximum(s, 0.0) * w[:, h:h + 1]
        key = _sortable_key(acc)
        if diag:
            key = jnp.where(col <= row, key, INT_MIN)
        keys_ref[kt] = key

    def score_body(kt, c):
        score_tile(kt, False)
        return c

    lax.fori_loop(0, qt, score_body, 0)
    score_tile(qt, True)

    def count_where(pred):
        def body(kt, c):
            return c + jnp.where(pred(keys_ref[kt], kt), 1.0, 0.0)
        c = lax.fori_loop(0, qt + 1, body, jnp.zeros((TILE, TILE), F32))
        return jnp.sum(c, axis=1, keepdims=True)

    def count_ge(cand):
        return count_where(lambda k, kt: k >= cand)

    t = _kth_largest_key(count_ge, TILE, topk)
    thr = jnp.maximum(t, INT_MIN + 1)

    cnt_gt = count_ge(thr + 1)
    need = topk - cnt_gt
    cnt_eq = count_ge(thr) - cnt_gt
    tie = jnp.where(t > INT_MIN, jnp.where(cnt_eq > need, 1.0, 0.0), 0.0)

    @pl.when(jnp.max(tie) > 0.0)
    def _():
        n_bits = int(keys_ref.shape[0] * TILE).bit_length()

        def idx_body(i, mm):
            cand = mm + lax.shift_left(jnp.int32(1), n_bits - 1 - i)
            c = count_where(lambda k, kt: jnp.where(k == thr, kt * TILE + col, INT_MIN) < cand)
            c = c - count_where(lambda k, kt: k != thr)
            return jnp.where(c < need, cand, mm)

        last = lax.fori_loop(0, n_bits, idx_body, jnp.zeros((TILE, 1), I32))
        last = jnp.where(tie > 0.0, last, jnp.int32(2 ** 30))

        def demote(kt, c):
            k = keys_ref[kt]
            pos = jnp.where(k == thr, kt * TILE + col, INT_MIN)
            keys_ref[kt] = jnp.where(pos > last, thr - 1, k)
            return c

        lax.fori_loop(0, qt + 1, demote, 0)

    scale = HEAD_DIM ** -0.5
    q = [jnp.concatenate([aq_ref[:, (g * A_GROUP + r) * HEAD_DIM:(g * A_GROUP + r + 1) * HEAD_DIM]
                          for r in range(A_GROUP)], axis=0) for g in range(A_KV_HEADS)]

    def att_body(kt, carry):
        dlt = jnp.minimum(qt - kt, nd)
        madd = jnp.where(keys_ref[kt] >= thr, 0.0, NEG)
        madd = jnp.concatenate([madd] * A_GROUP, axis=0)
        k0 = pl.multiple_of(kt * TILE, TILE)
        out = []
        for g in range(A_KV_HEADS):
            c = g * 2 * HEAD_DIM
            kk = akv_ref[pl.ds(k0, TILE), c:c + HEAD_DIM].astype(BF16)
            vv = akv_ref[pl.ds(k0, TILE), c + HEAD_DIM:c + 2 * HEAD_DIM].astype(BF16)
            s = _dot_nt(q[g], kk) * scale + (bias_ref[dlt, g] + madd)
            out.append(_flash_step(s, *carry[g], vv))
        return tuple(out)

    carry = lax.fori_loop(0, qt + 1, att_body,
                          tuple(_flash_init(A_GROUP * TILE, HEAD_DIM) for _ in range(A_KV_HEADS)))
    for g in range(A_KV_HEADS):
        o = _flash_out(*carry[g])
        for r in range(A_GROUP):
            h = g * A_GROUP + r
            o_ref[:, h * HEAD_DIM:(h + 1) * HEAD_DIM] = o[r * TILE:(r + 1) * TILE].astype(o_ref.dtype)


def _prompt_mixer_a(qmat, fmat, bias_a, bsz, t, cols, nd):
    nt = t // TILE
    topk = min(TOPK_MAX, t // 4)
    qc, fc = cols['q'], cols['f']
    return pl.pallas_call(
        functools.partial(_ka_kernel, topk=topk, nd=nd),
        grid=(bsz, nt),
        in_specs=[
            pl.BlockSpec((TILE, IDX_HEADS * LANES), lambda b, i: (b * nt + i, qc['i_q'] // (IDX_HEADS * LANES))),
            _resident((t, LANES), lambda b, i: (b, fc['i_kw'] // LANES)),
            pl.BlockSpec((TILE, LANES), lambda b, i: (b * nt + i, fc['i_kw'] // LANES)),
            pl.BlockSpec((TILE, A_HEADS * HEAD_DIM), lambda b, i: (b * nt + i, qc['a_q'] // (A_HEADS * HEAD_DIM))),
            _resident((t, 4 * HEAD_DIM), lambda b, i: (b, fc['a_kv'] // (4 * HEAD_DIM))),
            _resident(bias_a.shape, lambda b, i: (0, 0, 0, 0)),
        ],
        out_specs=pl.BlockSpec((TILE, A_HEADS * HEAD_DIM), lambda b, i: (b * nt + i, 0)),
        out_shape=jax.ShapeDtypeStruct((bsz * t, A_HEADS * HEAD_DIM), BF16),
        scratch_shapes=[pltpu.VMEM((nt, TILE, TILE), I32)],
        compiler_params=_params(2),
    )(qmat, fmat, fmat, qmat, fmat, bias_a)


def _kb_kernel(bq_ref, gate_ref, ck_ref, cv_ref, ov_ref, sel_ref, w0_ref, w1_ref, w2_ref, w3_ref, w4_ref,
               bias_ref, o_ref, *, nd, n_sel):
    qt = pl.program_id(1)
    rows = B_GROUP * TILE
    row = lax.broadcasted_iota(I32, (TILE, TILE), 0)
    col = lax.broadcasted_iota(I32, (TILE, TILE), 1)
    scale = HEAD_DIM ** -0.5
    q = [jnp.concatenate([bq_ref[:, (g * B_GROUP + r) * HEAD_DIM:(g * B_GROUP + r + 1) * HEAD_DIM]
                          for r in range(B_GROUP)], axis=0) for g in range(B_KV_HEADS)]

    ncp = ck_ref.shape[1]
    q_pos = qt * TILE + lax.rem(lax.broadcasted_iota(I32, (rows, ncp), 0), TILE)
    cmp_end = lax.broadcasted_iota(I32, (rows, ncp), 1) * CMP_STRIDE + (CMP_BLOCK - 1)
    cmp_ok = cmp_end <= q_pos
    cur = 2 * qt + jnp.where(row >= SEL_BLOCK, 1, 0)
    o_cmp, selm = [], []
    for g in range(B_KV_HEADS):
        p = _softmax_rows(_dot_nt(q[g], ck_ref[0, :, g * HEAD_DIM:(g + 1) * HEAD_DIM]) * scale, cmp_ok)
        o_cmp.append(jnp.dot(p.astype(BF16), cv_ref[0, :, g * HEAD_DIM:(g + 1) * HEAD_DIM],
                             preferred_element_type=F32))
        psum = p[0:TILE]
        for r in range(1, B_GROUP):
            psum = psum + p[r * TILE:(r + 1) * TILE]
        imp = jnp.dot(psum, ov_ref[...], preferred_element_type=F32, precision=lax.Precision.HIGHEST)
        forced = jnp.where(col == 0, jnp.inf, jnp.where(col >= cur - 1, jnp.inf, imp))
        selm.append(_topn_mask(jnp.where(col <= cur, forced, -jnp.inf), n_sel).astype(BF16))

    def sel_step(kt, carry, diag):
        dlt = jnp.minimum(qt - kt, nd)
        k0 = pl.multiple_of(kt * TILE, TILE)
        expand = _block_expand(2 * kt, selm[0].shape[1])
        out = []
        for g in range(B_KV_HEADS):
            madd = (jnp.dot(selm[g], expand, preferred_element_type=F32) - 1.0) * (-NEG)
            if diag:
                madd = jnp.where(col <= row, madd, NEG)
            madd = jnp.concatenate([madd] * B_GROUP, axis=0)
            c = g * 2 * HEAD_DIM
            kk = sel_ref[pl.ds(k0, TILE), c:c + HEAD_DIM].astype(BF16)
            vv = sel_ref[pl.ds(k0, TILE), c + HEAD_DIM:c + 2 * HEAD_DIM].astype(BF16)
            s = _dot_nt(q[g], kk) * scale + (bias_ref[dlt, g] + madd)
            out.append(_flash_step(s, *carry[g], vv))
        return tuple(out)

    carry = lax.fori_loop(0, qt, lambda kt, c: sel_step(kt, c, False),
                          tuple(_flash_init(rows, HEAD_DIM) for _ in range(B_KV_HEADS)))
    carry = sel_step(qt, carry, True)
    o_sel = [_flash_out(*carry[g]) for g in range(B_KV_HEADS)]

    wcarry = [_flash_init(rows, HEAD_DIM) for _ in range(B_KV_HEADS)]
    for k, w_ref in enumerate((w0_ref, w1_ref, w2_ref, w3_ref, w4_ref)):
        if k == 0:
            ok = col <= row
        elif k == WINDOW // TILE:
            ok = row <= col
        else:
            ok = col >= 0
        madd = jnp.where(ok, jnp.where(qt >= k, 0.0, NEG), NEG)
        madd = jnp.concatenate([madd] * B_GROUP, axis=0)
        for g in range(B_KV_HEADS):
            c = g * 2 * HEAD_DIM
            kk = w_ref[:, c:c + HEAD_DIM].astype(BF16)
            vv = w_ref[:, c + HEAD_DIM:c + 2 * HEAD_DIM].astype(BF16)
            s = _dot_nt(q[g], kk) * scale + (bias_ref[k, g] + madd)
            wcarry[g] = _flash_step(s, *wcarry[g], vv)
    o_win = [_flash_out(*wcarry[g]) for g in range(B_KV_HEADS)]

    gate = jax.nn.sigmoid(gate_ref[...])
    for g in range(B_KV_HEADS):
        for r in range(B_GROUP):
            h = g * B_GROUP + r
            sl = slice(r * TILE, (r + 1) * TILE)
            o = (gate[:, 3 * h:3 * h + 1] * o_cmp[g][sl] + gate[:, 3 * h + 1:3 * h + 2] * o_sel[g][sl]
                 + gate[:, 3 * h + 2:3 * h + 3] * o_win[g][sl])
            o_ref[:, h * HEAD_DIM:(h + 1) * HEAD_DIM] = o.astype(o_ref.dtype)


def _overlap_matrix(length, n_rows, n_cols):
    nc = (length - CMP_BLOCK) // CMP_STRIDE + 1
    ns = -(-length // SEL_BLOCK)
    cs = np.arange(nc) * CMP_STRIDE
    ss = np.arange(ns) * SEL_BLOCK
    ov = np.minimum(cs[:, None] + CMP_BLOCK, ss[None, :] + SEL_BLOCK) - np.maximum(cs[:, None], ss[None, :])
    out = np.zeros((n_rows, n_cols), np.float32)
    out[:nc, :ns] = np.clip(ov, 0, None).astype(np.float32) / CMP_BLOCK
    return jnp.asarray(out)


def _prompt_mixer_b(qmat, fmat, cmp_k, cmp_v, bias_b, bsz, t, cols, nd):
    nt = t // TILE
    ns = -(-t // SEL_BLOCK)
    assert ns <= LANES and WINDOW // TILE == 4 and nd >= WINDOW // TILE
    qc, fc = cols['q'], cols['f']
    ncp = cmp_k.shape[1]
    ov = _overlap_matrix(t, ncp, LANES)
    kvw = 4 * HEAD_DIM

    def win_spec(k):
        return pl.BlockSpec((TILE, kvw), lambda b, i: (b * nt + jnp.maximum(i - k, 0), fc['b_win'] // kvw))

    return pl.pallas_call(
        functools.partial(_kb_kernel, nd=nd, n_sel=min(SEL_TOPN, ns)),
        grid=(bsz, nt),
        in_specs=[
            pl.BlockSpec((TILE, B_HEADS * HEAD_DIM), lambda b, i: (b * nt + i, qc['b_q'] // (B_HEADS * HEAD_DIM))),
            pl.BlockSpec((TILE, LANES), lambda b, i: (b * nt + i, fc['b_gate'] // LANES)),
            _resident((1, ncp, B_KV_HEADS * HEAD_DIM), lambda b, i: (b, 0, 0)),
            _resident((1, ncp, B_KV_HEADS * HEAD_DIM), lambda b, i: (b, 0, 0)),
            _resident(ov.shape, lambda b, i: (0, 0)),
            _resident((t, kvw), lambda b, i: (b, fc['b_sel'] // kvw)),
            win_spec(0), win_spec(1), win_spec(2), win_spec(3), win_spec(4),
            _resident(bias_b.shape, lambda b, i: (0, 0, 0, 0)),
        ],
        out_specs=pl.BlockSpec((TILE, B_HEADS * HEAD_DIM), lambda b, i: (b * nt + i, 0)),
        out_shape=jax.ShapeDtypeStruct((bsz * t, B_HEADS * HEAD_DIM), BF16),
        compiler_params=_params(2),
    )(qmat, fmat, cmp_k, cmp_v, ov, fmat, fmat, fmat, fmat, fmat, fmat, bias_b)


def _mem_kernel(q_ref, kv_ref, o_ref):
    scale = MEM_HEAD_DIM ** -0.5
    for h in range(MEM_HEADS):
        c = h * 2 * MEM_HEAD_DIM
        kk = kv_ref[:, c:c + MEM_HEAD_DIM].astype(BF16)
        vv = kv_ref[:, c + MEM_HEAD_DIM:c + 2 * MEM_HEAD_DIM].astype(BF16)
        s = _dot_nt(q_ref[:, h * MEM_HEAD_DIM:(h + 1) * MEM_HEAD_DIM], kk) * scale
        e = jnp.exp(s - jnp.max(s, axis=1, keepdims=True))
        p = e / jnp.sum(e, axis=1, keepdims=True)
        o = jnp.dot(p.astype(BF16), vv, preferred_element_type=F32)
        o_ref[:, h * MEM_HEAD_DIM:(h + 1) * MEM_HEAD_DIM] = o.astype(o_ref.dtype)


def _mem_attend(qmat, q_col, mem_kv2d, n_batch, rows_per_batch, n_mem, tq):
    width = MEM_HEADS * MEM_HEAD_DIM
    tq = min(tq, rows_per_batch)
    nq = rows_per_batch // tq
    return pl.pallas_call(
        _mem_kernel,
        grid=(n_batch, nq),
        in_specs=[pl.BlockSpec((tq, width), lambda b, i: (b * nq + i, q_col // width)),
                  pl.BlockSpec((n_mem, 2 * width), lambda b, i: (b, 0))],
        out_specs=pl.BlockSpec((tq, width), lambda b, i: (b * nq + i, 0)),
        out_shape=jax.ShapeDtypeStruct((n_batch * rows_per_batch, width), BF16),
        compiler_params=_params(2),
    )(qmat, mem_kv2d)


def _gated_proj_kernel(ga_ref, gb_ref, gm_ref, oa_ref, ob_ref, om_ref, wpa_ref, wpb_ref, wpm_ref, o_ref):
    merged = jax.nn.sigmoid(ga_ref[...]) * jnp.dot(oa_ref[...], wpa_ref[...], preferred_element_type=F32)
    merged = merged + jax.nn.sigmoid(gb_ref[...]) * jnp.dot(ob_ref[...], wpb_ref[...], preferred_element_type=F32)
    merged = merged + jax.nn.sigmoid(gm_ref[...]) * jnp.dot(om_ref[...], wpm_ref[...], preferred_element_type=F32)
    o_ref[...] = merged.astype(o_ref.dtype)


def _out_proj_kernel(x_ref, mg_ref, wo_ref, lg_ref, lb_ref, o_ref, *, alpha):
    y = alpha * x_ref[...] + jnp.dot(mg_ref[...], wo_ref[...], preferred_element_type=F32)
    o_ref[...] = _layer_norm(y, lg_ref[...], lb_ref[...])


def _merge(x2d, gmat, oa, ob, om, wpa, wpb, wpm, wo, ln_g, ln_b, alpha, tm=512, tn=1024):
    m, d = x2d.shape
    tm = min(tm, m)
    assert m % tm == 0 and d % tn == 0
    nj = d // tn
    row = lambda i, j: (i, 0)
    wcol = lambda i, j: (0, j)
    merged = pl.pallas_call(
        _gated_proj_kernel,
        grid=(m // tm, nj),
        in_specs=[pl.BlockSpec((tm, tn), lambda i, j: (i, j)),
                  pl.BlockSpec((tm, tn), lambda i, j: (i, nj + j)),
                  pl.BlockSpec((tm, tn), lambda i, j: (i, 2 * nj + j)),
                  pl.BlockSpec((tm, oa.shape[1]), row), pl.BlockSpec((tm, ob.shape[1]), row),
                  pl.BlockSpec((tm, om.shape[1]), row),
                  pl.BlockSpec((wpa.shape[0], tn), wcol), pl.BlockSpec((wpb.shape[0], tn), wcol),
                  pl.BlockSpec((wpm.shape[0], tn), wcol)],
        out_specs=pl.BlockSpec((tm, tn), lambda i, j: (i, j)),
        out_shape=jax.ShapeDtypeStruct((m, d), BF16),
        compiler_params=_params(2),
    )(gmat, gmat, gmat, oa, ob, om, wpa, wpb, wpm)
    fixed = lambda i: (0, 0)
    return pl.pallas_call(
        functools.partial(_out_proj_kernel, alpha=alpha),
        grid=(m // tm,),
        in_specs=[pl.BlockSpec((tm, d), lambda i: (i, 0)), pl.BlockSpec((tm, d), lambda i: (i, 0)),
                  _resident(wo.shape, fixed), _resident((1, d), fixed), _resident((1, d), fixed)],
        out_specs=pl.BlockSpec((tm, d), lambda i: (i, 0)),
        out_shape=jax.ShapeDtypeStruct((m, d), F32),
        compiler_params=_params(1),
    )(x2d, merged, wo, ln_g, ln_b)


def _ffn_kernel(x_ref, wu_ref, bu_ref, wd_ref, bd_ref, lg_ref, lb_ref, o_ref, acc_ref, *, alpha):
    j = pl.program_id(1)

    @pl.when(j == 0)
    def _():
        acc_ref[...] = jnp.zeros_like(acc_ref)

    u = jnp.dot(x_ref[...].astype(BF16), wu_ref[...], preferred_element_type=F32) + bu_ref[...]
    u = jnp.square(jnp.maximum(u, 0.0))
    acc_ref[...] += jnp.dot(u.astype(BF16), wd_ref[...], preferred_element_type=F32)

    @pl.when(j == pl.num_programs(1) - 1)
    def _():
        y = alpha * x_ref[...] + acc_ref[...] + bd_ref[...]
        o_ref[...] = _layer_norm(y, lg_ref[...], lb_ref[...])


def _ffn(x2d, wu, bu, wd, bd, ln_g, ln_b, alpha, tm=512, tf=1024):
    m, d = x2d.shape
    dff = wu.shape[1]
    tm = min(tm, m)
    assert m % tm == 0 and dff % tf == 0
    return pl.pallas_call(
        functools.partial(_ffn_kernel, alpha=alpha),
        grid=(m // tm, dff // tf),
        in_specs=[pl.BlockSpec((tm, d), lambda i, j: (i, 0)),
                  pl.BlockSpec((d, tf), lambda i, j: (0, j)), pl.BlockSpec((1, tf), lambda i, j: (0, j)),
                  pl.BlockSpec((tf, d), lambda i, j: (j, 0)),
                  _resident((1, d), lambda i, j: (0, 0)), _resident((1, d), lambda i, j: (0, 0)),
                  _resident((1, d), lambda i, j: (0, 0))],
        out_specs=pl.BlockSpec((tm, d), lambda i, j: (i, 0)),
        out_shape=jax.ShapeDtypeStruct((m, d), F32),
        scratch_shapes=[pltpu.VMEM((tm, d), F32)],
        compiler_params=_params(2),
    )(x2d, wu, bu, wd, bd, ln_g, ln_b)


def _sidx_kernel(pt_ref, iq_ref, w_ref, knew_ref, pool_ref, o_ref, buf_ref, sem_ref, keys_ref, *, topk, n_q, past):
    g = pl.program_id(1)
    slot = _page_pipeline(pt_ref, pool_ref, buf_ref, sem_ref)
    span = PAGES_PER_STEP * PAGE_SIZE

    def scores(kb, width):
        s = _dot_nt(iq_ref[0], kb.astype(BF16))
        acc = jnp.zeros((n_q, width), F32)
        for h in range(IDX_HEADS):
            acc = acc + jnp.maximum(s[h * n_q:(h + 1) * n_q], 0.0) * w_ref[0, h * n_q:(h + 1) * n_q, 0:1]
        return _sortable_key(acc)

    keys_ref[:, pl.ds(pl.multiple_of(g * span, span), span)] = scores(
        buf_ref[slot].reshape(span, buf_ref.shape[3]), span)

    @pl.when(g == pl.num_programs(1) - 1)
    def _():
        lp = keys_ref.shape[1]
        rown = lax.broadcasted_iota(I32, (n_q, TILE), 0)
        coln = lax.broadcasted_iota(I32, (n_q, TILE), 1)
        keys_ref[:, past:lp] = jnp.where(coln <= rown, scores(knew_ref[0], TILE), INT_MIN)
        keys = keys_ref[...]
        pos = lax.broadcasted_iota(I32, (n_q, lp), 1)

        def count(pred):
            return jnp.sum(jnp.where(pred, 1.0, 0.0), axis=1, keepdims=True)

        t = _kth_largest_key(lambda cand: count(keys >= cand), n_q, topk)
        thr = jnp.maximum(t, INT_MIN + 1)
        cnt_gt = count(keys >= thr + 1)
        need = topk - cnt_gt
        cnt_eq = count(keys >= thr) - cnt_gt
        tie = jnp.where(t > INT_MIN, jnp.where(cnt_eq > need, 1.0, 0.0), 0.0)
        eq_pos = jnp.where(keys == thr, pos, jnp.int32(2 ** 30))
        n_bits = int(lp).bit_length()

        def idx_body(i, mm):
            cand = mm + lax.shift_left(jnp.int32(1), n_bits - 1 - i)
            return jnp.where(count(eq_pos < cand) < need, cand, mm)

        last = lax.fori_loop(0, n_bits, idx_body, jnp.zeros((n_q, 1), I32))
        last = jnp.where(tie > 0.0, last, jnp.int32(2 ** 30))
        sel = jnp.where(keys > thr, 1.0, jnp.where(keys == thr, jnp.where(pos <= last, 1.0, 0.0), 0.0))
        o_ref[0, 0] = sel


def _sample_index_mask(page_table, iq_s, w_s, knew, pool_idx, topk):
    db, n_pages = page_table.shape
    n_q = iq_s.shape[1] // IDX_HEADS
    npg = n_pages // PAGES_PER_STEP
    lp = n_pages * PAGE_SIZE + TILE
    return pl.pallas_call(
        functools.partial(_sidx_kernel, topk=topk, n_q=n_q, past=n_pages * PAGE_SIZE),
        grid_spec=pltpu.PrefetchScalarGridSpec(
            num_scalar_prefetch=1,
            grid=(db, npg),
            in_specs=[pl.BlockSpec((1,) + iq_s.shape[1:], lambda b, g, pt: (b, 0, 0)),
                      pl.BlockSpec((1,) + w_s.shape[1:], lambda b, g, pt: (b, 0, 0)),
                      pl.BlockSpec((1,) + knew.shape[1:], lambda b, g, pt: (b, 0, 0)),
                      pl.BlockSpec(memory_space=pl.ANY)],
            out_specs=pl.BlockSpec((1, 1, n_q, lp), lambda b, g, pt: (b, 0, 0, 0)),
            scratch_shapes=[pltpu.VMEM((2, PAGES_PER_STEP) + pool_idx.shape[1:], F32),
                            pltpu.SemaphoreType.DMA((2,)),
                            pltpu.VMEM((n_q, lp), I32)]),
        out_shape=jax.ShapeDtypeStruct((db, 1, n_q, lp), F32),
        compiler_params=_params(2),
    )(page_table.reshape(-1), iq_s, w_s, knew, pool_idx)


def _pattn_kernel(pt_ref, q_ref, mask_ref, new_ref, bias_ref, pool_ref, o_ref, buf_ref, sem_ref,
                  m_ref, l_ref, acc_ref, *, nd, n_q, n_pages):
    g = pl.program_id(1)
    npg = pl.num_programs(1)
    slot = _page_pipeline(pt_ref, pool_ref, buf_ref, sem_ref)
    rows = q_ref.shape[1]
    half = rows // 2
    scale = HEAD_DIM ** -0.5
    mask_groups = mask_ref.shape[1]

    @pl.when(g == 0)
    def _():
        m0, l0, a0 = _flash_init(rows, 2 * HEAD_DIM)
        m_ref[...] = m0
        l_ref[...] = l0
        acc_ref[...] = a0

    def tile_update(kv, p, dlt, carry):
        kk = jnp.concatenate([kv[:, 0:HEAD_DIM], kv[:, 2 * HEAD_DIM:3 * HEAD_DIM]], axis=1).astype(BF16)
        vv = jnp.concatenate([kv[:, HEAD_DIM:2 * HEAD_DIM], kv[:, 3 * HEAD_DIM:4 * HEAD_DIM]], axis=1).astype(BF16)
        mk = mask_ref[0, :, :, pl.ds(pl.multiple_of(p * TILE, TILE), TILE)]
        madd = (mk - 1.0) * (-NEG)
        reps = rows // (mask_groups * n_q)
        madd = jnp.concatenate([madd[i] for i in range(mask_groups) for _ in range(reps)], axis=0)
        s = _dot_nt(q_ref[0], kk) * scale + (bias_ref[dlt] + madd)
        return _flash_step(s, *carry, vv)

    def page_body(k, carry):
        p = g * PAGES_PER_STEP + k
        return tile_update(buf_ref[slot, k], p, jnp.minimum(n_pages - p, nd), carry)

    carry = lax.fori_loop(0, PAGES_PER_STEP, page_body, (m_ref[...], l_ref[...], acc_ref[...]))
    m_ref[...], l_ref[...], acc_ref[...] = carry

    @pl.when(g == npg - 1)
    def _():
        o = _flash_out(*tile_update(new_ref[0], n_pages, 0, carry))
        o_ref[0, 0:half] = o[0:half, 0:HEAD_DIM]
        o_ref[0, half:rows] = o[half:rows, HEAD_DIM:2 * HEAD_DIM]


def _paged_attention(page_table, qblk, mask, new_kv, bias_s, pool, nd, n_q):
    db, n_pages = page_table.shape
    npg = n_pages // PAGES_PER_STEP
    rows = qblk.shape[1]
    return pl.pallas_call(
        functools.partial(_pattn_kernel, nd=nd, n_q=n_q, n_pages=n_pages),
        grid_spec=pltpu.PrefetchScalarGridSpec(
            num_scalar_prefetch=1,
            grid=(db, npg),
            in_specs=[pl.BlockSpec((1,) + qblk.shape[1:], lambda b, g, pt: (b, 0, 0)),
                      pl.BlockSpec((1,) + mask.shape[1:], lambda b, g, pt: (b, 0, 0, 0)),
                      pl.BlockSpec((1,) + new_kv.shape[1:], lambda b, g, pt: (b, 0, 0)),
                      _resident(bias_s.shape, lambda b, g, pt: (0, 0, 0)),
                      pl.BlockSpec(memory_space=pl.ANY)],
            out_specs=pl.BlockSpec((1, rows, HEAD_DIM), lambda b, g, pt: (b, 0, 0)),
            scratch_shapes=[pltpu.VMEM((2, PAGES_PER_STEP) + pool.shape[1:], F32),
                            pltpu.SemaphoreType.DMA((2,)),
                            pltpu.VMEM((rows, 1), F32), pltpu.VMEM((rows, 1), F32),
                            pltpu.VMEM((rows, 2 * HEAD_DIM), F32)]),
        out_shape=jax.ShapeDtypeStruct((db, rows, HEAD_DIM), F32),
        compiler_params=_params(2),
    )(page_table.reshape(-1), qblk, mask, new_kv, bias_s, pool)


def _scmp_kernel(q_ref, ck_ref, cv_ref, ov_ref, ocmp_ref, mask_ref, *, past, n_q, n_sel):
    rows = q_ref.shape[2]
    ncp = ck_ref.shape[1]
    nsp = ov_ref.shape[1]
    scale = HEAD_DIM ** -0.5
    qi = lax.rem(lax.broadcasted_iota(I32, (rows, ncp), 0), n_q)
    cmp_end = lax.broadcasted_iota(I32, (rows, ncp), 1) * CMP_STRIDE + (CMP_BLOCK - 1)
    cmp_ok = cmp_end <= past + qi
    blk = lax.broadcasted_iota(I32, (n_q, nsp), 1)
    cur = lax.shift_right_logical(past + lax.broadcasted_iota(I32, (n_q, nsp), 0), int(math.log2(SEL_BLOCK)))
    q_pos = past + lax.broadcasted_iota(I32, (n_q, TILE), 0)
    coln = lax.broadcasted_iota(I32, (n_q, TILE), 1)
    for g in range(B_KV_HEADS):
        p = _softmax_rows(_dot_nt(q_ref[0, g], ck_ref[0, :, g * HEAD_DIM:(g + 1) * HEAD_DIM]) * scale, cmp_ok)
        ocmp_ref[0, g] = jnp.dot(p.astype(BF16), cv_ref[0, :, g * HEAD_DIM:(g + 1) * HEAD_DIM],
                                 preferred_element_type=F32)
        psum = p[0:n_q]
        for r in range(1, rows // n_q):
            psum = psum + p[r * n_q:(r + 1) * n_q]
        imp = jnp.dot(psum, ov_ref[...], preferred_element_type=F32, precision=lax.Precision.HIGHEST)
        forced = jnp.where(blk == 0, jnp.inf, jnp.where(blk >= cur - 1, jnp.inf, imp))
        selm = _topn_mask(jnp.where(blk <= cur, forced, -jnp.inf), n_sel).astype(BF16)

        def expand_body(kt, c, g=g, selm=selm):
            e = jnp.dot(selm, _block_expand(2 * kt, nsp), preferred_element_type=F32)
            k0 = pl.multiple_of(kt * TILE, TILE)
            mask_ref[0, g, :, pl.ds(k0, TILE)] = jnp.where(kt * TILE + coln <= q_pos, e, 0.0)
            return c

        lax.fori_loop(0, mask_ref.shape[3] // TILE, expand_body, 0)


def _sample_cmp_select(bq_s, cmp_k, cmp_v, length, past, n_q):
    db = bq_s.shape[0]
    ncp = cmp_k.shape[1]
    ns = -(-length // SEL_BLOCK)
    nsp = -(-ns // LANES) * LANES
    ov = _overlap_matrix(length, ncp, nsp)
    lp = past + TILE
    return pl.pallas_call(
        functools.partial(_scmp_kernel, past=past, n_q=n_q, n_sel=min(SEL_TOPN, ns)),
        grid=(db,),
        in_specs=[pl.BlockSpec((1,) + bq_s.shape[1:], lambda b: (b, 0, 0, 0)),
                  pl.BlockSpec((1, ncp, B_KV_HEADS * HEAD_DIM), lambda b: (b, 0, 0)),
                  pl.BlockSpec((1, ncp, B_KV_HEADS * HEAD_DIM), lambda b: (b, 0, 0)),
                  _resident(ov.shape, lambda b: (0, 0))],
        out_specs=[pl.BlockSpec((1,) + bq_s.shape[1:], lambda b: (b, 0, 0, 0)),
                   pl.BlockSpec((1, B_KV_HEADS, n_q, lp), lambda b: (b, 0, 0, 0))],
        out_shape=[jax.ShapeDtypeStruct(bq_s.shape, F32),
                   jax.ShapeDtypeStruct((db, B_KV_HEADS, n_q, lp), F32)],
        compiler_params=_params(1),
    )(bq_s, cmp_k, cmp_v, ov)


def _swin_kernel(q_ref, win_ref, new_ref, bias_ref, gate_ref, ocmp_ref, osel_ref, o_ref, *, n_q):
    rows = q_ref.shape[1]
    half = rows // 2
    wb = win_ref.shape[1]
    scale = HEAD_DIM ** -0.5
    qi = lax.rem(lax.broadcasted_iota(I32, (rows, TILE), 0), n_q)
    col = lax.broadcasted_iota(I32, (rows, TILE), 1)
    carry = _flash_init(rows, 2 * HEAD_DIM)

    def tile_update(kv, dlt, ok, carry):
        kk = jnp.concatenate([kv[:, 0:HEAD_DIM], kv[:, 2 * HEAD_DIM:3 * HEAD_DIM]], axis=1).astype(BF16)
        vv = jnp.concatenate([kv[:, HEAD_DIM:2 * HEAD_DIM], kv[:, 3 * HEAD_DIM:4 * HEAD_DIM]], axis=1).astype(BF16)
        s = _dot_nt(q_ref[0], kk) * scale + (bias_ref[dlt] + jnp.where(ok, 0.0, NEG))
        return _flash_step(s, *carry, vv)

    for kt in range(wb // TILE):
        ok = col + kt * TILE >= qi + (wb - WINDOW)
        carry = tile_update(win_ref[0, kt * TILE:(kt + 1) * TILE, :], wb // TILE - kt, ok, carry)
    carry = tile_update(new_ref[0], 0, col <= qi, carry)
    o = _flash_out(*carry)
    gate = jax.nn.sigmoid(gate_ref[0])
    for h in range(B_HEADS):
        sl = slice(h * n_q, (h + 1) * n_q)
        ow = o[sl, 0:HEAD_DIM] if h < B_GROUP else o[sl, HEAD_DIM:2 * HEAD_DIM]
        o_ref[0, :, h * HEAD_DIM:(h + 1) * HEAD_DIM] = (
            gate[:, 3 * h:3 * h + 1] * ocmp_ref[0, sl] + gate[:, 3 * h + 1:3 * h + 2] * osel_ref[0, sl]
            + gate[:, 3 * h + 2:3 * h + 3] * ow).astype(o_ref.dtype)


def _sample_window_combine(qblk, win_state, new_win, bias_s, gates, o_cmp, o_sel, n_q):
    db, rows, _ = qblk.shape
    return pl.pallas_call(
        functools.partial(_swin_kernel, n_q=n_q),
        grid=(db,),
        in_specs=[pl.BlockSpec((1,) + qblk.shape[1:], lambda b: (b, 0, 0)),
                  pl.BlockSpec((1,) + win_state.shape[1:], lambda b: (b, 0, 0)),
                  pl.BlockSpec((1,) + new_win.shape[1:], lambda b: (b, 0, 0)),
                  _resident(bias_s.shape, lambda b: (0, 0, 0)),
                  pl.BlockSpec((1,) + gates.shape[1:], lambda b: (b, 0, 0)),
                  pl.BlockSpec((1, rows, HEAD_DIM), lambda b: (b, 0, 0)),
                  pl.BlockSpec((1, rows, HEAD_DIM), lambda b: (b, 0, 0))],
        out_specs=pl.BlockSpec((1, n_q, B_HEADS * HEAD_DIM), lambda b: (b, 0, 0)),
        out_shape=jax.ShapeDtypeStruct((db, n_q, B_HEADS * HEAD_DIM), BF16),
        compiler_params=_params(1),
    )(qblk, win_state, new_win, bias_s, gates, o_cmp, o_sel)


def _pack_weights(w_in):
    d = w_in.shape[0]
    sizes = _split_sizes(d)
    off, o = {}, 0
    for name in _GROUPS:
        off[name] = o
        o += sizes[name]
    take = lambda name: w_in[:, off[name]:off[name] + sizes[name]]
    zeros = lambda n: jnp.zeros((d, n), w_in.dtype)
    iq = take('i_q').reshape(d, IDX_HEADS, IDX_DIM)
    iq = jnp.concatenate([iq, jnp.zeros_like(iq)], axis=2).reshape(d, IDX_HEADS * LANES)
    w_f = jnp.concatenate([take('a_kv'), take('b_cmp'), take('b_sel'), take('b_win'),
                           take('i_k'), take('i_w'), zeros(LANES - IDX_DIM - IDX_HEADS),
                           take('b_gate'), zeros(LANES - B_HEADS * 3)], axis=1).astype(BF16)
    w_q = jnp.concatenate([take('a_q'), take('b_q'), take('m_q'), iq], axis=1).astype(BF16)
    w_g = take('g_merge').astype(BF16)
    kvw = 4 * HEAD_DIM
    cols = dict(f=dict(a_kv=0, b_cmp=kvw, b_sel=2 * kvw, b_win=3 * kvw, i_kw=4 * kvw, b_gate=4 * kvw + LANES),
                q=dict(a_q=0, b_q=1024, m_q=2048, i_q=3072))
    return w_f, w_q, w_g, cols


def kernel(x_prompt, x_sample, mem_prompt, cache_a_kv, cache_a_idx, cache_b_cmp, cache_b_sel, state_b_win,
           cache_mem, page_table, rel_table, w_in, w_mem_kv, cmp_pe_k, cmp_w1_k, cmp_w2_k, cmp_pe_v, cmp_w1_v,
           cmp_w2_v, w_pa, w_pb, w_pm, w_o, ln1_g, ln1_b, w_up, b_up, w_down, b_down, ln2_g, ln2_b):
    depth = w_in.shape[0]
    assert depth == 1
    bsz, t, d = x_prompt.shape
    db, ds, _ = x_sample.shape
    n_mem = mem_prompt.shape[1]
    n_pool = cache_a_kv.shape[1]
    n_pages = page_table.shape[1]
    past = n_pages * PAGE_SIZE
    wb = state_b_win.shape[2]
    alpha = (2 * depth) ** 0.25
    kvw = 4 * HEAD_DIM
    assert t % TILE == 0 and ds == 8 and wb % TILE == 0 and n_pages % PAGES_PER_STEP == 0

    w_f, w_q, w_g, cols = _pack_weights(w_in[0])
    fc, qc = cols['f'], cols['q']
    w1cat = jnp.stack([jnp.concatenate([w[0][:CMP_STRIDE * HEAD_DIM], w[0][CMP_STRIDE * HEAD_DIM:]], axis=1)
                       for w in (cmp_w1_k, cmp_w1_v)]).astype(BF16)
    w1 = jnp.stack([cmp_w1_k[0], cmp_w1_v[0]]).astype(BF16)
    w2 = jnp.stack([cmp_w2_k[0], cmp_w2_v[0]]).astype(BF16)
    pe8 = jnp.broadcast_to(jnp.stack([cmp_pe_k[0].reshape(1, -1), cmp_pe_v[0].reshape(1, -1)]),
                           (2, 8, CMP_BLOCK * HEAD_DIM)).astype(BF16)
    wpa, wpb, wpm, wo = (w[0].astype(BF16) for w in (w_pa, w_pb, w_pm, w_o))
    wu, wd = w_up[0].astype(BF16), w_down[0].astype(BF16)
    nd = _num_near_tiles()
    bias = _bias_tiles(rel_table, nd)

    def dense_tail(x2d, gmat, oa, ob, om):
        x1 = _merge(x2d, gmat, oa, ob, om, wpa, wpb, wpm, wo, ln1_g, ln1_b, alpha)
        return _ffn(x1, wu, b_up, wd, b_down, ln2_g, ln2_b, alpha)

    xp = x_prompt.reshape(bsz * t, d)
    fp = _matmul(xp, w_f, F32, tn=w_f.shape[1] // 2)
    qp = _matmul(xp, w_q, BF16, tn=1024)
    gp = _matmul(xp, w_g, F32, tn=1024)
    p_a_kv = fp[:, fc['a_kv']:fc['a_kv'] + kvw]
    p_b_cmp = fp[:, fc['b_cmp']:fc['b_cmp'] + kvw]
    p_b_sel = fp[:, fc['b_sel']:fc['b_sel'] + kvw]
    p_b_win = fp[:, fc['b_win']:fc['b_win'] + kvw]
    p_a_idx = fp[:, fc['i_kw']:fc['i_kw'] + IDX_DIM]

    zp = _cmpz_dense(p_b_cmp.reshape(bsz * t // CMP_STRIDE, CMP_STRIDE * kvw), w1cat)
    cmp_k, cmp_v = _cmp_finish(zp.reshape(bsz, t // CMP_STRIDE, -1), pe8, w1, w2)
    o_a = _prompt_mixer_a(qp, fp, bias[0], bsz, t, cols, nd)
    o_b = _prompt_mixer_b(qp, fp, cmp_k, cmp_v, bias[1], bsz, t, cols, nd)
    mem_kv = _matmul(mem_prompt.reshape(bsz * n_mem, d), w_mem_kv[0].astype(BF16), F32, tn=1024)
    o_m = _mem_attend(qp, qc['m_q'], mem_kv, bsz, t, n_mem, tq=512)
    y_prompt = dense_tail(xp, gp, o_a, o_b, o_m).reshape(bsz, t, d)

    xs = x_sample.reshape(db * ds, d)
    fs = _matmul(xs, w_f, F32, tn=w_f.shape[1] // 2)
    qs = _matmul(xs, w_q, BF16, tn=1024)
    gs = _matmul(xs, w_g, F32, tn=1024)
    s_a_kv = fs[:, fc['a_kv']:fc['a_kv'] + kvw]
    s_b_cmp = fs[:, fc['b_cmp']:fc['b_cmp'] + kvw]
    s_b_sel = fs[:, fc['b_sel']:fc['b_sel'] + kvw]
    s_b_win = fs[:, fc['b_win']:fc['b_win'] + kvw]
    s_a_idx = fs[:, fc['i_kw']:fc['i_kw'] + IDX_DIM]
    length = past + ds

    def pad_new(rows2d):
        r = rows2d.reshape(db, ds, -1)
        return jnp.concatenate([r, jnp.zeros((db, TILE - ds, r.shape[2]), r.dtype)], axis=1)

    def head_major(q2d, heads):
        return q2d.reshape(db, ds, heads, -1).transpose(0, 2, 1, 3).reshape(db, heads * ds, -1)

    def block_q(q2d):
        qh = head_major(q2d, A_HEADS).reshape(db, A_KV_HEADS, A_GROUP * ds, HEAD_DIM)
        z = jnp.zeros_like(qh[:, 0])
        return jnp.concatenate([jnp.concatenate([qh[:, 0], z], axis=2),
                                jnp.concatenate([z, qh[:, 1]], axis=2)], axis=1)

    def sample_bias(tiles):
        n = tiles.shape[0]
        return tiles.reshape(n, A_KV_HEADS, A_GROUP, TILE, TILE)[:, :, :, :ds].reshape(n, A_HEADS * ds, TILE)

    iq_s = head_major(qs[:, qc['i_q']:qc['i_q'] + IDX_HEADS * LANES], IDX_HEADS)[:, :, :IDX_DIM]
    w_s = fs[:, fc['i_kw'] + IDX_DIM:fc['i_kw'] + IDX_DIM + IDX_HEADS] * (IDX_HEADS ** -0.5 * IDX_DIM ** -0.5)
    w_s = jnp.broadcast_to(head_major(w_s, IDX_HEADS), (db, IDX_HEADS * ds, LANES))
    mask_a = _sample_index_mask(page_table, iq_s, w_s, pad_new(s_a_idx), cache_a_idx[0],
                                min(TOPK_MAX, length // 4))
    qa_blk = block_q(qs[:, qc['a_q']:qc['a_q'] + A_HEADS * HEAD_DIM])
    o_a_s = _paged_attention(page_table, qa_blk, mask_a, pad_new(s_a_kv), sample_bias(bias[0]),
                             cache_a_kv[0].reshape(n_pool, PAGE_SIZE, kvw), nd, ds)

    zs = _cmpz_paged(cache_b_cmp[0], page_table, w1cat)
    cmp_k_s, cmp_v_s = _cmp_finish(zs.reshape(db, past // CMP_STRIDE, -1), pe8, w1, w2)
    bq2d = qs[:, qc['b_q']:qc['b_q'] + B_HEADS * HEAD_DIM]
    bq_s = head_major(bq2d, B_HEADS).reshape(db, B_KV_HEADS, B_GROUP * ds, HEAD_DIM)
    o_cmp_s, mask_b = _sample_cmp_select(bq_s, cmp_k_s, cmp_v_s, length, past, ds)
    qb_blk = block_q(bq2d)
    bias_sb = sample_bias(bias[1])
    o_sel_s = _paged_attention(page_table, qb_blk, mask_b, pad_new(s_b_sel), bias_sb,
                               cache_b_sel[0].reshape(n_pool, PAGE_SIZE, kvw), nd, ds)
    gates_s = fs[:, fc['b_gate']:fc['b_gate'] + LANES].reshape(db, ds, LANES)
    o_b_s = _sample_window_combine(qb_blk, state_b_win[0].reshape(db, wb, kvw), pad_new(s_b_win), bias_sb,
                                   gates_s, o_cmp_s.reshape(db, B_HEADS * ds, HEAD_DIM), o_sel_s, ds)

    o_a_s = o_a_s.reshape(db, A_HEADS, ds, HEAD_DIM).transpose(0, 2, 1, 3).reshape(db * ds, -1).astype(BF16)
    o_m_s = _mem_attend(qs, qc['m_q'], cache_mem[0].reshape(db * n_mem, -1), db, ds, n_mem, tq=ds)
    y_sample = dense_tail(xs, gs, o_a_s, o_b_s.reshape(db * ds, -1), o_m_s).reshape(db, ds, d)

    kv6 = lambda a, n, rows: a.reshape(1, n, rows, 2, 2, HEAD_DIM)
    wp = min(WINDOW, t)
    new_win = jnp.concatenate([state_b_win[0], s_b_win.reshape(db, ds, 2, 2, HEAD_DIM)], axis=1)[:, ds:]
    return (y_prompt, y_sample,
            kv6(p_a_kv, bsz, t), p_a_idx.reshape(1, bsz, t, IDX_DIM), kv6(p_b_cmp, bsz, t), kv6(p_b_sel, bsz, t),
            kv6(p_b_win, bsz, t)[:, :, t - wp:],
            mem_kv.reshape(1, bsz, n_mem, MEM_HEADS, 2, MEM_HEAD_DIM),
            kv6(s_a_kv, db, ds), s_a_idx.reshape(1, db, ds, IDX_DIM), kv6(s_b_cmp, db, ds), kv6(s_b_sel, db, ds),
            new_win[None])
```

```python
import functools
import math

import numpy as np
import jax
import jax.numpy as jnp
from jax import lax
from jax.experimental import pallas as pl
from jax.experimental.pallas import tpu as pltpu

F32 = jnp.float32
BF16 = jnp.bfloat16
I32 = jnp.int32

HEAD_DIM = 128
A_HEADS = 8
A_KV_HEADS = 2
A_GROUP = A_HEADS // A_KV_HEADS
IDX_HEADS = 8
IDX_DIM = 64
TOPK_MAX = 256
B_HEADS = 8
B_KV_HEADS = 2
B_GROUP = B_HEADS // B_KV_HEADS
CMP_BLOCK = 32
CMP_STRIDE = 16
CMP_RATIO = CMP_BLOCK // CMP_STRIDE
CMP_HID = 128
SEL_BLOCK = 64
SEL_TOPN = 16
WINDOW = 512
MEM_HEADS = 4
MEM_HEAD_DIM = 256
N_BUCKETS = 32
MAX_DISTANCE = 1024
LN_EPS = 1e-5
PAGE_SIZE = 128

LANES = 128
VMEM_LIMIT = 56 * 1024 * 1024

TILE = 128
PAIR = 2 * TILE
NEG = -1e30
INT_MIN = -2 ** 31
PAGES_PER_STEP = 16
PAGES_PER_BLOCK = 8

_GROUPS = ('a_q', 'a_kv', 'i_q', 'i_k', 'i_w', 'b_q', 'b_cmp', 'b_sel', 'b_win', 'b_gate', 'm_q', 'g_merge')


def _split_sizes(d_model):
    return dict(
        a_q=A_HEADS * HEAD_DIM, a_kv=A_KV_HEADS * 2 * HEAD_DIM, i_q=IDX_HEADS * IDX_DIM, i_k=IDX_DIM,
        i_w=IDX_HEADS, b_q=B_HEADS * HEAD_DIM, b_cmp=B_KV_HEADS * 2 * HEAD_DIM, b_sel=B_KV_HEADS * 2 * HEAD_DIM,
        b_win=B_KV_HEADS * 2 * HEAD_DIM, b_gate=B_HEADS * 3, m_q=MEM_HEADS * MEM_HEAD_DIM, g_merge=3 * d_model)


def _params(n_grid, vmem=VMEM_LIMIT):
    return pltpu.CompilerParams(dimension_semantics=('arbitrary',) * n_grid, vmem_limit_bytes=vmem)


def _resident(block, index_map):
    return pl.BlockSpec(block, index_map, pipeline_mode=pl.Buffered(1))


def _mm_kernel(x_ref, w_ref, o_ref):
    o_ref[...] = jnp.dot(x_ref[...].astype(BF16), w_ref[...], preferred_element_type=F32).astype(o_ref.dtype)


def _matmul(x, w, out_dtype, tn, tm=1024, name='matmul'):
    m, k = x.shape
    n = w.shape[1]
    tm = min(tm, m)
    assert m % tm == 0 and n % tn == 0
    return pl.pallas_call(
        _mm_kernel,
        grid=(m // tm, n // tn),
        in_specs=[pl.BlockSpec((tm, k), lambda i, j: (i, 0)), pl.BlockSpec((k, tn), lambda i, j: (0, j))],
        out_specs=pl.BlockSpec((tm, tn), lambda i, j: (i, j)),
        out_shape=jax.ShapeDtypeStruct((m, n), out_dtype),
        compiler_params=_params(2), name=name,
    )(x, w)


def _rel_bucket(dist):
    d = jnp.maximum(dist, 0)
    exact = N_BUCKETS // 2
    df = jnp.maximum(d, 1).astype(F32)
    large = exact + (jnp.log(df / exact) / math.log(MAX_DISTANCE / exact) * (N_BUCKETS - exact)).astype(I32)
    return jnp.where(d < exact, d, jnp.minimum(large, N_BUCKETS - 1))


def _num_near_tiles():
    exact = N_BUCKETS // 2
    d = np.arange(1, 4 * MAX_DISTANCE, dtype=np.float64)
    large = exact + np.floor(np.log(d / exact) / math.log(MAX_DISTANCE / exact) * (N_BUCKETS - exact))
    bucket = np.where(d < exact, d, np.minimum(large, N_BUCKETS - 1))
    d_const = int(d[np.argmax(bucket == N_BUCKETS - 1)])
    return -(-(d_const + TILE // 2 + TILE - 1) // TILE)


def _bias_kernel(u_ref, o_ref, *, n_tiles, d_top):
    for dt in range(n_tiles):
        start = d_top - dt * TILE - (TILE - 1)
        row = u_ref[0, :, start:start + 2 * TILE]
        x = jnp.broadcast_to(row, (TILE, 2 * TILE))
        x = pltpu.roll(x, TILE + 1, 1, stride=1, stride_axis=0)
        o_ref[dt] = x[:, :TILE]


def _bias_tiles(rel_table, nd):
    n_tiles = nd + 1
    d_top = n_tiles * TILE
    ul = d_top + 2 * TILE
    n_heads = rel_table.shape[1]
    dist = d_top - jnp.arange(ul)
    u = rel_table[_rel_bucket(dist)].T.reshape(n_heads, 1, ul)
    out = pl.pallas_call(
        functools.partial(_bias_kernel, n_tiles=n_tiles, d_top=d_top),
        grid=(n_heads,),
        in_specs=[pl.BlockSpec((1, 1, ul), lambda h: (h, 0, 0))],
        out_specs=pl.BlockSpec((None, n_tiles, None, None, TILE, TILE),
                               lambda h: (h // 8, 0, (h % 8) // 4, h % 4, 0, 0)),
        out_shape=jax.ShapeDtypeStruct((2, n_tiles, 2, 4, TILE, TILE), F32),
        compiler_params=_params(1), name='bias_tiles',
    )(u)
    return out.reshape(2, n_tiles, 2, 4 * TILE, TILE)


def _dot_nt(a, b):
    return lax.dot_general(a, b, (((1,), (1,)), ((), ())), preferred_element_type=F32)


def _flash_step(s, m, l, acc, v):
    m_new = jnp.maximum(m, jnp.max(s, axis=1, keepdims=True))
    alpha = jnp.exp(m - m_new)
    p = jnp.exp(s - m_new)
    l = alpha * l + jnp.sum(p, axis=1, keepdims=True)
    acc = alpha * acc + jnp.dot(p.astype(BF16), v, preferred_element_type=F32)
    return m_new, l, acc


def _flash_init(rows, width):
    return (jnp.full((rows, 1), NEG, F32), jnp.zeros((rows, 1), F32), jnp.zeros((rows, width), F32))


def _flash_out(m, l, acc):
    return jnp.where(m > 0.5 * NEG, acc / jnp.maximum(l, 1e-30), 0.0)


def _sortable_key(x):
    bits = pltpu.bitcast(x, I32)
    bits = jnp.where(bits == INT_MIN, 0, bits)
    return jnp.where(bits < 0, bits ^ 0x7FFFFFFF, bits)


def _kth_largest_key(count_ge, k, shape):
    def bit_body(i, t):
        cand = t + lax.shift_left(jnp.int32(1), 31 - i)
        return jnp.where(count_ge(cand) >= k, cand, t)
    return lax.fori_loop(0, 32, bit_body, jnp.full(shape, INT_MIN, I32))


def _attn_update(q_g, kk, vv, bias_of, madd, m_ref, l_ref, acc_ref, h0, n_heads):
    tk = kk.shape[0]
    s_all = _dot_nt(q_g, kk) * (HEAD_DIM ** -0.5)
    ps, alphas = [], []
    for r in range(n_heads):
        s = s_all[r * TILE:(r + 1) * TILE] + (bias_of(r) + madd)
        m_prev = m_ref[h0 + r]
        m_next = jnp.maximum(m_prev, jnp.max(s, axis=1, keepdims=True))
        alpha = jnp.exp(m_prev - m_next)
        p = jnp.exp(s - jnp.concatenate([m_next] * (tk // LANES), axis=1))
        l_ref[h0 + r] = alpha * l_ref[h0 + r] + jnp.sum(p, axis=1, keepdims=True)
        m_ref[h0 + r] = m_next
        ps.append(p.astype(BF16))
        alphas.append(alpha)
    pv = jnp.dot(jnp.concatenate(ps, axis=0), vv, preferred_element_type=F32)
    for r in range(n_heads):
        acc_ref[h0 + r] = alphas[r] * acc_ref[h0 + r] + pv[r * TILE:(r + 1) * TILE]


def _attn_reset(m_ref, l_ref, acc_ref):
    m_ref[...] = jnp.full(m_ref.shape, NEG, F32)
    l_ref[...] = jnp.zeros(l_ref.shape, F32)
    acc_ref[...] = jnp.zeros(acc_ref.shape, F32)


def _attn_out(m_ref, l_ref, acc_ref, h):
    return jnp.where(m_ref[h] > 0.5 * NEG, acc_ref[h] / jnp.maximum(l_ref[h], 1e-30), 0.0)


def _topn_mask(scores, n):
    colf = lax.broadcasted_iota(I32, scores[0].shape, 1).astype(F32)

    def body(_, carry):
        out = []
        for sc, selm in carry:
            mx = jnp.max(sc, axis=1, keepdims=True)
            first = jnp.min(jnp.where(sc == mx, colf, 1e9), axis=1, keepdims=True)
            hit = colf == first
            selm = jnp.maximum(selm, jnp.where(hit, jnp.where(mx > -jnp.inf, 1.0, 0.0), 0.0))
            out.append((jnp.where(hit, -jnp.inf, sc), selm))
        return tuple(out)

    init = tuple((sc, jnp.zeros(sc.shape, F32)) for sc in scores)
    return [c[1] for c in lax.fori_loop(0, n, body, init)]


def _softmax_rows(s, ok):
    s = jnp.where(ok, s, NEG)
    m = jnp.max(s, axis=1, keepdims=True)
    e = jnp.where(ok, jnp.exp(s - m), 0.0)
    return e / jnp.maximum(jnp.sum(e, axis=1, keepdims=True), 1e-30)


def _block_expand(blk0, n_blk, width=TILE):
    rb = lax.broadcasted_iota(I32, (n_blk, width), 0)
    cj = lax.broadcasted_iota(I32, (n_blk, width), 1)
    target = blk0 + lax.shift_right_logical(cj, int(math.log2(SEL_BLOCK)))
    return jnp.where(rb == target, 1.0, 0.0).astype(BF16)


def _gelu(x):
    return 0.5 * x * (1.0 + jnp.tanh(math.sqrt(2.0 / math.pi) * (x + 0.044715 * (x * x * x))))


def _layer_norm(x, g, b):
    xc = x - jnp.mean(x, axis=1, keepdims=True)
    var = jnp.mean(xc * xc, axis=1, keepdims=True)
    return xc * lax.rsqrt(var + LN_EPS) * g + b


def _cmpz_compute(x_ref, w_ref, o_ref):
    rows = x_ref.shape[0]
    n_col = B_KV_HEADS * 2 * HEAD_DIM
    for kv in range(2):
        xs = []
        for g in range(B_KV_HEADS):
            c0 = (g * 2 + kv) * HEAD_DIM
            xs.append(jnp.concatenate(
                [x_ref[:, p * n_col + c0:p * n_col + c0 + HEAD_DIM] for p in range(CMP_STRIDE)], axis=1))
        z = jnp.dot(jnp.concatenate(xs, axis=0).astype(BF16), w_ref[kv], preferred_element_type=F32)
        for g in range(B_KV_HEADS):
            c = (g * 2 + kv) * CMP_RATIO * CMP_HID
            o_ref[:, c:c + CMP_RATIO * CMP_HID] = z[g * rows:(g + 1) * rows]


def _cmpz_kernel(x_ref, w_ref, o_ref):
    _cmpz_compute(x_ref, w_ref, o_ref)


def _cmpz_dense(x2d, w1cat, tc=256):
    n = x2d.shape[0]
    tc = min(tc, n)
    assert n % tc == 0
    return pl.pallas_call(
        _cmpz_kernel,
        grid=(n // tc,),
        in_specs=[pl.BlockSpec((tc, x2d.shape[1]), lambda i: (i, 0)),
                  _resident(w1cat.shape, lambda i: (0, 0, 0))],
        out_specs=pl.BlockSpec((tc, 4 * CMP_RATIO * CMP_HID), lambda i: (i, 0)),
        out_shape=jax.ShapeDtypeStruct((n, 4 * CMP_RATIO * CMP_HID), F32),
        compiler_params=_params(1), name='cmpz_dense',
    )(x2d, w1cat)


def _page_fetch(pt_ref, pool_ref, buf_ref, sem_ref, step, slot):
    for k in range(PAGES_PER_STEP):
        pid = pt_ref[step * PAGES_PER_STEP + k]
        pltpu.make_async_copy(pool_ref.at[pid], buf_ref.at[slot, k], sem_ref.at[slot]).start()


def _page_wait(pool_ref, buf_ref, sem_ref, slot):
    for k in range(PAGES_PER_STEP):
        pltpu.make_async_copy(pool_ref.at[0], buf_ref.at[slot, k], sem_ref.at[slot]).wait()


def _page_pipeline(pt_ref, pool_ref, buf_ref, sem_ref):
    step = pl.program_id(0) * pl.num_programs(1) + pl.program_id(1)
    total = pl.num_programs(0) * pl.num_programs(1)
    slot = lax.rem(step, 2)

    @pl.when(step == 0)
    def _():
        _page_fetch(pt_ref, pool_ref, buf_ref, sem_ref, step, slot)

    @pl.when(step + 1 < total)
    def _():
        _page_fetch(pt_ref, pool_ref, buf_ref, sem_ref, step + 1, 1 - slot)

    _page_wait(pool_ref, buf_ref, sem_ref, slot)
    return slot


def _cmpz_paged_kernel(pt_ref, pool_ref, w_ref, o_ref, buf_ref, sem_ref, x_ref):
    slot = _page_pipeline(pt_ref, pool_ref, buf_ref, sem_ref)
    chunks = PAGE_SIZE // CMP_STRIDE
    for k in range(PAGES_PER_STEP):
        x_ref[k * chunks:(k + 1) * chunks, :] = buf_ref[slot, k]
    _cmpz_compute(x_ref, w_ref, o_ref)


def _cmpz_paged(pool, page_table, w1cat):
    n_pool = pool.shape[0]
    db, n_pages = page_table.shape
    chunks = PAGE_SIZE // CMP_STRIDE
    width = CMP_STRIDE * B_KV_HEADS * 2 * HEAD_DIM
    pool3 = pool.reshape(n_pool, chunks, width)
    npg = n_pages // PAGES_PER_STEP
    rows = PAGES_PER_STEP * chunks
    return pl.pallas_call(
        _cmpz_paged_kernel,
        grid_spec=pltpu.PrefetchScalarGridSpec(
            num_scalar_prefetch=1,
            grid=(db, npg),
            in_specs=[pl.BlockSpec(memory_space=pl.ANY),
                      _resident(w1cat.shape, lambda b, g, pt: (0, 0, 0))],
            out_specs=pl.BlockSpec((rows, 4 * CMP_RATIO * CMP_HID), lambda b, g, pt: (b * npg + g, 0)),
            scratch_shapes=[pltpu.VMEM((2, PAGES_PER_STEP, chunks, width), F32),
                            pltpu.SemaphoreType.DMA((2,)),
                            pltpu.VMEM((rows, width), F32)]),
        out_shape=jax.ShapeDtypeStruct((db * n_pages * chunks, 4 * CMP_RATIO * CMP_HID), F32),
        compiler_params=_params(2), name='cmpz_paged',
    )(page_table.reshape(-1), pool3, w1cat)


def _cmp_finish_kernel(z_ref, pe_ref, w1_ref, w2_ref, k_ref, v_ref):
    n = z_ref.shape[1]
    for kv, o_ref in ((0, k_ref), (1, v_ref)):
        pew = jnp.dot(pe_ref[kv], w1_ref[kv], preferred_element_type=F32)[0:1]
        for g in range(B_KV_HEADS):
            c = (g * 2 + kv) * CMP_RATIO * CMP_HID
            z0 = z_ref[0, :, c:c + CMP_HID]
            z1 = z_ref[0, :, c + CMP_HID:c + 2 * CMP_HID]
            pre = z0 + pltpu.roll(z1, n - 1, 0) + pew
            out = jnp.dot(_gelu(pre).astype(BF16), w2_ref[kv], preferred_element_type=F32)
            o_ref[0, :, g * HEAD_DIM:(g + 1) * HEAD_DIM] = out.astype(o_ref.dtype)


def _cmp_finish(z3, pe8, w1, w2):
    nb, n, zc = z3.shape
    out = jax.ShapeDtypeStruct((nb, n, B_KV_HEADS * HEAD_DIM), BF16)
    return pl.pallas_call(
        _cmp_finish_kernel,
        grid=(nb,),
        in_specs=[pl.BlockSpec((1, n, zc), lambda b: (b, 0, 0)),
                  _resident(pe8.shape, lambda b: (0, 0, 0)),
                  _resident(w1.shape, lambda b: (0, 0, 0)),
                  _resident(w2.shape, lambda b: (0, 0, 0))],
        out_specs=[pl.BlockSpec((1, n, B_KV_HEADS * HEAD_DIM), lambda b: (b, 0, 0))] * 2,
        out_shape=[out, out],
        compiler_params=_params(1), name='cmp_finish',
    )(z3, pe8, w1, w2)


def _ka_kernel(iq_ref, ikw_ref, ikwq_ref, aq_ref, akv_ref, bias_ref, o_ref, keys_ref, m_ref, l_ref, acc_ref,
               *, topk, nd):
    qt = pl.program_id(1)
    last_pair = lax.shift_right_logical(qt, 1)
    key_row = lax.broadcasted_iota(I32, (PAIR, TILE), 0)
    q_pos = qt * TILE + lax.broadcasted_iota(I32, (PAIR, TILE), 1)
    w_t = ikwq_ref[...].T[IDX_DIM:IDX_DIM + IDX_HEADS] * (IDX_HEADS ** -0.5 * IDX_DIM ** -0.5)
    iq_all = jnp.concatenate([iq_ref[:, h * LANES:(h + 1) * LANES] for h in range(IDX_HEADS)], axis=0)

    def score_pair(kp, masked):
        kb = ikw_ref[pl.ds(pl.multiple_of(kp * PAIR, PAIR), PAIR), :].astype(BF16)
        s = _dot_nt(kb, iq_all)
        acc = jnp.zeros((PAIR, TILE), F32)
        for h in range(IDX_HEADS):
            acc = acc + jnp.maximum(s[:, h * TILE:(h + 1) * TILE], 0.0) * w_t[h:h + 1]
        key = _sortable_key(acc)
        if masked:
            key = jnp.where(kp * PAIR + key_row <= q_pos, key, INT_MIN)
        keys_ref[kp] = key

    def score_body(kp, c):
        score_pair(kp, False)
        return c

    lax.fori_loop(0, last_pair, score_body, 0)
    score_pair(last_pair, True)

    def count_where(pred):
        def body(kp, c):
            v = jnp.where(pred(keys_ref[kp], kp), 1.0, 0.0)
            return c + jnp.sum(v.reshape(4, PAIR // 4, TILE), axis=0)
        c = lax.fori_loop(0, last_pair + 1, body, jnp.zeros((PAIR // 4, TILE), F32))
        return jnp.sum(c, axis=0, keepdims=True)

    def count_ge(cand):
        return count_where(lambda k, kp: k >= cand)

    t = _kth_largest_key(count_ge, topk, (1, TILE))
    thr = jnp.maximum(t, INT_MIN + 1)

    cnt_gt = count_ge(thr + 1)
    need = topk - cnt_gt
    cnt_eq = count_ge(thr) - cnt_gt
    tie = jnp.where(t > INT_MIN, jnp.where(cnt_eq > need, 1.0, 0.0), 0.0)

    @pl.when(jnp.max(tie) > 0.0)
    def _():
        n_bits = int(keys_ref.shape[0] * PAIR).bit_length()

        def idx_body(i, mm):
            cand = mm + lax.shift_left(jnp.int32(1), n_bits - 1 - i)
            c = count_where(lambda k, kp: jnp.where(k == thr, kp * PAIR + key_row, INT_MIN) < cand)
            c = c - count_where(lambda k, kp: k != thr)
            return jnp.where(c < need, cand, mm)

        last = lax.fori_loop(0, n_bits, idx_body, jnp.zeros((1, TILE), I32))
        last = jnp.where(tie > 0.0, last, jnp.int32(2 ** 30))

        def demote(kp, c):
            k = keys_ref[kp]
            pos = jnp.where(k == thr, kp * PAIR + key_row, INT_MIN)
            keys_ref[kp] = jnp.where(pos > last, thr - 1, k)
            return c

        lax.fori_loop(0, last_pair + 1, demote, 0)

    q = [jnp.concatenate([aq_ref[:, (g * A_GROUP + r) * HEAD_DIM:(g * A_GROUP + r + 1) * HEAD_DIM]
                          for r in range(A_GROUP)], axis=0) for g in range(A_KV_HEADS)]
    _attn_reset(m_ref, l_ref, acc_ref)

    def att_body(kp, c):
        d0 = jnp.clip(qt - 2 * kp, 0, nd)
        d1 = jnp.clip(qt - 2 * kp - 1, 0, nd)
        madd = jnp.where(keys_ref[kp] >= thr, 0.0, NEG).T
        k0 = pl.multiple_of(kp * PAIR, PAIR)
        for g in range(A_KV_HEADS):
            kc = g * 2 * HEAD_DIM
            kk = akv_ref[pl.ds(k0, PAIR), kc:kc + HEAD_DIM].astype(BF16)
            vv = akv_ref[pl.ds(k0, PAIR), kc + HEAD_DIM:kc + 2 * HEAD_DIM].astype(BF16)

            def bias_of(r, g=g):
                rs = slice(r * TILE, (r + 1) * TILE)
                return jnp.concatenate([bias_ref[d0, g, rs, :], bias_ref[d1, g, rs, :]], axis=1)

            _attn_update(q[g], kk, vv, bias_of, madd, m_ref, l_ref, acc_ref, g * A_GROUP, A_GROUP)
        return c

    lax.fori_loop(0, last_pair + 1, att_body, 0)
    for h in range(A_HEADS):
        o_ref[:, h * HEAD_DIM:(h + 1) * HEAD_DIM] = _attn_out(m_ref, l_ref, acc_ref, h).astype(o_ref.dtype)


def _prompt_mixer_a(qmat, fmat, bias_a, bsz, t, cols, nd):
    nt = t // TILE
    assert nt % 2 == 0
    topk = min(TOPK_MAX, t // 4)
    qc, fc = cols['q'], cols['f']
    return pl.pallas_call(
        functools.partial(_ka_kernel, topk=topk, nd=nd),
        grid=(bsz, nt),
        in_specs=[
            pl.BlockSpec((TILE, IDX_HEADS * LANES), lambda b, i: (b * nt + i, qc['i_q'] // (IDX_HEADS * LANES))),
            _resident((t, LANES), lambda b, i: (b, fc['i_kw'] // LANES)),
            pl.BlockSpec((TILE, LANES), lambda b, i: (b * nt + i, fc['i_kw'] // LANES)),
            pl.BlockSpec((TILE, A_HEADS * HEAD_DIM), lambda b, i: (b * nt + i, qc['a_q'] // (A_HEADS * HEAD_DIM))),
            _resident((t, 4 * HEAD_DIM), lambda b, i: (b, fc['a_kv'] // (4 * HEAD_DIM))),
            _resident(bias_a.shape, lambda b, i: (0, 0, 0, 0)),
        ],
        out_specs=pl.BlockSpec((TILE, A_HEADS * HEAD_DIM), lambda b, i: (b * nt + i, 0)),
        out_shape=jax.ShapeDtypeStruct((bsz * t, A_HEADS * HEAD_DIM), BF16),
        scratch_shapes=[pltpu.VMEM((nt // 2, PAIR, TILE), I32)] + [pltpu.VMEM((A_HEADS, TILE, LANES), F32)] * 3,
        compiler_params=_params(2), name='prompt_mixer_a',
    )(qmat, fmat, fmat, qmat, fmat, bias_a)


def _kb_kernel(bq_ref, gate_ref, ck_ref, cv_ref, ov_ref, sel_ref, w0_ref, w1_ref, w2_ref, w3_ref, w4_ref,
               bias_ref, o_ref, m_ref, l_ref, acc_ref, ocmp_ref, osel_ref, *, nd, n_sel):
    qt = pl.program_id(1)
    last_pair = lax.shift_right_logical(qt, 1)
    rows = B_GROUP * TILE
    row = lax.broadcasted_iota(I32, (TILE, TILE), 0)
    col = lax.broadcasted_iota(I32, (TILE, TILE), 1)
    colp = lax.broadcasted_iota(I32, (TILE, PAIR), 1)
    q_pos_p = qt * TILE + lax.broadcasted_iota(I32, (TILE, PAIR), 0)
    scale = HEAD_DIM ** -0.5
    q = [jnp.concatenate([bq_ref[:, (g * B_GROUP + r) * HEAD_DIM:(g * B_GROUP + r + 1) * HEAD_DIM]
                          for r in range(B_GROUP)], axis=0) for g in range(B_KV_HEADS)]

    ncp = ck_ref.shape[1]
    q_pos = qt * TILE + lax.rem(lax.broadcasted_iota(I32, (rows, ncp), 0), TILE)
    cmp_end = lax.broadcasted_iota(I32, (rows, ncp), 1) * CMP_STRIDE + (CMP_BLOCK - 1)
    cmp_ok = cmp_end <= q_pos
    cur = 2 * qt + jnp.where(row >= SEL_BLOCK, 1, 0)
    scores = []
    for g in range(B_KV_HEADS):
        p = _softmax_rows(_dot_nt(q[g], ck_ref[0, :, g * HEAD_DIM:(g + 1) * HEAD_DIM]) * scale, cmp_ok)
        o_cmp = jnp.dot(p.astype(BF16), cv_ref[0, :, g * HEAD_DIM:(g + 1) * HEAD_DIM], preferred_element_type=F32)
        psum = p[0:TILE]
        ocmp_ref[g * B_GROUP] = o_cmp[0:TILE]
        for r in range(1, B_GROUP):
            psum = psum + p[r * TILE:(r + 1) * TILE]
            ocmp_ref[g * B_GROUP + r] = o_cmp[r * TILE:(r + 1) * TILE]
        imp = jnp.dot(psum, ov_ref[...], preferred_element_type=F32, precision=lax.Precision.HIGHEST)
        forced = jnp.where(col == 0, jnp.inf, jnp.where(col >= cur - 1, jnp.inf, imp))
        scores.append(jnp.where(col <= cur, forced, -jnp.inf))
    selm = [s.astype(BF16) for s in _topn_mask(scores, n_sel)]

    _attn_reset(m_ref, l_ref, acc_ref)

    def sel_pair(kp, masked):
        d0 = jnp.clip(qt - 2 * kp, 0, nd)
        d1 = jnp.clip(qt - 2 * kp - 1, 0, nd)
        k0 = pl.multiple_of(kp * PAIR, PAIR)
        expand = _block_expand((PAIR // SEL_BLOCK) * kp, selm[0].shape[1], PAIR)
        for g in range(B_KV_HEADS):
            madd = (jnp.dot(selm[g], expand, preferred_element_type=F32) - 1.0) * (-NEG)
            if masked:
                madd = jnp.where(kp * PAIR + colp <= q_pos_p, madd, NEG)
            kc = g * 2 * HEAD_DIM
            kk = sel_ref[pl.ds(k0, PAIR), kc:kc + HEAD_DIM].astype(BF16)
            vv = sel_ref[pl.ds(k0, PAIR), kc + HEAD_DIM:kc + 2 * HEAD_DIM].astype(BF16)

            def bias_of(r, g=g):
                rs = slice(r * TILE, (r + 1) * TILE)
                return jnp.concatenate([bias_ref[d0, g, rs, :], bias_ref[d1, g, rs, :]], axis=1)

            _attn_update(q[g], kk, vv, bias_of, madd, m_ref, l_ref, acc_ref, g * B_GROUP, B_GROUP)

    def sel_body(kp, c):
        sel_pair(kp, False)
        return c

    lax.fori_loop(0, last_pair, sel_body, 0)
    sel_pair(last_pair, True)
    for h in range(B_HEADS):
        osel_ref[h] = _attn_out(m_ref, l_ref, acc_ref, h)

    _attn_reset(m_ref, l_ref, acc_ref)
    for k, w_ref in enumerate((w0_ref, w1_ref, w2_ref, w3_ref, w4_ref)):
        if k == 0:
            ok = col <= row
        elif k == WINDOW // TILE:
            ok = row <= col
        else:
            ok = col >= 0
        madd = jnp.where(ok, jnp.where(qt >= k, 0.0, NEG), NEG)
        for g in range(B_KV_HEADS):
            kc = g * 2 * HEAD_DIM
            kk = w_ref[:, kc:kc + HEAD_DIM].astype(BF16)
            vv = w_ref[:, kc + HEAD_DIM:kc + 2 * HEAD_DIM].astype(BF16)

            def bias_of(r, g=g, k=k):
                return bias_ref[k, g, r * TILE:(r + 1) * TILE, :]

            _attn_update(q[g], kk, vv, bias_of, madd, m_ref, l_ref, acc_ref, g * B_GROUP, B_GROUP)

    gate = jax.nn.sigmoid(gate_ref[...])
    for h in range(B_HEADS):
        o = (gate[:, 3 * h:3 * h + 1] * ocmp_ref[h] + gate[:, 3 * h + 1:3 * h + 2] * osel_ref[h]
             + gate[:, 3 * h + 2:3 * h + 3] * _attn_out(m_ref, l_ref, acc_ref, h))
        o_ref[:, h * HEAD_DIM:(h + 1) * HEAD_DIM] = o.astype(o_ref.dtype)


def _overlap_matrix(length, n_rows, n_cols):
    nc = (length - CMP_BLOCK) // CMP_STRIDE + 1
    ns = -(-length // SEL_BLOCK)
    cs = np.arange(nc) * CMP_STRIDE
    ss = np.arange(ns) * SEL_BLOCK
    ov = np.minimum(cs[:, None] + CMP_BLOCK, ss[None, :] + SEL_BLOCK) - np.maximum(cs[:, None], ss[None, :])
    out = np.zeros((n_rows, n_cols), np.float32)
    out[:nc, :ns] = np.clip(ov, 0, None).astype(np.float32) / CMP_BLOCK
    return jnp.asarray(out)


def _prompt_mixer_b(qmat, fmat, cmp_k, cmp_v, bias_b, bsz, t, cols, nd):
    nt = t // TILE
    ns = -(-t // SEL_BLOCK)
    assert ns <= LANES and WINDOW // TILE == 4 and nd >= WINDOW // TILE
    qc, fc = cols['q'], cols['f']
    ncp = cmp_k.shape[1]
    ov = _overlap_matrix(t, ncp, LANES)
    kvw = 4 * HEAD_DIM

    def win_spec(k):
        return pl.BlockSpec((TILE, kvw), lambda b, i: (b * nt + jnp.maximum(i - k, 0), fc['b_win'] // kvw))

    return pl.pallas_call(
        functools.partial(_kb_kernel, nd=nd, n_sel=min(SEL_TOPN, ns)),
        grid=(bsz, nt),
        in_specs=[
            pl.BlockSpec((TILE, B_HEADS * HEAD_DIM), lambda b, i: (b * nt + i, qc['b_q'] // (B_HEADS * HEAD_DIM))),
            pl.BlockSpec((TILE, LANES), lambda b, i: (b * nt + i, fc['b_gate'] // LANES)),
            _resident((1, ncp, B_KV_HEADS * HEAD_DIM), lambda b, i: (b, 0, 0)),
            _resident((1, ncp, B_KV_HEADS * HEAD_DIM), lambda b, i: (b, 0, 0)),
            _resident(ov.shape, lambda b, i: (0, 0)),
            _resident((t, kvw), lambda b, i: (b, fc['b_sel'] // kvw)),
            win_spec(0), win_spec(1), win_spec(2), win_spec(3), win_spec(4),
            _resident(bias_b.shape, lambda b, i: (0, 0, 0, 0)),
        ],
        out_specs=pl.BlockSpec((TILE, B_HEADS * HEAD_DIM), lambda b, i: (b * nt + i, 0)),
        out_shape=jax.ShapeDtypeStruct((bsz * t, B_HEADS * HEAD_DIM), BF16),
        scratch_shapes=[pltpu.VMEM((B_HEADS, TILE, LANES), F32)] * 5,
        compiler_params=_params(2), name='prompt_mixer_b',
    )(qmat, fmat, cmp_k, cmp_v, ov, fmat, fmat, fmat, fmat, fmat, fmat, bias_b)


def _mem_kernel(q_ref, kv_ref, o_ref):
    scale = MEM_HEAD_DIM ** -0.5
    for h in range(MEM_HEADS):
        c = h * 2 * MEM_HEAD_DIM
        kk = kv_ref[:, c:c + MEM_HEAD_DIM].astype(BF16)
        vv = kv_ref[:, c + MEM_HEAD_DIM:c + 2 * MEM_HEAD_DIM].astype(BF16)
        s = _dot_nt(q_ref[:, h * MEM_HEAD_DIM:(h + 1) * MEM_HEAD_DIM], kk) * scale
        e = jnp.exp(s - jnp.max(s, axis=1, keepdims=True))
        p = e / jnp.sum(e, axis=1, keepdims=True)
        o = jnp.dot(p.astype(BF16), vv, preferred_element_type=F32)
        o_ref[:, h * MEM_HEAD_DIM:(h + 1) * MEM_HEAD_DIM] = o.astype(o_ref.dtype)


def _mem_attend(qmat, q_col, mem_kv2d, n_batch, rows_per_batch, n_mem, tq):
    width = MEM_HEADS * MEM_HEAD_DIM
    tq = min(tq, rows_per_batch)
    nq = rows_per_batch // tq
    return pl.pallas_call(
        _mem_kernel,
        grid=(n_batch, nq),
        in_specs=[pl.BlockSpec((tq, width), lambda b, i: (b * nq + i, q_col // width)),
                  pl.BlockSpec((n_mem, 2 * width), lambda b, i: (b, 0))],
        out_specs=pl.BlockSpec((tq, width), lambda b, i: (b * nq + i, 0)),
        out_shape=jax.ShapeDtypeStruct((n_batch * rows_per_batch, width), BF16),
        compiler_params=_params(2), name='mem_attend',
    )(qmat, mem_kv2d)


def _gated_proj_kernel(ga_ref, gb_ref, gm_ref, oa_ref, ob_ref, om_ref, wpa_ref, wpb_ref, wpm_ref, o_ref):
    merged = jax.nn.sigmoid(ga_ref[...]) * jnp.dot(oa_ref[...], wpa_ref[...], preferred_element_type=F32)
    merged = merged + jax.nn.sigmoid(gb_ref[...]) * jnp.dot(ob_ref[...], wpb_ref[...], preferred_element_type=F32)
    merged = merged + jax.nn.sigmoid(gm_ref[...]) * jnp.dot(om_ref[...], wpm_ref[...], preferred_element_type=F32)
    o_ref[...] = merged.astype(o_ref.dtype)


def _out_proj_kernel(x_ref, mg_ref, wo_ref, lg_ref, lb_ref, o_ref, *, alpha):
    y = alpha * x_ref[...] + jnp.dot(mg_ref[...], wo_ref[...], preferred_element_type=F32)
    o_ref[...] = _layer_norm(y, lg_ref[...], lb_ref[...])


def _merge(x2d, gmat, oa, ob, om, wpa, wpb, wpm, wo, ln_g, ln_b, alpha, tm=512, tn=1024):
    m, d = x2d.shape
    tm = min(tm, m)
    assert m % tm == 0 and d % tn == 0
    nj = d // tn
    row = lambda i, j: (i, 0)
    wcol = lambda i, j: (0, j)
    merged = pl.pallas_call(
        _gated_proj_kernel,
        grid=(m // tm, nj),
        in_specs=[pl.BlockSpec((tm, tn), lambda i, j: (i, j)),
                  pl.BlockSpec((tm, tn), lambda i, j: (i, nj + j)),
                  pl.BlockSpec((tm, tn), lambda i, j: (i, 2 * nj + j)),
                  pl.BlockSpec((tm, oa.shape[1]), row), pl.BlockSpec((tm, ob.shape[1]), row),
                  pl.BlockSpec((tm, om.shape[1]), row),
                  pl.BlockSpec((wpa.shape[0], tn), wcol), pl.BlockSpec((wpb.shape[0], tn), wcol),
                  pl.BlockSpec((wpm.shape[0], tn), wcol)],
        out_specs=pl.BlockSpec((tm, tn), lambda i, j: (i, j)),
        out_shape=jax.ShapeDtypeStruct((m, d), BF16),
        compiler_params=_params(2), name='gated_proj',
    )(gmat, gmat, gmat, oa, ob, om, wpa, wpb, wpm)
    fixed = lambda i: (0, 0)
    return pl.pallas_call(
        functools.partial(_out_proj_kernel, alpha=alpha),
        grid=(m // tm,),
        in_specs=[pl.BlockSpec((tm, d), lambda i: (i, 0)), pl.BlockSpec((tm, d), lambda i: (i, 0)),
                  _resident(wo.shape, fixed), _resident((1, d), fixed), _resident((1, d), fixed)],
        out_specs=pl.BlockSpec((tm, d), lambda i: (i, 0)),
        out_shape=jax.ShapeDtypeStruct((m, d), F32),
        compiler_params=_params(1), name='out_proj_ln',
    )(x2d, merged, wo, ln_g, ln_b)


def _ffn_kernel(x_ref, wu_ref, bu_ref, wd_ref, bd_ref, lg_ref, lb_ref, o_ref, acc_ref, *, alpha):
    j = pl.program_id(1)

    @pl.when(j == 0)
    def _():
        acc_ref[...] = jnp.zeros_like(acc_ref)

    u = jnp.dot(x_ref[...].astype(BF16), wu_ref[...], preferred_element_type=F32) + bu_ref[...]
    u = jnp.square(jnp.maximum(u, 0.0))
    acc_ref[...] += jnp.dot(u.astype(BF16), wd_ref[...], preferred_element_type=F32)

    @pl.when(j == pl.num_programs(1) - 1)
    def _():
        y = alpha * x_ref[...] + acc_ref[...] + bd_ref[...]
        o_ref[...] = _layer_norm(y, lg_ref[...], lb_ref[...])


def _ffn(x2d, wu, bu, wd, bd, ln_g, ln_b, alpha, tm=512, tf=1024):
    m, d = x2d.shape
    dff = wu.shape[1]
    tm = min(tm, m)
    assert m % tm == 0 and dff % tf == 0
    return pl.pallas_call(
        functools.partial(_ffn_kernel, alpha=alpha),
        grid=(m // tm, dff // tf),
        in_specs=[pl.BlockSpec((tm, d), lambda i, j: (i, 0)),
                  pl.BlockSpec((d, tf), lambda i, j: (0, j)), pl.BlockSpec((1, tf), lambda i, j: (0, j)),
                  pl.BlockSpec((tf, d), lambda i, j: (j, 0)),
                  _resident((1, d), lambda i, j: (0, 0)), _resident((1, d), lambda i, j: (0, 0)),
                  _resident((1, d), lambda i, j: (0, 0))],
        out_specs=pl.BlockSpec((tm, d), lambda i, j: (i, 0)),
        out_shape=jax.ShapeDtypeStruct((m, d), F32),
        scratch_shapes=[pltpu.VMEM((tm, d), F32)],
        compiler_params=_params(2), name='ffn_ln',
    )(x2d, wu, bu, wd, bd, ln_g, ln_b)


def _sidx_kernel(pt_ref, iq_ref, w_ref, knew_ref, pool_ref, o_ref, buf_ref, sem_ref, keys_ref, *, topk, n_q, past):
    g = pl.program_id(1)
    slot = _page_pipeline(pt_ref, pool_ref, buf_ref, sem_ref)
    span = PAGES_PER_STEP * PAGE_SIZE

    def scores(kb, width):
        s = _dot_nt(iq_ref[0], kb.astype(BF16))
        acc = jnp.zeros((n_q, width), F32)
        for h in range(IDX_HEADS):
            acc = acc + jnp.maximum(s[h * n_q:(h + 1) * n_q], 0.0) * w_ref[0, h * n_q:(h + 1) * n_q, 0:1]
        return _sortable_key(acc)

    keys_ref[:, pl.ds(pl.multiple_of(g * span, span), span)] = scores(
        buf_ref[slot].reshape(span, buf_ref.shape[3]), span)

    @pl.when(g == pl.num_programs(1) - 1)
    def _():
        lp = keys_ref.shape[1]
        rown = lax.broadcasted_iota(I32, (n_q, TILE), 0)
        coln = lax.broadcasted_iota(I32, (n_q, TILE), 1)
        keys_ref[:, past:lp] = jnp.where(coln <= rown, scores(knew_ref[0], TILE), INT_MIN)
        keys = keys_ref[...]
        pos = lax.broadcasted_iota(I32, (n_q, lp), 1)

        def count(pred):
            return jnp.sum(jnp.where(pred, 1.0, 0.0), axis=1, keepdims=True)

        t = _kth_largest_key(lambda cand: count(keys >= cand), topk, (n_q, 1))
        thr = jnp.maximum(t, INT_MIN + 1)
        cnt_gt = count(keys >= thr + 1)
        need = topk - cnt_gt
        cnt_eq = count(keys >= thr) - cnt_gt
        tie = jnp.where(t > INT_MIN, jnp.where(cnt_eq > need, 1.0, 0.0), 0.0)
        eq_pos = jnp.where(keys == thr, pos, jnp.int32(2 ** 30))
        n_bits = int(lp).bit_length()

        def idx_body(i, mm):
            cand = mm + lax.shift_left(jnp.int32(1), n_bits - 1 - i)
            return jnp.where(count(eq_pos < cand) < need, cand, mm)

        last = lax.fori_loop(0, n_bits, idx_body, jnp.zeros((n_q, 1), I32))
        last = jnp.where(tie > 0.0, last, jnp.int32(2 ** 30))
        sel = jnp.where(keys > thr, 1.0, jnp.where(keys == thr, jnp.where(pos <= last, 1.0, 0.0), 0.0))
        o_ref[0, 0] = sel


def _sample_index_mask(page_table, iq_s, w_s, knew, pool_idx, topk):
    db, n_pages = page_table.shape
    n_q = iq_s.shape[1] // IDX_HEADS
    npg = n_pages // PAGES_PER_STEP
    lp = n_pages * PAGE_SIZE + TILE
    return pl.pallas_call(
        functools.partial(_sidx_kernel, topk=topk, n_q=n_q, past=n_pages * PAGE_SIZE),
        grid_spec=pltpu.PrefetchScalarGridSpec(
            num_scalar_prefetch=1,
            grid=(db, npg),
            in_specs=[pl.BlockSpec((1,) + iq_s.shape[1:], lambda b, g, pt: (b, 0, 0)),
                      pl.BlockSpec((1,) + w_s.shape[1:], lambda b, g, pt: (b, 0, 0)),
                      pl.BlockSpec((1,) + knew.shape[1:], lambda b, g, pt: (b, 0, 0)),
                      pl.BlockSpec(memory_space=pl.ANY)],
            out_specs=pl.BlockSpec((1, 1, n_q, lp), lambda b, g, pt: (b, 0, 0, 0)),
            scratch_shapes=[pltpu.VMEM((2, PAGES_PER_STEP) + pool_idx.shape[1:], F32),
                            pltpu.SemaphoreType.DMA((2,)),
                            pltpu.VMEM((n_q, lp), I32)]),
        out_shape=jax.ShapeDtypeStruct((db, 1, n_q, lp), F32),
        compiler_params=_params(2), name='sample_index_mask',
    )(page_table.reshape(-1), iq_s, w_s, knew, pool_idx)


def _pattn_kernel(pt_ref, q_ref, mask_ref, new_ref, bias_ref, pool_ref, o_ref, buf_ref, sem_ref,
                  m_ref, l_ref, acc_ref, *, nd, n_q, n_pages):
    g = pl.program_id(1)
    npg = pl.num_programs(1)
    slot = _page_pipeline(pt_ref, pool_ref, buf_ref, sem_ref)
    rows = q_ref.shape[1]
    half = rows // 2
    scale = HEAD_DIM ** -0.5
    mask_groups = mask_ref.shape[1]

    @pl.when(g == 0)
    def _():
        m0, l0, a0 = _flash_init(rows, 2 * HEAD_DIM)
        m_ref[...] = m0
        l_ref[...] = l0
        acc_ref[...] = a0

    def block_update(kv, p0, n_tiles, carry):
        kk = jnp.concatenate([kv[:, 0:HEAD_DIM], kv[:, 2 * HEAD_DIM:3 * HEAD_DIM]], axis=1).astype(BF16)
        vv = jnp.concatenate([kv[:, HEAD_DIM:2 * HEAD_DIM], kv[:, 3 * HEAD_DIM:4 * HEAD_DIM]], axis=1).astype(BF16)
        width = n_tiles * TILE
        mk = mask_ref[0, :, :, pl.ds(pl.multiple_of(p0 * TILE, TILE), width)]
        madd = (mk - 1.0) * (-NEG)
        reps = rows // (mask_groups * n_q)
        madd = jnp.concatenate([madd[i] for i in range(mask_groups) for _ in range(reps)], axis=0)
        bias = jnp.concatenate([bias_ref[jnp.clip(n_pages - (p0 + i), 0, nd)] for i in range(n_tiles)], axis=1)
        s = _dot_nt(q_ref[0], kk) * scale + (bias + madd)
        return _flash_step(s, *carry, vv)

    carry = (m_ref[...], l_ref[...], acc_ref[...])
    for j in range(PAGES_PER_STEP // PAGES_PER_BLOCK):
        kv = buf_ref[slot, j * PAGES_PER_BLOCK:(j + 1) * PAGES_PER_BLOCK].reshape(PAGES_PER_BLOCK * PAGE_SIZE, -1)
        carry = block_update(kv, g * PAGES_PER_STEP + j * PAGES_PER_BLOCK, PAGES_PER_BLOCK, carry)
    m_ref[...], l_ref[...], acc_ref[...] = carry

    @pl.when(g == npg - 1)
    def _():
        o = _flash_out(*block_update(new_ref[0], n_pages, 1, carry))
        o_ref[0, 0:half] = o[0:half, 0:HEAD_DIM]
        o_ref[0, half:rows] = o[half:rows, HEAD_DIM:2 * HEAD_DIM]


def _paged_attention(page_table, qblk, mask, new_kv, bias_s, pool, nd, n_q):
    db, n_pages = page_table.shape
    npg = n_pages // PAGES_PER_STEP
    rows = qblk.shape[1]
    return pl.pallas_call(
        functools.partial(_pattn_kernel, nd=nd, n_q=n_q, n_pages=n_pages),
        grid_spec=pltpu.PrefetchScalarGridSpec(
            num_scalar_prefetch=1,
            grid=(db, npg),
            in_specs=[pl.BlockSpec((1,) + qblk.shape[1:], lambda b, g, pt: (b, 0, 0)),
                      pl.BlockSpec((1,) + mask.shape[1:], lambda b, g, pt: (b, 0, 0, 0)),
                      pl.BlockSpec((1,) + new_kv.shape[1:], lambda b, g, pt: (b, 0, 0)),
                      _resident(bias_s.shape, lambda b, g, pt: (0, 0, 0)),
                      pl.BlockSpec(memory_space=pl.ANY)],
            out_specs=pl.BlockSpec((1, rows, HEAD_DIM), lambda b, g, pt: (b, 0, 0)),
            scratch_shapes=[pltpu.VMEM((2, PAGES_PER_STEP) + pool.shape[1:], F32),
                            pltpu.SemaphoreType.DMA((2,)),
                            pltpu.VMEM((rows, 1), F32), pltpu.VMEM((rows, 1), F32),
                            pltpu.VMEM((rows, 2 * HEAD_DIM), F32)]),
        out_shape=jax.ShapeDtypeStruct((db, rows, HEAD_DIM), F32),
        compiler_params=_params(2), name='paged_attention',
    )(page_table.reshape(-1), qblk, mask, new_kv, bias_s, pool)


def _scmp_kernel(q_ref, ck_ref, cv_ref, ov_ref, ocmp_ref, mask_ref, selm_ref, *, past, n_q, n_sel):
    rows = q_ref.shape[2]
    ncp = ck_ref.shape[1]
    nsp = ov_ref.shape[1]
    scale = HEAD_DIM ** -0.5
    qi = lax.rem(lax.broadcasted_iota(I32, (rows, ncp), 0), n_q)
    cmp_end = lax.broadcasted_iota(I32, (rows, ncp), 1) * CMP_STRIDE + (CMP_BLOCK - 1)
    cmp_ok = cmp_end <= past + qi
    blk = lax.broadcasted_iota(I32, (n_q, nsp), 1)
    cur = lax.shift_right_logical(past + lax.broadcasted_iota(I32, (n_q, nsp), 0), int(math.log2(SEL_BLOCK)))
    q_pos = past + lax.broadcasted_iota(I32, (n_q, TILE), 0)
    coln = lax.broadcasted_iota(I32, (n_q, TILE), 1)
    scores = []
    for g in range(B_KV_HEADS):
        p = _softmax_rows(_dot_nt(q_ref[0, g], ck_ref[0, :, g * HEAD_DIM:(g + 1) * HEAD_DIM]) * scale, cmp_ok)
        ocmp_ref[0, g] = jnp.dot(p.astype(BF16), cv_ref[0, :, g * HEAD_DIM:(g + 1) * HEAD_DIM],
                                 preferred_element_type=F32)
        psum = p[0:n_q]
        for r in range(1, rows // n_q):
            psum = psum + p[r * n_q:(r + 1) * n_q]
        imp = jnp.dot(psum, ov_ref[...], preferred_element_type=F32, precision=lax.Precision.HIGHEST)
        forced = jnp.where(blk == 0, jnp.inf, jnp.where(blk >= cur - 1, jnp.inf, imp))
        scores.append(jnp.where(blk <= cur, forced, -jnp.inf))
    for g, selm in enumerate(_topn_mask(scores, n_sel)):
        selm_ref[g] = selm

    blocks_per_tile = TILE // SEL_BLOCK
    half = lax.shift_right_logical(coln, int(math.log2(SEL_BLOCK)))

    def expand_body(kt, c):
        blk0 = blocks_per_tile * kt
        win0 = pl.multiple_of(lax.shift_right_logical(blk0, int(math.log2(LANES))) * LANES, LANES)
        idx = (blk0 - win0) + half
        k0 = pl.multiple_of(kt * TILE, TILE)
        for g in range(B_KV_HEADS):
            e = jnp.take_along_axis(selm_ref[g, :, pl.ds(win0, LANES)], idx, axis=1)
            mask_ref[0, g, :, pl.ds(k0, TILE)] = jnp.where(kt * TILE + coln <= q_pos, e, 0.0)
        return c

    lax.fori_loop(0, mask_ref.shape[3] // TILE, expand_body, 0)


def _sample_cmp_select(bq_s, cmp_k, cmp_v, length, past, n_q):
    db = bq_s.shape[0]
    ncp = cmp_k.shape[1]
    ns = -(-length // SEL_BLOCK)
    nsp = -(-ns // LANES) * LANES
    ov = _overlap_matrix(length, ncp, nsp)
    lp = past + TILE
    return pl.pallas_call(
        functools.partial(_scmp_kernel, past=past, n_q=n_q, n_sel=min(SEL_TOPN, ns)),
        grid=(db,),
        in_specs=[pl.BlockSpec((1,) + bq_s.shape[1:], lambda b: (b, 0, 0, 0)),
                  pl.BlockSpec((1, ncp, B_KV_HEADS * HEAD_DIM), lambda b: (b, 0, 0)),
                  pl.BlockSpec((1, ncp, B_KV_HEADS * HEAD_DIM), lambda b: (b, 0, 0)),
                  _resident(ov.shape, lambda b: (0, 0))],
        out_specs=[pl.BlockSpec((1,) + bq_s.shape[1:], lambda b: (b, 0, 0, 0)),
                   pl.BlockSpec((1, B_KV_HEADS, n_q, lp), lambda b: (b, 0, 0, 0))],
        out_shape=[jax.ShapeDtypeStruct(bq_s.shape, F32),
                   jax.ShapeDtypeStruct((db, B_KV_HEADS, n_q, lp), F32)],
        scratch_shapes=[pltpu.VMEM((B_KV_HEADS, n_q, nsp), F32)],
        compiler_params=_params(1), name='sample_cmp_select',
    )(bq_s, cmp_k, cmp_v, ov)


def _swin_kernel(q_ref, win_ref, new_ref, bias_ref, gate_ref, ocmp_ref, osel_ref, o_ref, *, n_q):
    rows = q_ref.shape[1]
    half = rows // 2
    wb = win_ref.shape[1]
    scale = HEAD_DIM ** -0.5
    qi = lax.rem(lax.broadcasted_iota(I32, (rows, TILE), 0), n_q)
    col = lax.broadcasted_iota(I32, (rows, TILE), 1)
    carry = _flash_init(rows, 2 * HEAD_DIM)

    def tile_update(kv, dlt, ok, carry):
        kk = jnp.concatenate([kv[:, 0:HEAD_DIM], kv[:, 2 * HEAD_DIM:3 * HEAD_DIM]], axis=1).astype(BF16)
        vv = jnp.concatenate([kv[:, HEAD_DIM:2 * HEAD_DIM], kv[:, 3 * HEAD_DIM:4 * HEAD_DIM]], axis=1).astype(BF16)
        s = _dot_nt(q_ref[0], kk) * scale + (bias_ref[dlt] + jnp.where(ok, 0.0, NEG))
        return _flash_step(s, *carry, vv)

    for kt in range(wb // TILE):
        ok = col + kt * TILE >= qi + (wb - WINDOW)
        carry = tile_update(win_ref[0, kt * TILE:(kt + 1) * TILE, :], wb // TILE - kt, ok, carry)
    carry = tile_update(new_ref[0], 0, col <= qi, carry)
    o = _flash_out(*carry)
    gate = jax.nn.sigmoid(gate_ref[0])
    for h in range(B_HEADS):
        sl = slice(h * n_q, (h + 1) * n_q)
        ow = o[sl, 0:HEAD_DIM] if h < B_GROUP else o[sl, HEAD_DIM:2 * HEAD_DIM]
        o_ref[0, :, h * HEAD_DIM:(h + 1) * HEAD_DIM] = (
            gate[:, 3 * h:3 * h + 1] * ocmp_ref[0, sl] + gate[:, 3 * h + 1:3 * h + 2] * osel_ref[0, sl]
            + gate[:, 3 * h + 2:3 * h + 3] * ow).astype(o_ref.dtype)


def _sample_window_combine(qblk, win_state, new_win, bias_s, gates, o_cmp, o_sel, n_q):
    db, rows, _ = qblk.shape
    return pl.pallas_call(
        functools.partial(_swin_kernel, n_q=n_q),
        grid=(db,),
        in_specs=[pl.BlockSpec((1,) + qblk.shape[1:], lambda b: (b, 0, 0)),
                  pl.BlockSpec((1,) + win_state.shape[1:], lambda b: (b, 0, 0)),
                  pl.BlockSpec((1,) + new_win.shape[1:], lambda b: (b, 0, 0)),
                  _resident(bias_s.shape, lambda b: (0, 0, 0)),
                  pl.BlockSpec((1,) + gates.shape[1:], lambda b: (b, 0, 0)),
                  pl.BlockSpec((1, rows, HEAD_DIM), lambda b: (b, 0, 0)),
                  pl.BlockSpec((1, rows, HEAD_DIM), lambda b: (b, 0, 0))],
        out_specs=pl.BlockSpec((1, n_q, B_HEADS * HEAD_DIM), lambda b: (b, 0, 0)),
        out_shape=jax.ShapeDtypeStruct((db, n_q, B_HEADS * HEAD_DIM), BF16),
        compiler_params=_params(1), name='sample_window_combine',
    )(qblk, win_state, new_win, bias_s, gates, o_cmp, o_sel)


def _pack_weights(w_in):
    d = w_in.shape[0]
    sizes = _split_sizes(d)
    off, o = {}, 0
    for name in _GROUPS:
        off[name] = o
        o += sizes[name]
    take = lambda name: w_in[:, off[name]:off[name] + sizes[name]]
    zeros = lambda n: jnp.zeros((d, n), w_in.dtype)
    iq = take('i_q').reshape(d, IDX_HEADS, IDX_DIM)
    iq = jnp.concatenate([iq, jnp.zeros_like(iq)], axis=2).reshape(d, IDX_HEADS * LANES)
    w_f = jnp.concatenate([take('a_kv'), take('b_cmp'), take('b_sel'), take('b_win'),
                           take('i_k'), take('i_w'), zeros(LANES - IDX_DIM - IDX_HEADS),
                           take('b_gate'), zeros(LANES - B_HEADS * 3)], axis=1).astype(BF16)
    w_q = jnp.concatenate([take('a_q'), take('b_q'), take('m_q'), iq], axis=1).astype(BF16)
    w_g = take('g_merge').astype(BF16)
    kvw = 4 * HEAD_DIM
    cols = dict(f=dict(a_kv=0, b_cmp=kvw, b_sel=2 * kvw, b_win=3 * kvw, i_kw=4 * kvw, b_gate=4 * kvw + LANES),
                q=dict(a_q=0, b_q=1024, m_q=2048, i_q=3072))
    return w_f, w_q, w_g, cols


def kernel(x_prompt, x_sample, mem_prompt, cache_a_kv, cache_a_idx, cache_b_cmp, cache_b_sel, state_b_win,
           cache_mem, page_table, rel_table, w_in, w_mem_kv, cmp_pe_k, cmp_w1_k, cmp_w2_k, cmp_pe_v, cmp_w1_v,
           cmp_w2_v, w_pa, w_pb, w_pm, w_o, ln1_g, ln1_b, w_up, b_up, w_down, b_down, ln2_g, ln2_b):
    depth = w_in.shape[0]
    assert depth == 1
    bsz, t, d = x_prompt.shape
    db, ds, _ = x_sample.shape
    n_mem = mem_prompt.shape[1]
    n_pool = cache_a_kv.shape[1]
    n_pages = page_table.shape[1]
    past = n_pages * PAGE_SIZE
    wb = state_b_win.shape[2]
    alpha = (2 * depth) ** 0.25
    kvw = 4 * HEAD_DIM
    assert t % TILE == 0 and ds == 8 and wb % TILE == 0 and n_pages % PAGES_PER_STEP == 0

    w_f, w_q, w_g, cols = _pack_weights(w_in[0])
    fc, qc = cols['f'], cols['q']
    w1cat = jnp.stack([jnp.concatenate([w[0][:CMP_STRIDE * HEAD_DIM], w[0][CMP_STRIDE * HEAD_DIM:]], axis=1)
                       for w in (cmp_w1_k, cmp_w1_v)]).astype(BF16)
    w1 = jnp.stack([cmp_w1_k[0], cmp_w1_v[0]]).astype(BF16)
    w2 = jnp.stack([cmp_w2_k[0], cmp_w2_v[0]]).astype(BF16)
    pe8 = jnp.broadcast_to(jnp.stack([cmp_pe_k[0].reshape(1, -1), cmp_pe_v[0].reshape(1, -1)]),
                           (2, 8, CMP_BLOCK * HEAD_DIM)).astype(BF16)
    wpa, wpb, wpm, wo = (w[0].astype(BF16) for w in (w_pa, w_pb, w_pm, w_o))
    wu, wd = w_up[0].astype(BF16), w_down[0].astype(BF16)
    nd = _num_near_tiles()
    bias = _bias_tiles(rel_table, nd)

    def dense_tail(x2d, gmat, oa, ob, om):
        x1 = _merge(x2d, gmat, oa, ob, om, wpa, wpb, wpm, wo, ln1_g, ln1_b, alpha)
        return _ffn(x1, wu, b_up, wd, b_down, ln2_g, ln2_b, alpha)

    xp = x_prompt.reshape(bsz * t, d)
    fp = _matmul(xp, w_f, F32, tn=w_f.shape[1] // 2)
    qp = _matmul(xp, w_q, BF16, tn=1024)
    gp = _matmul(xp, w_g, F32, tn=1024)
    p_a_kv = fp[:, fc['a_kv']:fc['a_kv'] + kvw]
    p_b_cmp = fp[:, fc['b_cmp']:fc['b_cmp'] + kvw]
    p_b_sel = fp[:, fc['b_sel']:fc['b_sel'] + kvw]
    p_b_win = fp[:, fc['b_win']:fc['b_win'] + kvw]
    p_a_idx = fp[:, fc['i_kw']:fc['i_kw'] + IDX_DIM]

    zp = _cmpz_dense(p_b_cmp.reshape(bsz * t // CMP_STRIDE, CMP_STRIDE * kvw), w1cat)
    cmp_k, cmp_v = _cmp_finish(zp.reshape(bsz, t // CMP_STRIDE, -1), pe8, w1, w2)
    o_a = _prompt_mixer_a(qp, fp, bias[0], bsz, t, cols, nd)
    o_b = _prompt_mixer_b(qp, fp, cmp_k, cmp_v, bias[1], bsz, t, cols, nd)
    mem_kv = _matmul(mem_prompt.reshape(bsz * n_mem, d), w_mem_kv[0].astype(BF16), F32, tn=1024)
    o_m = _mem_attend(qp, qc['m_q'], mem_kv, bsz, t, n_mem, tq=512)
    y_prompt = dense_tail(xp, gp, o_a, o_b, o_m).reshape(bsz, t, d)

    xs = x_sample.reshape(db * ds, d)
    fs = _matmul(xs, w_f, F32, tn=w_f.shape[1] // 2)
    qs = _matmul(xs, w_q, BF16, tn=1024)
    gs = _matmul(xs, w_g, F32, tn=1024)
    s_a_kv = fs[:, fc['a_kv']:fc['a_kv'] + kvw]
    s_b_cmp = fs[:, fc['b_cmp']:fc['b_cmp'] + kvw]
    s_b_sel = fs[:, fc['b_sel']:fc['b_sel'] + kvw]
    s_b_win = fs[:, fc['b_win']:fc['b_win'] + kvw]
    s_a_idx = fs[:, fc['i_kw']:fc['i_kw'] + IDX_DIM]
    length = past + ds

    def pad_new(rows2d):
        r = rows2d.reshape(db, ds, -1)
        return jnp.concatenate([r, jnp.zeros((db, TILE - ds, r.shape[2]), r.dtype)], axis=1)

    def head_major(q2d, heads):
        return q2d.reshape(db, ds, heads, -1).transpose(0, 2, 1, 3).reshape(db, heads * ds, -1)

    def block_q(q2d):
        qh = head_major(q2d, A_HEADS).reshape(db, A_KV_HEADS, A_GROUP * ds, HEAD_DIM)
        z = jnp.zeros_like(qh[:, 0])
        return jnp.concatenate([jnp.concatenate([qh[:, 0], z], axis=2),
                                jnp.concatenate([z, qh[:, 1]], axis=2)], axis=1)

    def sample_bias(tiles):
        n = tiles.shape[0]
        return tiles.reshape(n, A_KV_HEADS, A_GROUP, TILE, TILE)[:, :, :, :ds].reshape(n, A_HEADS * ds, TILE)

    iq_s = head_major(qs[:, qc['i_q']:qc['i_q'] + IDX_HEADS * LANES], IDX_HEADS)[:, :, :IDX_DIM]
    w_s = fs[:, fc['i_kw'] + IDX_DIM:fc['i_kw'] + IDX_DIM + IDX_HEADS] * (IDX_HEADS ** -0.5 * IDX_DIM ** -0.5)
    w_s = jnp.broadcast_to(head_major(w_s, IDX_HEADS), (db, IDX_HEADS * ds, LANES))
    mask_a = _sample_index_mask(page_table, iq_s, w_s, pad_new(s_a_idx),
                                cache_a_idx.reshape(n_pool, PAGE_SIZE, IDX_DIM), min(TOPK_MAX, length // 4))
    qa_blk = block_q(qs[:, qc['a_q']:qc['a_q'] + A_HEADS * HEAD_DIM])
    o_a_s = _paged_attention(page_table, qa_blk, mask_a, pad_new(s_a_kv), sample_bias(bias[0]),
                             cache_a_kv.reshape(n_pool, PAGE_SIZE, kvw), nd, ds)

    zs = _cmpz_paged(cache_b_cmp.reshape(n_pool, PAGE_SIZE, kvw), page_table, w1cat)
    cmp_k_s, cmp_v_s = _cmp_finish(zs.reshape(db, past // CMP_STRIDE, -1), pe8, w1, w2)
    bq2d = qs[:, qc['b_q']:qc['b_q'] + B_HEADS * HEAD_DIM]
    bq_s = head_major(bq2d, B_HEADS).reshape(db, B_KV_HEADS, B_GROUP * ds, HEAD_DIM)
    o_cmp_s, mask_b = _sample_cmp_select(bq_s, cmp_k_s, cmp_v_s, length, past, ds)
    qb_blk = block_q(bq2d)
    bias_sb = sample_bias(bias[1])
    o_sel_s = _paged_attention(page_table, qb_blk, mask_b, pad_new(s_b_sel), bias_sb,
                               cache_b_sel.reshape(n_pool, PAGE_SIZE, kvw), nd, ds)
    gates_s = fs[:, fc['b_gate']:fc['b_gate'] + LANES].reshape(db, ds, LANES)
    o_b_s = _sample_window_combine(qb_blk, state_b_win.reshape(db, wb, kvw), pad_new(s_b_win), bias_sb,
                                   gates_s, o_cmp_s.reshape(db, B_HEADS * ds, HEAD_DIM), o_sel_s, ds)

    o_a_s = o_a_s.reshape(db, A_HEADS, ds, HEAD_DIM).transpose(0, 2, 1, 3).reshape(db * ds, -1).astype(BF16)
    o_m_s = _mem_attend(qs, qc['m_q'], cache_mem.reshape(db * n_mem, -1), db, ds, n_mem, tq=ds)
    y_sample = dense_tail(xs, gs, o_a_s, o_b_s.reshape(db * ds, -1), o_m_s).reshape(db, ds, d)

    kv6 = lambda a, n, rows: a.reshape(1, n, rows, 2, 2, HEAD_DIM)
    wp = min(WINDOW, t)
    new_win = jnp.concatenate([state_b_win.reshape(db, wb, 2, 2, HEAD_DIM)[:, ds:],
                               s_b_win.reshape(db, ds, 2, 2, HEAD_DIM)], axis=1)
    return (y_prompt, y_sample,
            kv6(p_a_kv, bsz, t), p_a_idx.reshape(1, bsz, t, IDX_DIM), kv6(p_b_cmp, bsz, t), kv6(p_b_sel, bsz, t),
            kv6(p_b_win, bsz, t)[:, :, t - wp:],
            mem_kv.reshape(1, bsz, n_mem, MEM_HEADS, 2, MEM_HEAD_DIM),
            kv6(s_a_kv, db, ds), s_a_idx.reshape(1, db, ds, IDX_DIM), kv6(s_b_cmp, db, ds), kv6(s_b_sel, db, ds),
            new_win[None])
```

```python
import functools
import math

import numpy as np
import jax
import jax.numpy as jnp
from jax import lax
from jax.experimental import pallas as pl
from jax.experimental.pallas import tpu as pltpu

F32 = jnp.float32
BF16 = jnp.bfloat16
I32 = jnp.int32

HEAD_DIM = 128
A_HEADS = 8
A_KV_HEADS = 2
A_GROUP = A_HEADS // A_KV_HEADS
IDX_HEADS = 8
IDX_DIM = 64
TOPK_MAX = 256
B_HEADS = 8
B_KV_HEADS = 2
B_GROUP = B_HEADS // B_KV_HEADS
CMP_BLOCK = 32
CMP_STRIDE = 16
CMP_RATIO = CMP_BLOCK // CMP_STRIDE
CMP_HID = 128
SEL_BLOCK = 64
SEL_TOPN = 16
WINDOW = 512
MEM_HEADS = 4
MEM_HEAD_DIM = 256
N_BUCKETS = 32
MAX_DISTANCE = 1024
LN_EPS = 1e-5
PAGE_SIZE = 128
KV_ROWS = 4

LANES = 128
VMEM_LIMIT = 56 * 1024 * 1024

TILE = 128
PAIR = 2 * TILE
NEG = -1e30
INT_MIN = -2 ** 31
PAGES_PER_STEP = 16
PAGES_PER_BLOCK = 8

_GROUPS = ('a_q', 'a_kv', 'i_q', 'i_k', 'i_w', 'b_q', 'b_cmp', 'b_sel', 'b_win', 'b_gate', 'm_q', 'g_merge')


def _split_sizes(d_model):
    return dict(
        a_q=A_HEADS * HEAD_DIM, a_kv=A_KV_HEADS * 2 * HEAD_DIM, i_q=IDX_HEADS * IDX_DIM, i_k=IDX_DIM,
        i_w=IDX_HEADS, b_q=B_HEADS * HEAD_DIM, b_cmp=B_KV_HEADS * 2 * HEAD_DIM, b_sel=B_KV_HEADS * 2 * HEAD_DIM,
        b_win=B_KV_HEADS * 2 * HEAD_DIM, b_gate=B_HEADS * 3, m_q=MEM_HEADS * MEM_HEAD_DIM, g_merge=3 * d_model)


def _params(n_grid, vmem=VMEM_LIMIT):
    return pltpu.CompilerParams(dimension_semantics=('arbitrary',) * n_grid, vmem_limit_bytes=vmem)


def _resident(block, index_map):
    return pl.BlockSpec(block, index_map, pipeline_mode=pl.Buffered(1))


def _mm_kernel(x_ref, w_ref, o_ref):
    o_ref[...] = jnp.dot(x_ref[...].astype(BF16), w_ref[...], preferred_element_type=F32).astype(o_ref.dtype)


def _matmul(x, w, out_dtype, tn, tm=1024, name='matmul'):
    m, k = x.shape
    n = w.shape[1]
    tm = min(tm, m)
    assert m % tm == 0 and n % tn == 0
    return pl.pallas_call(
        _mm_kernel,
        grid=(m // tm, n // tn),
        in_specs=[pl.BlockSpec((tm, k), lambda i, j: (i, 0)), pl.BlockSpec((k, tn), lambda i, j: (0, j))],
        out_specs=pl.BlockSpec((tm, tn), lambda i, j: (i, j)),
        out_shape=jax.ShapeDtypeStruct((m, n), out_dtype),
        compiler_params=_params(2), name=name,
    )(x, w)


def _rel_bucket(dist):
    d = jnp.maximum(dist, 0)
    exact = N_BUCKETS // 2
    df = jnp.maximum(d, 1).astype(F32)
    large = exact + (jnp.log(df / exact) / math.log(MAX_DISTANCE / exact) * (N_BUCKETS - exact)).astype(I32)
    return jnp.where(d < exact, d, jnp.minimum(large, N_BUCKETS - 1))


def _num_near_tiles():
    exact = N_BUCKETS // 2
    d = np.arange(1, 4 * MAX_DISTANCE, dtype=np.float64)
    large = exact + np.floor(np.log(d / exact) / math.log(MAX_DISTANCE / exact) * (N_BUCKETS - exact))
    bucket = np.where(d < exact, d, np.minimum(large, N_BUCKETS - 1))
    d_const = int(d[np.argmax(bucket == N_BUCKETS - 1)])
    return -(-(d_const + TILE // 2 + TILE - 1) // TILE)


def _bias_kernel(u_ref, o_ref, *, n_tiles, d_top):
    for dt in range(n_tiles):
        start = d_top - dt * TILE - (TILE - 1)
        row = u_ref[0, :, start:start + 2 * TILE]
        x = jnp.broadcast_to(row, (TILE, 2 * TILE))
        x = pltpu.roll(x, TILE + 1, 1, stride=1, stride_axis=0)
        o_ref[dt] = x[:, :TILE]


def _bias_tiles(rel_table, nd):
    n_tiles = nd + 1
    d_top = n_tiles * TILE
    ul = d_top + 2 * TILE
    n_heads = rel_table.shape[1]
    dist = d_top - jnp.arange(ul)
    u = rel_table[_rel_bucket(dist)].T.reshape(n_heads, 1, ul)
    out = pl.pallas_call(
        functools.partial(_bias_kernel, n_tiles=n_tiles, d_top=d_top),
        grid=(n_heads,),
        in_specs=[pl.BlockSpec((1, 1, ul), lambda h: (h, 0, 0))],
        out_specs=pl.BlockSpec((None, n_tiles, None, None, TILE, TILE),
                               lambda h: (h // 8, 0, (h % 8) // 4, h % 4, 0, 0)),
        out_shape=jax.ShapeDtypeStruct((2, n_tiles, 2, 4, TILE, TILE), F32),
        compiler_params=_params(1), name='bias_tiles',
    )(u)
    return out.reshape(2, n_tiles, 2, 4 * TILE, TILE)


def _dot_nt(a, b):
    return lax.dot_general(a, b, (((1,), (1,)), ((), ())), preferred_element_type=F32)


def _flash_step(s, m, l, acc, v):
    m_new = jnp.maximum(m, jnp.max(s, axis=1, keepdims=True))
    alpha = jnp.exp(m - m_new)
    p = jnp.exp(s - m_new)
    l = alpha * l + jnp.sum(p, axis=1, keepdims=True)
    acc = alpha * acc + jnp.dot(p.astype(BF16), v, preferred_element_type=F32)
    return m_new, l, acc


def _flash_init(rows, width):
    return (jnp.full((rows, 1), NEG, F32), jnp.zeros((rows, 1), F32), jnp.zeros((rows, width), F32))


def _flash_out(m, l, acc):
    return jnp.where(m > 0.5 * NEG, acc / jnp.maximum(l, 1e-30), 0.0)


def _sortable_key(x):
    bits = pltpu.bitcast(x, I32)
    bits = jnp.where(bits == INT_MIN, 0, bits)
    return jnp.where(bits < 0, bits ^ 0x7FFFFFFF, bits)


def _kth_largest_key(count_ge, k, shape):
    def bit_body(i, t):
        cand = t + lax.shift_left(jnp.int32(1), 31 - i)
        return jnp.where(count_ge(cand) >= k, cand, t)
    return lax.fori_loop(0, 32, bit_body, jnp.full(shape, INT_MIN, I32))


def _attn_update(q_g, kk, vv, bias_of, madd, m_ref, l_ref, acc_ref, h0, n_heads):
    tk = kk.shape[0]
    s_all = _dot_nt(q_g, kk) * (HEAD_DIM ** -0.5)
    ps, alphas = [], []
    for r in range(n_heads):
        s = s_all[r * TILE:(r + 1) * TILE] + (bias_of(r) + madd)
        m_prev = m_ref[h0 + r]
        m_next = jnp.maximum(m_prev, jnp.max(s, axis=1, keepdims=True))
        alpha = jnp.exp(m_prev - m_next)
        p = jnp.exp(s - jnp.concatenate([m_next] * (tk // LANES), axis=1))
        l_ref[h0 + r] = alpha * l_ref[h0 + r] + jnp.sum(p, axis=1, keepdims=True)
        m_ref[h0 + r] = m_next
        ps.append(p.astype(BF16))
        alphas.append(alpha)
    pv = jnp.dot(jnp.concatenate(ps, axis=0), vv, preferred_element_type=F32)
    for r in range(n_heads):
        acc_ref[h0 + r] = alphas[r] * acc_ref[h0 + r] + pv[r * TILE:(r + 1) * TILE]


def _attn_reset(m_ref, l_ref, acc_ref):
    m_ref[...] = jnp.full(m_ref.shape, NEG, F32)
    l_ref[...] = jnp.zeros(l_ref.shape, F32)
    acc_ref[...] = jnp.zeros(acc_ref.shape, F32)


def _attn_out(m_ref, l_ref, acc_ref, h):
    return jnp.where(m_ref[h] > 0.5 * NEG, acc_ref[h] / jnp.maximum(l_ref[h], 1e-30), 0.0)


def _topn_mask(scores, n):
    colf = lax.broadcasted_iota(I32, scores[0].shape, 1).astype(F32)

    def body(_, carry):
        out = []
        for sc, selm in carry:
            mx = jnp.max(sc, axis=1, keepdims=True)
            first = jnp.min(jnp.where(sc == mx, colf, 1e9), axis=1, keepdims=True)
            hit = colf == first
            selm = jnp.maximum(selm, jnp.where(hit, jnp.where(mx > -jnp.inf, 1.0, 0.0), 0.0))
            out.append((jnp.where(hit, -jnp.inf, sc), selm))
        return tuple(out)

    init = tuple((sc, jnp.zeros(sc.shape, F32)) for sc in scores)
    return [c[1] for c in lax.fori_loop(0, n, body, init)]


def _softmax_rows(s, ok):
    s = jnp.where(ok, s, NEG)
    m = jnp.max(s, axis=1, keepdims=True)
    e = jnp.where(ok, jnp.exp(s - m), 0.0)
    return e / jnp.maximum(jnp.sum(e, axis=1, keepdims=True), 1e-30)


def _block_expand(blk0, n_blk, width=TILE):
    rb = lax.broadcasted_iota(I32, (n_blk, width), 0)
    cj = lax.broadcasted_iota(I32, (n_blk, width), 1)
    target = blk0 + lax.shift_right_logical(cj, int(math.log2(SEL_BLOCK)))
    return jnp.where(rb == target, 1.0, 0.0).astype(BF16)


def _gelu(x):
    return 0.5 * x * (1.0 + jnp.tanh(math.sqrt(2.0 / math.pi) * (x + 0.044715 * (x * x * x))))


def _layer_norm(x, g, b):
    xc = x - jnp.mean(x, axis=1, keepdims=True)
    var = jnp.mean(xc * xc, axis=1, keepdims=True)
    return xc * lax.rsqrt(var + LN_EPS) * g + b


def _cmpz_compute(load, w_ref, o_ref):
    rows = o_ref.shape[0]
    for kv in range(2):
        xs = [jnp.concatenate([load(p, g, kv) for p in range(CMP_STRIDE)], axis=1) for g in range(B_KV_HEADS)]
        z = jnp.dot(jnp.concatenate(xs, axis=0).astype(BF16), w_ref[kv], preferred_element_type=F32)
        for g in range(B_KV_HEADS):
            c = (g * 2 + kv) * CMP_RATIO * CMP_HID
            o_ref[:, c:c + CMP_RATIO * CMP_HID] = z[g * rows:(g + 1) * rows]


def _cmpz_kernel(x_ref, w_ref, o_ref):
    n_col = B_KV_HEADS * 2 * HEAD_DIM

    def load(p, g, kv):
        c0 = p * n_col + (g * 2 + kv) * HEAD_DIM
        return x_ref[:, c0:c0 + HEAD_DIM]

    _cmpz_compute(load, w_ref, o_ref)


def _cmpz_dense(x2d, w1cat, tc=256):
    n = x2d.shape[0]
    tc = min(tc, n)
    assert n % tc == 0
    return pl.pallas_call(
        _cmpz_kernel,
        grid=(n // tc,),
        in_specs=[pl.BlockSpec((tc, x2d.shape[1]), lambda i: (i, 0)),
                  _resident(w1cat.shape, lambda i: (0, 0, 0))],
        out_specs=pl.BlockSpec((tc, 4 * CMP_RATIO * CMP_HID), lambda i: (i, 0)),
        out_shape=jax.ShapeDtypeStruct((n, 4 * CMP_RATIO * CMP_HID), F32),
        compiler_params=_params(1), name='cmpz_dense',
    )(x2d, w1cat)


def _page_copy(pool_ref, buf_ref, sem_ref, pid, slot, k):
    rows = pool_ref.shape[1]
    return pltpu.make_async_copy(pool_ref.at[pid], buf_ref.at[slot, pl.ds(k * rows, rows)], sem_ref.at[slot])


def _page_fetch(pt_ref, pool_ref, buf_ref, sem_ref, step, slot):
    for k in range(PAGES_PER_STEP):
        _page_copy(pool_ref, buf_ref, sem_ref, pt_ref[step * PAGES_PER_STEP + k], slot, k).start()


def _page_wait(pool_ref, buf_ref, sem_ref, slot):
    for k in range(PAGES_PER_STEP):
        _page_copy(pool_ref, buf_ref, sem_ref, 0, slot, k).wait()


def _page_pipeline(pt_ref, pool_ref, buf_ref, sem_ref):
    step = pl.program_id(0) * pl.num_programs(1) + pl.program_id(1)
    total = pl.num_programs(0) * pl.num_programs(1)
    slot = lax.rem(step, 2)

    @pl.when(step == 0)
    def _():
        _page_fetch(pt_ref, pool_ref, buf_ref, sem_ref, step, slot)

    @pl.when(step + 1 < total)
    def _():
        _page_fetch(pt_ref, pool_ref, buf_ref, sem_ref, step + 1, 1 - slot)

    _page_wait(pool_ref, buf_ref, sem_ref, slot)
    return slot


def _cmpz_paged_kernel(pt_ref, pool_ref, w_ref, o_ref, buf_ref, sem_ref):
    slot = _page_pipeline(pt_ref, pool_ref, buf_ref, sem_ref)
    rows = o_ref.shape[0]

    def load(p, g, kv):
        return buf_ref[slot, pl.ds(p * KV_ROWS + g * 2 + kv, rows, stride=CMP_STRIDE * KV_ROWS), :]

    _cmpz_compute(load, w_ref, o_ref)


def _cmpz_paged(pool, page_table, w1cat):
    db, n_pages = page_table.shape
    chunks = PAGE_SIZE // CMP_STRIDE
    npg = n_pages // PAGES_PER_STEP
    rows = PAGES_PER_STEP * chunks
    return pl.pallas_call(
        _cmpz_paged_kernel,
        grid_spec=pltpu.PrefetchScalarGridSpec(
            num_scalar_prefetch=1,
            grid=(db, npg),
            in_specs=[pl.BlockSpec(memory_space=pl.ANY),
                      _resident(w1cat.shape, lambda b, g, pt: (0, 0, 0))],
            out_specs=pl.BlockSpec((rows, 4 * CMP_RATIO * CMP_HID), lambda b, g, pt: (b * npg + g, 0)),
            scratch_shapes=[pltpu.VMEM((2, PAGES_PER_STEP * pool.shape[1], pool.shape[2]), F32),
                            pltpu.SemaphoreType.DMA((2,))]),
        out_shape=jax.ShapeDtypeStruct((db * n_pages * chunks, 4 * CMP_RATIO * CMP_HID), F32),
        compiler_params=_params(2), name='cmpz_paged',
    )(page_table.reshape(-1), pool, w1cat)


def _cmp_finish_kernel(z_ref, pe_ref, w1_ref, w2_ref, k_ref, v_ref):
    n = z_ref.shape[1]
    for kv, o_ref in ((0, k_ref), (1, v_ref)):
        pew = jnp.dot(pe_ref[kv], w1_ref[kv], preferred_element_type=F32)[0:1]
        for g in range(B_KV_HEADS):
            c = (g * 2 + kv) * CMP_RATIO * CMP_HID
            z0 = z_ref[0, :, c:c + CMP_HID]
            z1 = z_ref[0, :, c + CMP_HID:c + 2 * CMP_HID]
            pre = z0 + pltpu.roll(z1, n - 1, 0) + pew
            out = jnp.dot(_gelu(pre).astype(BF16), w2_ref[kv], preferred_element_type=F32)
            o_ref[0, :, g * HEAD_DIM:(g + 1) * HEAD_DIM] = out.astype(o_ref.dtype)


def _cmp_finish(z3, pe8, w1, w2):
    nb, n, zc = z3.shape
    out = jax.ShapeDtypeStruct((nb, n, B_KV_HEADS * HEAD_DIM), BF16)
    return pl.pallas_call(
        _cmp_finish_kernel,
        grid=(nb,),
        in_specs=[pl.BlockSpec((1, n, zc), lambda b: (b, 0, 0)),
                  _resident(pe8.shape, lambda b: (0, 0, 0)),
                  _resident(w1.shape, lambda b: (0, 0, 0)),
                  _resident(w2.shape, lambda b: (0, 0, 0))],
        out_specs=[pl.BlockSpec((1, n, B_KV_HEADS * HEAD_DIM), lambda b: (b, 0, 0))] * 2,
        out_shape=[out, out],
        compiler_params=_params(1), name='cmp_finish',
    )(z3, pe8, w1, w2)


def _ka_kernel(iq_ref, ikw_ref, ikwq_ref, aq_ref, akv_ref, bias_ref, o_ref, keys_ref, m_ref, l_ref, acc_ref,
               *, topk, nd):
    qt = pl.program_id(1)
    last_pair = lax.shift_right_logical(qt, 1)
    key_row = lax.broadcasted_iota(I32, (PAIR, TILE), 0)
    q_pos = qt * TILE + lax.broadcasted_iota(I32, (PAIR, TILE), 1)
    w_t = ikwq_ref[...].T[IDX_DIM:IDX_DIM + IDX_HEADS] * (IDX_HEADS ** -0.5 * IDX_DIM ** -0.5)
    iq_all = jnp.concatenate([iq_ref[:, h * LANES:(h + 1) * LANES] for h in range(IDX_HEADS)], axis=0)

    def score_pair(kp, masked):
        kb = ikw_ref[pl.ds(pl.multiple_of(kp * PAIR, PAIR), PAIR), :].astype(BF16)
        s = _dot_nt(kb, iq_all)
        acc = jnp.zeros((PAIR, TILE), F32)
        for h in range(IDX_HEADS):
            acc = acc + jnp.maximum(s[:, h * TILE:(h + 1) * TILE], 0.0) * w_t[h:h + 1]
        key = _sortable_key(acc)
        if masked:
            key = jnp.where(kp * PAIR + key_row <= q_pos, key, INT_MIN)
        keys_ref[kp] = key

    def score_body(kp, c):
        score_pair(kp, False)
        return c

    lax.fori_loop(0, last_pair, score_body, 0)
    score_pair(last_pair, True)

    def count_where(pred):
        def body(kp, c):
            v = jnp.where(pred(keys_ref[kp], kp), 1.0, 0.0)
            return c + jnp.sum(v.reshape(4, PAIR // 4, TILE), axis=0)
        c = lax.fori_loop(0, last_pair + 1, body, jnp.zeros((PAIR // 4, TILE), F32))
        return jnp.sum(c, axis=0, keepdims=True)

    def count_ge(cand):
        return count_where(lambda k, kp: k >= cand)

    t = _kth_largest_key(count_ge, topk, (1, TILE))
    thr = jnp.maximum(t, INT_MIN + 1)

    cnt_gt = count_ge(thr + 1)
    need = topk - cnt_gt
    cnt_eq = count_ge(thr) - cnt_gt
    tie = jnp.where(t > INT_MIN, jnp.where(cnt_eq > need, 1.0, 0.0), 0.0)

    @pl.when(jnp.max(tie) > 0.0)
    def _():
        n_bits = int(keys_ref.shape[0] * PAIR).bit_length()

        def idx_body(i, mm):
            cand = mm + lax.shift_left(jnp.int32(1), n_bits - 1 - i)
            c = count_where(lambda k, kp: jnp.where(k == thr, kp * PAIR + key_row, INT_MIN) < cand)
            c = c - count_where(lambda k, kp: k != thr)
            return jnp.where(c < need, cand, mm)

        last = lax.fori_loop(0, n_bits, idx_body, jnp.zeros((1, TILE), I32))
        last = jnp.where(tie > 0.0, last, jnp.int32(2 ** 30))

        def demote(kp, c):
            k = keys_ref[kp]
            pos = jnp.where(k == thr, kp * PAIR + key_row, INT_MIN)
            keys_ref[kp] = jnp.where(pos > last, thr - 1, k)
            return c

        lax.fori_loop(0, last_pair + 1, demote, 0)

    q = [jnp.concatenate([aq_ref[:, (g * A_GROUP + r) * HEAD_DIM:(g * A_GROUP + r + 1) * HEAD_DIM]
                          for r in range(A_GROUP)], axis=0) for g in range(A_KV_HEADS)]
    _attn_reset(m_ref, l_ref, acc_ref)

    def att_body(kp, c):
        d0 = jnp.clip(qt - 2 * kp, 0, nd)
        d1 = jnp.clip(qt - 2 * kp - 1, 0, nd)
        madd = jnp.where(keys_ref[kp] >= thr, 0.0, NEG).T
        k0 = pl.multiple_of(kp * PAIR, PAIR)
        for g in range(A_KV_HEADS):
            kc = g * 2 * HEAD_DIM
            kk = akv_ref[pl.ds(k0, PAIR), kc:kc + HEAD_DIM].astype(BF16)
            vv = akv_ref[pl.ds(k0, PAIR), kc + HEAD_DIM:kc + 2 * HEAD_DIM].astype(BF16)

            def bias_of(r, g=g):
                rs = slice(r * TILE, (r + 1) * TILE)
                return jnp.concatenate([bias_ref[d0, g, rs, :], bias_ref[d1, g, rs, :]], axis=1)

            _attn_update(q[g], kk, vv, bias_of, madd, m_ref, l_ref, acc_ref, g * A_GROUP, A_GROUP)
        return c

    lax.fori_loop(0, last_pair + 1, att_body, 0)
    for h in range(A_HEADS):
        o_ref[:, h * HEAD_DIM:(h + 1) * HEAD_DIM] = _attn_out(m_ref, l_ref, acc_ref, h).astype(o_ref.dtype)


def _prompt_mixer_a(qmat, fmat, bias_a, bsz, t, cols, nd):
    nt = t // TILE
    assert nt % 2 == 0
    topk = min(TOPK_MAX, t // 4)
    qc, fc = cols['q'], cols['f']
    return pl.pallas_call(
        functools.partial(_ka_kernel, topk=topk, nd=nd),
        grid=(bsz, nt),
        in_specs=[
            pl.BlockSpec((TILE, IDX_HEADS * LANES), lambda b, i: (b * nt + i, qc['i_q'] // (IDX_HEADS * LANES))),
            _resident((t, LANES), lambda b, i: (b, fc['i_kw'] // LANES)),
            pl.BlockSpec((TILE, LANES), lambda b, i: (b * nt + i, fc['i_kw'] // LANES)),
            pl.BlockSpec((TILE, A_HEADS * HEAD_DIM), lambda b, i: (b * nt + i, qc['a_q'] // (A_HEADS * HEAD_DIM))),
            _resident((t, 4 * HEAD_DIM), lambda b, i: (b, fc['a_kv'] // (4 * HEAD_DIM))),
            _resident(bias_a.shape, lambda b, i: (0, 0, 0, 0)),
        ],
        out_specs=pl.BlockSpec((TILE, A_HEADS * HEAD_DIM), lambda b, i: (b * nt + i, 0)),
        out_shape=jax.ShapeDtypeStruct((bsz * t, A_HEADS * HEAD_DIM), BF16),
        scratch_shapes=[pltpu.VMEM((nt // 2, PAIR, TILE), I32)] + [pltpu.VMEM((A_HEADS, TILE, LANES), F32)] * 3,
        compiler_params=_params(2), name='prompt_mixer_a',
    )(qmat, fmat, fmat, qmat, fmat, bias_a)


def _kb_kernel(bq_ref, gate_ref, ck_ref, cv_ref, ov_ref, sel_ref, w0_ref, w1_ref, w2_ref, w3_ref, w4_ref,
               bias_ref, o_ref, m_ref, l_ref, acc_ref, ocmp_ref, osel_ref, *, nd, n_sel):
    qt = pl.program_id(1)
    last_pair = lax.shift_right_logical(qt, 1)
    rows = B_GROUP * TILE
    row = lax.broadcasted_iota(I32, (TILE, TILE), 0)
    col = lax.broadcasted_iota(I32, (TILE, TILE), 1)
    colp = lax.broadcasted_iota(I32, (TILE, PAIR), 1)
    q_pos_p = qt * TILE + lax.broadcasted_iota(I32, (TILE, PAIR), 0)
    scale = HEAD_DIM ** -0.5
    q = [jnp.concatenate([bq_ref[:, (g * B_GROUP + r) * HEAD_DIM:(g * B_GROUP + r + 1) * HEAD_DIM]
                          for r in range(B_GROUP)], axis=0) for g in range(B_KV_HEADS)]

    ncp = ck_ref.shape[1]
    q_pos = qt * TILE + lax.rem(lax.broadcasted_iota(I32, (rows, ncp), 0), TILE)
    cmp_end = lax.broadcasted_iota(I32, (rows, ncp), 1) * CMP_STRIDE + (CMP_BLOCK - 1)
    cmp_ok = cmp_end <= q_pos
    cur = 2 * qt + jnp.where(row >= SEL_BLOCK, 1, 0)
    scores = []
    for g in range(B_KV_HEADS):
        p = _softmax_rows(_dot_nt(q[g], ck_ref[0, :, g * HEAD_DIM:(g + 1) * HEAD_DIM]) * scale, cmp_ok)
        o_cmp = jnp.dot(p.astype(BF16), cv_ref[0, :, g * HEAD_DIM:(g + 1) * HEAD_DIM], preferred_element_type=F32)
        psum = p[0:TILE]
        ocmp_ref[g * B_GROUP] = o_cmp[0:TILE]
        for r in range(1, B_GROUP):
            psum = psum + p[r * TILE:(r + 1) * TILE]
            ocmp_ref[g * B_GROUP + r] = o_cmp[r * TILE:(r + 1) * TILE]
        imp = jnp.dot(psum, ov_ref[...], preferred_element_type=F32, precision=lax.Precision.HIGHEST)
        forced = jnp.where(col == 0, jnp.inf, jnp.where(col >= cur - 1, jnp.inf, imp))
        scores.append(jnp.where(col <= cur, forced, -jnp.inf))
    selm = [s.astype(BF16) for s in _topn_mask(scores, n_sel)]

    _attn_reset(m_ref, l_ref, acc_ref)

    def sel_pair(kp, masked):
        d0 = jnp.clip(qt - 2 * kp, 0, nd)
        d1 = jnp.clip(qt - 2 * kp - 1, 0, nd)
        k0 = pl.multiple_of(kp * PAIR, PAIR)
        expand = _block_expand((PAIR // SEL_BLOCK) * kp, selm[0].shape[1], PAIR)
        for g in range(B_KV_HEADS):
            madd = (jnp.dot(selm[g], expand, preferred_element_type=F32) - 1.0) * (-NEG)
            if masked:
                madd = jnp.where(kp * PAIR + colp <= q_pos_p, madd, NEG)
            kc = g * 2 * HEAD_DIM
            kk = sel_ref[pl.ds(k0, PAIR), kc:kc + HEAD_DIM].astype(BF16)
            vv = sel_ref[pl.ds(k0, PAIR), kc + HEAD_DIM:kc + 2 * HEAD_DIM].astype(BF16)

            def bias_of(r, g=g):
                rs = slice(r * TILE, (r + 1) * TILE)
                return jnp.concatenate([bias_ref[d0, g, rs, :], bias_ref[d1, g, rs, :]], axis=1)

            _attn_update(q[g], kk, vv, bias_of, madd, m_ref, l_ref, acc_ref, g * B_GROUP, B_GROUP)

    def sel_body(kp, c):
        sel_pair(kp, False)
        return c

    lax.fori_loop(0, last_pair, sel_body, 0)
    sel_pair(last_pair, True)
    for h in range(B_HEADS):
        osel_ref[h] = _attn_out(m_ref, l_ref, acc_ref, h)

    _attn_reset(m_ref, l_ref, acc_ref)
    for k, w_ref in enumerate((w0_ref, w1_ref, w2_ref, w3_ref, w4_ref)):
        if k == 0:
            ok = col <= row
        elif k == WINDOW // TILE:
            ok = row <= col
        else:
            ok = col >= 0
        madd = jnp.where(ok, jnp.where(qt >= k, 0.0, NEG), NEG)
        for g in range(B_KV_HEADS):
            kc = g * 2 * HEAD_DIM
            kk = w_ref[:, kc:kc + HEAD_DIM].astype(BF16)
            vv = w_ref[:, kc + HEAD_DIM:kc + 2 * HEAD_DIM].astype(BF16)

            def bias_of(r, g=g, k=k):
                return bias_ref[k, g, r * TILE:(r + 1) * TILE, :]

            _attn_update(q[g], kk, vv, bias_of, madd, m_ref, l_ref, acc_ref, g * B_GROUP, B_GROUP)

    gate = jax.nn.sigmoid(gate_ref[...])
    for h in range(B_HEADS):
        o = (gate[:, 3 * h:3 * h + 1] * ocmp_ref[h] + gate[:, 3 * h + 1:3 * h + 2] * osel_ref[h]
             + gate[:, 3 * h + 2:3 * h + 3] * _attn_out(m_ref, l_ref, acc_ref, h))
        o_ref[:, h * HEAD_DIM:(h + 1) * HEAD_DIM] = o.astype(o_ref.dtype)


def _overlap_matrix(length, n_rows, n_cols):
    nc = (length - CMP_BLOCK) // CMP_STRIDE + 1
    ns = -(-length // SEL_BLOCK)
    cs = np.arange(nc) * CMP_STRIDE
    ss = np.arange(ns) * SEL_BLOCK
    ov = np.minimum(cs[:, None] + CMP_BLOCK, ss[None, :] + SEL_BLOCK) - np.maximum(cs[:, None], ss[None, :])
    out = np.zeros((n_rows, n_cols), np.float32)
    out[:nc, :ns] = np.clip(ov, 0, None).astype(np.float32) / CMP_BLOCK
    return jnp.asarray(out)


def _prompt_mixer_b(qmat, fmat, cmp_k, cmp_v, bias_b, bsz, t, cols, nd):
    nt = t // TILE
    ns = -(-t // SEL_BLOCK)
    assert ns <= LANES and WINDOW // TILE == 4 and nd >= WINDOW // TILE
    qc, fc = cols['q'], cols['f']
    ncp = cmp_k.shape[1]
    ov = _overlap_matrix(t, ncp, LANES)
    kvw = 4 * HEAD_DIM

    def win_spec(k):
        return pl.BlockSpec((TILE, kvw), lambda b, i: (b * nt + jnp.maximum(i - k, 0), fc['b_win'] // kvw))

    return pl.pallas_call(
        functools.partial(_kb_kernel, nd=nd, n_sel=min(SEL_TOPN, ns)),
        grid=(bsz, nt),
        in_specs=[
            pl.BlockSpec((TILE, B_HEADS * HEAD_DIM), lambda b, i: (b * nt + i, qc['b_q'] // (B_HEADS * HEAD_DIM))),
            pl.BlockSpec((TILE, LANES), lambda b, i: (b * nt + i, fc['b_gate'] // LANES)),
            _resident((1, ncp, B_KV_HEADS * HEAD_DIM), lambda b, i: (b, 0, 0)),
            _resident((1, ncp, B_KV_HEADS * HEAD_DIM), lambda b, i: (b, 0, 0)),
            _resident(ov.shape, lambda b, i: (0, 0)),
            _resident((t, kvw), lambda b, i: (b, fc['b_sel'] // kvw)),
            win_spec(0), win_spec(1), win_spec(2), win_spec(3), win_spec(4),
            _resident(bias_b.shape, lambda b, i: (0, 0, 0, 0)),
        ],
        out_specs=pl.BlockSpec((TILE, B_HEADS * HEAD_DIM), lambda b, i: (b * nt + i, 0)),
        out_shape=jax.ShapeDtypeStruct((bsz * t, B_HEADS * HEAD_DIM), BF16),
        scratch_shapes=[pltpu.VMEM((B_HEADS, TILE, LANES), F32)] * 5,
        compiler_params=_params(2), name='prompt_mixer_b',
    )(qmat, fmat, cmp_k, cmp_v, ov, fmat, fmat, fmat, fmat, fmat, fmat, bias_b)


def _mem_kernel(q_ref, kv_ref, o_ref):
    scale = MEM_HEAD_DIM ** -0.5
    for h in range(MEM_HEADS):
        c = h * 2 * MEM_HEAD_DIM
        kk = kv_ref[:, c:c + MEM_HEAD_DIM].astype(BF16)
        vv = kv_ref[:, c + MEM_HEAD_DIM:c + 2 * MEM_HEAD_DIM].astype(BF16)
        s = _dot_nt(q_ref[:, h * MEM_HEAD_DIM:(h + 1) * MEM_HEAD_DIM], kk) * scale
        e = jnp.exp(s - jnp.max(s, axis=1, keepdims=True))
        p = e / jnp.sum(e, axis=1, keepdims=True)
        o = jnp.dot(p.astype(BF16), vv, preferred_element_type=F32)
        o_ref[:, h * MEM_HEAD_DIM:(h + 1) * MEM_HEAD_DIM] = o.astype(o_ref.dtype)


def _mem_attend(qmat, q_col, mem_kv2d, n_batch, rows_per_batch, n_mem, tq):
    width = MEM_HEADS * MEM_HEAD_DIM
    tq = min(tq, rows_per_batch)
    nq = rows_per_batch // tq
    return pl.pallas_call(
        _mem_kernel,
        grid=(n_batch, nq),
        in_specs=[pl.BlockSpec((tq, width), lambda b, i: (b * nq + i, q_col // width)),
                  pl.BlockSpec((n_mem, 2 * width), lambda b, i: (b, 0))],
        out_specs=pl.BlockSpec((tq, width), lambda b, i: (b * nq + i, 0)),
        out_shape=jax.ShapeDtypeStruct((n_batch * rows_per_batch, width), BF16),
        compiler_params=_params(2), name='mem_attend',
    )(qmat, mem_kv2d)


def _gated_proj_kernel(ga_ref, gb_ref, gm_ref, oa_ref, ob_ref, om_ref, wpa_ref, wpb_ref, wpm_ref, o_ref):
    merged = jax.nn.sigmoid(ga_ref[...]) * jnp.dot(oa_ref[...], wpa_ref[...], preferred_element_type=F32)
    merged = merged + jax.nn.sigmoid(gb_ref[...]) * jnp.dot(ob_ref[...], wpb_ref[...], preferred_element_type=F32)
    merged = merged + jax.nn.sigmoid(gm_ref[...]) * jnp.dot(om_ref[...], wpm_ref[...], preferred_element_type=F32)
    o_ref[...] = merged.astype(o_ref.dtype)


def _out_proj_kernel(x_ref, mg_ref, wo_ref, lg_ref, lb_ref, o_ref, *, alpha):
    y = alpha * x_ref[...] + jnp.dot(mg_ref[...], wo_ref[...], preferred_element_type=F32)
    o_ref[...] = _layer_norm(y, lg_ref[...], lb_ref[...])


def _merge(x2d, gmat, oa, ob, om, wpa, wpb, wpm, wo, ln_g, ln_b, alpha, tm=512, tn=1024):
    m, d = x2d.shape
    tm = min(tm, m)
    assert m % tm == 0 and d % tn == 0
    nj = d // tn
    row = lambda i, j: (i, 0)
    wcol = lambda i, j: (0, j)
    merged = pl.pallas_call(
        _gated_proj_kernel,
        grid=(m // tm, nj),
        in_specs=[pl.BlockSpec((tm, tn), lambda i, j: (i, j)),
                  pl.BlockSpec((tm, tn), lambda i, j: (i, nj + j)),
                  pl.BlockSpec((tm, tn), lambda i, j: (i, 2 * nj + j)),
                  pl.BlockSpec((tm, oa.shape[1]), row), pl.BlockSpec((tm, ob.shape[1]), row),
                  pl.BlockSpec((tm, om.shape[1]), row),
                  pl.BlockSpec((wpa.shape[0], tn), wcol), pl.BlockSpec((wpb.shape[0], tn), wcol),
                  pl.BlockSpec((wpm.shape[0], tn), wcol)],
        out_specs=pl.BlockSpec((tm, tn), lambda i, j: (i, j)),
        out_shape=jax.ShapeDtypeStruct((m, d), BF16),
        compiler_params=_params(2), name='gated_proj',
    )(gmat, gmat, gmat, oa, ob, om, wpa, wpb, wpm)
    fixed = lambda i: (0, 0)
    return pl.pallas_call(
        functools.partial(_out_proj_kernel, alpha=alpha),
        grid=(m // tm,),
        in_specs=[pl.BlockSpec((tm, d), lambda i: (i, 0)), pl.BlockSpec((tm, d), lambda i: (i, 0)),
                  _resident(wo.shape, fixed), _resident((1, d), fixed), _resident((1, d), fixed)],
        out_specs=pl.BlockSpec((tm, d), lambda i: (i, 0)),
        out_shape=jax.ShapeDtypeStruct((m, d), F32),
        compiler_params=_params(1), name='out_proj_ln',
    )(x2d, merged, wo, ln_g, ln_b)


def _ffn_kernel(x_ref, wu_ref, bu_ref, wd_ref, bd_ref, lg_ref, lb_ref, o_ref, acc_ref, *, alpha):
    j = pl.program_id(1)

    @pl.when(j == 0)
    def _():
        acc_ref[...] = jnp.zeros_like(acc_ref)

    u = jnp.dot(x_ref[...].astype(BF16), wu_ref[...], preferred_element_type=F32) + bu_ref[...]
    u = jnp.square(jnp.maximum(u, 0.0))
    acc_ref[...] += jnp.dot(u.astype(BF16), wd_ref[...], preferred_element_type=F32)

    @pl.when(j == pl.num_programs(1) - 1)
    def _():
        y = alpha * x_ref[...] + acc_ref[...] + bd_ref[...]
        o_ref[...] = _layer_norm(y, lg_ref[...], lb_ref[...])


def _ffn(x2d, wu, bu, wd, bd, ln_g, ln_b, alpha, tm=512, tf=1024):
    m, d = x2d.shape
    dff = wu.shape[1]
    tm = min(tm, m)
    assert m % tm == 0 and dff % tf == 0
    return pl.pallas_call(
        functools.partial(_ffn_kernel, alpha=alpha),
        grid=(m // tm, dff // tf),
        in_specs=[pl.BlockSpec((tm, d), lambda i, j: (i, 0)),
                  pl.BlockSpec((d, tf), lambda i, j: (0, j)), pl.BlockSpec((1, tf), lambda i, j: (0, j)),
                  pl.BlockSpec((tf, d), lambda i, j: (j, 0)),
                  _resident((1, d), lambda i, j: (0, 0)), _resident((1, d), lambda i, j: (0, 0)),
                  _resident((1, d), lambda i, j: (0, 0))],
        out_specs=pl.BlockSpec((tm, d), lambda i, j: (i, 0)),
        out_shape=jax.ShapeDtypeStruct((m, d), F32),
        scratch_shapes=[pltpu.VMEM((tm, d), F32)],
        compiler_params=_params(2), name='ffn_ln',
    )(x2d, wu, bu, wd, bd, ln_g, ln_b)


def _sidx_kernel(pt_ref, iq_ref, w_ref, knew_ref, pool_ref, o_ref, buf_ref, sem_ref, keys_ref, *, topk, n_q, past):
    g = pl.program_id(1)
    slot = _page_pipeline(pt_ref, pool_ref, buf_ref, sem_ref)
    span = PAGES_PER_STEP * PAGE_SIZE

    def scores(kb, width):
        s = _dot_nt(iq_ref[0], kb.astype(BF16))
        acc = jnp.zeros((n_q, width), F32)
        for h in range(IDX_HEADS):
            acc = acc + jnp.maximum(s[h * n_q:(h + 1) * n_q], 0.0) * w_ref[0, h * n_q:(h + 1) * n_q, 0:1]
        return _sortable_key(acc)

    keys_ref[:, pl.ds(pl.multiple_of(g * span, span), span)] = scores(buf_ref[slot], span)

    @pl.when(g == pl.num_programs(1) - 1)
    def _():
        lp = keys_ref.shape[1]
        rown = lax.broadcasted_iota(I32, (n_q, TILE), 0)
        coln = lax.broadcasted_iota(I32, (n_q, TILE), 1)
        keys_ref[:, past:lp] = jnp.where(coln <= rown, scores(knew_ref[0], TILE), INT_MIN)
        keys = keys_ref[...]
        pos = lax.broadcasted_iota(I32, (n_q, lp), 1)

        def count(pred):
            return jnp.sum(jnp.where(pred, 1.0, 0.0), axis=1, keepdims=True)

        t = _kth_largest_key(lambda cand: count(keys >= cand), topk, (n_q, 1))
        thr = jnp.maximum(t, INT_MIN + 1)
        cnt_gt = count(keys >= thr + 1)
        need = topk - cnt_gt
        cnt_eq = count(keys >= thr) - cnt_gt
        tie = jnp.where(t > INT_MIN, jnp.where(cnt_eq > need, 1.0, 0.0), 0.0)
        eq_pos = jnp.where(keys == thr, pos, jnp.int32(2 ** 30))
        n_bits = int(lp).bit_length()

        def idx_body(i, mm):
            cand = mm + lax.shift_left(jnp.int32(1), n_bits - 1 - i)
            return jnp.where(count(eq_pos < cand) < need, cand, mm)

        last = lax.fori_loop(0, n_bits, idx_body, jnp.zeros((n_q, 1), I32))
        last = jnp.where(tie > 0.0, last, jnp.int32(2 ** 30))
        sel = jnp.where(keys > thr, 1.0, jnp.where(keys == thr, jnp.where(pos <= last, 1.0, 0.0), 0.0))
        o_ref[0, 0] = sel


def _sample_index_mask(page_table, iq_s, w_s, knew, pool_idx, topk):
    db, n_pages = page_table.shape
    n_q = iq_s.shape[1] // IDX_HEADS
    npg = n_pages // PAGES_PER_STEP
    lp = n_pages * PAGE_SIZE + TILE
    return pl.pallas_call(
        functools.partial(_sidx_kernel, topk=topk, n_q=n_q, past=n_pages * PAGE_SIZE),
        grid_spec=pltpu.PrefetchScalarGridSpec(
            num_scalar_prefetch=1,
            grid=(db, npg),
            in_specs=[pl.BlockSpec((1,) + iq_s.shape[1:], lambda b, g, pt: (b, 0, 0)),
                      pl.BlockSpec((1,) + w_s.shape[1:], lambda b, g, pt: (b, 0, 0)),
                      pl.BlockSpec((1,) + knew.shape[1:], lambda b, g, pt: (b, 0, 0)),
                      pl.BlockSpec(memory_space=pl.ANY)],
            out_specs=pl.BlockSpec((1, 1, n_q, lp), lambda b, g, pt: (b, 0, 0, 0)),
            scratch_shapes=[pltpu.VMEM((2, PAGES_PER_STEP * pool_idx.shape[1], pool_idx.shape[2]), F32),
                            pltpu.SemaphoreType.DMA((2,)),
                            pltpu.VMEM((n_q, lp), I32)]),
        out_shape=jax.ShapeDtypeStruct((db, 1, n_q, lp), F32),
        compiler_params=_params(2), name='sample_index_mask',
    )(page_table.reshape(-1), iq_s, w_s, knew, pool_idx)


def _pattn_kernel(pt_ref, q_ref, mask_ref, new_ref, bias_ref, pool_ref, o_ref, buf_ref, sem_ref,
                  m_ref, l_ref, acc_ref, *, nd, n_q, n_pages):
    g = pl.program_id(1)
    npg = pl.num_programs(1)
    slot = _page_pipeline(pt_ref, pool_ref, buf_ref, sem_ref)
    rows = q_ref.shape[1]
    half = rows // 2
    scale = HEAD_DIM ** -0.5
    mask_groups = mask_ref.shape[1]

    @pl.when(g == 0)
    def _():
        m0, l0, a0 = _flash_init(rows, 2 * HEAD_DIM)
        m_ref[...] = m0
        l_ref[...] = l0
        acc_ref[...] = a0

    def block_update(rows_of, p0, n_tiles, carry):
        kk = jnp.concatenate([rows_of(0), rows_of(2)], axis=1).astype(BF16)
        vv = jnp.concatenate([rows_of(1), rows_of(3)], axis=1).astype(BF16)
        width = n_tiles * TILE
        mk = mask_ref[0, :, :, pl.ds(pl.multiple_of(p0 * TILE, TILE), width)]
        madd = (mk - 1.0) * (-NEG)
        reps = rows // (mask_groups * n_q)
        madd = jnp.concatenate([madd[i] for i in range(mask_groups) for _ in range(reps)], axis=0)
        bias = jnp.concatenate([bias_ref[jnp.clip(n_pages - (p0 + i), 0, nd)] for i in range(n_tiles)], axis=1)
        s = _dot_nt(q_ref[0], kk) * scale + (bias + madd)
        return _flash_step(s, *carry, vv)

    carry = (m_ref[...], l_ref[...], acc_ref[...])
    keys = PAGES_PER_BLOCK * PAGE_SIZE
    for blk in range(PAGES_PER_STEP // PAGES_PER_BLOCK):
        def rows_of(j, blk=blk):
            return buf_ref[slot, pl.ds(blk * keys * KV_ROWS + j, keys, stride=KV_ROWS), :]
        carry = block_update(rows_of, g * PAGES_PER_STEP + blk * PAGES_PER_BLOCK, PAGES_PER_BLOCK, carry)
    m_ref[...], l_ref[...], acc_ref[...] = carry

    @pl.when(g == npg - 1)
    def _():
        def new_rows(j):
            return new_ref[0, :, j * HEAD_DIM:(j + 1) * HEAD_DIM]
        o = _flash_out(*block_update(new_rows, n_pages, 1, carry))
        o_ref[0, 0:half] = o[0:half, 0:HEAD_DIM]
        o_ref[0, half:rows] = o[half:rows, HEAD_DIM:2 * HEAD_DIM]


def _paged_attention(page_table, qblk, mask, new_kv, bias_s, pool, nd, n_q):
    db, n_pages = page_table.shape
    npg = n_pages // PAGES_PER_STEP
    rows = qblk.shape[1]
    return pl.pallas_call(
        functools.partial(_pattn_kernel, nd=nd, n_q=n_q, n_pages=n_pages),
        grid_spec=pltpu.PrefetchScalarGridSpec(
            num_scalar_prefetch=1,
            grid=(db, npg),
            in_specs=[pl.BlockSpec((1,) + qblk.shape[1:], lambda b, g, pt: (b, 0, 0)),
                      pl.BlockSpec((1,) + mask.shape[1:], lambda b, g, pt: (b, 0, 0, 0)),
                      pl.BlockSpec((1,) + new_kv.shape[1:], lambda b, g, pt: (b, 0, 0)),
                      _resident(bias_s.shape, lambda b, g, pt: (0, 0, 0)),
                      pl.BlockSpec(memory_space=pl.ANY)],
            out_specs=pl.BlockSpec((1, rows, HEAD_DIM), lambda b, g, pt: (b, 0, 0)),
            scratch_shapes=[pltpu.VMEM((2, PAGES_PER_STEP * pool.shape[1], pool.shape[2]), F32),
                            pltpu.SemaphoreType.DMA((2,)),
                            pltpu.VMEM((rows, 1), F32), pltpu.VMEM((rows, 1), F32),
                            pltpu.VMEM((rows, 2 * HEAD_DIM), F32)]),
        out_shape=jax.ShapeDtypeStruct((db, rows, HEAD_DIM), F32),
        compiler_params=_params(2), name='paged_attention',
    )(page_table.reshape(-1), qblk, mask, new_kv, bias_s, pool)


def _scmp_kernel(q_ref, ck_ref, cv_ref, ov_ref, ocmp_ref, mask_ref, selm_ref, *, past, n_q, n_sel):
    rows = q_ref.shape[2]
    ncp = ck_ref.shape[1]
    nsp = ov_ref.shape[1]
    scale = HEAD_DIM ** -0.5
    qi = lax.rem(lax.broadcasted_iota(I32, (rows, ncp), 0), n_q)
    cmp_end = lax.broadcasted_iota(I32, (rows, ncp), 1) * CMP_STRIDE + (CMP_BLOCK - 1)
    cmp_ok = cmp_end <= past + qi
    blk = lax.broadcasted_iota(I32, (n_q, nsp), 1)
    cur = lax.shift_right_logical(past + lax.broadcasted_iota(I32, (n_q, nsp), 0), int(math.log2(SEL_BLOCK)))
    q_pos = past + lax.broadcasted_iota(I32, (n_q, TILE), 0)
    coln = lax.broadcasted_iota(I32, (n_q, TILE), 1)
    scores = []
    for g in range(B_KV_HEADS):
        p = _softmax_rows(_dot_nt(q_ref[0, g], ck_ref[0, :, g * HEAD_DIM:(g + 1) * HEAD_DIM]) * scale, cmp_ok)
        ocmp_ref[0, g] = jnp.dot(p.astype(BF16), cv_ref[0, :, g * HEAD_DIM:(g + 1) * HEAD_DIM],
                                 preferred_element_type=F32)
        psum = p[0:n_q]
        for r in range(1, rows // n_q):
            psum = psum + p[r * n_q:(r + 1) * n_q]
        imp = jnp.dot(psum, ov_ref[...], preferred_element_type=F32, precision=lax.Precision.HIGHEST)
        forced = jnp.where(blk == 0, jnp.inf, jnp.where(blk >= cur - 1, jnp.inf, imp))
        scores.append(jnp.where(blk <= cur, forced, -jnp.inf))
    for g, selm in enumerate(_topn_mask(scores, n_sel)):
        selm_ref[g] = selm

    blocks_per_tile = TILE // SEL_BLOCK
    half = lax.shift_right_logical(coln, int(math.log2(SEL_BLOCK)))

    def expand_body(kt, c):
        blk0 = blocks_per_tile * kt
        win0 = pl.multiple_of(lax.shift_right_logical(blk0, int(math.log2(LANES))) * LANES, LANES)
        idx = (blk0 - win0) + half
        k0 = pl.multiple_of(kt * TILE, TILE)
        for g in range(B_KV_HEADS):
            e = jnp.take_along_axis(selm_ref[g, :, pl.ds(win0, LANES)], idx, axis=1)
            mask_ref[0, g, :, pl.ds(k0, TILE)] = jnp.where(kt * TILE + coln <= q_pos, e, 0.0)
        return c

    lax.fori_loop(0, mask_ref.shape[3] // TILE, expand_body, 0)


def _sample_cmp_select(bq_s, cmp_k, cmp_v, length, past, n_q):
    db = bq_s.shape[0]
    ncp = cmp_k.shape[1]
    ns = -(-length // SEL_BLOCK)
    nsp = -(-ns // LANES) * LANES
    ov = _overlap_matrix(length, ncp, nsp)
    lp = past + TILE
    return pl.pallas_call(
        functools.partial(_scmp_kernel, past=past, n_q=n_q, n_sel=min(SEL_TOPN, ns)),
        grid=(db,),
        in_specs=[pl.BlockSpec((1,) + bq_s.shape[1:], lambda b: (b, 0, 0, 0)),
                  pl.BlockSpec((1, ncp, B_KV_HEADS * HEAD_DIM), lambda b: (b, 0, 0)),
                  pl.BlockSpec((1, ncp, B_KV_HEADS * HEAD_DIM), lambda b: (b, 0, 0)),
                  _resident(ov.shape, lambda b: (0, 0))],
        out_specs=[pl.BlockSpec((1,) + bq_s.shape[1:], lambda b: (b, 0, 0, 0)),
                   pl.BlockSpec((1, B_KV_HEADS, n_q, lp), lambda b: (b, 0, 0, 0))],
        out_shape=[jax.ShapeDtypeStruct(bq_s.shape, F32),
                   jax.ShapeDtypeStruct((db, B_KV_HEADS, n_q, lp), F32)],
        scratch_shapes=[pltpu.VMEM((B_KV_HEADS, n_q, nsp), F32)],
        compiler_params=_params(1), name='sample_cmp_select',
    )(bq_s, cmp_k, cmp_v, ov)


def _swin_kernel(q_ref, win_ref, new_ref, bias_ref, gate_ref, ocmp_ref, osel_ref, o_ref, *, n_q):
    rows = q_ref.shape[1]
    half = rows // 2
    wb = win_ref.shape[1]
    scale = HEAD_DIM ** -0.5
    qi = lax.rem(lax.broadcasted_iota(I32, (rows, TILE), 0), n_q)
    col = lax.broadcasted_iota(I32, (rows, TILE), 1)
    carry = _flash_init(rows, 2 * HEAD_DIM)

    def tile_update(kv, dlt, ok, carry):
        kk = jnp.concatenate([kv[:, 0:HEAD_DIM], kv[:, 2 * HEAD_DIM:3 * HEAD_DIM]], axis=1).astype(BF16)
        vv = jnp.concatenate([kv[:, HEAD_DIM:2 * HEAD_DIM], kv[:, 3 * HEAD_DIM:4 * HEAD_DIM]], axis=1).astype(BF16)
        s = _dot_nt(q_ref[0], kk) * scale + (bias_ref[dlt] + jnp.where(ok, 0.0, NEG))
        return _flash_step(s, *carry, vv)

    for kt in range(wb // TILE):
        ok = col + kt * TILE >= qi + (wb - WINDOW)
        carry = tile_update(win_ref[0, kt * TILE:(kt + 1) * TILE, :], wb // TILE - kt, ok, carry)
    carry = tile_update(new_ref[0], 0, col <= qi, carry)
    o = _flash_out(*carry)
    gate = jax.nn.sigmoid(gate_ref[0])
    for h in range(B_HEADS):
        sl = slice(h * n_q, (h + 1) * n_q)
        ow = o[sl, 0:HEAD_DIM] if h < B_GROUP else o[sl, HEAD_DIM:2 * HEAD_DIM]
        o_ref[0, :, h * HEAD_DIM:(h + 1) * HEAD_DIM] = (
            gate[:, 3 * h:3 * h + 1] * ocmp_ref[0, sl] + gate[:, 3 * h + 1:3 * h + 2] * osel_ref[0, sl]
            + gate[:, 3 * h + 2:3 * h + 3] * ow).astype(o_ref.dtype)


def _sample_window_combine(qblk, win_state, new_win, bias_s, gates, o_cmp, o_sel, n_q):
    db, rows, _ = qblk.shape
    return pl.pallas_call(
        functools.partial(_swin_kernel, n_q=n_q),
        grid=(db,),
        in_specs=[pl.BlockSpec((1,) + qblk.shape[1:], lambda b: (b, 0, 0)),
                  pl.BlockSpec((1,) + win_state.shape[1:], lambda b: (b, 0, 0)),
                  pl.BlockSpec((1,) + new_win.shape[1:], lambda b: (b, 0, 0)),
                  _resident(bias_s.shape, lambda b: (0, 0, 0)),
                  pl.BlockSpec((1,) + gates.shape[1:], lambda b: (b, 0, 0)),
                  pl.BlockSpec((1, rows, HEAD_DIM), lambda b: (b, 0, 0)),
                  pl.BlockSpec((1, rows, HEAD_DIM), lambda b: (b, 0, 0))],
        out_specs=pl.BlockSpec((1, n_q, B_HEADS * HEAD_DIM), lambda b: (b, 0, 0)),
        out_shape=jax.ShapeDtypeStruct((db, n_q, B_HEADS * HEAD_DIM), BF16),
        compiler_params=_params(1), name='sample_window_combine',
    )(qblk, win_state, new_win, bias_s, gates, o_cmp, o_sel)


def _pack_weights(w_in):
    d = w_in.shape[0]
    sizes = _split_sizes(d)
    off, o = {}, 0
    for name in _GROUPS:
        off[name] = o
        o += sizes[name]
    take = lambda name: w_in[:, off[name]:off[name] + sizes[name]]
    zeros = lambda n: jnp.zeros((d, n), w_in.dtype)
    iq = take('i_q').reshape(d, IDX_HEADS, IDX_DIM)
    iq = jnp.concatenate([iq, jnp.zeros_like(iq)], axis=2).reshape(d, IDX_HEADS * LANES)
    w_f = jnp.concatenate([take('a_kv'), take('b_cmp'), take('b_sel'), take('b_win'),
                           take('i_k'), take('i_w'), zeros(LANES - IDX_DIM - IDX_HEADS),
                           take('b_gate'), zeros(LANES - B_HEADS * 3)], axis=1).astype(BF16)
    w_q = jnp.concatenate([take('a_q'), take('b_q'), take('m_q'), iq], axis=1).astype(BF16)
    w_g = take('g_merge').astype(BF16)
    kvw = 4 * HEAD_DIM
    cols = dict(f=dict(a_kv=0, b_cmp=kvw, b_sel=2 * kvw, b_win=3 * kvw, i_kw=4 * kvw, b_gate=4 * kvw + LANES),
                q=dict(a_q=0, b_q=1024, m_q=2048, i_q=3072))
    return w_f, w_q, w_g, cols


def kernel(x_prompt, x_sample, mem_prompt, cache_a_kv, cache_a_idx, cache_b_cmp, cache_b_sel, state_b_win,
           cache_mem, page_table, rel_table, w_in, w_mem_kv, cmp_pe_k, cmp_w1_k, cmp_w2_k, cmp_pe_v, cmp_w1_v,
           cmp_w2_v, w_pa, w_pb, w_pm, w_o, ln1_g, ln1_b, w_up, b_up, w_down, b_down, ln2_g, ln2_b):
    depth = w_in.shape[0]
    assert depth == 1
    bsz, t, d = x_prompt.shape
    db, ds, _ = x_sample.shape
    n_mem = mem_prompt.shape[1]
    n_pool = cache_a_kv.shape[1]
    n_pages = page_table.shape[1]
    past = n_pages * PAGE_SIZE
    wb = state_b_win.shape[2]
    alpha = (2 * depth) ** 0.25
    kvw = 4 * HEAD_DIM
    assert t % TILE == 0 and ds == 8 and wb % TILE == 0 and n_pages % PAGES_PER_STEP == 0

    w_f, w_q, w_g, cols = _pack_weights(w_in[0])
    fc, qc = cols['f'], cols['q']
    w1cat = jnp.stack([jnp.concatenate([w[0][:CMP_STRIDE * HEAD_DIM], w[0][CMP_STRIDE * HEAD_DIM:]], axis=1)
                       for w in (cmp_w1_k, cmp_w1_v)]).astype(BF16)
    w1 = jnp.stack([cmp_w1_k[0], cmp_w1_v[0]]).astype(BF16)
    w2 = jnp.stack([cmp_w2_k[0], cmp_w2_v[0]]).astype(BF16)
    pe8 = jnp.broadcast_to(jnp.stack([cmp_pe_k[0].reshape(1, -1), cmp_pe_v[0].reshape(1, -1)]),
                           (2, 8, CMP_BLOCK * HEAD_DIM)).astype(BF16)
    wpa, wpb, wpm, wo = (w[0].astype(BF16) for w in (w_pa, w_pb, w_pm, w_o))
    wu, wd = w_up[0].astype(BF16), w_down[0].astype(BF16)
    nd = _num_near_tiles()
    bias = _bias_tiles(rel_table, nd)

    def dense_tail(x2d, gmat, oa, ob, om):
        x1 = _merge(x2d, gmat, oa, ob, om, wpa, wpb, wpm, wo, ln1_g, ln1_b, alpha)
        return _ffn(x1, wu, b_up, wd, b_down, ln2_g, ln2_b, alpha)

    xp = x_prompt.reshape(bsz * t, d)
    fp = _matmul(xp, w_f, F32, tn=w_f.shape[1] // 2)
    qp = _matmul(xp, w_q, BF16, tn=1024)
    gp = _matmul(xp, w_g, F32, tn=1024)
    p_a_kv = fp[:, fc['a_kv']:fc['a_kv'] + kvw]
    p_b_cmp = fp[:, fc['b_cmp']:fc['b_cmp'] + kvw]
    p_b_sel = fp[:, fc['b_sel']:fc['b_sel'] + kvw]
    p_b_win = fp[:, fc['b_win']:fc['b_win'] + kvw]
    p_a_idx = fp[:, fc['i_kw']:fc['i_kw'] + IDX_DIM]

    zp = _cmpz_dense(p_b_cmp.reshape(bsz * t // CMP_STRIDE, CMP_STRIDE * kvw), w1cat)
    cmp_k, cmp_v = _cmp_finish(zp.reshape(bsz, t // CMP_STRIDE, -1), pe8, w1, w2)
    o_a = _prompt_mixer_a(qp, fp, bias[0], bsz, t, cols, nd)
    o_b = _prompt_mixer_b(qp, fp, cmp_k, cmp_v, bias[1], bsz, t, cols, nd)
    mem_kv = _matmul(mem_prompt.reshape(bsz * n_mem, d), w_mem_kv[0].astype(BF16), F32, tn=1024)
    o_m = _mem_attend(qp, qc['m_q'], mem_kv, bsz, t, n_mem, tq=512)
    y_prompt = dense_tail(xp, gp, o_a, o_b, o_m).reshape(bsz, t, d)

    xs = x_sample.reshape(db * ds, d)
    fs = _matmul(xs, w_f, F32, tn=w_f.shape[1] // 2)
    qs = _matmul(xs, w_q, BF16, tn=1024)
    gs = _matmul(xs, w_g, F32, tn=1024)
    s_a_kv = fs[:, fc['a_kv']:fc['a_kv'] + kvw]
    s_b_cmp = fs[:, fc['b_cmp']:fc['b_cmp'] + kvw]
    s_b_sel = fs[:, fc['b_sel']:fc['b_sel'] + kvw]
    s_b_win = fs[:, fc['b_win']:fc['b_win'] + kvw]
    s_a_idx = fs[:, fc['i_kw']:fc['i_kw'] + IDX_DIM]
    length = past + ds

    def pad_new(rows2d):
        r = rows2d.reshape(db, ds, -1)
        return jnp.concatenate([r, jnp.zeros((db, TILE - ds, r.shape[2]), r.dtype)], axis=1)

    def head_major(q2d, heads):
        return q2d.reshape(db, ds, heads, -1).transpose(0, 2, 1, 3).reshape(db, heads * ds, -1)

    def block_q(q2d):
        qh = head_major(q2d, A_HEADS).reshape(db, A_KV_HEADS, A_GROUP * ds, HEAD_DIM)
        z = jnp.zeros_like(qh[:, 0])
        return jnp.concatenate([jnp.concatenate([qh[:, 0], z], axis=2),
                                jnp.concatenate([z, qh[:, 1]], axis=2)], axis=1)

    def sample_bias(tiles):
        n = tiles.shape[0]
        return tiles.reshape(n, A_KV_HEADS, A_GROUP, TILE, TILE)[:, :, :, :ds].reshape(n, A_HEADS * ds, TILE)

    iq_s = head_major(qs[:, qc['i_q']:qc['i_q'] + IDX_HEADS * LANES], IDX_HEADS)[:, :, :IDX_DIM]
    w_s = fs[:, fc['i_kw'] + IDX_DIM:fc['i_kw'] + IDX_DIM + IDX_HEADS] * (IDX_HEADS ** -0.5 * IDX_DIM ** -0.5)
    w_s = jnp.broadcast_to(head_major(w_s, IDX_HEADS), (db, IDX_HEADS * ds, LANES))
    mask_a = _sample_index_mask(page_table, iq_s, w_s, pad_new(s_a_idx),
                                cache_a_idx.reshape(n_pool, PAGE_SIZE, IDX_DIM), min(TOPK_MAX, length // 4))
    qa_blk = block_q(qs[:, qc['a_q']:qc['a_q'] + A_HEADS * HEAD_DIM])
    o_a_s = _paged_attention(page_table, qa_blk, mask_a, pad_new(s_a_kv), sample_bias(bias[0]),
                             cache_a_kv.reshape(n_pool, PAGE_SIZE * KV_ROWS, HEAD_DIM), nd, ds)

    zs = _cmpz_paged(cache_b_cmp.reshape(n_pool, PAGE_SIZE * KV_ROWS, HEAD_DIM), page_table, w1cat)
    cmp_k_s, cmp_v_s = _cmp_finish(zs.reshape(db, past // CMP_STRIDE, -1), pe8, w1, w2)
    bq2d = qs[:, qc['b_q']:qc['b_q'] + B_HEADS * HEAD_DIM]
    bq_s = head_major(bq2d, B_HEADS).reshape(db, B_KV_HEADS, B_GROUP * ds, HEAD_DIM)
    o_cmp_s, mask_b = _sample_cmp_select(bq_s, cmp_k_s, cmp_v_s, length, past, ds)
    qb_blk = block_q(bq2d)
    bias_sb = sample_bias(bias[1])
    o_sel_s = _paged_attention(page_table, qb_blk, mask_b, pad_new(s_b_sel), bias_sb,
                               cache_b_sel.reshape(n_pool, PAGE_SIZE * KV_ROWS, HEAD_DIM), nd, ds)
    gates_s = fs[:, fc['b_gate']:fc['b_gate'] + LANES].reshape(db, ds, LANES)
    o_b_s = _sample_window_combine(qb_blk, state_b_win.reshape(db, wb, kvw), pad_new(s_b_win), bias_sb,
                                   gates_s, o_cmp_s.reshape(db, B_HEADS * ds, HEAD_DIM), o_sel_s, ds)

    o_a_s = o_a_s.reshape(db, A_HEADS, ds, HEAD_DIM).transpose(0, 2, 1, 3).reshape(db * ds, -1).astype(BF16)
    o_m_s = _mem_attend(qs, qc['m_q'], cache_mem.reshape(db * n_mem, -1), db, ds, n_mem, tq=ds)
    y_sample = dense_tail(xs, gs, o_a_s, o_b_s.reshape(db * ds, -1), o_m_s).reshape(db, ds, d)

    kv6 = lambda a, n, rows: a.reshape(1, n, rows, 2, 2, HEAD_DIM)
    wp = min(WINDOW, t)
    new_win = jnp.concatenate([state_b_win.reshape(db, wb, 2, 2, HEAD_DIM)[:, ds:],
                               s_b_win.reshape(db, ds, 2, 2, HEAD_DIM)], axis=1)
    return (y_prompt, y_sample,
            kv6(p_a_kv, bsz, t), p_a_idx.reshape(1, bsz, t, IDX_DIM), kv6(p_b_cmp, bsz, t), kv6(p_b_sel, bsz, t),
            kv6(p_b_win, bsz, t)[:, :, t - wp:],
            mem_kv.reshape(1, bsz, n_mem, MEM_HEADS, 2, MEM_HEAD_DIM),
            kv6(s_a_kv, db, ds), s_a_idx.reshape(1, db, ds, IDX_DIM), kv6(s_b_cmp, db, ds), kv6(s_b_sel, db, ds),
            new_win[None])
```

```python
import functools
import math

import numpy as np
import jax
import jax.numpy as jnp
from jax import lax
from jax.experimental import pallas as pl
from jax.experimental.pallas import tpu as pltpu

F32 = jnp.float32
BF16 = jnp.bfloat16
I32 = jnp.int32

HEAD_DIM = 128
A_HEADS = 8
A_KV_HEADS = 2
A_GROUP = A_HEADS // A_KV_HEADS
IDX_HEADS = 8
IDX_DIM = 64
TOPK_MAX = 256
B_HEADS = 8
B_KV_HEADS = 2
B_GROUP = B_HEADS // B_KV_HEADS
CMP_BLOCK = 32
CMP_STRIDE = 16
CMP_RATIO = CMP_BLOCK // CMP_STRIDE
CMP_HID = 128
SEL_BLOCK = 64
SEL_TOPN = 16
WINDOW = 512
MEM_HEADS = 4
MEM_HEAD_DIM = 256
N_BUCKETS = 32
MAX_DISTANCE = 1024
LN_EPS = 1e-5
PAGE_SIZE = 128
KV_ROWS = 4

LANES = 128
VMEM_LIMIT = 56 * 1024 * 1024

TILE = 128
PAIR = 2 * TILE
NEG = -1e30
INT_MIN = -2 ** 31
PAGES_PER_STEP = 16
PAGES_PER_BLOCK = 8
INDEX_PAGES_PER_STEP = 64

_GROUPS = ('a_q', 'a_kv', 'i_q', 'i_k', 'i_w', 'b_q', 'b_cmp', 'b_sel', 'b_win', 'b_gate', 'm_q', 'g_merge')


def _split_sizes(d_model):
    return dict(
        a_q=A_HEADS * HEAD_DIM, a_kv=A_KV_HEADS * 2 * HEAD_DIM, i_q=IDX_HEADS * IDX_DIM, i_k=IDX_DIM,
        i_w=IDX_HEADS, b_q=B_HEADS * HEAD_DIM, b_cmp=B_KV_HEADS * 2 * HEAD_DIM, b_sel=B_KV_HEADS * 2 * HEAD_DIM,
        b_win=B_KV_HEADS * 2 * HEAD_DIM, b_gate=B_HEADS * 3, m_q=MEM_HEADS * MEM_HEAD_DIM, g_merge=3 * d_model)


def _params(n_grid, vmem=VMEM_LIMIT):
    return pltpu.CompilerParams(dimension_semantics=('arbitrary',) * n_grid, vmem_limit_bytes=vmem)


def _resident(block, index_map):
    return pl.BlockSpec(block, index_map, pipeline_mode=pl.Buffered(1))


def _mm_kernel(x_ref, w_ref, o_ref):
    o_ref[...] = jnp.dot(x_ref[...].astype(BF16), w_ref[...], preferred_element_type=F32).astype(o_ref.dtype)


def _matmul(x, w, out_dtype, tn, tm=1024, name='matmul'):
    m, k = x.shape
    n = w.shape[1]
    tm = min(tm, m)
    assert m % tm == 0 and n % tn == 0
    return pl.pallas_call(
        _mm_kernel,
        grid=(m // tm, n // tn),
        in_specs=[pl.BlockSpec((tm, k), lambda i, j: (i, 0)), pl.BlockSpec((k, tn), lambda i, j: (0, j))],
        out_specs=pl.BlockSpec((tm, tn), lambda i, j: (i, j)),
        out_shape=jax.ShapeDtypeStruct((m, n), out_dtype),
        compiler_params=_params(2), name=name,
    )(x, w)


def _rel_bucket(dist):
    d = jnp.maximum(dist, 0)
    exact = N_BUCKETS // 2
    df = jnp.maximum(d, 1).astype(F32)
    large = exact + (jnp.log(df / exact) / math.log(MAX_DISTANCE / exact) * (N_BUCKETS - exact)).astype(I32)
    return jnp.where(d < exact, d, jnp.minimum(large, N_BUCKETS - 1))


def _num_near_tiles():
    exact = N_BUCKETS // 2
    d = np.arange(1, 4 * MAX_DISTANCE, dtype=np.float64)
    large = exact + np.floor(np.log(d / exact) / math.log(MAX_DISTANCE / exact) * (N_BUCKETS - exact))
    bucket = np.where(d < exact, d, np.minimum(large, N_BUCKETS - 1))
    d_const = int(d[np.argmax(bucket == N_BUCKETS - 1)])
    return -(-(d_const + TILE // 2 + TILE - 1) // TILE)


def _bias_kernel(u_ref, o_ref, *, n_tiles, d_top):
    for dt in range(n_tiles):
        start = d_top - dt * TILE - (TILE - 1)
        row = u_ref[0, :, start:start + 2 * TILE]
        x = jnp.broadcast_to(row, (TILE, 2 * TILE))
        x = pltpu.roll(x, TILE + 1, 1, stride=1, stride_axis=0)
        o_ref[dt] = x[:, :TILE]


def _bias_tiles(rel_table, nd):
    n_tiles = nd + 1
    d_top = n_tiles * TILE
    ul = d_top + 2 * TILE
    n_heads = rel_table.shape[1]
    dist = d_top - jnp.arange(ul)
    u = rel_table[_rel_bucket(dist)].T.reshape(n_heads, 1, ul)
    out = pl.pallas_call(
        functools.partial(_bias_kernel, n_tiles=n_tiles, d_top=d_top),
        grid=(n_heads,),
        in_specs=[pl.BlockSpec((1, 1, ul), lambda h: (h, 0, 0))],
        out_specs=pl.BlockSpec((None, n_tiles, None, None, TILE, TILE),
                               lambda h: (h // 8, 0, (h % 8) // 4, h % 4, 0, 0)),
        out_shape=jax.ShapeDtypeStruct((2, n_tiles, 2, 4, TILE, TILE), F32),
        compiler_params=_params(1), name='bias_tiles',
    )(u)
    return out.reshape(2, n_tiles, 2, 4 * TILE, TILE)


def _dot_nt(a, b):
    return lax.dot_general(a, b, (((1,), (1,)), ((), ())), preferred_element_type=F32)


def _flash_step(s, m, l, acc, v):
    m_new = jnp.maximum(m, jnp.max(s, axis=1, keepdims=True))
    alpha = jnp.exp(m - m_new)
    p = jnp.exp(s - m_new)
    l = alpha * l + jnp.sum(p, axis=1, keepdims=True)
    acc = alpha * acc + jnp.dot(p.astype(BF16), v, preferred_element_type=F32)
    return m_new, l, acc


def _flash_init(rows, width):
    return (jnp.full((rows, 1), NEG, F32), jnp.zeros((rows, 1), F32), jnp.zeros((rows, width), F32))


def _flash_out(m, l, acc):
    return jnp.where(m > 0.5 * NEG, acc / jnp.maximum(l, 1e-30), 0.0)


def _sortable_key(x):
    bits = pltpu.bitcast(x, I32)
    bits = jnp.where(bits == INT_MIN, 0, bits)
    return jnp.where(bits < 0, bits ^ 0x7FFFFFFF, bits)


def _kth_largest_key(count_ge, k, shape):
    def bit_body(i, t):
        cand = t + lax.shift_left(jnp.int32(1), 31 - i)
        return jnp.where(count_ge(cand) >= k, cand, t)
    return lax.fori_loop(0, 32, bit_body, jnp.full(shape, INT_MIN, I32))


def _attn_update(q_g, kk, vv, bias_of, madd, m_ref, l_ref, acc_ref, h0, n_heads):
    tk = kk.shape[0]
    s_all = _dot_nt(q_g, kk) * (HEAD_DIM ** -0.5)
    ps, alphas = [], []
    for r in range(n_heads):
        s = s_all[r * TILE:(r + 1) * TILE] + (bias_of(r) + madd)
        m_prev = m_ref[h0 + r]
        m_next = jnp.maximum(m_prev, jnp.max(s, axis=1, keepdims=True))
        alpha = jnp.exp(m_prev - m_next)
        p = jnp.exp(s - jnp.concatenate([m_next] * (tk // LANES), axis=1))
        l_ref[h0 + r] = alpha * l_ref[h0 + r] + jnp.sum(p, axis=1, keepdims=True)
        m_ref[h0 + r] = m_next
        ps.append(p.astype(BF16))
        alphas.append(alpha)
    pv = jnp.dot(jnp.concatenate(ps, axis=0), vv, preferred_element_type=F32)
    for r in range(n_heads):
        acc_ref[h0 + r] = alphas[r] * acc_ref[h0 + r] + pv[r * TILE:(r + 1) * TILE]


def _attn_reset(m_ref, l_ref, acc_ref):
    m_ref[...] = jnp.full(m_ref.shape, NEG, F32)
    l_ref[...] = jnp.zeros(l_ref.shape, F32)
    acc_ref[...] = jnp.zeros(acc_ref.shape, F32)


def _attn_out(m_ref, l_ref, acc_ref, h):
    return jnp.where(m_ref[h] > 0.5 * NEG, acc_ref[h] / jnp.maximum(l_ref[h], 1e-30), 0.0)


def _topn_mask(scores, n):
    colf = lax.broadcasted_iota(I32, scores[0].shape, 1).astype(F32)

    def body(_, carry):
        out = []
        for sc, selm in carry:
            mx = jnp.max(sc, axis=1, keepdims=True)
            first = jnp.min(jnp.where(sc == mx, colf, 1e9), axis=1, keepdims=True)
            hit = colf == first
            selm = jnp.maximum(selm, jnp.where(hit, jnp.where(mx > -jnp.inf, 1.0, 0.0), 0.0))
            out.append((jnp.where(hit, -jnp.inf, sc), selm))
        return tuple(out)

    init = tuple((sc, jnp.zeros(sc.shape, F32)) for sc in scores)
    return [c[1] for c in lax.fori_loop(0, n, body, init)]


def _topn_mask_columns(scores_t, n):
    n_cand, n_col = scores_t[0].shape
    keys = [jnp.where(s > -jnp.inf, _sortable_key(s), INT_MIN) for s in scores_t]
    idx = lax.broadcasted_iota(I32, (n_cand, n_col), 0)

    def count(pred):
        v = jnp.where(pred, 1.0, 0.0)
        return jnp.sum(jnp.sum(v.reshape(4, n_cand // 4, n_col), axis=0), axis=0, keepdims=True)

    def bit_body(i, ts):
        step = lax.shift_left(jnp.int32(1), 31 - i)
        return tuple(jnp.where(count(k >= t + step) >= n, t + step, t) for k, t in zip(keys, ts))

    ts = lax.fori_loop(0, 32, bit_body, tuple(jnp.full((1, n_col), INT_MIN, I32) for _ in keys))
    thrs = [jnp.maximum(t, INT_MIN + 1) for t in ts]
    needs = [n - count(k > thr) for k, thr in zip(keys, thrs)]
    eq_idx = [jnp.where(k == thr, idx, jnp.int32(2 ** 30)) for k, thr in zip(keys, thrs)]
    n_bits = int(n_cand).bit_length()

    def idx_body(i, ms):
        step = lax.shift_left(jnp.int32(1), n_bits - 1 - i)
        return tuple(jnp.where(count(e < m + step) < need, m + step, m) for e, m, need in zip(eq_idx, ms, needs))

    lasts = lax.fori_loop(0, n_bits, idx_body, tuple(jnp.zeros((1, n_col), I32) for _ in keys))
    return [jnp.where(k > thr, 1.0, jnp.where(e <= last, 1.0, 0.0))
            for k, thr, e, last in zip(keys, thrs, eq_idx, lasts)]


def _softmax_rows(s, ok):
    s = jnp.where(ok, s, NEG)
    m = jnp.max(s, axis=1, keepdims=True)
    e = jnp.where(ok, jnp.exp(s - m), 0.0)
    return e / jnp.maximum(jnp.sum(e, axis=1, keepdims=True), 1e-30)


def _block_expand(blk0, n_blk, width=TILE):
    rb = lax.broadcasted_iota(I32, (n_blk, width), 0)
    cj = lax.broadcasted_iota(I32, (n_blk, width), 1)
    target = blk0 + lax.shift_right_logical(cj, int(math.log2(SEL_BLOCK)))
    return jnp.where(rb == target, 1.0, 0.0).astype(BF16)


def _gelu(x):
    return 0.5 * x * (1.0 + jnp.tanh(math.sqrt(2.0 / math.pi) * (x + 0.044715 * (x * x * x))))


def _layer_norm(x, g, b):
    xc = x - jnp.mean(x, axis=1, keepdims=True)
    var = jnp.mean(xc * xc, axis=1, keepdims=True)
    return xc * lax.rsqrt(var + LN_EPS) * g + b


def _cmpz_compute(load, w_ref, o_ref):
    rows = o_ref.shape[0]
    for kv in range(2):
        xs = [jnp.concatenate([load(p, g, kv) for p in range(CMP_STRIDE)], axis=1) for g in range(B_KV_HEADS)]
        z = jnp.dot(jnp.concatenate(xs, axis=0).astype(BF16), w_ref[kv], preferred_element_type=F32)
        for g in range(B_KV_HEADS):
            c = (g * 2 + kv) * CMP_RATIO * CMP_HID
            o_ref[:, c:c + CMP_RATIO * CMP_HID] = z[g * rows:(g + 1) * rows]


def _cmpz_kernel(x_ref, w_ref, o_ref):
    n_col = B_KV_HEADS * 2 * HEAD_DIM

    def load(p, g, kv):
        c0 = p * n_col + (g * 2 + kv) * HEAD_DIM
        return x_ref[:, c0:c0 + HEAD_DIM]

    _cmpz_compute(load, w_ref, o_ref)


def _cmpz_dense(x2d, w1cat, tc=256):
    n = x2d.shape[0]
    tc = min(tc, n)
    assert n % tc == 0
    return pl.pallas_call(
        _cmpz_kernel,
        grid=(n // tc,),
        in_specs=[pl.BlockSpec((tc, x2d.shape[1]), lambda i: (i, 0)),
                  _resident(w1cat.shape, lambda i: (0, 0, 0))],
        out_specs=pl.BlockSpec((tc, 4 * CMP_RATIO * CMP_HID), lambda i: (i, 0)),
        out_shape=jax.ShapeDtypeStruct((n, 4 * CMP_RATIO * CMP_HID), F32),
        compiler_params=_params(1), name='cmpz_dense',
    )(x2d, w1cat)


def _page_copy(pool_ref, buf_ref, sem_ref, pid, slot, k):
    rows = pool_ref.shape[1]
    return pltpu.make_async_copy(pool_ref.at[pid], buf_ref.at[slot, pl.ds(k * rows, rows)], sem_ref.at[slot])


def _pages_per_step(pool_ref, buf_ref):
    return buf_ref.shape[1] // pool_ref.shape[1]


def _page_fetch(pt_ref, pool_ref, buf_ref, sem_ref, step, slot):
    n = _pages_per_step(pool_ref, buf_ref)
    for k in range(n):
        _page_copy(pool_ref, buf_ref, sem_ref, pt_ref[step * n + k], slot, k).start()


def _page_wait(pool_ref, buf_ref, sem_ref, slot):
    for k in range(_pages_per_step(pool_ref, buf_ref)):
        _page_copy(pool_ref, buf_ref, sem_ref, 0, slot, k).wait()


def _page_pipeline(pt_ref, pool_ref, buf_ref, sem_ref):
    step = pl.program_id(0) * pl.num_programs(1) + pl.program_id(1)
    total = pl.num_programs(0) * pl.num_programs(1)
    slot = lax.rem(step, 2)

    @pl.when(step == 0)
    def _():
        _page_fetch(pt_ref, pool_ref, buf_ref, sem_ref, step, slot)

    @pl.when(step + 1 < total)
    def _():
        _page_fetch(pt_ref, pool_ref, buf_ref, sem_ref, step + 1, 1 - slot)

    _page_wait(pool_ref, buf_ref, sem_ref, slot)
    return slot


def _cmpz_paged_kernel(pt_ref, pool_ref, w_ref, o_ref, buf_ref, sem_ref):
    slot = _page_pipeline(pt_ref, pool_ref, buf_ref, sem_ref)
    rows = o_ref.shape[0]

    def load(p, g, kv):
        return buf_ref[slot, pl.ds(p * KV_ROWS + g * 2 + kv, rows, stride=CMP_STRIDE * KV_ROWS), :]

    _cmpz_compute(load, w_ref, o_ref)


def _cmpz_paged(pool, page_table, w1cat):
    db, n_pages = page_table.shape
    chunks = PAGE_SIZE // CMP_STRIDE
    npg = n_pages // PAGES_PER_STEP
    rows = PAGES_PER_STEP * chunks
    return pl.pallas_call(
        _cmpz_paged_kernel,
        grid_spec=pltpu.PrefetchScalarGridSpec(
            num_scalar_prefetch=1,
            grid=(db, npg),
            in_specs=[pl.BlockSpec(memory_space=pl.ANY),
                      _resident(w1cat.shape, lambda b, g, pt: (0, 0, 0))],
            out_specs=pl.BlockSpec((rows, 4 * CMP_RATIO * CMP_HID), lambda b, g, pt: (b * npg + g, 0)),
            scratch_shapes=[pltpu.VMEM((2, PAGES_PER_STEP * pool.shape[1], pool.shape[2]), F32),
                            pltpu.SemaphoreType.DMA((2,))]),
        out_shape=jax.ShapeDtypeStruct((db * n_pages * chunks, 4 * CMP_RATIO * CMP_HID), F32),
        compiler_params=_params(2), name='cmpz_paged',
    )(page_table.reshape(-1), pool, w1cat)


def _cmp_finish_kernel(z_ref, pe_ref, w1_ref, w2_ref, k_ref, v_ref):
    n = z_ref.shape[1]
    for kv, o_ref in ((0, k_ref), (1, v_ref)):
        pew = jnp.dot(pe_ref[kv], w1_ref[kv], preferred_element_type=F32)[0:1]
        for g in range(B_KV_HEADS):
            c = (g * 2 + kv) * CMP_RATIO * CMP_HID
            z0 = z_ref[0, :, c:c + CMP_HID]
            z1 = z_ref[0, :, c + CMP_HID:c + 2 * CMP_HID]
            pre = z0 + pltpu.roll(z1, n - 1, 0) + pew
            out = jnp.dot(_gelu(pre).astype(BF16), w2_ref[kv], preferred_element_type=F32)
            o_ref[0, :, g * HEAD_DIM:(g + 1) * HEAD_DIM] = out.astype(o_ref.dtype)


def _cmp_finish(z3, pe8, w1, w2):
    nb, n, zc = z3.shape
    out = jax.ShapeDtypeStruct((nb, n, B_KV_HEADS * HEAD_DIM), BF16)
    return pl.pallas_call(
        _cmp_finish_kernel,
        grid=(nb,),
        in_specs=[pl.BlockSpec((1, n, zc), lambda b: (b, 0, 0)),
                  _resident(pe8.shape, lambda b: (0, 0, 0)),
                  _resident(w1.shape, lambda b: (0, 0, 0)),
                  _resident(w2.shape, lambda b: (0, 0, 0))],
        out_specs=[pl.BlockSpec((1, n, B_KV_HEADS * HEAD_DIM), lambda b: (b, 0, 0))] * 2,
        out_shape=[out, out],
        compiler_params=_params(1), name='cmp_finish',
    )(z3, pe8, w1, w2)


def _ka_kernel(iq_ref, ikw_ref, ikwq_ref, aq_ref, akv_ref, bias_ref, o_ref, keys_ref, m_ref, l_ref, acc_ref,
               *, topk, nd):
    qt = pl.program_id(1)
    last_pair = lax.shift_right_logical(qt, 1)
    key_row = lax.broadcasted_iota(I32, (PAIR, TILE), 0)
    q_pos = qt * TILE + lax.broadcasted_iota(I32, (PAIR, TILE), 1)
    w_t = ikwq_ref[...].T[IDX_DIM:IDX_DIM + IDX_HEADS] * (IDX_HEADS ** -0.5 * IDX_DIM ** -0.5)
    iq_all = jnp.concatenate([iq_ref[:, h * LANES:(h + 1) * LANES] for h in range(IDX_HEADS)], axis=0)

    def score_pair(kp, masked):
        kb = ikw_ref[pl.ds(pl.multiple_of(kp * PAIR, PAIR), PAIR), :].astype(BF16)
        s = _dot_nt(kb, iq_all)
        acc = jnp.zeros((PAIR, TILE), F32)
        for h in range(IDX_HEADS):
            acc = acc + jnp.maximum(s[:, h * TILE:(h + 1) * TILE], 0.0) * w_t[h:h + 1]
        key = _sortable_key(acc)
        if masked:
            key = jnp.where(kp * PAIR + key_row <= q_pos, key, INT_MIN)
        keys_ref[kp] = key

    def score_body(kp, c):
        score_pair(kp, False)
        return c

    lax.fori_loop(0, last_pair, score_body, 0)
    score_pair(last_pair, True)

    def count_where(pred):
        def body(kp, c):
            v = jnp.where(pred(keys_ref[kp], kp), 1.0, 0.0)
            return c + jnp.sum(v.reshape(4, PAIR // 4, TILE), axis=0)
        c = lax.fori_loop(0, last_pair + 1, body, jnp.zeros((PAIR // 4, TILE), F32))
        return jnp.sum(c, axis=0, keepdims=True)

    def count_ge(cand):
        return count_where(lambda k, kp: k >= cand)

    t = _kth_largest_key(count_ge, topk, (1, TILE))
    thr = jnp.maximum(t, INT_MIN + 1)

    cnt_gt = count_ge(thr + 1)
    need = topk - cnt_gt
    cnt_eq = count_ge(thr) - cnt_gt
    tie = jnp.where(t > INT_MIN, jnp.where(cnt_eq > need, 1.0, 0.0), 0.0)

    @pl.when(jnp.max(tie) > 0.0)
    def _():
        n_bits = int(keys_ref.shape[0] * PAIR).bit_length()

        def idx_body(i, mm):
            cand = mm + lax.shift_left(jnp.int32(1), n_bits - 1 - i)
            c = count_where(lambda k, kp: jnp.where(k == thr, kp * PAIR + key_row, INT_MIN) < cand)
            c = c - count_where(lambda k, kp: k != thr)
            return jnp.where(c < need, cand, mm)

        last = lax.fori_loop(0, n_bits, idx_body, jnp.zeros((1, TILE), I32))
        last = jnp.where(tie > 0.0, last, jnp.int32(2 ** 30))

        def demote(kp, c):
            k = keys_ref[kp]
            pos = jnp.where(k == thr, kp * PAIR + key_row, INT_MIN)
            keys_ref[kp] = jnp.where(pos > last, thr - 1, k)
            return c

        lax.fori_loop(0, last_pair + 1, demote, 0)

    q = [jnp.concatenate([aq_ref[:, (g * A_GROUP + r) * HEAD_DIM:(g * A_GROUP + r + 1) * HEAD_DIM]
                          for r in range(A_GROUP)], axis=0) for g in range(A_KV_HEADS)]
    _attn_reset(m_ref, l_ref, acc_ref)

    def att_body(kp, c):
        d0 = jnp.clip(qt - 2 * kp, 0, nd)
        d1 = jnp.clip(qt - 2 * kp - 1, 0, nd)
        madd = jnp.where(keys_ref[kp] >= thr, 0.0, NEG).T
        k0 = pl.multiple_of(kp * PAIR, PAIR)
        for g in range(A_KV_HEADS):
            kc = g * 2 * HEAD_DIM
            kk = akv_ref[pl.ds(k0, PAIR), kc:kc + HEAD_DIM].astype(BF16)
            vv = akv_ref[pl.ds(k0, PAIR), kc + HEAD_DIM:kc + 2 * HEAD_DIM].astype(BF16)

            def bias_of(r, g=g):
                rs = slice(r * TILE, (r + 1) * TILE)
                return jnp.concatenate([bias_ref[d0, g, rs, :], bias_ref[d1, g, rs, :]], axis=1)

            _attn_update(q[g], kk, vv, bias_of, madd, m_ref, l_ref, acc_ref, g * A_GROUP, A_GROUP)
        return c

    lax.fori_loop(0, last_pair + 1, att_body, 0)
    for h in range(A_HEADS):
        o_ref[:, h * HEAD_DIM:(h + 1) * HEAD_DIM] = _attn_out(m_ref, l_ref, acc_ref, h).astype(o_ref.dtype)


def _prompt_mixer_a(qmat, fmat, bias_a, bsz, t, cols, nd):
    nt = t // TILE
    assert nt % 2 == 0
    topk = min(TOPK_MAX, t // 4)
    qc, fc = cols['q'], cols['f']
    return pl.pallas_call(
        functools.partial(_ka_kernel, topk=topk, nd=nd),
        grid=(bsz, nt),
        in_specs=[
            pl.BlockSpec((TILE, IDX_HEADS * LANES), lambda b, i: (b * nt + i, qc['i_q'] // (IDX_HEADS * LANES))),
            _resident((t, LANES), lambda b, i: (b, fc['i_kw'] // LANES)),
            pl.BlockSpec((TILE, LANES), lambda b, i: (b * nt + i, fc['i_kw'] // LANES)),
            pl.BlockSpec((TILE, A_HEADS * HEAD_DIM), lambda b, i: (b * nt + i, qc['a_q'] // (A_HEADS * HEAD_DIM))),
            _resident((t, 4 * HEAD_DIM), lambda b, i: (b, fc['a_kv'] // (4 * HEAD_DIM))),
            _resident(bias_a.shape, lambda b, i: (0, 0, 0, 0)),
        ],
        out_specs=pl.BlockSpec((TILE, A_HEADS * HEAD_DIM), lambda b, i: (b * nt + i, 0)),
        out_shape=jax.ShapeDtypeStruct((bsz * t, A_HEADS * HEAD_DIM), BF16),
        scratch_shapes=[pltpu.VMEM((nt // 2, PAIR, TILE), I32)] + [pltpu.VMEM((A_HEADS, TILE, LANES), F32)] * 3,
        compiler_params=_params(2), name='prompt_mixer_a',
    )(qmat, fmat, fmat, qmat, fmat, bias_a)


def _kb_kernel(bq_ref, gate_ref, ck_ref, cv_ref, ovt_ref, sel_ref, w0_ref, w1_ref, w2_ref, w3_ref, w4_ref,
               bias_ref, o_ref, m_ref, l_ref, acc_ref, ocmp_ref, osel_ref, *, nd, n_sel):
    qt = pl.program_id(1)
    last_pair = lax.shift_right_logical(qt, 1)
    rows = B_GROUP * TILE
    row = lax.broadcasted_iota(I32, (TILE, TILE), 0)
    col = lax.broadcasted_iota(I32, (TILE, TILE), 1)
    colp = lax.broadcasted_iota(I32, (TILE, PAIR), 1)
    q_pos_p = qt * TILE + lax.broadcasted_iota(I32, (TILE, PAIR), 0)
    scale = HEAD_DIM ** -0.5
    q = [jnp.concatenate([bq_ref[:, (g * B_GROUP + r) * HEAD_DIM:(g * B_GROUP + r + 1) * HEAD_DIM]
                          for r in range(B_GROUP)], axis=0) for g in range(B_KV_HEADS)]

    ncp = ck_ref.shape[1]
    q_pos = qt * TILE + lax.rem(lax.broadcasted_iota(I32, (rows, ncp), 0), TILE)
    cmp_end = lax.broadcasted_iota(I32, (rows, ncp), 1) * CMP_STRIDE + (CMP_BLOCK - 1)
    cmp_ok = cmp_end <= q_pos
    blk = row
    cur = 2 * qt + jnp.where(col >= SEL_BLOCK, 1, 0)
    scores_t = []
    for g in range(B_KV_HEADS):
        p = _softmax_rows(_dot_nt(q[g], ck_ref[0, :, g * HEAD_DIM:(g + 1) * HEAD_DIM]) * scale, cmp_ok)
        o_cmp = jnp.dot(p.astype(BF16), cv_ref[0, :, g * HEAD_DIM:(g + 1) * HEAD_DIM], preferred_element_type=F32)
        psum = p[0:TILE]
        ocmp_ref[g * B_GROUP] = o_cmp[0:TILE]
        for r in range(1, B_GROUP):
            psum = psum + p[r * TILE:(r + 1) * TILE]
            ocmp_ref[g * B_GROUP + r] = o_cmp[r * TILE:(r + 1) * TILE]
        imp_t = lax.dot_general(ovt_ref[...], psum, (((1,), (1,)), ((), ())), preferred_element_type=F32,
                                precision=lax.Precision.HIGHEST)
        forced = jnp.where(blk == 0, jnp.inf, jnp.where(blk >= cur - 1, jnp.inf, imp_t))
        scores_t.append(jnp.where(blk <= cur, forced, -jnp.inf))
    selm = [m.T.astype(BF16) for m in _topn_mask_columns(scores_t, n_sel)]

    _attn_reset(m_ref, l_ref, acc_ref)

    def sel_pair(kp, masked):
        d0 = jnp.clip(qt - 2 * kp, 0, nd)
        d1 = jnp.clip(qt - 2 * kp - 1, 0, nd)
        k0 = pl.multiple_of(kp * PAIR, PAIR)
        expand = _block_expand((PAIR // SEL_BLOCK) * kp, selm[0].shape[1], PAIR)
        for g in range(B_KV_HEADS):
            madd = (jnp.dot(selm[g], expand, preferred_element_type=F32) - 1.0) * (-NEG)
            if masked:
                madd = jnp.where(kp * PAIR + colp <= q_pos_p, madd, NEG)
            kc = g * 2 * HEAD_DIM
            kk = sel_ref[pl.ds(k0, PAIR), kc:kc + HEAD_DIM].astype(BF16)
            vv = sel_ref[pl.ds(k0, PAIR), kc + HEAD_DIM:kc + 2 * HEAD_DIM].astype(BF16)

            def bias_of(r, g=g):
                rs = slice(r * TILE, (r + 1) * TILE)
                return jnp.concatenate([bias_ref[d0, g, rs, :], bias_ref[d1, g, rs, :]], axis=1)

            _attn_update(q[g], kk, vv, bias_of, madd, m_ref, l_ref, acc_ref, g * B_GROUP, B_GROUP)

    def sel_body(kp, c):
        sel_pair(kp, False)
        return c

    lax.fori_loop(0, last_pair, sel_body, 0)
    sel_pair(last_pair, True)
    for h in range(B_HEADS):
        osel_ref[h] = _attn_out(m_ref, l_ref, acc_ref, h)

    _attn_reset(m_ref, l_ref, acc_ref)
    for k, w_ref in enumerate((w0_ref, w1_ref, w2_ref, w3_ref, w4_ref)):
        if k == 0:
            ok = col <= row
        elif k == WINDOW // TILE:
            ok = row <= col
        else:
            ok = col >= 0
        madd = jnp.where(ok, jnp.where(qt >= k, 0.0, NEG), NEG)
        for g in range(B_KV_HEADS):
            kc = g * 2 * HEAD_DIM
            kk = w_ref[:, kc:kc + HEAD_DIM].astype(BF16)
            vv = w_ref[:, kc + HEAD_DIM:kc + 2 * HEAD_DIM].astype(BF16)

            def bias_of(r, g=g, k=k):
                return bias_ref[k, g, r * TILE:(r + 1) * TILE, :]

            _attn_update(q[g], kk, vv, bias_of, madd, m_ref, l_ref, acc_ref, g * B_GROUP, B_GROUP)

    gate = jax.nn.sigmoid(gate_ref[...])
    for h in range(B_HEADS):
        o = (gate[:, 3 * h:3 * h + 1] * ocmp_ref[h] + gate[:, 3 * h + 1:3 * h + 2] * osel_ref[h]
             + gate[:, 3 * h + 2:3 * h + 3] * _attn_out(m_ref, l_ref, acc_ref, h))
        o_ref[:, h * HEAD_DIM:(h + 1) * HEAD_DIM] = o.astype(o_ref.dtype)


def _overlap_matrix(length, n_rows, n_cols):
    nc = (length - CMP_BLOCK) // CMP_STRIDE + 1
    ns = -(-length // SEL_BLOCK)
    cs = np.arange(nc) * CMP_STRIDE
    ss = np.arange(ns) * SEL_BLOCK
    ov = np.minimum(cs[:, None] + CMP_BLOCK, ss[None, :] + SEL_BLOCK) - np.maximum(cs[:, None], ss[None, :])
    out = np.zeros((n_rows, n_cols), np.float32)
    out[:nc, :ns] = np.clip(ov, 0, None).astype(np.float32) / CMP_BLOCK
    return jnp.asarray(out)


def _prompt_mixer_b(qmat, fmat, cmp_k, cmp_v, bias_b, bsz, t, cols, nd):
    nt = t // TILE
    ns = -(-t // SEL_BLOCK)
    assert ns <= LANES and WINDOW // TILE == 4 and nd >= WINDOW // TILE
    qc, fc = cols['q'], cols['f']
    ncp = cmp_k.shape[1]
    ov = _overlap_matrix(t, ncp, LANES).T
    kvw = 4 * HEAD_DIM

    def win_spec(k):
        return pl.BlockSpec((TILE, kvw), lambda b, i: (b * nt + jnp.maximum(i - k, 0), fc['b_win'] // kvw))

    return pl.pallas_call(
        functools.partial(_kb_kernel, nd=nd, n_sel=min(SEL_TOPN, ns)),
        grid=(bsz, nt),
        in_specs=[
            pl.BlockSpec((TILE, B_HEADS * HEAD_DIM), lambda b, i: (b * nt + i, qc['b_q'] // (B_HEADS * HEAD_DIM))),
            pl.BlockSpec((TILE, LANES), lambda b, i: (b * nt + i, fc['b_gate'] // LANES)),
            _resident((1, ncp, B_KV_HEADS * HEAD_DIM), lambda b, i: (b, 0, 0)),
            _resident((1, ncp, B_KV_HEADS * HEAD_DIM), lambda b, i: (b, 0, 0)),
            _resident(ov.shape, lambda b, i: (0, 0)),
            _resident((t, kvw), lambda b, i: (b, fc['b_sel'] // kvw)),
            win_spec(0), win_spec(1), win_spec(2), win_spec(3), win_spec(4),
            _resident(bias_b.shape, lambda b, i: (0, 0, 0, 0)),
        ],
        out_specs=pl.BlockSpec((TILE, B_HEADS * HEAD_DIM), lambda b, i: (b * nt + i, 0)),
        out_shape=jax.ShapeDtypeStruct((bsz * t, B_HEADS * HEAD_DIM), BF16),
        scratch_shapes=[pltpu.VMEM((B_HEADS, TILE, LANES), F32)] * 5,
        compiler_params=_params(2), name='prompt_mixer_b',
    )(qmat, fmat, cmp_k, cmp_v, ov, fmat, fmat, fmat, fmat, fmat, fmat, bias_b)


def _mem_kernel(q_ref, kv_ref, o_ref):
    scale = MEM_HEAD_DIM ** -0.5
    for h in range(MEM_HEADS):
        c = h * 2 * MEM_HEAD_DIM
        kk = kv_ref[:, c:c + MEM_HEAD_DIM].astype(BF16)
        vv = kv_ref[:, c + MEM_HEAD_DIM:c + 2 * MEM_HEAD_DIM].astype(BF16)
        s = _dot_nt(q_ref[:, h * MEM_HEAD_DIM:(h + 1) * MEM_HEAD_DIM], kk) * scale
        e = jnp.exp(s - jnp.max(s, axis=1, keepdims=True))
        p = e / jnp.sum(e, axis=1, keepdims=True)
        o = jnp.dot(p.astype(BF16), vv, preferred_element_type=F32)
        o_ref[:, h * MEM_HEAD_DIM:(h + 1) * MEM_HEAD_DIM] = o.astype(o_ref.dtype)


def _mem_attend(qmat, q_col, mem_kv2d, n_batch, rows_per_batch, n_mem, tq):
    width = MEM_HEADS * MEM_HEAD_DIM
    tq = min(tq, rows_per_batch)
    nq = rows_per_batch // tq
    return pl.pallas_call(
        _mem_kernel,
        grid=(n_batch, nq),
        in_specs=[pl.BlockSpec((tq, width), lambda b, i: (b * nq + i, q_col // width)),
                  pl.BlockSpec((n_mem, 2 * width), lambda b, i: (b, 0))],
        out_specs=pl.BlockSpec((tq, width), lambda b, i: (b * nq + i, 0)),
        out_shape=jax.ShapeDtypeStruct((n_batch * rows_per_batch, width), BF16),
        compiler_params=_params(2), name='mem_attend',
    )(qmat, mem_kv2d)


def _gated_proj_kernel(ga_ref, gb_ref, gm_ref, oa_ref, ob_ref, om_ref, wpa_ref, wpb_ref, wpm_ref, o_ref):
    merged = jax.nn.sigmoid(ga_ref[...]) * jnp.dot(oa_ref[...], wpa_ref[...], preferred_element_type=F32)
    merged = merged + jax.nn.sigmoid(gb_ref[...]) * jnp.dot(ob_ref[...], wpb_ref[...], preferred_element_type=F32)
    merged = merged + jax.nn.sigmoid(gm_ref[...]) * jnp.dot(om_ref[...], wpm_ref[...], preferred_element_type=F32)
    o_ref[...] = merged.astype(o_ref.dtype)


def _out_proj_kernel(x_ref, mg_ref, wo_ref, lg_ref, lb_ref, o_ref, *, alpha):
    y = alpha * x_ref[...] + jnp.dot(mg_ref[...], wo_ref[...], preferred_element_type=F32)
    o_ref[...] = _layer_norm(y, lg_ref[...], lb_ref[...])


def _merge(x2d, gmat, oa, ob, om, wpa, wpb, wpm, wo, ln_g, ln_b, alpha, tm=512, tn=1024):
    m, d = x2d.shape
    tm = min(tm, m)
    assert m % tm == 0 and d % tn == 0
    nj = d // tn
    row = lambda i, j: (i, 0)
    wcol = lambda i, j: (0, j)
    merged = pl.pallas_call(
        _gated_proj_kernel,
        grid=(m // tm, nj),
        in_specs=[pl.BlockSpec((tm, tn), lambda i, j: (i, j)),
                  pl.BlockSpec((tm, tn), lambda i, j: (i, nj + j)),
                  pl.BlockSpec((tm, tn), lambda i, j: (i, 2 * nj + j)),
                  pl.BlockSpec((tm, oa.shape[1]), row), pl.BlockSpec((tm, ob.shape[1]), row),
                  pl.BlockSpec((tm, om.shape[1]), row),
                  pl.BlockSpec((wpa.shape[0], tn), wcol), pl.BlockSpec((wpb.shape[0], tn), wcol),
                  pl.BlockSpec((wpm.shape[0], tn), wcol)],
        out_specs=pl.BlockSpec((tm, tn), lambda i, j: (i, j)),
        out_shape=jax.ShapeDtypeStruct((m, d), BF16),
        compiler_params=_params(2), name='gated_proj',
    )(gmat, gmat, gmat, oa, ob, om, wpa, wpb, wpm)
    fixed = lambda i: (0, 0)
    return pl.pallas_call(
        functools.partial(_out_proj_kernel, alpha=alpha),
        grid=(m // tm,),
        in_specs=[pl.BlockSpec((tm, d), lambda i: (i, 0)), pl.BlockSpec((tm, d), lambda i: (i, 0)),
                  _resident(wo.shape, fixed), _resident((1, d), fixed), _resident((1, d), fixed)],
        out_specs=pl.BlockSpec((tm, d), lambda i: (i, 0)),
        out_shape=jax.ShapeDtypeStruct((m, d), F32),
        compiler_params=_params(1), name='out_proj_ln',
    )(x2d, merged, wo, ln_g, ln_b)


def _ffn_kernel(x_ref, wu_ref, bu_ref, wd_ref, bd_ref, lg_ref, lb_ref, o_ref, acc_ref, *, alpha):
    j = pl.program_id(1)

    @pl.when(j == 0)
    def _():
        acc_ref[...] = jnp.zeros_like(acc_ref)

    u = jnp.dot(x_ref[...].astype(BF16), wu_ref[...], preferred_element_type=F32) + bu_ref[...]
    u = jnp.square(jnp.maximum(u, 0.0))
    acc_ref[...] += jnp.dot(u.astype(BF16), wd_ref[...], preferred_element_type=F32)

    @pl.when(j == pl.num_programs(1) - 1)
    def _():
        y = alpha * x_ref[...] + acc_ref[...] + bd_ref[...]
        o_ref[...] = _layer_norm(y, lg_ref[...], lb_ref[...])


def _ffn(x2d, wu, bu, wd, bd, ln_g, ln_b, alpha, tm=512, tf=1024):
    m, d = x2d.shape
    dff = wu.shape[1]
    tm = min(tm, m)
    assert m % tm == 0 and dff % tf == 0
    return pl.pallas_call(
        functools.partial(_ffn_kernel, alpha=alpha),
        grid=(m // tm, dff // tf),
        in_specs=[pl.BlockSpec((tm, d), lambda i, j: (i, 0)),
                  pl.BlockSpec((d, tf), lambda i, j: (0, j)), pl.BlockSpec((1, tf), lambda i, j: (0, j)),
                  pl.BlockSpec((tf, d), lambda i, j: (j, 0)),
                  _resident((1, d), lambda i, j: (0, 0)), _resident((1, d), lambda i, j: (0, 0)),
                  _resident((1, d), lambda i, j: (0, 0))],
        out_specs=pl.BlockSpec((tm, d), lambda i, j: (i, 0)),
        out_shape=jax.ShapeDtypeStruct((m, d), F32),
        scratch_shapes=[pltpu.VMEM((tm, d), F32)],
        compiler_params=_params(2), name='ffn_ln',
    )(x2d, wu, bu, wd, bd, ln_g, ln_b)


def _sidx_kernel(pt_ref, iq_ref, w_ref, knew_ref, pool_ref, o_ref, buf_ref, sem_ref, keys_ref, *, topk, n_q, past):
    g = pl.program_id(1)
    slot = _page_pipeline(pt_ref, pool_ref, buf_ref, sem_ref)
    span = buf_ref.shape[1]

    def scores(kb, width):
        s = _dot_nt(iq_ref[0], kb.astype(BF16))
        acc = jnp.zeros((n_q, width), F32)
        for h in range(IDX_HEADS):
            acc = acc + jnp.maximum(s[h * n_q:(h + 1) * n_q], 0.0) * w_ref[0, h * n_q:(h + 1) * n_q, 0:1]
        return _sortable_key(acc)

    keys_ref[:, pl.ds(pl.multiple_of(g * span, span), span)] = scores(buf_ref[slot], span)

    @pl.when(g == pl.num_programs(1) - 1)
    def _():
        lp = keys_ref.shape[1]
        rown = lax.broadcasted_iota(I32, (n_q, TILE), 0)
        coln = lax.broadcasted_iota(I32, (n_q, TILE), 1)
        keys_ref[:, past:lp] = jnp.where(coln <= rown, scores(knew_ref[0], TILE), INT_MIN)
        keys = keys_ref[...]
        pos = lax.broadcasted_iota(I32, (n_q, lp), 1)

        def count(pred):
            v = jnp.where(pred, 1.0, 0.0)
            part = (lp // (4 * LANES)) * LANES
            sums = [jnp.sum(v[:, i * part:(i + 1) * part], axis=1, keepdims=True) for i in range(4)]
            sums.append(jnp.sum(v[:, 4 * part:], axis=1, keepdims=True))
            return (sums[0] + sums[1]) + (sums[2] + sums[3]) + sums[4]

        t = _kth_largest_key(lambda cand: count(keys >= cand), topk, (n_q, 1))
        thr = jnp.maximum(t, INT_MIN + 1)
        cnt_gt = count(keys >= thr + 1)
        need = topk - cnt_gt
        cnt_eq = count(keys >= thr) - cnt_gt
        tie = jnp.where(t > INT_MIN, jnp.where(cnt_eq > need, 1.0, 0.0), 0.0)
        eq_pos = jnp.where(keys == thr, pos, jnp.int32(2 ** 30))
        n_bits = int(lp).bit_length()

        def idx_body(i, mm):
            cand = mm + lax.shift_left(jnp.int32(1), n_bits - 1 - i)
            return jnp.where(count(eq_pos < cand) < need, cand, mm)

        last = lax.fori_loop(0, n_bits, idx_body, jnp.zeros((n_q, 1), I32))
        last = jnp.where(tie > 0.0, last, jnp.int32(2 ** 30))
        sel = jnp.where(keys > thr, 1.0, jnp.where(keys == thr, jnp.where(pos <= last, 1.0, 0.0), 0.0))
        o_ref[0, 0] = sel


def _sample_index_mask(page_table, iq_s, w_s, knew, pool_idx, topk):
    db, n_pages = page_table.shape
    n_q = iq_s.shape[1] // IDX_HEADS
    pages = math.gcd(n_pages, INDEX_PAGES_PER_STEP)
    npg = n_pages // pages
    lp = n_pages * PAGE_SIZE + TILE
    return pl.pallas_call(
        functools.partial(_sidx_kernel, topk=topk, n_q=n_q, past=n_pages * PAGE_SIZE),
        grid_spec=pltpu.PrefetchScalarGridSpec(
            num_scalar_prefetch=1,
            grid=(db, npg),
            in_specs=[pl.BlockSpec((1,) + iq_s.shape[1:], lambda b, g, pt: (b, 0, 0)),
                      pl.BlockSpec((1,) + w_s.shape[1:], lambda b, g, pt: (b, 0, 0)),
                      pl.BlockSpec((1,) + knew.shape[1:], lambda b, g, pt: (b, 0, 0)),
                      pl.BlockSpec(memory_space=pl.ANY)],
            out_specs=pl.BlockSpec((1, 1, n_q, lp), lambda b, g, pt: (b, 0, 0, 0)),
            scratch_shapes=[pltpu.VMEM((2, pages * pool_idx.shape[1], pool_idx.shape[2]), F32),
                            pltpu.SemaphoreType.DMA((2,)),
                            pltpu.VMEM((n_q, lp), I32)]),
        out_shape=jax.ShapeDtypeStruct((db, 1, n_q, lp), F32),
        compiler_params=_params(2), name='sample_index_mask',
    )(page_table.reshape(-1), iq_s, w_s, knew, pool_idx)


def _pattn_kernel(pt_ref, q_ref, mask_ref, new_ref, bias_ref, pool_ref, o_ref, buf_ref, sem_ref,
                  m_ref, l_ref, acc_ref, *, nd, n_q, n_pages):
    g = pl.program_id(1)
    npg = pl.num_programs(1)
    slot = _page_pipeline(pt_ref, pool_ref, buf_ref, sem_ref)
    rows = q_ref.shape[1]
    half = rows // 2
    scale = HEAD_DIM ** -0.5
    mask_groups = mask_ref.shape[1]

    @pl.when(g == 0)
    def _():
        m0, l0, a0 = _flash_init(rows, 2 * HEAD_DIM)
        m_ref[...] = m0
        l_ref[...] = l0
        acc_ref[...] = a0

    def block_update(rows_of, p0, n_tiles, carry):
        kk = jnp.concatenate([rows_of(0), rows_of(2)], axis=1).astype(BF16)
        vv = jnp.concatenate([rows_of(1), rows_of(3)], axis=1).astype(BF16)
        width = n_tiles * TILE
        mk = mask_ref[0, :, :, pl.ds(pl.multiple_of(p0 * TILE, TILE), width)]
        madd = (mk - 1.0) * (-NEG)
        reps = rows // (mask_groups * n_q)
        madd = jnp.concatenate([madd[i] for i in range(mask_groups) for _ in range(reps)], axis=0)
        bias = jnp.concatenate([bias_ref[jnp.clip(n_pages - (p0 + i), 0, nd)] for i in range(n_tiles)], axis=1)
        s = _dot_nt(q_ref[0], kk) * scale + (bias + madd)
        return _flash_step(s, *carry, vv)

    carry = (m_ref[...], l_ref[...], acc_ref[...])
    keys = PAGES_PER_BLOCK * PAGE_SIZE
    for blk in range(PAGES_PER_STEP // PAGES_PER_BLOCK):
        def rows_of(j, blk=blk):
            return buf_ref[slot, pl.ds(blk * keys * KV_ROWS + j, keys, stride=KV_ROWS), :]
        carry = block_update(rows_of, g * PAGES_PER_STEP + blk * PAGES_PER_BLOCK, PAGES_PER_BLOCK, carry)
    m_ref[...], l_ref[...], acc_ref[...] = carry

    @pl.when(g == npg - 1)
    def _():
        def new_rows(j):
            return new_ref[0, :, j * HEAD_DIM:(j + 1) * HEAD_DIM]
        o = _flash_out(*block_update(new_rows, n_pages, 1, carry))
        o_ref[0, 0:half] = o[0:half, 0:HEAD_DIM]
        o_ref[0, half:rows] = o[half:rows, HEAD_DIM:2 * HEAD_DIM]


def _paged_attention(page_table, qblk, mask, new_kv, bias_s, pool, nd, n_q):
    db, n_pages = page_table.shape
    npg = n_pages // PAGES_PER_STEP
    rows = qblk.shape[1]
    return pl.pallas_call(
        functools.partial(_pattn_kernel, nd=nd, n_q=n_q, n_pages=n_pages),
        grid_spec=pltpu.PrefetchScalarGridSpec(
            num_scalar_prefetch=1,
            grid=(db, npg),
            in_specs=[pl.BlockSpec((1,) + qblk.shape[1:], lambda b, g, pt: (b, 0, 0)),
                      pl.BlockSpec((1,) + mask.shape[1:], lambda b, g, pt: (b, 0, 0, 0)),
                      pl.BlockSpec((1,) + new_kv.shape[1:], lambda b, g, pt: (b, 0, 0)),
                      _resident(bias_s.shape, lambda b, g, pt: (0, 0, 0)),
                      pl.BlockSpec(memory_space=pl.ANY)],
            out_specs=pl.BlockSpec((1, rows, HEAD_DIM), lambda b, g, pt: (b, 0, 0)),
            scratch_shapes=[pltpu.VMEM((2, PAGES_PER_STEP * pool.shape[1], pool.shape[2]), F32),
                            pltpu.SemaphoreType.DMA((2,)),
                            pltpu.VMEM((rows, 1), F32), pltpu.VMEM((rows, 1), F32),
                            pltpu.VMEM((rows, 2 * HEAD_DIM), F32)]),
        out_shape=jax.ShapeDtypeStruct((db, rows, HEAD_DIM), F32),
        compiler_params=_params(2), name='paged_attention',
    )(page_table.reshape(-1), qblk, mask, new_kv, bias_s, pool)


def _scmp_kernel(q_ref, ck_ref, cv_ref, ov_ref, ocmp_ref, mask_ref, selm_ref, *, past, n_q, n_sel):
    rows = q_ref.shape[2]
    ncp = ck_ref.shape[1]
    nsp = ov_ref.shape[1]
    scale = HEAD_DIM ** -0.5
    qi = lax.rem(lax.broadcasted_iota(I32, (rows, ncp), 0), n_q)
    cmp_end = lax.broadcasted_iota(I32, (rows, ncp), 1) * CMP_STRIDE + (CMP_BLOCK - 1)
    cmp_ok = cmp_end <= past + qi
    blk = lax.broadcasted_iota(I32, (n_q, nsp), 1)
    cur = lax.shift_right_logical(past + lax.broadcasted_iota(I32, (n_q, nsp), 0), int(math.log2(SEL_BLOCK)))
    q_pos = past + lax.broadcasted_iota(I32, (n_q, TILE), 0)
    coln = lax.broadcasted_iota(I32, (n_q, TILE), 1)
    scores = []
    for g in range(B_KV_HEADS):
        p = _softmax_rows(_dot_nt(q_ref[0, g], ck_ref[0, :, g * HEAD_DIM:(g + 1) * HEAD_DIM]) * scale, cmp_ok)
        ocmp_ref[0, g] = jnp.dot(p.astype(BF16), cv_ref[0, :, g * HEAD_DIM:(g + 1) * HEAD_DIM],
                                 preferred_element_type=F32)
        psum = p[0:n_q]
        for r in range(1, rows // n_q):
            psum = psum + p[r * n_q:(r + 1) * n_q]
        imp = jnp.dot(psum, ov_ref[...], preferred_element_type=F32, precision=lax.Precision.HIGHEST)
        forced = jnp.where(blk == 0, jnp.inf, jnp.where(blk >= cur - 1, jnp.inf, imp))
        scores.append(jnp.where(blk <= cur, forced, -jnp.inf))
    for g, selm in enumerate(_topn_mask(scores, n_sel)):
        selm_ref[g] = selm

    blocks_per_tile = TILE // SEL_BLOCK
    half = lax.shift_right_logical(coln, int(math.log2(SEL_BLOCK)))

    def expand_tile(kt):
        blk0 = blocks_per_tile * kt
        win0 = pl.multiple_of(lax.shift_right_logical(blk0, int(math.log2(LANES))) * LANES, LANES)
        idx = (blk0 - win0) + half
        k0 = pl.multiple_of(kt * TILE, TILE)
        for g in range(B_KV_HEADS):
            e = jnp.take_along_axis(selm_ref[g, :, pl.ds(win0, LANES)], idx, axis=1)
            mask_ref[0, g, :, pl.ds(k0, TILE)] = jnp.where(kt * TILE + coln <= q_pos, e, 0.0)

    unroll = 4
    n_tiles = mask_ref.shape[3] // TILE

    def expand_body(j, c):
        for u in range(unroll):
            expand_tile(j * unroll + u)
        return c

    lax.fori_loop(0, n_tiles // unroll, expand_body, 0)
    for kt in range(n_tiles - n_tiles % unroll, n_tiles):
        expand_tile(jnp.int32(kt))


def _sample_cmp_select(bq_s, cmp_k, cmp_v, length, past, n_q):
    db = bq_s.shape[0]
    ncp = cmp_k.shape[1]
    ns = -(-length // SEL_BLOCK)
    nsp = -(-ns // LANES) * LANES
    ov = _overlap_matrix(length, ncp, nsp)
    lp = past + TILE
    return pl.pallas_call(
        functools.partial(_scmp_kernel, past=past, n_q=n_q, n_sel=min(SEL_TOPN, ns)),
        grid=(db,),
        in_specs=[pl.BlockSpec((1,) + bq_s.shape[1:], lambda b: (b, 0, 0, 0)),
                  pl.BlockSpec((1, ncp, B_KV_HEADS * HEAD_DIM), lambda b: (b, 0, 0)),
                  pl.BlockSpec((1, ncp, B_KV_HEADS * HEAD_DIM), lambda b: (b, 0, 0)),
                  _resident(ov.shape, lambda b: (0, 0))],
        out_specs=[pl.BlockSpec((1,) + bq_s.shape[1:], lambda b: (b, 0, 0, 0)),
                   pl.BlockSpec((1, B_KV_HEADS, n_q, lp), lambda b: (b, 0, 0, 0))],
        out_shape=[jax.ShapeDtypeStruct(bq_s.shape, F32),
                   jax.ShapeDtypeStruct((db, B_KV_HEADS, n_q, lp), F32)],
        scratch_shapes=[pltpu.VMEM((B_KV_HEADS, n_q, nsp), F32)],
        compiler_params=_params(1), name='sample_cmp_select',
    )(bq_s, cmp_k, cmp_v, ov)


def _swin_kernel(q_ref, win_ref, new_ref, bias_ref, gate_ref, ocmp_ref, osel_ref, o_ref, *, n_q):
    rows = q_ref.shape[1]
    half = rows // 2
    wb = win_ref.shape[1]
    scale = HEAD_DIM ** -0.5
    qi = lax.rem(lax.broadcasted_iota(I32, (rows, TILE), 0), n_q)
    col = lax.broadcasted_iota(I32, (rows, TILE), 1)
    carry = _flash_init(rows, 2 * HEAD_DIM)

    def tile_update(kv, dlt, ok, carry):
        kk = jnp.concatenate([kv[:, 0:HEAD_DIM], kv[:, 2 * HEAD_DIM:3 * HEAD_DIM]], axis=1).astype(BF16)
        vv = jnp.concatenate([kv[:, HEAD_DIM:2 * HEAD_DIM], kv[:, 3 * HEAD_DIM:4 * HEAD_DIM]], axis=1).astype(BF16)
        s = _dot_nt(q_ref[0], kk) * scale + (bias_ref[dlt] + jnp.where(ok, 0.0, NEG))
        return _flash_step(s, *carry, vv)

    for kt in range(wb // TILE):
        ok = col + kt * TILE >= qi + (wb - WINDOW)
        carry = tile_update(win_ref[0, kt * TILE:(kt + 1) * TILE, :], wb // TILE - kt, ok, carry)
    carry = tile_update(new_ref[0], 0, col <= qi, carry)
    o = _flash_out(*carry)
    gate = jax.nn.sigmoid(gate_ref[0])
    for h in range(B_HEADS):
        sl = slice(h * n_q, (h + 1) * n_q)
        ow = o[sl, 0:HEAD_DIM] if h < B_GROUP else o[sl, HEAD_DIM:2 * HEAD_DIM]
        o_ref[0, :, h * HEAD_DIM:(h + 1) * HEAD_DIM] = (
            gate[:, 3 * h:3 * h + 1] * ocmp_ref[0, sl] + gate[:, 3 * h + 1:3 * h + 2] * osel_ref[0, sl]
            + gate[:, 3 * h + 2:3 * h + 3] * ow).astype(o_ref.dtype)


def _sample_window_combine(qblk, win_state, new_win, bias_s, gates, o_cmp, o_sel, n_q):
    db, rows, _ = qblk.shape
    return pl.pallas_call(
        functools.partial(_swin_kernel, n_q=n_q),
        grid=(db,),
        in_specs=[pl.BlockSpec((1,) + qblk.shape[1:], lambda b: (b, 0, 0)),
                  pl.BlockSpec((1,) + win_state.shape[1:], lambda b: (b, 0, 0)),
                  pl.BlockSpec((1,) + new_win.shape[1:], lambda b: (b, 0, 0)),
                  _resident(bias_s.shape, lambda b: (0, 0, 0)),
                  pl.BlockSpec((1,) + gates.shape[1:], lambda b: (b, 0, 0)),
                  pl.BlockSpec((1, rows, HEAD_DIM), lambda b: (b, 0, 0)),
                  pl.BlockSpec((1, rows, HEAD_DIM), lambda b: (b, 0, 0))],
        out_specs=pl.BlockSpec((1, n_q, B_HEADS * HEAD_DIM), lambda b: (b, 0, 0)),
        out_shape=jax.ShapeDtypeStruct((db, n_q, B_HEADS * HEAD_DIM), BF16),
        compiler_params=_params(1), name='sample_window_combine',
    )(qblk, win_state, new_win, bias_s, gates, o_cmp, o_sel)


def _pack_weights(w_in):
    d = w_in.shape[0]
    sizes = _split_sizes(d)
    off, o = {}, 0
    for name in _GROUPS:
        off[name] = o
        o += sizes[name]
    take = lambda name: w_in[:, off[name]:off[name] + sizes[name]]
    zeros = lambda n: jnp.zeros((d, n), w_in.dtype)
    iq = take('i_q').reshape(d, IDX_HEADS, IDX_DIM)
    iq = jnp.concatenate([iq, jnp.zeros_like(iq)], axis=2).reshape(d, IDX_HEADS * LANES)
    w_f = jnp.concatenate([take('a_kv'), take('b_cmp'), take('b_sel'), take('b_win'),
                           take('i_k'), take('i_w'), zeros(LANES - IDX_DIM - IDX_HEADS),
                           take('b_gate'), zeros(LANES - B_HEADS * 3)], axis=1).astype(BF16)
    w_q = jnp.concatenate([take('a_q'), take('b_q'), take('m_q'), iq], axis=1).astype(BF16)
    w_g = take('g_merge').astype(BF16)
    kvw = 4 * HEAD_DIM
    cols = dict(f=dict(a_kv=0, b_cmp=kvw, b_sel=2 * kvw, b_win=3 * kvw, i_kw=4 * kvw, b_gate=4 * kvw + LANES),
                q=dict(a_q=0, b_q=1024, m_q=2048, i_q=3072))
    return w_f, w_q, w_g, cols


def kernel(x_prompt, x_sample, mem_prompt, cache_a_kv, cache_a_idx, cache_b_cmp, cache_b_sel, state_b_win,
           cache_mem, page_table, rel_table, w_in, w_mem_kv, cmp_pe_k, cmp_w1_k, cmp_w2_k, cmp_pe_v, cmp_w1_v,
           cmp_w2_v, w_pa, w_pb, w_pm, w_o, ln1_g, ln1_b, w_up, b_up, w_down, b_down, ln2_g, ln2_b):
    depth = w_in.shape[0]
    assert depth == 1
    bsz, t, d = x_prompt.shape
    db, ds, _ = x_sample.shape
    n_mem = mem_prompt.shape[1]
    n_pool = cache_a_kv.shape[1]
    n_pages = page_table.shape[1]
    past = n_pages * PAGE_SIZE
    wb = state_b_win.shape[2]
    alpha = (2 * depth) ** 0.25
    kvw = 4 * HEAD_DIM
    assert t % TILE == 0 and ds == 8 and wb % TILE == 0 and n_pages % PAGES_PER_STEP == 0

    w_f, w_q, w_g, cols = _pack_weights(w_in[0])
    fc, qc = cols['f'], cols['q']
    w1cat = jnp.stack([jnp.concatenate([w[0][:CMP_STRIDE * HEAD_DIM], w[0][CMP_STRIDE * HEAD_DIM:]], axis=1)
                       for w in (cmp_w1_k, cmp_w1_v)]).astype(BF16)
    w1 = jnp.stack([cmp_w1_k[0], cmp_w1_v[0]]).astype(BF16)
    w2 = jnp.stack([cmp_w2_k[0], cmp_w2_v[0]]).astype(BF16)
    pe8 = jnp.broadcast_to(jnp.stack([cmp_pe_k[0].reshape(1, -1), cmp_pe_v[0].reshape(1, -1)]),
                           (2, 8, CMP_BLOCK * HEAD_DIM)).astype(BF16)
    wpa, wpb, wpm, wo = (w[0].astype(BF16) for w in (w_pa, w_pb, w_pm, w_o))
    wu, wd = w_up[0].astype(BF16), w_down[0].astype(BF16)
    nd = _num_near_tiles()
    bias = _bias_tiles(rel_table, nd)

    def dense_tail(x2d, gmat, oa, ob, om):
        x1 = _merge(x2d, gmat, oa, ob, om, wpa, wpb, wpm, wo, ln1_g, ln1_b, alpha)
        return _ffn(x1, wu, b_up, wd, b_down, ln2_g, ln2_b, alpha)

    xp = x_prompt.reshape(bsz * t, d)
    fp = _matmul(xp, w_f, F32, tn=w_f.shape[1] // 2)
    qp = _matmul(xp, w_q, BF16, tn=1024)
    gp = _matmul(xp, w_g, F32, tn=1024)
    p_a_kv = fp[:, fc['a_kv']:fc['a_kv'] + kvw]
    p_b_cmp = fp[:, fc['b_cmp']:fc['b_cmp'] + kvw]
    p_b_sel = fp[:, fc['b_sel']:fc['b_sel'] + kvw]
    p_b_win = fp[:, fc['b_win']:fc['b_win'] + kvw]
    p_a_idx = fp[:, fc['i_kw']:fc['i_kw'] + IDX_DIM]

    zp = _cmpz_dense(p_b_cmp.reshape(bsz * t // CMP_STRIDE, CMP_STRIDE * kvw), w1cat)
    cmp_k, cmp_v = _cmp_finish(zp.reshape(bsz, t // CMP_STRIDE, -1), pe8, w1, w2)
    o_a = _prompt_mixer_a(qp, fp, bias[0], bsz, t, cols, nd)
    o_b = _prompt_mixer_b(qp, fp, cmp_k, cmp_v, bias[1], bsz, t, cols, nd)
    mem_kv = _matmul(mem_prompt.reshape(bsz * n_mem, d), w_mem_kv[0].astype(BF16), F32, tn=1024)
    o_m = _mem_attend(qp, qc['m_q'], mem_kv, bsz, t, n_mem, tq=512)
    y_prompt = dense_tail(xp, gp, o_a, o_b, o_m).reshape(bsz, t, d)

    xs = x_sample.reshape(db * ds, d)
    fs = _matmul(xs, w_f, F32, tn=w_f.shape[1] // 2)
    qs = _matmul(xs, w_q, BF16, tn=1024)
    gs = _matmul(xs, w_g, F32, tn=1024)
    s_a_kv = fs[:, fc['a_kv']:fc['a_kv'] + kvw]
    s_b_cmp = fs[:, fc['b_cmp']:fc['b_cmp'] + kvw]
    s_b_sel = fs[:, fc['b_sel']:fc['b_sel'] + kvw]
    s_b_win = fs[:, fc['b_win']:fc['b_win'] + kvw]
    s_a_idx = fs[:, fc['i_kw']:fc['i_kw'] + IDX_DIM]
    length = past + ds

    def pad_new(rows2d):
        r = rows2d.reshape(db, ds, -1)
        return jnp.concatenate([r, jnp.zeros((db, TILE - ds, r.shape[2]), r.dtype)], axis=1)

    def head_major(q2d, heads):
        return q2d.reshape(db, ds, heads, -1).transpose(0, 2, 1, 3).reshape(db, heads * ds, -1)

    def block_q(q2d):
        qh = head_major(q2d, A_HEADS).reshape(db, A_KV_HEADS, A_GROUP * ds, HEAD_DIM)
        z = jnp.zeros_like(qh[:, 0])
        return jnp.concatenate([jnp.concatenate([qh[:, 0], z], axis=2),
                                jnp.concatenate([z, qh[:, 1]], axis=2)], axis=1)

    def sample_bias(tiles):
        n = tiles.shape[0]
        return tiles.reshape(n, A_KV_HEADS, A_GROUP, TILE, TILE)[:, :, :, :ds].reshape(n, A_HEADS * ds, TILE)

    iq_s = head_major(qs[:, qc['i_q']:qc['i_q'] + IDX_HEADS * LANES], IDX_HEADS)[:, :, :IDX_DIM]
    w_s = fs[:, fc['i_kw'] + IDX_DIM:fc['i_kw'] + IDX_DIM + IDX_HEADS] * (IDX_HEADS ** -0.5 * IDX_DIM ** -0.5)
    w_s = jnp.broadcast_to(head_major(w_s, IDX_HEADS), (db, IDX_HEADS * ds, LANES))
    mask_a = _sample_index_mask(page_table, iq_s, w_s, pad_new(s_a_idx),
                                cache_a_idx.reshape(n_pool, PAGE_SIZE, IDX_DIM), min(TOPK_MAX, length // 4))
    qa_blk = block_q(qs[:, qc['a_q']:qc['a_q'] + A_HEADS * HEAD_DIM])
    o_a_s = _paged_attention(page_table, qa_blk, mask_a, pad_new(s_a_kv), sample_bias(bias[0]),
                             cache_a_kv.reshape(n_pool, PAGE_SIZE * KV_ROWS, HEAD_DIM), nd, ds)

    zs = _cmpz_paged(cache_b_cmp.reshape(n_pool, PAGE_SIZE * KV_ROWS, HEAD_DIM), page_table, w1cat)
    cmp_k_s, cmp_v_s = _cmp_finish(zs.reshape(db, past // CMP_STRIDE, -1), pe8, w1, w2)
    bq2d = qs[:, qc['b_q']:qc['b_q'] + B_HEADS * HEAD_DIM]
    bq_s = head_major(bq2d, B_HEADS).reshape(db, B_KV_HEADS, B_GROUP * ds, HEAD_DIM)
    o_cmp_s, mask_b = _sample_cmp_select(bq_s, cmp_k_s, cmp_v_s, length, past, ds)
    qb_blk = block_q(bq2d)
    bias_sb = sample_bias(bias[1])
    o_sel_s = _paged_attention(page_table, qb_blk, mask_b, pad_new(s_b_sel), bias_sb,
                               cache_b_sel.reshape(n_pool, PAGE_SIZE * KV_ROWS, HEAD_DIM), nd, ds)
    gates_s = fs[:, fc['b_gate']:fc['b_gate'] + LANES].reshape(db, ds, LANES)
    o_b_s = _sample_window_combine(qb_blk, state_b_win.reshape(db, wb, kvw), pad_new(s_b_win), bias_sb,
                                   gates_s, o_cmp_s.reshape(db, B_HEADS * ds, HEAD_DIM), o_sel_s, ds)

    o_a_s = o_a_s.reshape(db, A_HEADS, ds, HEAD_DIM).transpose(0, 2, 1, 3).reshape(db * ds, -1).astype(BF16)
    o_m_s = _mem_attend(qs, qc['m_q'], cache_mem.reshape(db * n_mem, -1), db, ds, n_mem, tq=ds)
    y_sample = dense_tail(xs, gs, o_a_s, o_b_s.reshape(db * ds, -1), o_m_s).reshape(db, ds, d)

    kv6 = lambda a, n, rows: a.reshape(1, n, rows, 2, 2, HEAD_DIM)
    wp = min(WINDOW, t)
    new_win = jnp.concatenate([state_b_win.reshape(db, wb, 2, 2, HEAD_DIM)[:, ds:],
                               s_b_win.reshape(db, ds, 2, 2, HEAD_DIM)], axis=1)
    return (y_prompt, y_sample,
            kv6(p_a_kv, bsz, t), p_a_idx.reshape(1, bsz, t, IDX_DIM), kv6(p_b_cmp, bsz, t), kv6(p_b_sel, bsz, t),
            kv6(p_b_win, bsz, t)[:, :, t - wp:],
            mem_kv.reshape(1, bsz, n_mem, MEM_HEADS, 2, MEM_HEAD_DIM),
            kv6(s_a_kv, db, ds), s_a_idx.reshape(1, db, ds, IDX_DIM), kv6(s_b_cmp, db, ds), kv6(s_b_sel, db, ds),
            new_win[None])
```

```python
import functools
import math

import numpy as np
import jax
import jax.numpy as jnp
from jax import lax
from jax.experimental import pallas as pl
from jax.experimental.pallas import tpu as pltpu

F32 = jnp.float32
BF16 = jnp.bfloat16
I32 = jnp.int32

HEAD_DIM = 128
A_HEADS = 8
A_KV_HEADS = 2
A_GROUP = A_HEADS // A_KV_HEADS
IDX_HEADS = 8
IDX_DIM = 64
TOPK_MAX = 256
B_HEADS = 8
B_KV_HEADS = 2
B_GROUP = B_HEADS // B_KV_HEADS
CMP_BLOCK = 32
CMP_STRIDE = 16
CMP_RATIO = CMP_BLOCK // CMP_STRIDE
CMP_HID = 128
SEL_BLOCK = 64
SEL_TOPN = 16
WINDOW = 512
MEM_HEADS = 4
MEM_HEAD_DIM = 256
N_BUCKETS = 32
MAX_DISTANCE = 1024
LN_EPS = 1e-5
PAGE_SIZE = 128
KV_ROWS = 4

LANES = 128
VMEM_LIMIT = 56 * 1024 * 1024

TILE = 128
PAIR = 2 * TILE
ATTN_UNROLL = 2
NEG = -1e30
LOG2_E = math.log2(math.e)
QK_SCALE = HEAD_DIM ** -0.5 * LOG2_E
INT_MIN = -2 ** 31
PAGES_PER_STEP = 16
PAGES_PER_BLOCK = 8
INDEX_PAGES_PER_STEP = 64

_GROUPS = ('a_q', 'a_kv', 'i_q', 'i_k', 'i_w', 'b_q', 'b_cmp', 'b_sel', 'b_win', 'b_gate', 'm_q', 'g_merge')


def _split_sizes(d_model):
    return dict(
        a_q=A_HEADS * HEAD_DIM, a_kv=A_KV_HEADS * 2 * HEAD_DIM, i_q=IDX_HEADS * IDX_DIM, i_k=IDX_DIM,
        i_w=IDX_HEADS, b_q=B_HEADS * HEAD_DIM, b_cmp=B_KV_HEADS * 2 * HEAD_DIM, b_sel=B_KV_HEADS * 2 * HEAD_DIM,
        b_win=B_KV_HEADS * 2 * HEAD_DIM, b_gate=B_HEADS * 3, m_q=MEM_HEADS * MEM_HEAD_DIM, g_merge=3 * d_model)


def _params(n_grid, vmem=VMEM_LIMIT):
    return pltpu.CompilerParams(dimension_semantics=('arbitrary',) * n_grid, vmem_limit_bytes=vmem)


def _resident(block, index_map):
    return pl.BlockSpec(block, index_map, pipeline_mode=pl.Buffered(1))


def _mm_kernel(x_ref, w_ref, o_ref):
    o_ref[...] = jnp.dot(x_ref[...].astype(BF16), w_ref[...], preferred_element_type=F32).astype(o_ref.dtype)


def _matmul(x, w, out_dtype, tn, tm=1024, name='matmul'):
    m, k = x.shape
    n = w.shape[1]
    tm = min(tm, m)
    assert m % tm == 0 and n % tn == 0
    return pl.pallas_call(
        _mm_kernel,
        grid=(m // tm, n // tn),
        in_specs=[pl.BlockSpec((tm, k), lambda i, j: (i, 0)), pl.BlockSpec((k, tn), lambda i, j: (0, j))],
        out_specs=pl.BlockSpec((tm, tn), lambda i, j: (i, j)),
        out_shape=jax.ShapeDtypeStruct((m, n), out_dtype),
        compiler_params=_params(2), name=name,
    )(x, w)


def _rel_bucket(dist):
    d = jnp.maximum(dist, 0)
    exact = N_BUCKETS // 2
    df = jnp.maximum(d, 1).astype(F32)
    large = exact + (jnp.log(df / exact) / math.log(MAX_DISTANCE / exact) * (N_BUCKETS - exact)).astype(I32)
    return jnp.where(d < exact, d, jnp.minimum(large, N_BUCKETS - 1))


def _num_near_tiles():
    exact = N_BUCKETS // 2
    d = np.arange(1, 4 * MAX_DISTANCE, dtype=np.float64)
    large = exact + np.floor(np.log(d / exact) / math.log(MAX_DISTANCE / exact) * (N_BUCKETS - exact))
    bucket = np.where(d < exact, d, np.minimum(large, N_BUCKETS - 1))
    d_const = int(d[np.argmax(bucket == N_BUCKETS - 1)])
    return -(-(d_const + TILE // 2 + TILE - 1) // TILE)


def _bias_kernel(u_ref, o_ref, *, n_tiles, d_top):
    for dt in range(n_tiles):
        start = d_top - dt * TILE - (TILE - 1)
        row = u_ref[0, :, start:start + 2 * TILE]
        x = jnp.broadcast_to(row, (TILE, 2 * TILE))
        x = pltpu.roll(x, TILE + 1, 1, stride=1, stride_axis=0)
        o_ref[dt] = x[:, :TILE]


def _bias_tiles(rel_table, nd):
    n_tiles = nd + 1
    d_top = n_tiles * TILE
    ul = d_top + 2 * TILE
    n_heads = rel_table.shape[1]
    dist = d_top - jnp.arange(ul)
    u = (rel_table[_rel_bucket(dist)] * LOG2_E).T.reshape(n_heads, 1, ul)
    out = pl.pallas_call(
        functools.partial(_bias_kernel, n_tiles=n_tiles, d_top=d_top),
        grid=(n_heads,),
        in_specs=[pl.BlockSpec((1, 1, ul), lambda h: (h, 0, 0))],
        out_specs=pl.BlockSpec((None, n_tiles, None, None, TILE, TILE),
                               lambda h: (h // 8, 0, (h % 8) // 4, h % 4, 0, 0)),
        out_shape=jax.ShapeDtypeStruct((2, n_tiles, 2, 4, TILE, TILE), F32),
        compiler_params=_params(1), name='bias_tiles',
    )(u)
    return out.reshape(2, n_tiles, 2, 4 * TILE, TILE)


def _dot_nt(a, b):
    return lax.dot_general(a, b, (((1,), (1,)), ((), ())), preferred_element_type=F32)


def _flash_step(s, m, l, acc, v):
    m_new = jnp.maximum(m, jnp.max(s, axis=1, keepdims=True))
    alpha = jnp.exp2(m - m_new)
    p = jnp.exp2(s - m_new)
    l = alpha * l + jnp.sum(p, axis=1, keepdims=True)
    acc = alpha * acc + jnp.dot(p.astype(BF16), v, preferred_element_type=F32)
    return m_new, l, acc


def _flash_init(rows, width):
    return (jnp.full((rows, 1), NEG, F32), jnp.zeros((rows, 1), F32), jnp.zeros((rows, width), F32))


def _flash_out(m, l, acc):
    return jnp.where(m > 0.5 * NEG, acc / jnp.maximum(l, 1e-30), 0.0)


def _sortable_key(x):
    bits = pltpu.bitcast(x, I32)
    bits = jnp.where(bits == INT_MIN, 0, bits)
    return jnp.where(bits < 0, bits ^ 0x7FFFFFFF, bits)


def _kth_largest_key(count_ge, k, shape):
    def bit_body(i, t):
        cand = t + lax.shift_left(jnp.int32(1), 31 - i)
        return jnp.where(count_ge(cand) >= k, cand, t)
    return lax.fori_loop(0, 32, bit_body, jnp.full(shape, INT_MIN, I32))


def _attn_update(q_g, kk, vv, bias_of, madd, m_ref, l_ref, acc_ref, h0, n_heads):
    tk = kk.shape[0]
    s_all = _dot_nt(q_g, kk)
    ps, alphas = [], []
    for r in range(n_heads):
        s = s_all[r * TILE:(r + 1) * TILE] + (bias_of(r) + madd)
        m_prev = m_ref[h0 + r]
        m_next = jnp.maximum(m_prev, jnp.max(s, axis=1, keepdims=True))
        alpha = jnp.exp2(m_prev - m_next)
        p = jnp.exp2(s - jnp.concatenate([m_next] * (tk // LANES), axis=1))
        l_ref[h0 + r] = alpha * l_ref[h0 + r] + jnp.sum(p, axis=1, keepdims=True)
        m_ref[h0 + r] = m_next
        ps.append(p.astype(BF16))
        alphas.append(alpha)
    pv = jnp.dot(jnp.concatenate(ps, axis=0), vv, preferred_element_type=F32)
    for r in range(n_heads):
        acc_ref[h0 + r] = alphas[r] * acc_ref[h0 + r] + pv[r * TILE:(r + 1) * TILE]


def _attn_reset(m_ref, l_ref, acc_ref):
    m_ref[...] = jnp.full(m_ref.shape, NEG, F32)
    l_ref[...] = jnp.zeros(l_ref.shape, F32)
    acc_ref[...] = jnp.zeros(acc_ref.shape, F32)


def _attn_out(m_ref, l_ref, acc_ref, h):
    return jnp.where(m_ref[h] > 0.5 * NEG, acc_ref[h] / jnp.maximum(l_ref[h], 1e-30), 0.0)


def _topn_mask(scores, n):
    colf = lax.broadcasted_iota(I32, scores[0].shape, 1).astype(F32)

    def body(_, carry):
        out = []
        for sc, selm in carry:
            mx = jnp.max(sc, axis=1, keepdims=True)
            first = jnp.min(jnp.where(sc == mx, colf, 1e9), axis=1, keepdims=True)
            hit = colf == first
            selm = jnp.maximum(selm, jnp.where(hit, jnp.where(mx > -jnp.inf, 1.0, 0.0), 0.0))
            out.append((jnp.where(hit, -jnp.inf, sc), selm))
        return tuple(out)

    init = tuple((sc, jnp.zeros(sc.shape, F32)) for sc in scores)
    return [c[1] for c in lax.fori_loop(0, n, body, init)]


def _topn_mask_columns(scores_t, n):
    n_cand, n_col = scores_t[0].shape
    keys = [jnp.where(s > -jnp.inf, _sortable_key(s), INT_MIN) for s in scores_t]
    idx = lax.broadcasted_iota(I32, (n_cand, n_col), 0)

    def count(pred):
        v = jnp.where(pred, 1.0, 0.0)
        return jnp.sum(jnp.sum(v.reshape(4, n_cand // 4, n_col), axis=0), axis=0, keepdims=True)

    def bit_body(i, ts):
        step = lax.shift_left(jnp.int32(1), 31 - i)
        return tuple(jnp.where(count(k >= t + step) >= n, t + step, t) for k, t in zip(keys, ts))

    ts = lax.fori_loop(0, 32, bit_body, tuple(jnp.full((1, n_col), INT_MIN, I32) for _ in keys))
    thrs = [jnp.maximum(t, INT_MIN + 1) for t in ts]
    needs = [n - count(k > thr) for k, thr in zip(keys, thrs)]
    eq_idx = [jnp.where(k == thr, idx, jnp.int32(2 ** 30)) for k, thr in zip(keys, thrs)]
    n_bits = int(n_cand).bit_length()

    def idx_body(i, ms):
        step = lax.shift_left(jnp.int32(1), n_bits - 1 - i)
        return tuple(jnp.where(count(e < m + step) < need, m + step, m) for e, m, need in zip(eq_idx, ms, needs))

    lasts = lax.fori_loop(0, n_bits, idx_body, tuple(jnp.zeros((1, n_col), I32) for _ in keys))
    return [jnp.where(k > thr, 1.0, jnp.where(e <= last, 1.0, 0.0))
            for k, thr, e, last in zip(keys, thrs, eq_idx, lasts)]


def _softmax_rows(s, ok):
    s = jnp.where(ok, s, NEG)
    m = jnp.max(s, axis=1, keepdims=True)
    e = jnp.where(ok, jnp.exp2(s - m), 0.0)
    return e / jnp.maximum(jnp.sum(e, axis=1, keepdims=True), 1e-30)


def _block_expand(blk0, n_blk, width=TILE):
    rb = lax.broadcasted_iota(I32, (n_blk, width), 0)
    cj = lax.broadcasted_iota(I32, (n_blk, width), 1)
    target = blk0 + lax.shift_right_logical(cj, int(math.log2(SEL_BLOCK)))
    return jnp.where(rb == target, 1.0, 0.0).astype(BF16)


def _gelu(x):
    return 0.5 * x * (1.0 + jnp.tanh(math.sqrt(2.0 / math.pi) * (x + 0.044715 * (x * x * x))))


def _layer_norm(x, g, b):
    xc = x - jnp.mean(x, axis=1, keepdims=True)
    var = jnp.mean(xc * xc, axis=1, keepdims=True)
    return xc * lax.rsqrt(var + LN_EPS) * g + b


def _cmpz_compute(load, w_ref, o_ref):
    rows = o_ref.shape[0]
    for kv in range(2):
        xs = [jnp.concatenate([load(p, g, kv) for p in range(CMP_STRIDE)], axis=1) for g in range(B_KV_HEADS)]
        z = jnp.dot(jnp.concatenate(xs, axis=0).astype(BF16), w_ref[kv], preferred_element_type=F32)
        for g in range(B_KV_HEADS):
            c = (g * 2 + kv) * CMP_RATIO * CMP_HID
            o_ref[:, c:c + CMP_RATIO * CMP_HID] = z[g * rows:(g + 1) * rows]


def _cmpz_kernel(x_ref, w_ref, o_ref):
    n_col = B_KV_HEADS * 2 * HEAD_DIM

    def load(p, g, kv):
        c0 = p * n_col + (g * 2 + kv) * HEAD_DIM
        return x_ref[:, c0:c0 + HEAD_DIM]

    _cmpz_compute(load, w_ref, o_ref)


def _cmpz_dense(x2d, w1cat, tc=256):
    n = x2d.shape[0]
    tc = min(tc, n)
    assert n % tc == 0
    return pl.pallas_call(
        _cmpz_kernel,
        grid=(n // tc,),
        in_specs=[pl.BlockSpec((tc, x2d.shape[1]), lambda i: (i, 0)),
                  _resident(w1cat.shape, lambda i: (0, 0, 0))],
        out_specs=pl.BlockSpec((tc, 4 * CMP_RATIO * CMP_HID), lambda i: (i, 0)),
        out_shape=jax.ShapeDtypeStruct((n, 4 * CMP_RATIO * CMP_HID), F32),
        compiler_params=_params(1), name='cmpz_dense',
    )(x2d, w1cat)


def _page_copy(pool_ref, buf_ref, sem_ref, pid, slot, k):
    rows = pool_ref.shape[1]
    return pltpu.make_async_copy(pool_ref.at[pid], buf_ref.at[slot, pl.ds(k * rows, rows)], sem_ref.at[slot])


def _pages_per_step(pool_ref, buf_ref):
    return buf_ref.shape[1] // pool_ref.shape[1]


def _page_fetch(pt_ref, pool_ref, buf_ref, sem_ref, step, slot):
    n = _pages_per_step(pool_ref, buf_ref)
    for k in range(n):
        _page_copy(pool_ref, buf_ref, sem_ref, pt_ref[step * n + k], slot, k).start()


def _page_wait(pool_ref, buf_ref, sem_ref, slot):
    for k in range(_pages_per_step(pool_ref, buf_ref)):
        _page_copy(pool_ref, buf_ref, sem_ref, 0, slot, k).wait()


def _page_pipeline(pt_ref, pool_ref, buf_ref, sem_ref):
    step = pl.program_id(0) * pl.num_programs(1) + pl.program_id(1)
    total = pl.num_programs(0) * pl.num_programs(1)
    slot = lax.rem(step, 2)

    @pl.when(step == 0)
    def _():
        _page_fetch(pt_ref, pool_ref, buf_ref, sem_ref, step, slot)

    @pl.when(step + 1 < total)
    def _():
        _page_fetch(pt_ref, pool_ref, buf_ref, sem_ref, step + 1, 1 - slot)

    _page_wait(pool_ref, buf_ref, sem_ref, slot)
    return slot


def _cmpz_paged_kernel(pt_ref, pool_ref, w_ref, o_ref, buf_ref, sem_ref):
    slot = _page_pipeline(pt_ref, pool_ref, buf_ref, sem_ref)
    rows = o_ref.shape[0]

    def load(p, g, kv):
        return buf_ref[slot, pl.ds(p * KV_ROWS + g * 2 + kv, rows, stride=CMP_STRIDE * KV_ROWS), :]

    _cmpz_compute(load, w_ref, o_ref)


def _cmpz_paged(pool, page_table, w1cat):
    db, n_pages = page_table.shape
    chunks = PAGE_SIZE // CMP_STRIDE
    npg = n_pages // PAGES_PER_STEP
    rows = PAGES_PER_STEP * chunks
    return pl.pallas_call(
        _cmpz_paged_kernel,
        grid_spec=pltpu.PrefetchScalarGridSpec(
            num_scalar_prefetch=1,
            grid=(db, npg),
            in_specs=[pl.BlockSpec(memory_space=pl.ANY),
                      _resident(w1cat.shape, lambda b, g, pt: (0, 0, 0))],
            out_specs=pl.BlockSpec((rows, 4 * CMP_RATIO * CMP_HID), lambda b, g, pt: (b * npg + g, 0)),
            scratch_shapes=[pltpu.VMEM((2, PAGES_PER_STEP * pool.shape[1], pool.shape[2]), F32),
                            pltpu.SemaphoreType.DMA((2,))]),
        out_shape=jax.ShapeDtypeStruct((db * n_pages * chunks, 4 * CMP_RATIO * CMP_HID), F32),
        compiler_params=_params(2), name='cmpz_paged',
    )(page_table.reshape(-1), pool, w1cat)


def _cmp_finish_kernel(z_ref, pe_ref, w1_ref, w2_ref, k_ref, v_ref):
    n = z_ref.shape[1]
    for kv, o_ref in ((0, k_ref), (1, v_ref)):
        pew = jnp.dot(pe_ref[kv], w1_ref[kv], preferred_element_type=F32)[0:1]
        for g in range(B_KV_HEADS):
            c = (g * 2 + kv) * CMP_RATIO * CMP_HID
            z0 = z_ref[0, :, c:c + CMP_HID]
            z1 = z_ref[0, :, c + CMP_HID:c + 2 * CMP_HID]
            pre = z0 + pltpu.roll(z1, n - 1, 0) + pew
            out = jnp.dot(_gelu(pre).astype(BF16), w2_ref[kv], preferred_element_type=F32)
            o_ref[0, :, g * HEAD_DIM:(g + 1) * HEAD_DIM] = out.astype(o_ref.dtype)


def _cmp_finish(z3, pe8, w1, w2):
    nb, n, zc = z3.shape
    out = jax.ShapeDtypeStruct((nb, n, B_KV_HEADS * HEAD_DIM), BF16)
    return pl.pallas_call(
        _cmp_finish_kernel,
        grid=(nb,),
        in_specs=[pl.BlockSpec((1, n, zc), lambda b: (b, 0, 0)),
                  _resident(pe8.shape, lambda b: (0, 0, 0)),
                  _resident(w1.shape, lambda b: (0, 0, 0)),
                  _resident(w2.shape, lambda b: (0, 0, 0))],
        out_specs=[pl.BlockSpec((1, n, B_KV_HEADS * HEAD_DIM), lambda b: (b, 0, 0))] * 2,
        out_shape=[out, out],
        compiler_params=_params(1), name='cmp_finish',
    )(z3, pe8, w1, w2)


def _ka_kernel(iq_ref, ikw_ref, ikwq_ref, aq_ref, akv_ref, bias_ref, o_ref, keys_ref, m_ref, l_ref, acc_ref,
               *, topk, nd):
    qt = pl.program_id(1)
    last_pair = lax.shift_right_logical(qt, 1)
    key_row = lax.broadcasted_iota(I32, (PAIR, TILE), 0)
    q_pos = qt * TILE + lax.broadcasted_iota(I32, (PAIR, TILE), 1)
    w_t = ikwq_ref[...].T[IDX_DIM:IDX_DIM + IDX_HEADS] * (IDX_HEADS ** -0.5 * IDX_DIM ** -0.5)
    iq_all = jnp.concatenate([iq_ref[:, h * LANES:(h + 1) * LANES] for h in range(IDX_HEADS)], axis=0)

    def score_pair(kp, masked):
        kb = ikw_ref[pl.ds(pl.multiple_of(kp * PAIR, PAIR), PAIR), :].astype(BF16)
        s = _dot_nt(kb, iq_all)
        acc = jnp.zeros((PAIR, TILE), F32)
        for h in range(IDX_HEADS):
            acc = acc + jnp.maximum(s[:, h * TILE:(h + 1) * TILE], 0.0) * w_t[h:h + 1]
        key = _sortable_key(acc)
        if masked:
            key = jnp.where(kp * PAIR + key_row <= q_pos, key, INT_MIN)
        keys_ref[kp] = key

    def score_body(kp, c):
        score_pair(kp, False)
        return c

    lax.fori_loop(0, last_pair, score_body, 0)
    score_pair(last_pair, True)

    def count_where(pred):
        def body(kp, c):
            v = jnp.where(pred(keys_ref[kp], kp), 1.0, 0.0)
            return c + jnp.sum(v.reshape(4, PAIR // 4, TILE), axis=0)
        c = lax.fori_loop(0, last_pair + 1, body, jnp.zeros((PAIR // 4, TILE), F32))
        return jnp.sum(c, axis=0, keepdims=True)

    def count_ge(cand):
        return count_where(lambda k, kp: k >= cand)

    t = _kth_largest_key(count_ge, topk, (1, TILE))
    thr = jnp.maximum(t, INT_MIN + 1)

    cnt_gt = count_ge(thr + 1)
    need = topk - cnt_gt
    cnt_eq = count_ge(thr) - cnt_gt
    tie = jnp.where(t > INT_MIN, jnp.where(cnt_eq > need, 1.0, 0.0), 0.0)

    @pl.when(jnp.max(tie) > 0.0)
    def _():
        n_bits = int(keys_ref.shape[0] * PAIR).bit_length()

        def idx_body(i, mm):
            cand = mm + lax.shift_left(jnp.int32(1), n_bits - 1 - i)
            c = count_where(lambda k, kp: jnp.where(k == thr, kp * PAIR + key_row, INT_MIN) < cand)
            c = c - count_where(lambda k, kp: k != thr)
            return jnp.where(c < need, cand, mm)

        last = lax.fori_loop(0, n_bits, idx_body, jnp.zeros((1, TILE), I32))
        last = jnp.where(tie > 0.0, last, jnp.int32(2 ** 30))

        def demote(kp, c):
            k = keys_ref[kp]
            pos = jnp.where(k == thr, kp * PAIR + key_row, INT_MIN)
            keys_ref[kp] = jnp.where(pos > last, thr - 1, k)
            return c

        lax.fori_loop(0, last_pair + 1, demote, 0)

    q = [jnp.concatenate([aq_ref[:, (g * A_GROUP + r) * HEAD_DIM:(g * A_GROUP + r + 1) * HEAD_DIM]
                          for r in range(A_GROUP)], axis=0) for g in range(A_KV_HEADS)]
    _attn_reset(m_ref, l_ref, acc_ref)

    def att_pair(kp_raw):
        kp = jnp.minimum(kp_raw, last_pair)
        d0 = jnp.clip(qt - 2 * kp, 0, nd)
        d1 = jnp.clip(qt - 2 * kp - 1, 0, nd)
        live = jnp.where(kp_raw <= last_pair, 0.0, NEG)
        madd = jnp.where(keys_ref[kp] >= thr, live, NEG).T
        k0 = pl.multiple_of(kp * PAIR, PAIR)
        for g in range(A_KV_HEADS):
            kc = g * 2 * HEAD_DIM
            kk = akv_ref[pl.ds(k0, PAIR), kc:kc + HEAD_DIM].astype(BF16)
            vv = akv_ref[pl.ds(k0, PAIR), kc + HEAD_DIM:kc + 2 * HEAD_DIM].astype(BF16)

            def bias_of(r, g=g):
                rs = slice(r * TILE, (r + 1) * TILE)
                return jnp.concatenate([bias_ref[d0, g, rs, :], bias_ref[d1, g, rs, :]], axis=1)

            _attn_update(q[g], kk, vv, bias_of, madd, m_ref, l_ref, acc_ref, g * A_GROUP, A_GROUP)

    def att_body(j, c):
        for u in range(ATTN_UNROLL):
            att_pair(j * ATTN_UNROLL + u)
        return c

    lax.fori_loop(0, (last_pair + ATTN_UNROLL) // ATTN_UNROLL, att_body, 0)
    for h in range(A_HEADS):
        o_ref[:, h * HEAD_DIM:(h + 1) * HEAD_DIM] = _attn_out(m_ref, l_ref, acc_ref, h).astype(o_ref.dtype)


def _prompt_mixer_a(qmat, fmat, bias_a, bsz, t, cols, nd):
    nt = t // TILE
    assert nt % 2 == 0
    topk = min(TOPK_MAX, t // 4)
    qc, fc = cols['q'], cols['f']
    return pl.pallas_call(
        functools.partial(_ka_kernel, topk=topk, nd=nd),
        grid=(bsz, nt),
        in_specs=[
            pl.BlockSpec((TILE, IDX_HEADS * LANES), lambda b, i: (b * nt + i, qc['i_q'] // (IDX_HEADS * LANES))),
            _resident((t, LANES), lambda b, i: (b, fc['i_kw'] // LANES)),
            pl.BlockSpec((TILE, LANES), lambda b, i: (b * nt + i, fc['i_kw'] // LANES)),
            pl.BlockSpec((TILE, A_HEADS * HEAD_DIM), lambda b, i: (b * nt + i, qc['a_q'] // (A_HEADS * HEAD_DIM))),
            _resident((t, 4 * HEAD_DIM), lambda b, i: (b, fc['a_kv'] // (4 * HEAD_DIM))),
            _resident(bias_a.shape, lambda b, i: (0, 0, 0, 0)),
        ],
        out_specs=pl.BlockSpec((TILE, A_HEADS * HEAD_DIM), lambda b, i: (b * nt + i, 0)),
        out_shape=jax.ShapeDtypeStruct((bsz * t, A_HEADS * HEAD_DIM), BF16),
        scratch_shapes=[pltpu.VMEM((nt // 2, PAIR, TILE), I32)] + [pltpu.VMEM((A_HEADS, TILE, LANES), F32)] * 3,
        compiler_params=_params(2), name='prompt_mixer_a',
    )(qmat, fmat, fmat, qmat, fmat, bias_a)


def _kb_kernel(bq_ref, gate_ref, ck_ref, cv_ref, ovt_ref, sel_ref, w0_ref, w1_ref, w2_ref, w3_ref, w4_ref,
               bias_ref, o_ref, m_ref, l_ref, acc_ref, ocmp_ref, osel_ref, *, nd, n_sel):
    qt = pl.program_id(1)
    last_pair = lax.shift_right_logical(qt, 1)
    rows = B_GROUP * TILE
    row = lax.broadcasted_iota(I32, (TILE, TILE), 0)
    col = lax.broadcasted_iota(I32, (TILE, TILE), 1)
    colp = lax.broadcasted_iota(I32, (TILE, PAIR), 1)
    q_pos_p = qt * TILE + lax.broadcasted_iota(I32, (TILE, PAIR), 0)
    q = [jnp.concatenate([bq_ref[:, (g * B_GROUP + r) * HEAD_DIM:(g * B_GROUP + r + 1) * HEAD_DIM]
                          for r in range(B_GROUP)], axis=0) for g in range(B_KV_HEADS)]

    ncp = ck_ref.shape[1]
    q_pos = qt * TILE + lax.rem(lax.broadcasted_iota(I32, (rows, ncp), 0), TILE)
    cmp_end = lax.broadcasted_iota(I32, (rows, ncp), 1) * CMP_STRIDE + (CMP_BLOCK - 1)
    cmp_ok = cmp_end <= q_pos
    blk = row
    cur = 2 * qt + jnp.where(col >= SEL_BLOCK, 1, 0)
    scores_t = []
    for g in range(B_KV_HEADS):
        p = _softmax_rows(_dot_nt(q[g], ck_ref[0, :, g * HEAD_DIM:(g + 1) * HEAD_DIM]), cmp_ok)
        o_cmp = jnp.dot(p.astype(BF16), cv_ref[0, :, g * HEAD_DIM:(g + 1) * HEAD_DIM], preferred_element_type=F32)
        psum = p[0:TILE]
        ocmp_ref[g * B_GROUP] = o_cmp[0:TILE]
        for r in range(1, B_GROUP):
            psum = psum + p[r * TILE:(r + 1) * TILE]
            ocmp_ref[g * B_GROUP + r] = o_cmp[r * TILE:(r + 1) * TILE]
        imp_t = lax.dot_general(ovt_ref[...], psum, (((1,), (1,)), ((), ())), preferred_element_type=F32,
                                precision=lax.Precision.HIGHEST)
        forced = jnp.where(blk == 0, jnp.inf, jnp.where(blk >= cur - 1, jnp.inf, imp_t))
        scores_t.append(jnp.where(blk <= cur, forced, -jnp.inf))
    selm = [m.T.astype(BF16) for m in _topn_mask_columns(scores_t, n_sel)]

    _attn_reset(m_ref, l_ref, acc_ref)

    def sel_pair(kp_raw, diagonal):
        if diagonal:
            kp = kp_raw
        else:
            kp = jnp.minimum(kp_raw, jnp.maximum(last_pair - 1, 0))
            dead = jnp.where(kp_raw < last_pair, 0.0, NEG)
        d0 = jnp.clip(qt - 2 * kp, 0, nd)
        d1 = jnp.clip(qt - 2 * kp - 1, 0, nd)
        k0 = pl.multiple_of(kp * PAIR, PAIR)
        expand = _block_expand((PAIR // SEL_BLOCK) * kp, selm[0].shape[1], PAIR)
        for g in range(B_KV_HEADS):
            madd = (jnp.dot(selm[g], expand, preferred_element_type=F32) - 1.0) * (-NEG)
            if diagonal:
                madd = jnp.where(kp * PAIR + colp <= q_pos_p, madd, NEG)
            else:
                madd = madd + dead
            kc = g * 2 * HEAD_DIM
            kk = sel_ref[pl.ds(k0, PAIR), kc:kc + HEAD_DIM].astype(BF16)
            vv = sel_ref[pl.ds(k0, PAIR), kc + HEAD_DIM:kc + 2 * HEAD_DIM].astype(BF16)

            def bias_of(r, g=g):
                rs = slice(r * TILE, (r + 1) * TILE)
                return jnp.concatenate([bias_ref[d0, g, rs, :], bias_ref[d1, g, rs, :]], axis=1)

            _attn_update(q[g], kk, vv, bias_of, madd, m_ref, l_ref, acc_ref, g * B_GROUP, B_GROUP)

    def sel_body(j, c):
        for u in range(ATTN_UNROLL):
            sel_pair(j * ATTN_UNROLL + u, False)
        return c

    lax.fori_loop(0, (last_pair + ATTN_UNROLL - 1) // ATTN_UNROLL, sel_body, 0)
    sel_pair(last_pair, True)
    for h in range(B_HEADS):
        osel_ref[h] = _attn_out(m_ref, l_ref, acc_ref, h)

    _attn_reset(m_ref, l_ref, acc_ref)
    for k, w_ref in enumerate((w0_ref, w1_ref, w2_ref, w3_ref, w4_ref)):
        if k == 0:
            ok = col <= row
        elif k == WINDOW // TILE:
            ok = row <= col
        else:
            ok = col >= 0
        madd = jnp.where(ok, jnp.where(qt >= k, 0.0, NEG), NEG)
        for g in range(B_KV_HEADS):
            kc = g * 2 * HEAD_DIM
            kk = w_ref[:, kc:kc + HEAD_DIM].astype(BF16)
            vv = w_ref[:, kc + HEAD_DIM:kc + 2 * HEAD_DIM].astype(BF16)

            def bias_of(r, g=g, k=k):
                return bias_ref[k, g, r * TILE:(r + 1) * TILE, :]

            _attn_update(q[g], kk, vv, bias_of, madd, m_ref, l_ref, acc_ref, g * B_GROUP, B_GROUP)

    gate = jax.nn.sigmoid(gate_ref[...])
    for h in range(B_HEADS):
        o = (gate[:, 3 * h:3 * h + 1] * ocmp_ref[h] + gate[:, 3 * h + 1:3 * h + 2] * osel_ref[h]
             + gate[:, 3 * h + 2:3 * h + 3] * _attn_out(m_ref, l_ref, acc_ref, h))
        o_ref[:, h * HEAD_DIM:(h + 1) * HEAD_DIM] = o.astype(o_ref.dtype)


def _overlap_matrix(length, n_rows, n_cols):
    nc = (length - CMP_BLOCK) // CMP_STRIDE + 1
    ns = -(-length // SEL_BLOCK)
    cs = np.arange(nc) * CMP_STRIDE
    ss = np.arange(ns) * SEL_BLOCK
    ov = np.minimum(cs[:, None] + CMP_BLOCK, ss[None, :] + SEL_BLOCK) - np.maximum(cs[:, None], ss[None, :])
    out = np.zeros((n_rows, n_cols), np.float32)
    out[:nc, :ns] = np.clip(ov, 0, None).astype(np.float32) / CMP_BLOCK
    return jnp.asarray(out)


def _prompt_mixer_b(qmat, fmat, cmp_k, cmp_v, bias_b, bsz, t, cols, nd):
    nt = t // TILE
    ns = -(-t // SEL_BLOCK)
    assert ns <= LANES and WINDOW // TILE == 4 and nd >= WINDOW // TILE
    qc, fc = cols['q'], cols['f']
    ncp = cmp_k.shape[1]
    ov = _overlap_matrix(t, ncp, LANES).T
    kvw = 4 * HEAD_DIM

    def win_spec(k):
        return pl.BlockSpec((TILE, kvw), lambda b, i: (b * nt + jnp.maximum(i - k, 0), fc['b_win'] // kvw))

    return pl.pallas_call(
        functools.partial(_kb_kernel, nd=nd, n_sel=min(SEL_TOPN, ns)),
        grid=(bsz, nt),
        in_specs=[
            pl.BlockSpec((TILE, B_HEADS * HEAD_DIM), lambda b, i: (b * nt + i, qc['b_q'] // (B_HEADS * HEAD_DIM))),
            pl.BlockSpec((TILE, LANES), lambda b, i: (b * nt + i, fc['b_gate'] // LANES)),
            _resident((1, ncp, B_KV_HEADS * HEAD_DIM), lambda b, i: (b, 0, 0)),
            _resident((1, ncp, B_KV_HEADS * HEAD_DIM), lambda b, i: (b, 0, 0)),
            _resident(ov.shape, lambda b, i: (0, 0)),
            _resident((t, kvw), lambda b, i: (b, fc['b_sel'] // kvw)),
            win_spec(0), win_spec(1), win_spec(2), win_spec(3), win_spec(4),
            _resident(bias_b.shape, lambda b, i: (0, 0, 0, 0)),
        ],
        out_specs=pl.BlockSpec((TILE, B_HEADS * HEAD_DIM), lambda b, i: (b * nt + i, 0)),
        out_shape=jax.ShapeDtypeStruct((bsz * t, B_HEADS * HEAD_DIM), BF16),
        scratch_shapes=[pltpu.VMEM((B_HEADS, TILE, LANES), F32)] * 5,
        compiler_params=_params(2), name='prompt_mixer_b',
    )(qmat, fmat, cmp_k, cmp_v, ov, fmat, fmat, fmat, fmat, fmat, fmat, bias_b)


def _mem_kernel(q_ref, kv_ref, o_ref):
    scale = MEM_HEAD_DIM ** -0.5
    for h in range(MEM_HEADS):
        c = h * 2 * MEM_HEAD_DIM
        kk = kv_ref[:, c:c + MEM_HEAD_DIM].astype(BF16)
        vv = kv_ref[:, c + MEM_HEAD_DIM:c + 2 * MEM_HEAD_DIM].astype(BF16)
        s = _dot_nt(q_ref[:, h * MEM_HEAD_DIM:(h + 1) * MEM_HEAD_DIM], kk) * scale
        e = jnp.exp(s - jnp.max(s, axis=1, keepdims=True))
        p = e / jnp.sum(e, axis=1, keepdims=True)
        o = jnp.dot(p.astype(BF16), vv, preferred_element_type=F32)
        o_ref[:, h * MEM_HEAD_DIM:(h + 1) * MEM_HEAD_DIM] = o.astype(o_ref.dtype)


def _mem_attend(qmat, q_col, mem_kv2d, n_batch, rows_per_batch, n_mem, tq):
    width = MEM_HEADS * MEM_HEAD_DIM
    tq = min(tq, rows_per_batch)
    nq = rows_per_batch // tq
    return pl.pallas_call(
        _mem_kernel,
        grid=(n_batch, nq),
        in_specs=[pl.BlockSpec((tq, width), lambda b, i: (b * nq + i, q_col // width)),
                  pl.BlockSpec((n_mem, 2 * width), lambda b, i: (b, 0))],
        out_specs=pl.BlockSpec((tq, width), lambda b, i: (b * nq + i, 0)),
        out_shape=jax.ShapeDtypeStruct((n_batch * rows_per_batch, width), BF16),
        compiler_params=_params(2), name='mem_attend',
    )(qmat, mem_kv2d)


def _gated_proj_kernel(ga_ref, gb_ref, gm_ref, oa_ref, ob_ref, om_ref, wpa_ref, wpb_ref, wpm_ref, o_ref):
    merged = jax.nn.sigmoid(ga_ref[...]) * jnp.dot(oa_ref[...], wpa_ref[...], preferred_element_type=F32)
    merged = merged + jax.nn.sigmoid(gb_ref[...]) * jnp.dot(ob_ref[...], wpb_ref[...], preferred_element_type=F32)
    merged = merged + jax.nn.sigmoid(gm_ref[...]) * jnp.dot(om_ref[...], wpm_ref[...], preferred_element_type=F32)
    o_ref[...] = merged.astype(o_ref.dtype)


def _out_proj_kernel(x_ref, mg_ref, wo_ref, lg_ref, lb_ref, o_ref, *, alpha):
    y = alpha * x_ref[...] + jnp.dot(mg_ref[...], wo_ref[...], preferred_element_type=F32)
    o_ref[...] = _layer_norm(y, lg_ref[...], lb_ref[...])


def _merge(x2d, gmat, oa, ob, om, wpa, wpb, wpm, wo, ln_g, ln_b, alpha, tm=512, tn=1024):
    m, d = x2d.shape
    tm = min(tm, m)
    assert m % tm == 0 and d % tn == 0
    nj = d // tn
    row = lambda i, j: (i, 0)
    wcol = lambda i, j: (0, j)
    merged = pl.pallas_call(
        _gated_proj_kernel,
        grid=(m // tm, nj),
        in_specs=[pl.BlockSpec((tm, tn), lambda i, j: (i, j)),
                  pl.BlockSpec((tm, tn), lambda i, j: (i, nj + j)),
                  pl.BlockSpec((tm, tn), lambda i, j: (i, 2 * nj + j)),
                  pl.BlockSpec((tm, oa.shape[1]), row), pl.BlockSpec((tm, ob.shape[1]), row),
                  pl.BlockSpec((tm, om.shape[1]), row),
                  pl.BlockSpec((wpa.shape[0], tn), wcol), pl.BlockSpec((wpb.shape[0], tn), wcol),
                  pl.BlockSpec((wpm.shape[0], tn), wcol)],
        out_specs=pl.BlockSpec((tm, tn), lambda i, j: (i, j)),
        out_shape=jax.ShapeDtypeStruct((m, d), BF16),
        compiler_params=_params(2), name='gated_proj',
    )(gmat, gmat, gmat, oa, ob, om, wpa, wpb, wpm)
    fixed = lambda i: (0, 0)
    return pl.pallas_call(
        functools.partial(_out_proj_kernel, alpha=alpha),
        grid=(m // tm,),
        in_specs=[pl.BlockSpec((tm, d), lambda i: (i, 0)), pl.BlockSpec((tm, d), lambda i: (i, 0)),
                  _resident(wo.shape, fixed), _resident((1, d), fixed), _resident((1, d), fixed)],
        out_specs=pl.BlockSpec((tm, d), lambda i: (i, 0)),
        out_shape=jax.ShapeDtypeStruct((m, d), F32),
        compiler_params=_params(1), name='out_proj_ln',
    )(x2d, merged, wo, ln_g, ln_b)


def _ffn_kernel(x_ref, wu_ref, bu_ref, wd_ref, bd_ref, lg_ref, lb_ref, o_ref, acc_ref, *, alpha):
    j = pl.program_id(1)

    @pl.when(j == 0)
    def _():
        acc_ref[...] = jnp.zeros_like(acc_ref)

    u = jnp.dot(x_ref[...].astype(BF16), wu_ref[...], preferred_element_type=F32) + bu_ref[...]
    u = jnp.square(jnp.maximum(u, 0.0))
    acc_ref[...] += jnp.dot(u.astype(BF16), wd_ref[...], preferred_element_type=F32)

    @pl.when(j == pl.num_programs(1) - 1)
    def _():
        y = alpha * x_ref[...] + acc_ref[...] + bd_ref[...]
        o_ref[...] = _layer_norm(y, lg_ref[...], lb_ref[...])


def _ffn(x2d, wu, bu, wd, bd, ln_g, ln_b, alpha, tm=512, tf=1024):
    m, d = x2d.shape
    dff = wu.shape[1]
    tm = min(tm, m)
    assert m % tm == 0 and dff % tf == 0
    return pl.pallas_call(
        functools.partial(_ffn_kernel, alpha=alpha),
        grid=(m // tm, dff // tf),
        in_specs=[pl.BlockSpec((tm, d), lambda i, j: (i, 0)),
                  pl.BlockSpec((d, tf), lambda i, j: (0, j)), pl.BlockSpec((1, tf), lambda i, j: (0, j)),
                  pl.BlockSpec((tf, d), lambda i, j: (j, 0)),
                  _resident((1, d), lambda i, j: (0, 0)), _resident((1, d), lambda i, j: (0, 0)),
                  _resident((1, d), lambda i, j: (0, 0))],
        out_specs=pl.BlockSpec((tm, d), lambda i, j: (i, 0)),
        out_shape=jax.ShapeDtypeStruct((m, d), F32),
        scratch_shapes=[pltpu.VMEM((tm, d), F32)],
        compiler_params=_params(2), name='ffn_ln',
    )(x2d, wu, bu, wd, bd, ln_g, ln_b)


def _sidx_kernel(pt_ref, iq_ref, w_ref, knew_ref, pool_ref, o_ref, buf_ref, sem_ref, keys_ref, *, topk, n_q, past):
    g = pl.program_id(1)
    slot = _page_pipeline(pt_ref, pool_ref, buf_ref, sem_ref)
    span = buf_ref.shape[1]

    def scores(kb, width):
        s = _dot_nt(iq_ref[0], kb.astype(BF16))
        acc = jnp.zeros((n_q, width), F32)
        for h in range(IDX_HEADS):
            acc = acc + jnp.maximum(s[h * n_q:(h + 1) * n_q], 0.0) * w_ref[0, h * n_q:(h + 1) * n_q, 0:1]
        return _sortable_key(acc)

    keys_ref[:, pl.ds(pl.multiple_of(g * span, span), span)] = scores(buf_ref[slot], span)

    @pl.when(g == pl.num_programs(1) - 1)
    def _():
        lp = keys_ref.shape[1]
        rown = lax.broadcasted_iota(I32, (n_q, TILE), 0)
        coln = lax.broadcasted_iota(I32, (n_q, TILE), 1)
        keys_ref[:, past:lp] = jnp.where(coln <= rown, scores(knew_ref[0], TILE), INT_MIN)
        keys = keys_ref[...]
        pos = lax.broadcasted_iota(I32, (n_q, lp), 1)

        def count(pred):
            v = jnp.where(pred, 1.0, 0.0)
            part = (lp // (4 * LANES)) * LANES
            sums = [jnp.sum(v[:, i * part:(i + 1) * part], axis=1, keepdims=True) for i in range(4)]
            sums.append(jnp.sum(v[:, 4 * part:], axis=1, keepdims=True))
            return (sums[0] + sums[1]) + (sums[2] + sums[3]) + sums[4]

        t = _kth_largest_key(lambda cand: count(keys >= cand), topk, (n_q, 1))
        thr = jnp.maximum(t, INT_MIN + 1)
        cnt_gt = count(keys >= thr + 1)
        need = topk - cnt_gt
        cnt_eq = count(keys >= thr) - cnt_gt
        tie = jnp.where(t > INT_MIN, jnp.where(cnt_eq > need, 1.0, 0.0), 0.0)
        eq_pos = jnp.where(keys == thr, pos, jnp.int32(2 ** 30))
        n_bits = int(lp).bit_length()

        def idx_body(i, mm):
            cand = mm + lax.shift_left(jnp.int32(1), n_bits - 1 - i)
            return jnp.where(count(eq_pos < cand) < need, cand, mm)

        last = lax.fori_loop(0, n_bits, idx_body, jnp.zeros((n_q, 1), I32))
        last = jnp.where(tie > 0.0, last, jnp.int32(2 ** 30))
        sel = jnp.where(keys > thr, 1.0, jnp.where(keys == thr, jnp.where(pos <= last, 1.0, 0.0), 0.0))
        o_ref[0, 0] = sel


def _sample_index_mask(page_table, iq_s, w_s, knew, pool_idx, topk):
    db, n_pages = page_table.shape
    n_q = iq_s.shape[1] // IDX_HEADS
    pages = math.gcd(n_pages, INDEX_PAGES_PER_STEP)
    npg = n_pages // pages
    lp = n_pages * PAGE_SIZE + TILE
    return pl.pallas_call(
        functools.partial(_sidx_kernel, topk=topk, n_q=n_q, past=n_pages * PAGE_SIZE),
        grid_spec=pltpu.PrefetchScalarGridSpec(
            num_scalar_prefetch=1,
            grid=(db, npg),
            in_specs=[pl.BlockSpec((1,) + iq_s.shape[1:], lambda b, g, pt: (b, 0, 0)),
                      pl.BlockSpec((1,) + w_s.shape[1:], lambda b, g, pt: (b, 0, 0)),
                      pl.BlockSpec((1,) + knew.shape[1:], lambda b, g, pt: (b, 0, 0)),
                      pl.BlockSpec(memory_space=pl.ANY)],
            out_specs=pl.BlockSpec((1, 1, n_q, lp), lambda b, g, pt: (b, 0, 0, 0)),
            scratch_shapes=[pltpu.VMEM((2, pages * pool_idx.shape[1], pool_idx.shape[2]), F32),
                            pltpu.SemaphoreType.DMA((2,)),
                            pltpu.VMEM((n_q, lp), I32)]),
        out_shape=jax.ShapeDtypeStruct((db, 1, n_q, lp), F32),
        compiler_params=_params(2), name='sample_index_mask',
    )(page_table.reshape(-1), iq_s, w_s, knew, pool_idx)


def _pattn_kernel(pt_ref, q_ref, mask_ref, new_ref, bias_ref, pool_ref, o_ref, buf_ref, sem_ref,
                  m_ref, l_ref, acc_ref, *, nd, n_q, n_pages):
    g = pl.program_id(1)
    npg = pl.num_programs(1)
    slot = _page_pipeline(pt_ref, pool_ref, buf_ref, sem_ref)
    rows = q_ref.shape[1]
    half = rows // 2
    mask_groups = mask_ref.shape[1]

    @pl.when(g == 0)
    def _():
        m0, l0, a0 = _flash_init(rows, 2 * HEAD_DIM)
        m_ref[...] = m0
        l_ref[...] = l0
        acc_ref[...] = a0

    def block_update(rows_of, p0, n_tiles, carry):
        kk = jnp.concatenate([rows_of(0), rows_of(2)], axis=1).astype(BF16)
        vv = jnp.concatenate([rows_of(1), rows_of(3)], axis=1).astype(BF16)
        width = n_tiles * TILE
        mk = mask_ref[0, :, :, pl.ds(pl.multiple_of(p0 * TILE, TILE), width)]
        madd = (mk - 1.0) * (-NEG)
        reps = rows // (mask_groups * n_q)
        madd = jnp.concatenate([madd[i] for i in range(mask_groups) for _ in range(reps)], axis=0)
        bias = jnp.concatenate([bias_ref[jnp.clip(n_pages - (p0 + i), 0, nd)] for i in range(n_tiles)], axis=1)
        s = _dot_nt(q_ref[0], kk) + (bias + madd)
        return _flash_step(s, *carry, vv)

    carry = (m_ref[...], l_ref[...], acc_ref[...])
    keys = PAGES_PER_BLOCK * PAGE_SIZE
    for blk in range(PAGES_PER_STEP // PAGES_PER_BLOCK):
        def rows_of(j, blk=blk):
            return buf_ref[slot, pl.ds(blk * keys * KV_ROWS + j, keys, stride=KV_ROWS), :]
        carry = block_update(rows_of, g * PAGES_PER_STEP + blk * PAGES_PER_BLOCK, PAGES_PER_BLOCK, carry)
    m_ref[...], l_ref[...], acc_ref[...] = carry

    @pl.when(g == npg - 1)
    def _():
        def new_rows(j):
            return new_ref[0, :, j * HEAD_DIM:(j + 1) * HEAD_DIM]
        o = _flash_out(*block_update(new_rows, n_pages, 1, carry))
        o_ref[0, 0:half] = o[0:half, 0:HEAD_DIM]
        o_ref[0, half:rows] = o[half:rows, HEAD_DIM:2 * HEAD_DIM]


def _paged_attention(page_table, qblk, mask, new_kv, bias_s, pool, nd, n_q):
    db, n_pages = page_table.shape
    npg = n_pages // PAGES_PER_STEP
    rows = qblk.shape[1]
    return pl.pallas_call(
        functools.partial(_pattn_kernel, nd=nd, n_q=n_q, n_pages=n_pages),
        grid_spec=pltpu.PrefetchScalarGridSpec(
            num_scalar_prefetch=1,
            grid=(db, npg),
            in_specs=[pl.BlockSpec((1,) + qblk.shape[1:], lambda b, g, pt: (b, 0, 0)),
                      pl.BlockSpec((1,) + mask.shape[1:], lambda b, g, pt: (b, 0, 0, 0)),
                      pl.BlockSpec((1,) + new_kv.shape[1:], lambda b, g, pt: (b, 0, 0)),
                      _resident(bias_s.shape, lambda b, g, pt: (0, 0, 0)),
                      pl.BlockSpec(memory_space=pl.ANY)],
            out_specs=pl.BlockSpec((1, rows, HEAD_DIM), lambda b, g, pt: (b, 0, 0)),
            scratch_shapes=[pltpu.VMEM((2, PAGES_PER_STEP * pool.shape[1], pool.shape[2]), F32),
                            pltpu.SemaphoreType.DMA((2,)),
                            pltpu.VMEM((rows, 1), F32), pltpu.VMEM((rows, 1), F32),
                            pltpu.VMEM((rows, 2 * HEAD_DIM), F32)]),
        out_shape=jax.ShapeDtypeStruct((db, rows, HEAD_DIM), F32),
        compiler_params=_params(2), name='paged_attention',
    )(page_table.reshape(-1), qblk, mask, new_kv, bias_s, pool)


def _scmp_kernel(q_ref, ck_ref, cv_ref, ov_ref, ocmp_ref, mask_ref, selm_ref, *, past, n_q, n_sel):
    rows = q_ref.shape[2]
    ncp = ck_ref.shape[1]
    nsp = ov_ref.shape[1]
    qi =lax.rem(lax.broadcasted_iota(I32, (rows, ncp), 0), n_q)
    cmp_end = lax.broadcasted_iota(I32, (rows, ncp), 1) * CMP_STRIDE + (CMP_BLOCK - 1)
    cmp_ok = cmp_end <= past + qi
    blk = lax.broadcasted_iota(I32, (n_q, nsp), 1)
    cur = lax.shift_right_logical(past + lax.broadcasted_iota(I32, (n_q, nsp), 0), int(math.log2(SEL_BLOCK)))
    q_pos = past + lax.broadcasted_iota(I32, (n_q, TILE), 0)
    coln = lax.broadcasted_iota(I32, (n_q, TILE), 1)
    scores = []
    for g in range(B_KV_HEADS):
        p = _softmax_rows(_dot_nt(q_ref[0, g], ck_ref[0, :, g * HEAD_DIM:(g + 1) * HEAD_DIM]), cmp_ok)
        ocmp_ref[0, g] = jnp.dot(p.astype(BF16), cv_ref[0, :, g * HEAD_DIM:(g + 1) * HEAD_DIM],
                                 preferred_element_type=F32)
        psum = p[0:n_q]
        for r in range(1, rows // n_q):
            psum = psum + p[r * n_q:(r + 1) * n_q]
        imp = jnp.dot(psum, ov_ref[...], preferred_element_type=F32, precision=lax.Precision.HIGHEST)
        forced = jnp.where(blk == 0, jnp.inf, jnp.where(blk >= cur - 1, jnp.inf, imp))
        scores.append(jnp.where(blk <= cur, forced, -jnp.inf))
    for g, selm in enumerate(_topn_mask(scores, n_sel)):
        selm_ref[g] = selm

    blocks_per_tile = TILE // SEL_BLOCK
    half = lax.shift_right_logical(coln, int(math.log2(SEL_BLOCK)))

    def expand_tile(kt):
        blk0 = blocks_per_tile * kt
        win0 = pl.multiple_of(lax.shift_right_logical(blk0, int(math.log2(LANES))) * LANES, LANES)
        idx = (blk0 - win0) + half
        k0 = pl.multiple_of(kt * TILE, TILE)
        for g in range(B_KV_HEADS):
            e = jnp.take_along_axis(selm_ref[g, :, pl.ds(win0, LANES)], idx, axis=1)
            mask_ref[0, g, :, pl.ds(k0, TILE)] = jnp.where(kt * TILE + coln <= q_pos, e, 0.0)

    unroll = 4
    n_tiles = mask_ref.shape[3] // TILE

    def expand_body(j, c):
        for u in range(unroll):
            expand_tile(j * unroll + u)
        return c

    lax.fori_loop(0, n_tiles // unroll, expand_body, 0)
    for kt in range(n_tiles - n_tiles % unroll, n_tiles):
        expand_tile(jnp.int32(kt))


def _sample_cmp_select(bq_s, cmp_k, cmp_v, length, past, n_q):
    db = bq_s.shape[0]
    ncp = cmp_k.shape[1]
    ns = -(-length // SEL_BLOCK)
    nsp = -(-ns // LANES) * LANES
    ov = _overlap_matrix(length, ncp, nsp)
    lp = past + TILE
    return pl.pallas_call(
        functools.partial(_scmp_kernel, past=past, n_q=n_q, n_sel=min(SEL_TOPN, ns)),
        grid=(db,),
        in_specs=[pl.BlockSpec((1,) + bq_s.shape[1:], lambda b: (b, 0, 0, 0)),
                  pl.BlockSpec((1, ncp, B_KV_HEADS * HEAD_DIM), lambda b: (b, 0, 0)),
                  pl.BlockSpec((1, ncp, B_KV_HEADS * HEAD_DIM), lambda b: (b, 0, 0)),
                  _resident(ov.shape, lambda b: (0, 0))],
        out_specs=[pl.BlockSpec((1,) + bq_s.shape[1:], lambda b: (b, 0, 0, 0)),
                   pl.BlockSpec((1, B_KV_HEADS, n_q, lp), lambda b: (b, 0, 0, 0))],
        out_shape=[jax.ShapeDtypeStruct(bq_s.shape, F32),
                   jax.ShapeDtypeStruct((db, B_KV_HEADS, n_q, lp), F32)],
        scratch_shapes=[pltpu.VMEM((B_KV_HEADS, n_q, nsp), F32)],
        compiler_params=_params(1), name='sample_cmp_select',
    )(bq_s, cmp_k, cmp_v, ov)


def _swin_kernel(q_ref, win_ref, new_ref, bias_ref, gate_ref, ocmp_ref, osel_ref, o_ref, *, n_q):
    rows = q_ref.shape[1]
    half = rows // 2
    wb = win_ref.shape[1]
    qi =lax.rem(lax.broadcasted_iota(I32, (rows, TILE), 0), n_q)
    col = lax.broadcasted_iota(I32, (rows, TILE), 1)
    carry = _flash_init(rows, 2 * HEAD_DIM)

    def tile_update(kv, dlt, ok, carry):
        kk = jnp.concatenate([kv[:, 0:HEAD_DIM], kv[:, 2 * HEAD_DIM:3 * HEAD_DIM]], axis=1).astype(BF16)
        vv = jnp.concatenate([kv[:, HEAD_DIM:2 * HEAD_DIM], kv[:, 3 * HEAD_DIM:4 * HEAD_DIM]], axis=1).astype(BF16)
        s = _dot_nt(q_ref[0], kk) + (bias_ref[dlt] + jnp.where(ok, 0.0, NEG))
        return _flash_step(s, *carry, vv)

    for kt in range(wb // TILE):
        ok = col + kt * TILE >= qi + (wb - WINDOW)
        carry = tile_update(win_ref[0, kt * TILE:(kt + 1) * TILE, :], wb // TILE - kt, ok, carry)
    carry = tile_update(new_ref[0], 0, col <= qi, carry)
    o = _flash_out(*carry)
    gate = jax.nn.sigmoid(gate_ref[0])
    for h in range(B_HEADS):
        sl = slice(h * n_q, (h + 1) * n_q)
        ow = o[sl, 0:HEAD_DIM] if h < B_GROUP else o[sl, HEAD_DIM:2 * HEAD_DIM]
        o_ref[0, :, h * HEAD_DIM:(h + 1) * HEAD_DIM] = (
            gate[:, 3 * h:3 * h + 1] * ocmp_ref[0, sl] + gate[:, 3 * h + 1:3 * h + 2] * osel_ref[0, sl]
            + gate[:, 3 * h + 2:3 * h + 3] * ow).astype(o_ref.dtype)


def _sample_window_combine(qblk, win_state, new_win, bias_s, gates, o_cmp, o_sel, n_q):
    db, rows, _ = qblk.shape
    return pl.pallas_call(
        functools.partial(_swin_kernel, n_q=n_q),
        grid=(db,),
        in_specs=[pl.BlockSpec((1,) + qblk.shape[1:], lambda b: (b, 0, 0)),
                  pl.BlockSpec((1,) + win_state.shape[1:], lambda b: (b, 0, 0)),
                  pl.BlockSpec((1,) + new_win.shape[1:], lambda b: (b, 0, 0)),
                  _resident(bias_s.shape, lambda b: (0, 0, 0)),
                  pl.BlockSpec((1,) + gates.shape[1:], lambda b: (b, 0, 0)),
                  pl.BlockSpec((1, rows, HEAD_DIM), lambda b: (b, 0, 0)),
                  pl.BlockSpec((1, rows, HEAD_DIM), lambda b: (b, 0, 0))],
        out_specs=pl.BlockSpec((1, n_q, B_HEADS * HEAD_DIM), lambda b: (b, 0, 0)),
        out_shape=jax.ShapeDtypeStruct((db, n_q, B_HEADS * HEAD_DIM), BF16),
        compiler_params=_params(1), name='sample_window_combine',
    )(qblk, win_state, new_win, bias_s, gates, o_cmp, o_sel)


def _pack_weights(w_in):
    d = w_in.shape[0]
    sizes = _split_sizes(d)
    off, o = {}, 0
    for name in _GROUPS:
        off[name] = o
        o += sizes[name]
    take = lambda name: w_in[:, off[name]:off[name] + sizes[name]]
    zeros = lambda n: jnp.zeros((d, n), w_in.dtype)
    iq = take('i_q').reshape(d, IDX_HEADS, IDX_DIM)
    iq = jnp.concatenate([iq, jnp.zeros_like(iq)], axis=2).reshape(d, IDX_HEADS * LANES)
    w_f = jnp.concatenate([take('a_kv'), take('b_cmp'), take('b_sel'), take('b_win'),
                           take('i_k'), take('i_w'), zeros(LANES - IDX_DIM - IDX_HEADS),
                           take('b_gate'), zeros(LANES - B_HEADS * 3)], axis=1).astype(BF16)
    w_q = jnp.concatenate([take('a_q') * QK_SCALE, take('b_q') * QK_SCALE, take('m_q'), iq], axis=1).astype(BF16)
    w_g = take('g_merge').astype(BF16)
    kvw = 4 * HEAD_DIM
    cols = dict(f=dict(a_kv=0, b_cmp=kvw, b_sel=2 * kvw, b_win=3 * kvw, i_kw=4 * kvw, b_gate=4 * kvw + LANES),
                q=dict(a_q=0, b_q=1024, m_q=2048, i_q=3072))
    return w_f, w_q, w_g, cols


def kernel(x_prompt, x_sample, mem_prompt, cache_a_kv, cache_a_idx, cache_b_cmp, cache_b_sel, state_b_win,
           cache_mem, page_table, rel_table, w_in, w_mem_kv, cmp_pe_k, cmp_w1_k, cmp_w2_k, cmp_pe_v, cmp_w1_v,
           cmp_w2_v, w_pa, w_pb, w_pm, w_o, ln1_g, ln1_b, w_up, b_up, w_down, b_down, ln2_g, ln2_b):
    depth = w_in.shape[0]
    assert depth == 1
    bsz, t, d = x_prompt.shape
    db, ds, _ = x_sample.shape
    n_mem = mem_prompt.shape[1]
    n_pool = cache_a_kv.shape[1]
    n_pages = page_table.shape[1]
    past = n_pages * PAGE_SIZE
    wb = state_b_win.shape[2]
    alpha = (2 * depth) ** 0.25
    kvw = 4 * HEAD_DIM
    assert t % TILE == 0 and ds == 8 and wb % TILE == 0 and n_pages % PAGES_PER_STEP == 0

    w_f, w_q, w_g, cols = _pack_weights(w_in[0])
    fc, qc = cols['f'], cols['q']
    w1cat = jnp.stack([jnp.concatenate([w[0][:CMP_STRIDE * HEAD_DIM], w[0][CMP_STRIDE * HEAD_DIM:]], axis=1)
                       for w in (cmp_w1_k, cmp_w1_v)]).astype(BF16)
    w1 = jnp.stack([cmp_w1_k[0], cmp_w1_v[0]]).astype(BF16)
    w2 = jnp.stack([cmp_w2_k[0], cmp_w2_v[0]]).astype(BF16)
    pe8 = jnp.broadcast_to(jnp.stack([cmp_pe_k[0].reshape(1, -1), cmp_pe_v[0].reshape(1, -1)]),
                           (2, 8, CMP_BLOCK * HEAD_DIM)).astype(BF16)
    wpa, wpb, wpm, wo = (w[0].astype(BF16) for w in (w_pa, w_pb, w_pm, w_o))
    wu, wd = w_up[0].astype(BF16), w_down[0].astype(BF16)
    nd = _num_near_tiles()
    bias = _bias_tiles(rel_table, nd)

    def dense_tail(x2d, gmat, oa, ob, om):
        x1 = _merge(x2d, gmat, oa, ob, om, wpa, wpb, wpm, wo, ln1_g, ln1_b, alpha)
        return _ffn(x1, wu, b_up, wd, b_down, ln2_g, ln2_b, alpha)

    xp = x_prompt.reshape(bsz * t, d)
    fp = _matmul(xp, w_f, F32, tn=w_f.shape[1] // 2)
    qp = _matmul(xp, w_q, BF16, tn=1024)
    gp = _matmul(xp, w_g, F32, tn=1024)
    p_a_kv = fp[:, fc['a_kv']:fc['a_kv'] + kvw]
    p_b_cmp = fp[:, fc['b_cmp']:fc['b_cmp'] + kvw]
    p_b_sel = fp[:, fc['b_sel']:fc['b_sel'] + kvw]
    p_b_win = fp[:, fc['b_win']:fc['b_win'] + kvw]
    p_a_idx = fp[:, fc['i_kw']:fc['i_kw'] + IDX_DIM]

    zp = _cmpz_dense(p_b_cmp.reshape(bsz * t // CMP_STRIDE, CMP_STRIDE * kvw), w1cat)
    cmp_k, cmp_v = _cmp_finish(zp.reshape(bsz, t // CMP_STRIDE, -1), pe8, w1, w2)
    o_a = _prompt_mixer_a(qp, fp, bias[0], bsz, t, cols, nd)
    o_b = _prompt_mixer_b(qp, fp, cmp_k, cmp_v, bias[1], bsz, t, cols, nd)
    mem_kv = _matmul(mem_prompt.reshape(bsz * n_mem, d), w_mem_kv[0].astype(BF16), F32, tn=1024)
    o_m = _mem_attend(qp, qc['m_q'], mem_kv, bsz, t, n_mem, tq=512)
    y_prompt = dense_tail(xp, gp, o_a, o_b, o_m).reshape(bsz, t, d)

    xs = x_sample.reshape(db * ds, d)
    fs = _matmul(xs, w_f, F32, tn=w_f.shape[1] // 2)
    qs = _matmul(xs, w_q, BF16, tn=1024)
    gs = _matmul(xs, w_g, F32, tn=1024)
    s_a_kv = fs[:, fc['a_kv']:fc['a_kv'] + kvw]
    s_b_cmp = fs[:, fc['b_cmp']:fc['b_cmp'] + kvw]
    s_b_sel = fs[:, fc['b_sel']:fc['b_sel'] + kvw]
    s_b_win = fs[:, fc['b_win']:fc['b_win'] + kvw]
    s_a_idx = fs[:, fc['i_kw']:fc['i_kw'] + IDX_DIM]
    length = past + ds

    def pad_new(rows2d):
        r = rows2d.reshape(db, ds, -1)
        return jnp.concatenate([r, jnp.zeros((db, TILE - ds, r.shape[2]), r.dtype)], axis=1)

    def head_major(q2d, heads):
        return q2d.reshape(db, ds, heads, -1).transpose(0, 2, 1, 3).reshape(db, heads * ds, -1)

    def block_q(q2d):
        qh = head_major(q2d, A_HEADS).reshape(db, A_KV_HEADS, A_GROUP * ds, HEAD_DIM)
        z = jnp.zeros_like(qh[:, 0])
        return jnp.concatenate([jnp.concatenate([qh[:, 0], z], axis=2),
                                jnp.concatenate([z, qh[:, 1]], axis=2)], axis=1)

    def sample_bias(tiles):
        n = tiles.shape[0]
        return tiles.reshape(n, A_KV_HEADS, A_GROUP, TILE, TILE)[:, :, :, :ds].reshape(n, A_HEADS * ds, TILE)

    iq_s = head_major(qs[:, qc['i_q']:qc['i_q'] + IDX_HEADS * LANES], IDX_HEADS)[:, :, :IDX_DIM]
    w_s = fs[:, fc['i_kw'] + IDX_DIM:fc['i_kw'] + IDX_DIM + IDX_HEADS] * (IDX_HEADS ** -0.5 * IDX_DIM ** -0.5)
    w_s = jnp.broadcast_to(head_major(w_s, IDX_HEADS), (db, IDX_HEADS * ds, LANES))
    mask_a = _sample_index_mask(page_table, iq_s, w_s, pad_new(s_a_idx),
                                cache_a_idx.reshape(n_pool, PAGE_SIZE, IDX_DIM), min(TOPK_MAX, length // 4))
    qa_blk = block_q(qs[:, qc['a_q']:qc['a_q'] + A_HEADS * HEAD_DIM])
    o_a_s = _paged_attention(page_table, qa_blk, mask_a, pad_new(s_a_kv), sample_bias(bias[0]),
                             cache_a_kv.reshape(n_pool, PAGE_SIZE * KV_ROWS, HEAD_DIM), nd, ds)

    zs = _cmpz_paged(cache_b_cmp.reshape(n_pool, PAGE_SIZE * KV_ROWS, HEAD_DIM), page_table, w1cat)
    cmp_k_s, cmp_v_s = _cmp_finish(zs.reshape(db, past // CMP_STRIDE, -1), pe8, w1, w2)
    bq2d = qs[:, qc['b_q']:qc['b_q'] + B_HEADS * HEAD_DIM]
    bq_s = head_major(bq2d, B_HEADS).reshape(db, B_KV_HEADS, B_GROUP * ds, HEAD_DIM)
    o_cmp_s, mask_b = _sample_cmp_select(bq_s, cmp_k_s, cmp_v_s, length, past, ds)
    qb_blk = block_q(bq2d)
    bias_sb = sample_bias(bias[1])
    o_sel_s = _paged_attention(page_table, qb_blk, mask_b, pad_new(s_b_sel), bias_sb,
                               cache_b_sel.reshape(n_pool, PAGE_SIZE * KV_ROWS, HEAD_DIM), nd, ds)
    gates_s = fs[:, fc['b_gate']:fc['b_gate'] + LANES].reshape(db, ds, LANES)
    o_b_s = _sample_window_combine(qb_blk, state_b_win.reshape(db, wb, kvw), pad_new(s_b_win), bias_sb,
                                   gates_s, o_cmp_s.reshape(db, B_HEADS * ds, HEAD_DIM), o_sel_s, ds)

    o_a_s = o_a_s.reshape(db, A_HEADS, ds, HEAD_DIM).transpose(0, 2, 1, 3).reshape(db * ds, -1).astype(BF16)
    o_m_s = _mem_attend(qs, qc['m_q'], cache_mem.reshape(db * n_mem, -1), db, ds, n_mem, tq=ds)
    y_sample = dense_tail(xs, gs, o_a_s, o_b_s.reshape(db * ds, -1), o_m_s).reshape(db, ds, d)

    kv6 = lambda a, n, rows: a.reshape(1, n, rows, 2, 2, HEAD_DIM)
    wp = min(WINDOW, t)
    new_win = jnp.concatenate([state_b_win.reshape(db, wb, 2, 2, HEAD_DIM)[:, ds:],
                               s_b_win.reshape(db, ds, 2, 2, HEAD_DIM)], axis=1)
    return (y_prompt, y_sample,
            kv6(p_a_kv, bsz, t), p_a_idx.reshape(1, bsz, t, IDX_DIM), kv6(p_b_cmp, bsz, t), kv6(p_b_sel, bsz, t),
            kv6(p_b_win, bsz, t)[:, :, t - wp:],
            mem_kv.reshape(1, bsz, n_mem, MEM_HEADS, 2, MEM_HEAD_DIM),
            kv6(s_a_kv, db, ds), s_a_idx.reshape(1, db, ds, IDX_DIM), kv6(s_b_cmp, db, ds), kv6(s_b_sel, db, ds),
            new_win[None])
```

```python
import functools
import math

import numpy as np
import jax
import jax.numpy as jnp
from jax import lax
from jax.experimental import pallas as pl
from jax.experimental.pallas import tpu as pltpu

F32 = jnp.float32
BF16 = jnp.bfloat16
I32 = jnp.int32

HEAD_DIM = 128
A_HEADS = 8
A_KV_HEADS = 2
A_GROUP = A_HEADS // A_KV_HEADS
IDX_HEADS = 8
IDX_DIM = 64
TOPK_MAX = 256
B_HEADS = 8
B_KV_HEADS = 2
B_GROUP = B_HEADS // B_KV_HEADS
CMP_BLOCK = 32
CMP_STRIDE = 16
CMP_RATIO = CMP_BLOCK // CMP_STRIDE
CMP_HID = 128
SEL_BLOCK = 64
SEL_TOPN = 16
WINDOW = 512
MEM_HEADS = 4
MEM_HEAD_DIM = 256
N_BUCKETS = 32
MAX_DISTANCE = 1024
LN_EPS = 1e-5
PAGE_SIZE = 128
KV_ROWS = 4

LANES = 128
VMEM_LIMIT = 56 * 1024 * 1024

TILE = 128
PAIR = 2 * TILE
ATTN_UNROLL = 2
NEG = -1e30
LOG2_E = math.log2(math.e)
QK_SCALE = HEAD_DIM ** -0.5 * LOG2_E
INT_MIN = -2 ** 31
PAGES_PER_STEP = 16
PAGES_PER_BLOCK = 8
INDEX_PAGES_PER_STEP = 64

_GROUPS = ('a_q', 'a_kv', 'i_q', 'i_k', 'i_w', 'b_q', 'b_cmp', 'b_sel', 'b_win', 'b_gate', 'm_q', 'g_merge')


def _split_sizes(d_model):
    return dict(
        a_q=A_HEADS * HEAD_DIM, a_kv=A_KV_HEADS * 2 * HEAD_DIM, i_q=IDX_HEADS * IDX_DIM, i_k=IDX_DIM,
        i_w=IDX_HEADS, b_q=B_HEADS * HEAD_DIM, b_cmp=B_KV_HEADS * 2 * HEAD_DIM, b_sel=B_KV_HEADS * 2 * HEAD_DIM,
        b_win=B_KV_HEADS * 2 * HEAD_DIM, b_gate=B_HEADS * 3, m_q=MEM_HEADS * MEM_HEAD_DIM, g_merge=3 * d_model)


def _params(n_grid, vmem=VMEM_LIMIT):
    return pltpu.CompilerParams(dimension_semantics=('arbitrary',) * n_grid, vmem_limit_bytes=vmem)


def _resident(block, index_map):
    return pl.BlockSpec(block, index_map, pipeline_mode=pl.Buffered(1))


def _mm_kernel(x_ref, w_ref, o_ref):
    o_ref[...] = jnp.dot(x_ref[...].astype(BF16), w_ref[...], preferred_element_type=F32).astype(o_ref.dtype)


def _matmul(x, w, out_dtype, tn, tm=1024, name='matmul'):
    m, k = x.shape
    n = w.shape[1]
    tm = min(tm, m)
    assert m % tm == 0 and n % tn == 0
    return pl.pallas_call(
        _mm_kernel,
        grid=(m // tm, n // tn),
        in_specs=[pl.BlockSpec((tm, k), lambda i, j: (i, 0)), pl.BlockSpec((k, tn), lambda i, j: (0, j))],
        out_specs=pl.BlockSpec((tm, tn), lambda i, j: (i, j)),
        out_shape=jax.ShapeDtypeStruct((m, n), out_dtype),
        compiler_params=_params(2), name=name,
    )(x, w)


def _proj_states_kernel(x_ref, w_ref, a_ref, c_ref, s_ref, n_ref, small_ref):
    j = pl.program_id(1)
    tm = x_ref.shape[0]
    res = jnp.dot(x_ref[...].astype(BF16), w_ref[...], preferred_element_type=F32)
    for k, o_ref in enumerate((a_ref, c_ref, s_ref, n_ref)):
        @pl.when(j == k)
        def _(o_ref=o_ref):
            for c in range(KV_ROWS):
                o_ref[pl.ds(c, tm, stride=KV_ROWS), :] = res[:, c * HEAD_DIM:(c + 1) * HEAD_DIM]

    @pl.when(j == 4)
    def _():
        small_ref[...] = res


def _project_states(x, w_f, tm=1024):
    m, k = x.shape
    kvw = KV_ROWS * HEAD_DIM
    assert m % tm == 0 and w_f.shape[1] == 5 * kvw
    state = jax.ShapeDtypeStruct((m * KV_ROWS, HEAD_DIM), F32)
    return pl.pallas_call(
        _proj_states_kernel,
        grid=(m // tm, 5),
        in_specs=[pl.BlockSpec((tm, k), lambda i, j: (i, 0)), pl.BlockSpec((k, kvw), lambda i, j: (0, j))],
        out_specs=[pl.BlockSpec((tm * KV_ROWS, HEAD_DIM), lambda i, j: (i, 0))] * 4
        + [pl.BlockSpec((tm, kvw), lambda i, j: (i, 0))],
        out_shape=[state] * 4 + [jax.ShapeDtypeStruct((m, kvw), F32)],
        compiler_params=_params(2), name='project_states',
    )(x, w_f)


def _rel_bucket(dist):
    d = jnp.maximum(dist, 0)
    exact = N_BUCKETS // 2
    df = jnp.maximum(d, 1).astype(F32)
    large = exact + (jnp.log(df / exact) / math.log(MAX_DISTANCE / exact) * (N_BUCKETS - exact)).astype(I32)
    return jnp.where(d < exact, d, jnp.minimum(large, N_BUCKETS - 1))


def _num_near_tiles():
    exact = N_BUCKETS // 2
    d = np.arange(1, 4 * MAX_DISTANCE, dtype=np.float64)
    large = exact + np.floor(np.log(d / exact) / math.log(MAX_DISTANCE / exact) * (N_BUCKETS - exact))
    bucket = np.where(d < exact, d, np.minimum(large, N_BUCKETS - 1))
    d_const = int(d[np.argmax(bucket == N_BUCKETS - 1)])
    return -(-(d_const + TILE // 2 + TILE - 1) // TILE)


def _bias_kernel(u_ref, o_ref, *, n_tiles, d_top):
    for dt in range(n_tiles):
        start = d_top - dt * TILE - (TILE - 1)
        row = u_ref[0, :, start:start + 2 * TILE]
        x = jnp.broadcast_to(row, (TILE, 2 * TILE))
        x = pltpu.roll(x, TILE + 1, 1, stride=1, stride_axis=0)
        o_ref[dt] = x[:, :TILE]


def _bias_tiles(rel_table, nd):
    n_tiles = nd + 1
    d_top = n_tiles * TILE
    ul = d_top + 2 * TILE
    n_heads = rel_table.shape[1]
    dist = d_top - jnp.arange(ul)
    u = (rel_table[_rel_bucket(dist)] * LOG2_E).T.reshape(n_heads, 1, ul)
    out = pl.pallas_call(
        functools.partial(_bias_kernel, n_tiles=n_tiles, d_top=d_top),
        grid=(n_heads,),
        in_specs=[pl.BlockSpec((1, 1, ul), lambda h: (h, 0, 0))],
        out_specs=pl.BlockSpec((None, n_tiles, None, None, TILE, TILE),
                               lambda h: (h // 8, 0, (h % 8) // 4, h % 4, 0, 0)),
        out_shape=jax.ShapeDtypeStruct((2, n_tiles, 2, 4, TILE, TILE), F32),
        compiler_params=_params(1), name='bias_tiles',
    )(u)
    return out.reshape(2, n_tiles, 2, 4 * TILE, TILE)


def _dot_nt(a, b):
    return lax.dot_general(a, b, (((1,), (1,)), ((), ())), preferred_element_type=F32)


def _flash_step(s, m, l, acc, v):
    m_new = jnp.maximum(m, jnp.max(s, axis=1, keepdims=True))
    alpha = jnp.exp2(m - m_new)
    p = jnp.exp2(s - m_new)
    l = alpha * l + jnp.sum(p, axis=1, keepdims=True)
    acc = alpha * acc + jnp.dot(p.astype(BF16), v, preferred_element_type=F32)
    return m_new, l, acc


def _flash_init(rows, width):
    return (jnp.full((rows, 1), NEG, F32), jnp.zeros((rows, 1), F32), jnp.zeros((rows, width), F32))


def _flash_out(m, l, acc):
    return jnp.where(m > 0.5 * NEG, acc / jnp.maximum(l, 1e-30), 0.0)


def _sortable_key(x):
    bits = pltpu.bitcast(x, I32)
    bits = jnp.where(bits == INT_MIN, 0, bits)
    return jnp.where(bits < 0, bits ^ 0x7FFFFFFF, bits)


def _kth_largest_key(count_ge, k, shape):
    def bit_body(i, t):
        cand = t + lax.shift_left(jnp.int32(1), 31 - i)
        return jnp.where(count_ge(cand) >= k, cand, t)
    return lax.fori_loop(0, 32, bit_body, jnp.full(shape, INT_MIN, I32))


def _attn_update(q_g, kk, vv, bias_of, madd, m_ref, l_ref, acc_ref, h0, n_heads):
    tk = kk.shape[0]
    s_all = _dot_nt(q_g, kk)
    ps, alphas = [], []
    for r in range(n_heads):
        s = s_all[r * TILE:(r + 1) * TILE] + (bias_of(r) + madd)
        m_prev = m_ref[h0 + r]
        m_next = jnp.maximum(m_prev, jnp.max(s, axis=1, keepdims=True))
        alpha = jnp.exp2(m_prev - m_next)
        p = jnp.exp2(s - jnp.concatenate([m_next] * (tk // LANES), axis=1))
        l_ref[h0 + r] = alpha * l_ref[h0 + r] + jnp.sum(p, axis=1, keepdims=True)
        m_ref[h0 + r] = m_next
        ps.append(p.astype(BF16))
        alphas.append(alpha)
    pv = jnp.dot(jnp.concatenate(ps, axis=0), vv, preferred_element_type=F32)
    for r in range(n_heads):
        acc_ref[h0 + r] = alphas[r] * acc_ref[h0 + r] + pv[r * TILE:(r + 1) * TILE]


def _attn_reset(m_ref, l_ref, acc_ref):
    m_ref[...] = jnp.full(m_ref.shape, NEG, F32)
    l_ref[...] = jnp.zeros(l_ref.shape, F32)
    acc_ref[...] = jnp.zeros(acc_ref.shape, F32)


def _attn_out(m_ref, l_ref, acc_ref, h):
    return jnp.where(m_ref[h] > 0.5 * NEG, acc_ref[h] / jnp.maximum(l_ref[h], 1e-30), 0.0)


def _topn_mask(scores, n):
    colf = lax.broadcasted_iota(I32, scores[0].shape, 1).astype(F32)

    def body(_, carry):
        out = []
        for sc, selm in carry:
            mx = jnp.max(sc, axis=1, keepdims=True)
            first = jnp.min(jnp.where(sc == mx, colf, 1e9), axis=1, keepdims=True)
            hit = colf == first
            selm = jnp.maximum(selm, jnp.where(hit, jnp.where(mx > -jnp.inf, 1.0, 0.0), 0.0))
            out.append((jnp.where(hit, -jnp.inf, sc), selm))
        return tuple(out)

    init = tuple((sc, jnp.zeros(sc.shape, F32)) for sc in scores)
    return [c[1] for c in lax.fori_loop(0, n, body, init)]


def _topn_mask_columns(scores_t, n):
    n_cand, n_col = scores_t[0].shape
    keys = [jnp.where(s > -jnp.inf, _sortable_key(s), INT_MIN) for s in scores_t]
    idx = lax.broadcasted_iota(I32, (n_cand, n_col), 0)

    def count(pred):
        v = jnp.where(pred, 1.0, 0.0)
        return jnp.sum(jnp.sum(v.reshape(4, n_cand // 4, n_col), axis=0), axis=0, keepdims=True)

    def bit_body(i, ts):
        step = lax.shift_left(jnp.int32(1), 31 - i)
        return tuple(jnp.where(count(k >= t + step) >= n, t + step, t) for k, t in zip(keys, ts))

    ts = lax.fori_loop(0, 32, bit_body, tuple(jnp.full((1, n_col), INT_MIN, I32) for _ in keys))
    thrs = [jnp.maximum(t, INT_MIN + 1) for t in ts]
    needs = [n - count(k > thr) for k, thr in zip(keys, thrs)]
    eq_idx = [jnp.where(k == thr, idx, jnp.int32(2 ** 30)) for k, thr in zip(keys, thrs)]
    n_bits = int(n_cand).bit_length()

    def idx_body(i, ms):
        step = lax.shift_left(jnp.int32(1), n_bits - 1 - i)
        return tuple(jnp.where(count(e < m + step) < need, m + step, m) for e, m, need in zip(eq_idx, ms, needs))

    lasts = lax.fori_loop(0, n_bits, idx_body, tuple(jnp.zeros((1, n_col), I32) for _ in keys))
    return [jnp.where(k > thr, 1.0, jnp.where(e <= last, 1.0, 0.0))
            for k, thr, e, last in zip(keys, thrs, eq_idx, lasts)]


def _softmax_rows(s, ok):
    s = jnp.where(ok, s, NEG)
    m = jnp.max(s, axis=1, keepdims=True)
    e = jnp.where(ok, jnp.exp2(s - m), 0.0)
    return e / jnp.maximum(jnp.sum(e, axis=1, keepdims=True), 1e-30)


def _block_expand(blk0, n_blk, width=TILE):
    rb = lax.broadcasted_iota(I32, (n_blk, width), 0)
    cj = lax.broadcasted_iota(I32, (n_blk, width), 1)
    target = blk0 + lax.shift_right_logical(cj, int(math.log2(SEL_BLOCK)))
    return jnp.where(rb == target, 1.0, 0.0).astype(BF16)


def _gelu(x):
    return 0.5 * x * (1.0 + jnp.tanh(math.sqrt(2.0 / math.pi) * (x + 0.044715 * (x * x * x))))


def _layer_norm(x, g, b):
    xc = x - jnp.mean(x, axis=1, keepdims=True)
    var = jnp.mean(xc * xc, axis=1, keepdims=True)
    return xc * lax.rsqrt(var + LN_EPS) * g + b


def _cmpz_compute(load, w_ref, o_ref):
    rows = o_ref.shape[0]
    for kv in range(2):
        xs = [jnp.concatenate([load(p, g, kv) for p in range(CMP_STRIDE)], axis=1) for g in range(B_KV_HEADS)]
        z = jnp.dot(jnp.concatenate(xs, axis=0).astype(BF16), w_ref[kv], preferred_element_type=F32)
        for g in range(B_KV_HEADS):
            c = (g * 2 + kv) * CMP_RATIO * CMP_HID
            o_ref[:, c:c + CMP_RATIO * CMP_HID] = z[g * rows:(g + 1) * rows]


def _cmpz_kernel(x_ref, w_ref, o_ref):
    rows = o_ref.shape[0]

    def load(p, g, kv):
        return x_ref[pl.ds(p * KV_ROWS + g * 2 + kv, rows, stride=CMP_STRIDE * KV_ROWS), :]

    _cmpz_compute(load, w_ref, o_ref)


def _cmpz_dense(state_rows, w1cat, tc=256):
    chunk_rows = CMP_STRIDE * KV_ROWS
    n = state_rows.shape[0] // chunk_rows
    tc = min(tc, n)
    assert n % tc == 0
    return pl.pallas_call(
        _cmpz_kernel,
        grid=(n // tc,),
        in_specs=[pl.BlockSpec((tc * chunk_rows, HEAD_DIM), lambda i: (i, 0)),
                  _resident(w1cat.shape, lambda i: (0, 0, 0))],
        out_specs=pl.BlockSpec((tc, 4 * CMP_RATIO * CMP_HID), lambda i: (i, 0)),
        out_shape=jax.ShapeDtypeStruct((n, 4 * CMP_RATIO * CMP_HID), F32),
        compiler_params=_params(1), name='cmpz_dense',
    )(state_rows, w1cat)


def _page_copy(pool_ref, buf_ref, sem_ref, pid, slot, k):
    rows = pool_ref.shape[1]
    return pltpu.make_async_copy(pool_ref.at[pid], buf_ref.at[slot, pl.ds(k * rows, rows)], sem_ref.at[slot])


def _pages_per_step(pool_ref, buf_ref):
    return buf_ref.shape[1] // pool_ref.shape[1]


def _page_fetch(pt_ref, pool_ref, buf_ref, sem_ref, step, slot):
    n = _pages_per_step(pool_ref, buf_ref)
    for k in range(n):
        _page_copy(pool_ref, buf_ref, sem_ref, pt_ref[step * n + k], slot, k).start()


def _page_wait(pool_ref, buf_ref, sem_ref, slot):
    for k in range(_pages_per_step(pool_ref, buf_ref)):
        _page_copy(pool_ref, buf_ref, sem_ref, 0, slot, k).wait()


def _page_pipeline(pt_ref, pool_ref, buf_ref, sem_ref):
    step = pl.program_id(0) * pl.num_programs(1) + pl.program_id(1)
    total = pl.num_programs(0) * pl.num_programs(1)
    slot = lax.rem(step, 2)

    @pl.when(step == 0)
    def _():
        _page_fetch(pt_ref, pool_ref, buf_ref, sem_ref, step, slot)

    @pl.when(step + 1 < total)
    def _():
        _page_fetch(pt_ref, pool_ref, buf_ref, sem_ref, step + 1, 1 - slot)

    _page_wait(pool_ref, buf_ref, sem_ref, slot)
    return slot


def _cmpz_paged_kernel(pt_ref, pool_ref, w_ref, o_ref, buf_ref, sem_ref):
    slot = _page_pipeline(pt_ref, pool_ref, buf_ref, sem_ref)
    rows = o_ref.shape[0]

    def load(p, g, kv):
        return buf_ref[slot, pl.ds(p * KV_ROWS + g * 2 + kv, rows, stride=CMP_STRIDE * KV_ROWS), :]

    _cmpz_compute(load, w_ref, o_ref)


def _cmpz_paged(pool, page_table, w1cat):
    db, n_pages = page_table.shape
    chunks = PAGE_SIZE // CMP_STRIDE
    npg = n_pages // PAGES_PER_STEP
    rows = PAGES_PER_STEP * chunks
    return pl.pallas_call(
        _cmpz_paged_kernel,
        grid_spec=pltpu.PrefetchScalarGridSpec(
            num_scalar_prefetch=1,
            grid=(db, npg),
            in_specs=[pl.BlockSpec(memory_space=pl.ANY),
                      _resident(w1cat.shape, lambda b, g, pt: (0, 0, 0))],
            out_specs=pl.BlockSpec((rows, 4 * CMP_RATIO * CMP_HID), lambda b, g, pt: (b * npg + g, 0)),
            scratch_shapes=[pltpu.VMEM((2, PAGES_PER_STEP * pool.shape[1], pool.shape[2]), F32),
                            pltpu.SemaphoreType.DMA((2,))]),
        out_shape=jax.ShapeDtypeStruct((db * n_pages * chunks, 4 * CMP_RATIO * CMP_HID), F32),
        compiler_params=_params(2), name='cmpz_paged',
    )(page_table.reshape(-1), pool, w1cat)


def _cmp_finish_kernel(z_ref, pe_ref, w1_ref, w2_ref, k_ref, v_ref):
    n = z_ref.shape[1]
    for kv, o_ref in ((0, k_ref), (1, v_ref)):
        pew = jnp.dot(pe_ref[kv], w1_ref[kv], preferred_element_type=F32)[0:1]
        for g in range(B_KV_HEADS):
            c = (g * 2 + kv) * CMP_RATIO * CMP_HID
            z0 = z_ref[0, :, c:c + CMP_HID]
            z1 = z_ref[0, :, c + CMP_HID:c + 2 * CMP_HID]
            pre = z0 + pltpu.roll(z1, n - 1, 0) + pew
            out = jnp.dot(_gelu(pre).astype(BF16), w2_ref[kv], preferred_element_type=F32)
            o_ref[0, :, g * HEAD_DIM:(g + 1) * HEAD_DIM] = out.astype(o_ref.dtype)


def _cmp_finish(z3, pe8, w1, w2):
    nb, n, zc = z3.shape
    out = jax.ShapeDtypeStruct((nb, n, B_KV_HEADS * HEAD_DIM), BF16)
    return pl.pallas_call(
        _cmp_finish_kernel,
        grid=(nb,),
        in_specs=[pl.BlockSpec((1, n, zc), lambda b: (b, 0, 0)),
                  _resident(pe8.shape, lambda b: (0, 0, 0)),
                  _resident(w1.shape, lambda b: (0, 0, 0)),
                  _resident(w2.shape, lambda b: (0, 0, 0))],
        out_specs=[pl.BlockSpec((1, n, B_KV_HEADS * HEAD_DIM), lambda b: (b, 0, 0))] * 2,
        out_shape=[out, out],
        compiler_params=_params(1), name='cmp_finish',
    )(z3, pe8, w1, w2)


def _ka_kernel(iq_ref, ikw_ref, ikwq_ref, aq_ref, akv_ref, bias_ref, o_ref, keys_ref, m_ref, l_ref, acc_ref,
               *, topk, nd):
    qt = pl.program_id(1)
    last_pair = lax.shift_right_logical(qt, 1)
    key_row = lax.broadcasted_iota(I32, (PAIR, TILE), 0)
    q_pos = qt * TILE + lax.broadcasted_iota(I32, (PAIR, TILE), 1)
    w_t = ikwq_ref[...].T[IDX_DIM:IDX_DIM + IDX_HEADS] * (IDX_HEADS ** -0.5 * IDX_DIM ** -0.5)
    iq_all = jnp.concatenate([iq_ref[:, h * LANES:(h + 1) * LANES] for h in range(IDX_HEADS)], axis=0)

    def score_pair(kp, masked):
        kb = ikw_ref[pl.ds(pl.multiple_of(kp * PAIR, PAIR), PAIR), :].astype(BF16)
        s = _dot_nt(kb, iq_all)
        acc = jnp.zeros((PAIR, TILE), F32)
        for h in range(IDX_HEADS):
            acc = acc + jnp.maximum(s[:, h * TILE:(h + 1) * TILE], 0.0) * w_t[h:h + 1]
        key = _sortable_key(acc)
        if masked:
            key = jnp.where(kp * PAIR + key_row <= q_pos, key, INT_MIN)
        keys_ref[kp] = key

    def score_body(kp, c):
        score_pair(kp, False)
        return c

    lax.fori_loop(0, last_pair, score_body, 0)
    score_pair(last_pair, True)

    def count_where(pred):
        def body(kp, c):
            v = jnp.where(pred(keys_ref[kp], kp), 1.0, 0.0)
            return c + jnp.sum(v.reshape(4, PAIR // 4, TILE), axis=0)
        c = lax.fori_loop(0, last_pair + 1, body, jnp.zeros((PAIR // 4, TILE), F32))
        return jnp.sum(c, axis=0, keepdims=True)

    def count_ge(cand):
        return count_where(lambda k, kp: k >= cand)

    t = _kth_largest_key(count_ge, topk, (1, TILE))
    thr = jnp.maximum(t, INT_MIN + 1)

    cnt_gt = count_ge(thr + 1)
    need = topk - cnt_gt
    cnt_eq = count_ge(thr) - cnt_gt
    tie = jnp.where(t > INT_MIN, jnp.where(cnt_eq > need, 1.0, 0.0), 0.0)

    @pl.when(jnp.max(tie) > 0.0)
    def _():
        n_bits = int(keys_ref.shape[0] * PAIR).bit_length()

        def idx_body(i, mm):
            cand = mm + lax.shift_left(jnp.int32(1), n_bits - 1 - i)
            c = count_where(lambda k, kp: jnp.where(k == thr, kp * PAIR + key_row, INT_MIN) < cand)
            c = c - count_where(lambda k, kp: k != thr)
            return jnp.where(c < need, cand, mm)

        last = lax.fori_loop(0, n_bits, idx_body, jnp.zeros((1, TILE), I32))
        last = jnp.where(tie > 0.0, last, jnp.int32(2 ** 30))

        def demote(kp, c):
            k = keys_ref[kp]
            pos = jnp.where(k == thr, kp * PAIR + key_row, INT_MIN)
            keys_ref[kp] = jnp.where(pos > last, thr - 1, k)
            return c

        lax.fori_loop(0, last_pair + 1, demote, 0)

    q = [jnp.concatenate([aq_ref[:, (g * A_GROUP + r) * HEAD_DIM:(g * A_GROUP + r + 1) * HEAD_DIM]
                          for r in range(A_GROUP)], axis=0) for g in range(A_KV_HEADS)]
    _attn_reset(m_ref, l_ref, acc_ref)

    def att_pair(kp_raw):
        kp = jnp.minimum(kp_raw, last_pair)
        d0 = jnp.clip(qt - 2 * kp, 0, nd)
        d1 = jnp.clip(qt - 2 * kp - 1, 0, nd)
        live = jnp.where(kp_raw <= last_pair, 0.0, NEG)
        madd = jnp.where(keys_ref[kp] >= thr, live, NEG).T
        r0 = pl.multiple_of(kp * (PAIR * KV_ROWS), PAIR * KV_ROWS)
        for g in range(A_KV_HEADS):
            kk = akv_ref[pl.ds(r0 + 2 * g, PAIR, stride=KV_ROWS), :].astype(BF16)
            vv = akv_ref[pl.ds(r0 + 2 * g + 1, PAIR, stride=KV_ROWS), :].astype(BF16)

            def bias_of(r, g=g):
                rs = slice(r * TILE, (r + 1) * TILE)
                return jnp.concatenate([bias_ref[d0, g, rs, :], bias_ref[d1, g, rs, :]], axis=1)

            _attn_update(q[g], kk, vv, bias_of, madd, m_ref, l_ref, acc_ref, g * A_GROUP, A_GROUP)

    def att_body(j, c):
        for u in range(ATTN_UNROLL):
            att_pair(j * ATTN_UNROLL + u)
        return c

    lax.fori_loop(0, (last_pair + ATTN_UNROLL) // ATTN_UNROLL, att_body, 0)
    for h in range(A_HEADS):
        o_ref[:, h * HEAD_DIM:(h + 1) * HEAD_DIM] = _attn_out(m_ref, l_ref, acc_ref, h).astype(o_ref.dtype)


def _prompt_mixer_a(qmat, small, a_kv_rows, bias_a, bsz, t, cols, nd):
    nt = t // TILE
    assert nt % 2 == 0
    topk = min(TOPK_MAX, t // 4)
    qc = cols['q']
    return pl.pallas_call(
        functools.partial(_ka_kernel, topk=topk, nd=nd),
        grid=(bsz, nt),
        in_specs=[
            pl.BlockSpec((TILE, IDX_HEADS * LANES), lambda b, i: (b * nt + i, qc['i_q'] // (IDX_HEADS * LANES))),
            _resident((t, LANES), lambda b, i: (b, 0)),
            pl.BlockSpec((TILE, LANES), lambda b, i: (b * nt + i, 0)),
            pl.BlockSpec((TILE, A_HEADS * HEAD_DIM), lambda b, i: (b * nt + i, qc['a_q'] // (A_HEADS * HEAD_DIM))),
            _resident((t * KV_ROWS, HEAD_DIM), lambda b, i: (b, 0)),
            _resident(bias_a.shape, lambda b, i: (0, 0, 0, 0)),
        ],
        out_specs=pl.BlockSpec((TILE, A_HEADS * HEAD_DIM), lambda b, i: (b * nt + i, 0)),
        out_shape=jax.ShapeDtypeStruct((bsz * t, A_HEADS * HEAD_DIM), BF16),
        scratch_shapes=[pltpu.VMEM((nt // 2, PAIR, TILE), I32)] + [pltpu.VMEM((A_HEADS, TILE, LANES), F32)] * 3,
        compiler_params=_params(2), name='prompt_mixer_a',
    )(qmat, small, small, qmat, a_kv_rows, bias_a)


def _kb_kernel(bq_ref, gate_ref, ck_ref, cv_ref, ovt_ref, sel_ref, w0_ref, w1_ref, w2_ref, w3_ref, w4_ref,
               bias_ref, o_ref, m_ref, l_ref, acc_ref, ocmp_ref, osel_ref, *, nd, n_sel):
    qt = pl.program_id(1)
    last_pair = lax.shift_right_logical(qt, 1)
    rows = B_GROUP * TILE
    row = lax.broadcasted_iota(I32, (TILE, TILE), 0)
    col = lax.broadcasted_iota(I32, (TILE, TILE), 1)
    colp = lax.broadcasted_iota(I32, (TILE, PAIR), 1)
    q_pos_p = qt * TILE + lax.broadcasted_iota(I32, (TILE, PAIR), 0)
    q = [jnp.concatenate([bq_ref[:, (g * B_GROUP + r) * HEAD_DIM:(g * B_GROUP + r + 1) * HEAD_DIM]
                          for r in range(B_GROUP)], axis=0) for g in range(B_KV_HEADS)]

    ncp = ck_ref.shape[1]
    q_pos = qt * TILE + lax.rem(lax.broadcasted_iota(I32, (rows, ncp), 0), TILE)
    cmp_end = lax.broadcasted_iota(I32, (rows, ncp), 1) * CMP_STRIDE + (CMP_BLOCK - 1)
    cmp_ok = cmp_end <= q_pos
    blk = row
    cur = 2 * qt + jnp.where(col >= SEL_BLOCK, 1, 0)
    scores_t = []
    for g in range(B_KV_HEADS):
        p = _softmax_rows(_dot_nt(q[g], ck_ref[0, :, g * HEAD_DIM:(g + 1) * HEAD_DIM]), cmp_ok)
        o_cmp = jnp.dot(p.astype(BF16), cv_ref[0, :, g * HEAD_DIM:(g + 1) * HEAD_DIM], preferred_element_type=F32)
        psum = p[0:TILE]
        ocmp_ref[g * B_GROUP] = o_cmp[0:TILE]
        for r in range(1, B_GROUP):
            psum = psum + p[r * TILE:(r + 1) * TILE]
            ocmp_ref[g * B_GROUP + r] = o_cmp[r * TILE:(r + 1) * TILE]
        imp_t = lax.dot_general(ovt_ref[...], psum, (((1,), (1,)), ((), ())), preferred_element_type=F32,
                                precision=lax.Precision.HIGHEST)
        forced = jnp.where(blk == 0, jnp.inf, jnp.where(blk >= cur - 1, jnp.inf, imp_t))
        scores_t.append(jnp.where(blk <= cur, forced, -jnp.inf))
    selm = [m.T.astype(BF16) for m in _topn_mask_columns(scores_t, n_sel)]

    _attn_reset(m_ref, l_ref, acc_ref)

    def sel_pair(kp_raw, diagonal):
        if diagonal:
            kp = kp_raw
        else:
            kp = jnp.minimum(kp_raw, jnp.maximum(last_pair - 1, 0))
            dead = jnp.where(kp_raw < last_pair, 0.0, NEG)
        d0 = jnp.clip(qt - 2 * kp, 0, nd)
        d1 = jnp.clip(qt - 2 * kp - 1, 0, nd)
        r0 = pl.multiple_of(kp * (PAIR * KV_ROWS), PAIR * KV_ROWS)
        expand = _block_expand((PAIR // SEL_BLOCK) * kp, selm[0].shape[1], PAIR)
        for g in range(B_KV_HEADS):
            madd = (jnp.dot(selm[g], expand, preferred_element_type=F32) - 1.0) * (-NEG)
            if diagonal:
                madd = jnp.where(kp * PAIR + colp <= q_pos_p, madd, NEG)
            else:
                madd = madd + dead
            kk = sel_ref[pl.ds(r0 + 2 * g, PAIR, stride=KV_ROWS), :].astype(BF16)
            vv = sel_ref[pl.ds(r0 + 2 * g + 1, PAIR, stride=KV_ROWS), :].astype(BF16)

            def bias_of(r, g=g):
                rs = slice(r * TILE, (r + 1) * TILE)
                return jnp.concatenate([bias_ref[d0, g, rs, :], bias_ref[d1, g, rs, :]], axis=1)

            _attn_update(q[g], kk, vv, bias_of, madd, m_ref, l_ref, acc_ref, g * B_GROUP, B_GROUP)

    def sel_body(j, c):
        for u in range(ATTN_UNROLL):
            sel_pair(j * ATTN_UNROLL + u, False)
        return c

    lax.fori_loop(0, (last_pair + ATTN_UNROLL - 1) // ATTN_UNROLL, sel_body, 0)
    sel_pair(last_pair, True)
    for h in range(B_HEADS):
        osel_ref[h] = _attn_out(m_ref, l_ref, acc_ref, h)

    _attn_reset(m_ref, l_ref, acc_ref)
    w_refs = (w0_ref, w1_ref, w2_ref, w3_ref, w4_ref)

    def win_mask(k):
        if k == 0:
            ok = col <= row
        elif k == WINDOW // TILE:
            ok = row <= col
        else:
            ok = col >= 0
        return jnp.where(ok, jnp.where(qt >= k, 0.0, NEG), NEG)

    for ks in ((0, 1), (2, 3), (4,)):
        madd = jnp.concatenate([win_mask(k) for k in ks], axis=1)
        for g in range(B_KV_HEADS):
            kk = jnp.concatenate([w_refs[k][pl.ds(2 * g, TILE, stride=KV_ROWS), :] for k in ks], axis=0).astype(BF16)
            vv = jnp.concatenate([w_refs[k][pl.ds(2 * g + 1, TILE, stride=KV_ROWS), :] for k in ks],
                                 axis=0).astype(BF16)

            def bias_of(r, g=g, ks=ks):
                return jnp.concatenate([bias_ref[k, g, r * TILE:(r + 1) * TILE, :] for k in ks], axis=1)

            _attn_update(q[g], kk, vv, bias_of, madd, m_ref, l_ref, acc_ref, g * B_GROUP, B_GROUP)

    gate = jax.nn.sigmoid(gate_ref[...])
    for h in range(B_HEADS):
        o = (gate[:, 3 * h:3 * h + 1] * ocmp_ref[h] + gate[:, 3 * h + 1:3 * h + 2] * osel_ref[h]
             + gate[:, 3 * h + 2:3 * h + 3] * _attn_out(m_ref, l_ref, acc_ref, h))
        o_ref[:, h * HEAD_DIM:(h + 1) * HEAD_DIM] = o.astype(o_ref.dtype)


def _overlap_matrix(length, n_rows, n_cols):
    nc = (length - CMP_BLOCK) // CMP_STRIDE + 1
    ns = -(-length // SEL_BLOCK)
    cs = np.arange(nc) * CMP_STRIDE
    ss = np.arange(ns) * SEL_BLOCK
    ov = np.minimum(cs[:, None] + CMP_BLOCK, ss[None, :] + SEL_BLOCK) - np.maximum(cs[:, None], ss[None, :])
    out = np.zeros((n_rows, n_cols), np.float32)
    out[:nc, :ns] = np.clip(ov, 0, None).astype(np.float32) / CMP_BLOCK
    return jnp.asarray(out)


def _prompt_mixer_b(qmat, small, sel_rows, win_rows, cmp_k, cmp_v, bias_b, bsz, t, cols, nd):
    nt = t // TILE
    ns = -(-t // SEL_BLOCK)
    assert ns <= LANES and WINDOW // TILE == 4 and nd >= WINDOW // TILE
    qc = cols['q']
    ncp = cmp_k.shape[1]
    ov = _overlap_matrix(t, ncp, LANES).T

    def win_spec(k):
        return pl.BlockSpec((TILE * KV_ROWS, HEAD_DIM), lambda b, i: (b * nt + jnp.maximum(i - k, 0), 0))

    return pl.pallas_call(
        functools.partial(_kb_kernel, nd=nd, n_sel=min(SEL_TOPN, ns)),
        grid=(bsz, nt),
        in_specs=[
            pl.BlockSpec((TILE, B_HEADS * HEAD_DIM), lambda b, i: (b * nt + i, qc['b_q'] // (B_HEADS * HEAD_DIM))),
            pl.BlockSpec((TILE, LANES), lambda b, i: (b * nt + i, 1)),
            _resident((1, ncp, B_KV_HEADS * HEAD_DIM), lambda b, i: (b, 0, 0)),
            _resident((1, ncp, B_KV_HEADS * HEAD_DIM), lambda b, i: (b, 0, 0)),
            _resident(ov.shape, lambda b, i: (0, 0)),
            _resident((t * KV_ROWS, HEAD_DIM), lambda b, i: (b, 0)),
            win_spec(0), win_spec(1), win_spec(2), win_spec(3), win_spec(4),
            _resident(bias_b.shape, lambda b, i: (0, 0, 0, 0)),
        ],
        out_specs=pl.BlockSpec((TILE, B_HEADS * HEAD_DIM), lambda b, i: (b * nt + i, 0)),
        out_shape=jax.ShapeDtypeStruct((bsz * t, B_HEADS * HEAD_DIM), BF16),
        scratch_shapes=[pltpu.VMEM((B_HEADS, TILE, LANES), F32)] * 5,
        compiler_params=_params(2), name='prompt_mixer_b',
    )(qmat, small, cmp_k, cmp_v, ov, sel_rows, win_rows, win_rows, win_rows, win_rows, win_rows, bias_b)


def _mem_kernel(q_ref, kv_ref, o_ref):
    scale = MEM_HEAD_DIM ** -0.5
    for h in range(MEM_HEADS):
        c = h * 2 * MEM_HEAD_DIM
        kk = kv_ref[:, c:c + MEM_HEAD_DIM].astype(BF16)
        vv = kv_ref[:, c + MEM_HEAD_DIM:c + 2 * MEM_HEAD_DIM].astype(BF16)
        s = _dot_nt(q_ref[:, h * MEM_HEAD_DIM:(h + 1) * MEM_HEAD_DIM], kk) * scale
        e = jnp.exp(s - jnp.max(s, axis=1, keepdims=True))
        p = e / jnp.sum(e, axis=1, keepdims=True)
        o = jnp.dot(p.astype(BF16), vv, preferred_element_type=F32)
        o_ref[:, h * MEM_HEAD_DIM:(h + 1) * MEM_HEAD_DIM] = o.astype(o_ref.dtype)


def _mem_attend(qmat, q_col, mem_kv2d, n_batch, rows_per_batch, n_mem, tq):
    width = MEM_HEADS * MEM_HEAD_DIM
    tq = min(tq, rows_per_batch)
    nq = rows_per_batch // tq
    return pl.pallas_call(
        _mem_kernel,
        grid=(n_batch, nq),
        in_specs=[pl.BlockSpec((tq, width), lambda b, i: (b * nq + i, q_col // width)),
                  pl.BlockSpec((n_mem, 2 * width), lambda b, i: (b, 0))],
        out_specs=pl.BlockSpec((tq, width), lambda b, i: (b * nq + i, 0)),
        out_shape=jax.ShapeDtypeStruct((n_batch * rows_per_batch, width), BF16),
        compiler_params=_params(2), name='mem_attend',
    )(qmat, mem_kv2d)


def _gated_proj_kernel(ga_ref, gb_ref, gm_ref, oa_ref, ob_ref, om_ref, wpa_ref, wpb_ref, wpm_ref, o_ref):
    merged = jax.nn.sigmoid(ga_ref[...]) * jnp.dot(oa_ref[...], wpa_ref[...], preferred_element_type=F32)
    merged = merged + jax.nn.sigmoid(gb_ref[...]) * jnp.dot(ob_ref[...], wpb_ref[...], preferred_element_type=F32)
    merged = merged + jax.nn.sigmoid(gm_ref[...]) * jnp.dot(om_ref[...], wpm_ref[...], preferred_element_type=F32)
    o_ref[...] = merged.astype(o_ref.dtype)


def _out_proj_kernel(x_ref, mg_ref, wo_ref, lg_ref, lb_ref, o_ref, *, alpha):
    y = alpha * x_ref[...] + jnp.dot(mg_ref[...], wo_ref[...], preferred_element_type=F32)
    o_ref[...] = _layer_norm(y, lg_ref[...], lb_ref[...])


def _merge(x2d, gmat, oa, ob, om, wpa, wpb, wpm, wo, ln_g, ln_b, alpha, tm=512, tn=1024):
    m, d = x2d.shape
    tm = min(tm, m)
    assert m % tm == 0 and d % tn == 0
    nj = d // tn
    row = lambda i, j: (i, 0)
    wcol = lambda i, j: (0, j)
    merged = pl.pallas_call(
        _gated_proj_kernel,
        grid=(m // tm, nj),
        in_specs=[pl.BlockSpec((tm, tn), lambda i, j: (i, j)),
                  pl.BlockSpec((tm, tn), lambda i, j: (i, nj + j)),
                  pl.BlockSpec((tm, tn), lambda i, j: (i, 2 * nj + j)),
                  pl.BlockSpec((tm, oa.shape[1]), row), pl.BlockSpec((tm, ob.shape[1]), row),
                  pl.BlockSpec((tm, om.shape[1]), row),
                  pl.BlockSpec((wpa.shape[0], tn), wcol), pl.BlockSpec((wpb.shape[0], tn), wcol),
                  pl.BlockSpec((wpm.shape[0], tn), wcol)],
        out_specs=pl.BlockSpec((tm, tn), lambda i, j: (i, j)),
        out_shape=jax.ShapeDtypeStruct((m, d), BF16),
        compiler_params=_params(2), name='gated_proj',
    )(gmat, gmat, gmat, oa, ob, om, wpa, wpb, wpm)
    fixed = lambda i: (0, 0)
    return pl.pallas_call(
        functools.partial(_out_proj_kernel, alpha=alpha),
        grid=(m // tm,),
        in_specs=[pl.BlockSpec((tm, d), lambda i: (i, 0)), pl.BlockSpec((tm, d), lambda i: (i, 0)),
                  _resident(wo.shape, fixed), _resident((1, d), fixed), _resident((1, d), fixed)],
        out_specs=pl.BlockSpec((tm, d), lambda i: (i, 0)),
        out_shape=jax.ShapeDtypeStruct((m, d), F32),
        compiler_params=_params(1), name='out_proj_ln',
    )(x2d, merged, wo, ln_g, ln_b)


def _ffn_kernel(x_ref, wu_ref, bu_ref, wd_ref, bd_ref, lg_ref, lb_ref, o_ref, acc_ref, *, alpha):
    j = pl.program_id(1)

    @pl.when(j == 0)
    def _():
        acc_ref[...] = jnp.zeros_like(acc_ref)

    u = jnp.dot(x_ref[...].astype(BF16), wu_ref[...], preferred_element_type=F32) + bu_ref[...]
    u = jnp.square(jnp.maximum(u, 0.0))
    acc_ref[...] += jnp.dot(u.astype(BF16), wd_ref[...], preferred_element_type=F32)

    @pl.when(j == pl.num_programs(1) - 1)
    def _():
        y = alpha * x_ref[...] + acc_ref[...] + bd_ref[...]
        o_ref[...] = _layer_norm(y, lg_ref[...], lb_ref[...])


def _ffn(x2d, wu, bu, wd, bd, ln_g, ln_b, alpha, tm=512, tf=1024):
    m, d = x2d.shape
    dff = wu.shape[1]
    tm = min(tm, m)
    assert m % tm == 0 and dff % tf == 0
    return pl.pallas_call(
        functools.partial(_ffn_kernel, alpha=alpha),
        grid=(m // tm, dff // tf),
        in_specs=[pl.BlockSpec((tm, d), lambda i, j: (i, 0)),
                  pl.BlockSpec((d, tf), lambda i, j: (0, j)), pl.BlockSpec((1, tf), lambda i, j: (0, j)),
                  pl.BlockSpec((tf, d), lambda i, j: (j, 0)),
                  _resident((1, d), lambda i, j: (0, 0)), _resident((1, d), lambda i, j: (0, 0)),
                  _resident((1, d), lambda i, j: (0, 0))],
        out_specs=pl.BlockSpec((tm, d), lambda i, j: (i, 0)),
        out_shape=jax.ShapeDtypeStruct((m, d), F32),
        scratch_shapes=[pltpu.VMEM((tm, d), F32)],
        compiler_params=_params(2), name='ffn_ln',
    )(x2d, wu, bu, wd, bd, ln_g, ln_b)


def _sidx_kernel(pt_ref, iq_ref, w_ref, knew_ref, pool_ref, o_ref, buf_ref, sem_ref, keys_ref, *, topk, n_q, past):
    g = pl.program_id(1)
    slot = _page_pipeline(pt_ref, pool_ref, buf_ref, sem_ref)
    pages = _pages_per_step(pool_ref, buf_ref)
    group = 8

    def scores(k_t):
        s = jnp.dot(iq_ref[0], k_t.astype(BF16), preferred_element_type=F32)
        acc = jnp.zeros((n_q, k_t.shape[1]), F32)
        for h in range(IDX_HEADS):
            acc = acc + jnp.maximum(s[h * n_q:(h + 1) * n_q], 0.0) * w_ref[0, h * n_q:(h + 1) * n_q, 0:1]
        return _sortable_key(acc)

    def page_body(j, c):
        k_t = jnp.concatenate(
            [buf_ref[slot, pl.ds(pl.multiple_of((j * group + u) * IDX_DIM, IDX_DIM), IDX_DIM), :]
             for u in range(group)], axis=1)
        width = group * PAGE_SIZE
        keys_ref[:, pl.ds(pl.multiple_of((g * pages + j * group) * PAGE_SIZE, width), width)] = scores(k_t)
        return c

    lax.fori_loop(0, pages // group, page_body, 0)

    @pl.when(g == pl.num_programs(1) - 1)
    def _():
        lp = keys_ref.shape[1]
        rown = lax.broadcasted_iota(I32, (n_q, TILE), 0)
        coln = lax.broadcasted_iota(I32, (n_q, TILE), 1)
        keys_ref[:, past:lp] = jnp.where(coln <= rown, scores(knew_ref[0]), INT_MIN)
        keys = keys_ref[...]
        pos = lax.broadcasted_iota(I32, (n_q, lp), 1)

        def count(pred):
            v = jnp.where(pred, 1.0, 0.0)
            part = (lp // (4 * LANES)) * LANES
            sums = [jnp.sum(v[:, i * part:(i + 1) * part], axis=1, keepdims=True) for i in range(4)]
            sums.append(jnp.sum(v[:, 4 * part:], axis=1, keepdims=True))
            return (sums[0] + sums[1]) + (sums[2] + sums[3]) + sums[4]

        t = _kth_largest_key(lambda cand: count(keys >= cand), topk, (n_q, 1))
        thr = jnp.maximum(t, INT_MIN + 1)
        cnt_gt = count(keys >= thr + 1)
        need = topk - cnt_gt
        cnt_eq = count(keys >= thr) - cnt_gt
        tie = jnp.where(t > INT_MIN, jnp.where(cnt_eq > need, 1.0, 0.0), 0.0)
        eq_pos = jnp.where(keys == thr, pos, jnp.int32(2 ** 30))
        n_bits = int(lp).bit_length()

        def idx_body(i, mm):
            cand = mm + lax.shift_left(jnp.int32(1), n_bits - 1 - i)
            return jnp.where(count(eq_pos < cand) < need, cand, mm)

        last = lax.fori_loop(0, n_bits, idx_body, jnp.zeros((n_q, 1), I32))
        last = jnp.where(tie > 0.0, last, jnp.int32(2 ** 30))
        sel = jnp.where(keys > thr, 1.0, jnp.where(keys == thr, jnp.where(pos <= last, 1.0, 0.0), 0.0))
        o_ref[0, 0] = sel


def _sample_index_mask(page_table, iq_s, w_s, knew, pool_idx, topk):
    db, n_pages = page_table.shape
    n_q = iq_s.shape[1] // IDX_HEADS
    pages = math.gcd(n_pages, INDEX_PAGES_PER_STEP)
    assert pages % 8 == 0 and pool_idx.shape[1:] == (IDX_DIM, PAGE_SIZE)
    npg = n_pages // pages
    lp = n_pages * PAGE_SIZE + TILE
    return pl.pallas_call(
        functools.partial(_sidx_kernel, topk=topk, n_q=n_q, past=n_pages * PAGE_SIZE),
        grid_spec=pltpu.PrefetchScalarGridSpec(
            num_scalar_prefetch=1,
            grid=(db, npg),
            in_specs=[pl.BlockSpec((1,) + iq_s.shape[1:], lambda b, g, pt: (b, 0, 0)),
                      pl.BlockSpec((1,) + w_s.shape[1:], lambda b, g, pt: (b, 0, 0)),
                      pl.BlockSpec((1,) + knew.shape[1:], lambda b, g, pt: (b, 0, 0)),
                      pl.BlockSpec(memory_space=pl.ANY)],
            out_specs=pl.BlockSpec((1, 1, n_q, lp), lambda b, g, pt: (b, 0, 0, 0)),
            scratch_shapes=[pltpu.VMEM((2, pages * pool_idx.shape[1], pool_idx.shape[2]), F32),
                            pltpu.SemaphoreType.DMA((2,)),
                            pltpu.VMEM((n_q, lp), I32)]),
        out_shape=jax.ShapeDtypeStruct((db, 1, n_q, lp), F32),
        compiler_params=_params(2), name='sample_index_mask',
    )(page_table.reshape(-1), iq_s, w_s, knew, pool_idx)


def _pattn_kernel(pt_ref, q_ref, mask_ref, new_ref, bias_ref, pool_ref, o_ref, buf_ref, sem_ref,
                  m_ref, l_ref, acc_ref, *, nd, n_q, n_pages):
    g = pl.program_id(1)
    npg = pl.num_programs(1)
    slot = _page_pipeline(pt_ref, pool_ref, buf_ref, sem_ref)
    rows = q_ref.shape[1]
    half = rows // 2
    mask_groups = mask_ref.shape[1]

    @pl.when(g == 0)
    def _():
        m0, l0, a0 = _flash_init(rows, 2 * HEAD_DIM)
        m_ref[...] = m0
        l_ref[...] = l0
        acc_ref[...] = a0

    def block_update(rows_of, p0, n_tiles, carry):
        kk = jnp.concatenate([rows_of(0), rows_of(2)], axis=1).astype(BF16)
        vv = jnp.concatenate([rows_of(1), rows_of(3)], axis=1).astype(BF16)
        width = n_tiles * TILE
        mk = mask_ref[0, :, :, pl.ds(pl.multiple_of(p0 * TILE, TILE), width)]
        madd = (mk - 1.0) * (-NEG)
        reps = rows // (mask_groups * n_q)
        madd = jnp.concatenate([madd[i] for i in range(mask_groups) for _ in range(reps)], axis=0)
        bias = jnp.concatenate([bias_ref[jnp.clip(n_pages - (p0 + i), 0, nd)] for i in range(n_tiles)], axis=1)
        s = _dot_nt(q_ref[0], kk) + (bias + madd)
        return _flash_step(s, *carry, vv)

    carry = (m_ref[...], l_ref[...], acc_ref[...])
    keys = PAGES_PER_BLOCK * PAGE_SIZE
    for blk in range(PAGES_PER_STEP // PAGES_PER_BLOCK):
        def rows_of(j, blk=blk):
            return buf_ref[slot, pl.ds(blk * keys * KV_ROWS + j, keys, stride=KV_ROWS), :]
        carry = block_update(rows_of, g * PAGES_PER_STEP + blk * PAGES_PER_BLOCK, PAGES_PER_BLOCK, carry)
    m_ref[...], l_ref[...], acc_ref[...] = carry

    @pl.when(g == npg - 1)
    def _():
        def new_rows(j):
            return new_ref[0, :, j * HEAD_DIM:(j + 1) * HEAD_DIM]
        o = _flash_out(*block_update(new_rows, n_pages, 1, carry))
        o_ref[0, 0:half] = o[0:half, 0:HEAD_DIM]
        o_ref[0, half:rows] = o[half:rows, HEAD_DIM:2 * HEAD_DIM]


def _paged_attention(page_table, qblk, mask, new_kv, bias_s, pool, nd, n_q):
    db, n_pages = page_table.shape
    npg = n_pages // PAGES_PER_STEP
    rows = qblk.shape[1]
    return pl.pallas_call(
        functools.partial(_pattn_kernel, nd=nd, n_q=n_q, n_pages=n_pages),
        grid_spec=pltpu.PrefetchScalarGridSpec(
            num_scalar_prefetch=1,
            grid=(db, npg),
            in_specs=[pl.BlockSpec((1,) + qblk.shape[1:], lambda b, g, pt: (b, 0, 0)),
                      pl.BlockSpec((1,) + mask.shape[1:], lambda b, g, pt: (b, 0, 0, 0)),
                      pl.BlockSpec((1,) + new_kv.shape[1:], lambda b, g, pt: (b, 0, 0)),
                      _resident(bias_s.shape, lambda b, g, pt: (0, 0, 0)),
                      pl.BlockSpec(memory_space=pl.ANY)],
            out_specs=pl.BlockSpec((1, rows, HEAD_DIM), lambda b, g, pt: (b, 0, 0)),
            scratch_shapes=[pltpu.VMEM((2, PAGES_PER_STEP * pool.shape[1], pool.shape[2]), F32),
                            pltpu.SemaphoreType.DMA((2,)),
                            pltpu.VMEM((rows, 1), F32), pltpu.VMEM((rows, 1), F32),
                            pltpu.VMEM((rows, 2 * HEAD_DIM), F32)]),
        out_shape=jax.ShapeDtypeStruct((db, rows, HEAD_DIM), F32),
        compiler_params=_params(2), name='paged_attention',
    )(page_table.reshape(-1), qblk, mask, new_kv, bias_s, pool)


def _scmp_kernel(q_ref, ck_ref, cv_ref, ov_ref, ocmp_ref, mask_ref, selm_ref, *, past, n_q, n_sel):
    rows = q_ref.shape[2]
    ncp = ck_ref.shape[1]
    nsp = ov_ref.shape[1]
    qi =lax.rem(lax.broadcasted_iota(I32, (rows, ncp), 0), n_q)
    cmp_end = lax.broadcasted_iota(I32, (rows, ncp), 1) * CMP_STRIDE + (CMP_BLOCK - 1)
    cmp_ok = cmp_end <= past + qi
    blk = lax.broadcasted_iota(I32, (n_q, nsp), 1)
    cur = lax.shift_right_logical(past + lax.broadcasted_iota(I32, (n_q, nsp), 0), int(math.log2(SEL_BLOCK)))
    q_pos = past + lax.broadcasted_iota(I32, (n_q, TILE), 0)
    coln = lax.broadcasted_iota(I32, (n_q, TILE), 1)
    scores = []
    for g in range(B_KV_HEADS):
        p = _softmax_rows(_dot_nt(q_ref[0, g], ck_ref[0, :, g * HEAD_DIM:(g + 1) * HEAD_DIM]), cmp_ok)
        ocmp_ref[0, g] = jnp.dot(p.astype(BF16), cv_ref[0, :, g * HEAD_DIM:(g + 1) * HEAD_DIM],
                                 preferred_element_type=F32)
        psum = p[0:n_q]
        for r in range(1, rows // n_q):
            psum = psum + p[r * n_q:(r + 1) * n_q]
        imp = jnp.dot(psum, ov_ref[...], preferred_element_type=F32, precision=lax.Precision.HIGHEST)
        forced = jnp.where(blk == 0, jnp.inf, jnp.where(blk >= cur - 1, jnp.inf, imp))
        scores.append(jnp.where(blk <= cur, forced, -jnp.inf))
    for g, selm in enumerate(_topn_mask(scores, n_sel)):
        selm_ref[g] = selm

    blocks_per_tile = TILE // SEL_BLOCK
    half = lax.shift_right_logical(coln, int(math.log2(SEL_BLOCK)))

    def expand_tile(kt):
        blk0 = blocks_per_tile * kt
        win0 = pl.multiple_of(lax.shift_right_logical(blk0, int(math.log2(LANES))) * LANES, LANES)
        idx = (blk0 - win0) + half
        k0 = pl.multiple_of(kt * TILE, TILE)
        for g in range(B_KV_HEADS):
            e = jnp.take_along_axis(selm_ref[g, :, pl.ds(win0, LANES)], idx, axis=1)
            mask_ref[0, g, :, pl.ds(k0, TILE)] = jnp.where(kt * TILE + coln <= q_pos, e, 0.0)

    unroll = 4
    n_tiles = mask_ref.shape[3] // TILE

    def expand_body(j, c):
        for u in range(unroll):
            expand_tile(j * unroll + u)
        return c

    lax.fori_loop(0, n_tiles // unroll, expand_body, 0)
    for kt in range(n_tiles - n_tiles % unroll, n_tiles):
        expand_tile(jnp.int32(kt))


def _sample_cmp_select(bq_s, cmp_k, cmp_v, length, past, n_q):
    db = bq_s.shape[0]
    ncp = cmp_k.shape[1]
    ns = -(-length // SEL_BLOCK)
    nsp = -(-ns // LANES) * LANES
    ov = _overlap_matrix(length, ncp, nsp)
    lp = past + TILE
    return pl.pallas_call(
        functools.partial(_scmp_kernel, past=past, n_q=n_q, n_sel=min(SEL_TOPN, ns)),
        grid=(db,),
        in_specs=[pl.BlockSpec((1,) + bq_s.shape[1:], lambda b: (b, 0, 0, 0)),
                  pl.BlockSpec((1, ncp, B_KV_HEADS * HEAD_DIM), lambda b: (b, 0, 0)),
                  pl.BlockSpec((1, ncp, B_KV_HEADS * HEAD_DIM), lambda b: (b, 0, 0)),
                  _resident(ov.shape, lambda b: (0, 0))],
        out_specs=[pl.BlockSpec((1,) + bq_s.shape[1:], lambda b: (b, 0, 0, 0)),
                   pl.BlockSpec((1, B_KV_HEADS, n_q, lp), lambda b: (b, 0, 0, 0))],
        out_shape=[jax.ShapeDtypeStruct(bq_s.shape, F32),
                   jax.ShapeDtypeStruct((db, B_KV_HEADS, n_q, lp), F32)],
        scratch_shapes=[pltpu.VMEM((B_KV_HEADS, n_q, nsp), F32)],
        compiler_params=_params(1), name='sample_cmp_select',
    )(bq_s, cmp_k, cmp_v, ov)


def _swin_kernel(q_ref, win_ref, new_ref, bias_ref, gate_ref, ocmp_ref, osel_ref, o_ref, *, n_q):
    rows = q_ref.shape[1]
    half = rows // 2
    wb = win_ref.shape[1]
    qi =lax.rem(lax.broadcasted_iota(I32, (rows, TILE), 0), n_q)
    col = lax.broadcasted_iota(I32, (rows, TILE), 1)
    carry = _flash_init(rows, 2 * HEAD_DIM)

    def tile_update(kv, dlt, ok, carry):
        kk = jnp.concatenate([kv[:, 0:HEAD_DIM], kv[:, 2 * HEAD_DIM:3 * HEAD_DIM]], axis=1).astype(BF16)
        vv = jnp.concatenate([kv[:, HEAD_DIM:2 * HEAD_DIM], kv[:, 3 * HEAD_DIM:4 * HEAD_DIM]], axis=1).astype(BF16)
        s = _dot_nt(q_ref[0], kk) + (bias_ref[dlt] + jnp.where(ok, 0.0, NEG))
        return _flash_step(s, *carry, vv)

    for kt in range(wb // TILE):
        ok = col + kt * TILE >= qi + (wb - WINDOW)
        carry = tile_update(win_ref[0, kt * TILE:(kt + 1) * TILE, :], wb // TILE - kt, ok, carry)
    carry = tile_update(new_ref[0], 0, col <= qi, carry)
    o = _flash_out(*carry)
    gate = jax.nn.sigmoid(gate_ref[0])
    for h in range(B_HEADS):
        sl = slice(h * n_q, (h + 1) * n_q)
        ow = o[sl, 0:HEAD_DIM] if h < B_GROUP else o[sl, HEAD_DIM:2 * HEAD_DIM]
        o_ref[0, :, h * HEAD_DIM:(h + 1) * HEAD_DIM] = (
            gate[:, 3 * h:3 * h + 1] * ocmp_ref[0, sl] + gate[:, 3 * h + 1:3 * h + 2] * osel_ref[0, sl]
            + gate[:, 3 * h + 2:3 * h + 3] * ow).astype(o_ref.dtype)


def _sample_window_combine(qblk, win_state, new_win, bias_s, gates, o_cmp, o_sel, n_q):
    db, rows, _ = qblk.shape
    return pl.pallas_call(
        functools.partial(_swin_kernel, n_q=n_q),
        grid=(db,),
        in_specs=[pl.BlockSpec((1,) + qblk.shape[1:], lambda b: (b, 0, 0)),
                  pl.BlockSpec((1,) + win_state.shape[1:], lambda b: (b, 0, 0)),
                  pl.BlockSpec((1,) + new_win.shape[1:], lambda b: (b, 0, 0)),
                  _resident(bias_s.shape, lambda b: (0, 0, 0)),
                  pl.BlockSpec((1,) + gates.shape[1:], lambda b: (b, 0, 0)),
                  pl.BlockSpec((1, rows, HEAD_DIM), lambda b: (b, 0, 0)),
                  pl.BlockSpec((1, rows, HEAD_DIM), lambda b: (b, 0, 0))],
        out_specs=pl.BlockSpec((1, n_q, B_HEADS * HEAD_DIM), lambda b: (b, 0, 0)),
        out_shape=jax.ShapeDtypeStruct((db, n_q, B_HEADS * HEAD_DIM), BF16),
        compiler_params=_params(1), name='sample_window_combine',
    )(qblk, win_state, new_win, bias_s, gates, o_cmp, o_sel)


def _pack_weights(w_in):
    d = w_in.shape[0]
    sizes = _split_sizes(d)
    off, o = {}, 0
    for name in _GROUPS:
        off[name] = o
        o += sizes[name]
    take = lambda name: w_in[:, off[name]:off[name] + sizes[name]]
    zeros = lambda n: jnp.zeros((d, n), w_in.dtype)
    iq = take('i_q').reshape(d, IDX_HEADS, IDX_DIM)
    iq = jnp.concatenate([iq, jnp.zeros_like(iq)], axis=2).reshape(d, IDX_HEADS * LANES)
    w_f = jnp.concatenate([take('a_kv'), take('b_cmp'), take('b_sel'), take('b_win'),
                           take('i_k'), take('i_w'), zeros(LANES - IDX_DIM - IDX_HEADS),
                           take('b_gate'), zeros(LANES - B_HEADS * 3), zeros(2 * LANES)], axis=1).astype(BF16)
    w_q = jnp.concatenate([take('a_q') * QK_SCALE, take('b_q') * QK_SCALE, take('m_q'), iq], axis=1).astype(BF16)
    w_g = take('g_merge').astype(BF16)
    kvw = 4 * HEAD_DIM
    cols = dict(f=dict(a_kv=0, b_cmp=kvw, b_sel=2 * kvw, b_win=3 * kvw, i_kw=4 * kvw, b_gate=4 * kvw + LANES),
                q=dict(a_q=0, b_q=1024, m_q=2048, i_q=3072))
    return w_f, w_q, w_g, cols


def kernel(x_prompt, x_sample, mem_prompt, cache_a_kv, cache_a_idx, cache_b_cmp, cache_b_sel, state_b_win,
           cache_mem, page_table, rel_table, w_in, w_mem_kv, cmp_pe_k, cmp_w1_k, cmp_w2_k, cmp_pe_v, cmp_w1_v,
           cmp_w2_v, w_pa, w_pb, w_pm, w_o, ln1_g, ln1_b, w_up, b_up, w_down, b_down, ln2_g, ln2_b):
    depth = w_in.shape[0]
    assert depth == 1
    bsz, t, d = x_prompt.shape
    db, ds, _ = x_sample.shape
    n_mem = mem_prompt.shape[1]
    n_pool = cache_a_kv.shape[1]
    n_pages = page_table.shape[1]
    past = n_pages * PAGE_SIZE
    wb = state_b_win.shape[2]
    alpha = (2 * depth) ** 0.25
    kvw = 4 * HEAD_DIM
    assert t % TILE == 0 and ds == 8 and wb % TILE == 0 and n_pages % PAGES_PER_STEP == 0

    w_f, w_q, w_g, cols = _pack_weights(w_in[0])
    fc, qc = cols['f'], cols['q']
    w1cat = jnp.stack([jnp.concatenate([w[0][:CMP_STRIDE * HEAD_DIM], w[0][CMP_STRIDE * HEAD_DIM:]], axis=1)
                       for w in (cmp_w1_k, cmp_w1_v)]).astype(BF16)
    w1 = jnp.stack([cmp_w1_k[0], cmp_w1_v[0]]).astype(BF16)
    w2 = jnp.stack([cmp_w2_k[0], cmp_w2_v[0]]).astype(BF16)
    pe8 = jnp.broadcast_to(jnp.stack([cmp_pe_k[0].reshape(1, -1), cmp_pe_v[0].reshape(1, -1)]),
                           (2, 8, CMP_BLOCK * HEAD_DIM)).astype(BF16)
    wpa, wpb, wpm, wo = (w[0].astype(BF16) for w in (w_pa, w_pb, w_pm, w_o))
    wu, wd = w_up[0].astype(BF16), w_down[0].astype(BF16)
    nd = _num_near_tiles()
    bias = _bias_tiles(rel_table, nd)

    def dense_tail(x2d, gmat, oa, ob, om):
        x1 = _merge(x2d, gmat, oa, ob, om, wpa, wpb, wpm, wo, ln1_g, ln1_b, alpha)
        return _ffn(x1, wu, b_up, wd, b_down, ln2_g, ln2_b, alpha)

    xp = x_prompt.reshape(bsz * t, d)
    p_a_kv, p_b_cmp, p_b_sel, p_b_win, small_p = _project_states(xp, w_f)
    qp = _matmul(xp, w_q, BF16, tn=1024)
    gp = _matmul(xp, w_g, F32, tn=1024)
    p_a_idx = small_p[:, :IDX_DIM]

    zp = _cmpz_dense(p_b_cmp, w1cat)
    cmp_k, cmp_v = _cmp_finish(zp.reshape(bsz, t // CMP_STRIDE, -1), pe8, w1, w2)
    o_a = _prompt_mixer_a(qp, small_p, p_a_kv, bias[0], bsz, t, cols, nd)
    o_b = _prompt_mixer_b(qp, small_p, p_b_sel, p_b_win, cmp_k, cmp_v, bias[1], bsz, t, cols, nd)
    mem_kv = _matmul(mem_prompt.reshape(bsz * n_mem, d), w_mem_kv[0].astype(BF16), F32, tn=1024)
    o_m = _mem_attend(qp, qc['m_q'], mem_kv, bsz, t, n_mem, tq=512)
    y_prompt = dense_tail(xp, gp, o_a, o_b, o_m).reshape(bsz, t, d)

    xs = x_sample.reshape(db * ds, d)
    fs = _matmul(xs, w_f, F32, tn=w_f.shape[1] // 2)
    qs = _matmul(xs, w_q, BF16, tn=1024)
    gs = _matmul(xs, w_g, F32, tn=1024)
    s_a_kv = fs[:, fc['a_kv']:fc['a_kv'] + kvw]
    s_b_cmp = fs[:, fc['b_cmp']:fc['b_cmp'] + kvw]
    s_b_sel = fs[:, fc['b_sel']:fc['b_sel'] + kvw]
    s_b_win = fs[:, fc['b_win']:fc['b_win'] + kvw]
    s_a_idx = fs[:, fc['i_kw']:fc['i_kw'] + IDX_DIM]
    length = past + ds

    def pad_new(rows2d):
        r = rows2d.reshape(db, ds, -1)
        return jnp.concatenate([r, jnp.zeros((db, TILE - ds, r.shape[2]), r.dtype)], axis=1)

    def head_major(q2d, heads):
        return q2d.reshape(db, ds, heads, -1).transpose(0, 2, 1, 3).reshape(db, heads * ds, -1)

    def block_q(q2d):
        qh = head_major(q2d, A_HEADS).reshape(db, A_KV_HEADS, A_GROUP * ds, HEAD_DIM)
        z = jnp.zeros_like(qh[:, 0])
        return jnp.concatenate([jnp.concatenate([qh[:, 0], z], axis=2),
                                jnp.concatenate([z, qh[:, 1]], axis=2)], axis=1)

    def sample_bias(tiles):
        n = tiles.shape[0]
        return tiles.reshape(n, A_KV_HEADS, A_GROUP, TILE, TILE)[:, :, :, :ds].reshape(n, A_HEADS * ds, TILE)

    iq_s = head_major(qs[:, qc['i_q']:qc['i_q'] + IDX_HEADS * LANES], IDX_HEADS)[:, :, :IDX_DIM]
    w_s = fs[:, fc['i_kw'] + IDX_DIM:fc['i_kw'] + IDX_DIM + IDX_HEADS] * (IDX_HEADS ** -0.5 * IDX_DIM ** -0.5)
    w_s = jnp.broadcast_to(head_major(w_s, IDX_HEADS), (db, IDX_HEADS * ds, LANES))
    mask_a = _sample_index_mask(page_table, iq_s, w_s, pad_new(s_a_idx).transpose(0, 2, 1),
                                cache_a_idx.reshape(n_pool, PAGE_SIZE, IDX_DIM).transpose(0, 2, 1),
                                min(TOPK_MAX, length // 4))
    qa_blk = block_q(qs[:, qc['a_q']:qc['a_q'] + A_HEADS * HEAD_DIM])
    o_a_s = _paged_attention(page_table, qa_blk, mask_a, pad_new(s_a_kv), sample_bias(bias[0]),
                             cache_a_kv.reshape(n_pool, PAGE_SIZE * KV_ROWS, HEAD_DIM), nd, ds)

    zs = _cmpz_paged(cache_b_cmp.reshape(n_pool, PAGE_SIZE * KV_ROWS, HEAD_DIM), page_table, w1cat)
    cmp_k_s, cmp_v_s = _cmp_finish(zs.reshape(db, past // CMP_STRIDE, -1), pe8, w1, w2)
    bq2d = qs[:, qc['b_q']:qc['b_q'] + B_HEADS * HEAD_DIM]
    bq_s = head_major(bq2d, B_HEADS).reshape(db, B_KV_HEADS, B_GROUP * ds, HEAD_DIM)
    o_cmp_s, mask_b = _sample_cmp_select(bq_s, cmp_k_s, cmp_v_s, length, past, ds)
    qb_blk = block_q(bq2d)
    bias_sb = sample_bias(bias[1])
    o_sel_s = _paged_attention(page_table, qb_blk, mask_b, pad_new(s_b_sel), bias_sb,
                               cache_b_sel.reshape(n_pool, PAGE_SIZE * KV_ROWS, HEAD_DIM), nd, ds)
    gates_s = fs[:, fc['b_gate']:fc['b_gate'] + LANES].reshape(db, ds, LANES)
    o_b_s = _sample_window_combine(qb_blk, state_b_win.reshape(db, wb, kvw), pad_new(s_b_win), bias_sb,
                                   gates_s, o_cmp_s.reshape(db, B_HEADS * ds, HEAD_DIM), o_sel_s, ds)

    o_a_s = o_a_s.reshape(db, A_HEADS, ds, HEAD_DIM).transpose(0, 2, 1, 3).reshape(db * ds, -1).astype(BF16)
    o_m_s = _mem_attend(qs, qc['m_q'], cache_mem.reshape(db * n_mem, -1), db, ds, n_mem, tq=ds)
    y_sample = dense_tail(xs, gs, o_a_s, o_b_s.reshape(db * ds, -1), o_m_s).reshape(db, ds, d)

    kv6 = lambda a, n, rows: a.reshape(1, n, rows, 2, 2, HEAD_DIM)
    wp = min(WINDOW, t)
    new_win = jnp.concatenate([state_b_win.reshape(db, wb, 2, 2, HEAD_DIM)[:, ds:],
                               s_b_win.reshape(db, ds, 2, 2, HEAD_DIM)], axis=1)
    return (y_prompt, y_sample,
            kv6(p_a_kv, bsz, t), p_a_idx.reshape(1, bsz, t, IDX_DIM), kv6(p_b_cmp, bsz, t), kv6(p_b_sel, bsz, t),
            kv6(p_b_win, bsz, t)[:, :, t - wp:],
            mem_kv.reshape(1, bsz, n_mem, MEM_HEADS, 2, MEM_HEAD_DIM),
            kv6(s_a_kv, db, ds), s_a_idx.reshape(1, db, ds, IDX_DIM), kv6(s_b_cmp, db, ds), kv6(s_b_sel, db, ds),
            new_win[None])
```

```python
import functools
import math

import numpy as np
import jax
import jax.numpy as jnp
from jax import lax
from jax.experimental import pallas as pl
from jax.experimental.pallas import tpu as pltpu

F32 = jnp.float32
BF16 = jnp.bfloat16
I32 = jnp.int32

HEAD_DIM = 128
A_HEADS = 8
A_KV_HEADS = 2
A_GROUP = A_HEADS // A_KV_HEADS
IDX_HEADS = 8
IDX_DIM = 64
TOPK_MAX = 256
B_HEADS = 8
B_KV_HEADS = 2
B_GROUP = B_HEADS // B_KV_HEADS
CMP_BLOCK = 32
CMP_STRIDE = 16
CMP_RATIO = CMP_BLOCK // CMP_STRIDE
CMP_HID = 128
SEL_BLOCK = 64
SEL_TOPN = 16
WINDOW = 512
MEM_HEADS = 4
MEM_HEAD_DIM = 256
N_BUCKETS = 32
MAX_DISTANCE = 1024
LN_EPS = 1e-5
PAGE_SIZE = 128
KV_ROWS = 4

LANES = 128
VMEM_LIMIT = 56 * 1024 * 1024

TILE = 128
PAIR = 2 * TILE
ATTN_UNROLL = 2
NEG = -1e30
LOG2_E = math.log2(math.e)
QK_SCALE = HEAD_DIM ** -0.5 * LOG2_E
INT_MIN = -2 ** 31
PAGES_PER_STEP = 32
PAGES_PER_BLOCK = 8
INDEX_PAGES_PER_STEP = 64

_GROUPS = ('a_q', 'a_kv', 'i_q', 'i_k', 'i_w', 'b_q', 'b_cmp', 'b_sel', 'b_win', 'b_gate', 'm_q', 'g_merge')


def _split_sizes(d_model):
    return dict(
        a_q=A_HEADS * HEAD_DIM, a_kv=A_KV_HEADS * 2 * HEAD_DIM, i_q=IDX_HEADS * IDX_DIM, i_k=IDX_DIM,
        i_w=IDX_HEADS, b_q=B_HEADS * HEAD_DIM, b_cmp=B_KV_HEADS * 2 * HEAD_DIM, b_sel=B_KV_HEADS * 2 * HEAD_DIM,
        b_win=B_KV_HEADS * 2 * HEAD_DIM, b_gate=B_HEADS * 3, m_q=MEM_HEADS * MEM_HEAD_DIM, g_merge=3 * d_model)


def _params(n_grid, vmem=VMEM_LIMIT):
    return pltpu.CompilerParams(dimension_semantics=('arbitrary',) * n_grid, vmem_limit_bytes=vmem)


def _resident(block, index_map):
    return pl.BlockSpec(block, index_map, pipeline_mode=pl.Buffered(1))


def _mm_kernel(x_ref, w_ref, o_ref):
    o_ref[...] = jnp.dot(x_ref[...].astype(BF16), w_ref[...], preferred_element_type=F32).astype(o_ref.dtype)


def _matmul(x, w, out_dtype, tn, tm=1024, name='matmul'):
    m, k = x.shape
    n = w.shape[1]
    tm = min(tm, m)
    assert m % tm == 0 and n % tn == 0
    return pl.pallas_call(
        _mm_kernel,
        grid=(m // tm, n // tn),
        in_specs=[pl.BlockSpec((tm, k), lambda i, j: (i, 0)), pl.BlockSpec((k, tn), lambda i, j: (0, j))],
        out_specs=pl.BlockSpec((tm, tn), lambda i, j: (i, j)),
        out_shape=jax.ShapeDtypeStruct((m, n), out_dtype),
        compiler_params=_params(2), name=name,
    )(x, w)


def _proj_states_kernel(x_ref, w_ref, a_ref, c_ref, s_ref, n_ref, small_ref):
    j = pl.program_id(1)
    tm = x_ref.shape[0]
    res = jnp.dot(x_ref[...].astype(BF16), w_ref[...], preferred_element_type=F32)
    for k, o_ref in enumerate((a_ref, c_ref, s_ref, n_ref)):
        @pl.when(j == k)
        def _(o_ref=o_ref):
            for c in range(KV_ROWS):
                o_ref[pl.ds(c, tm, stride=KV_ROWS), :] = res[:, c * HEAD_DIM:(c + 1) * HEAD_DIM]

    @pl.when(j == 4)
    def _():
        small_ref[...] = res


def _project_states(x, w_f, tm=1024):
    m, k = x.shape
    kvw = KV_ROWS * HEAD_DIM
    assert m % tm == 0 and w_f.shape[1] == 5 * kvw
    state = jax.ShapeDtypeStruct((m * KV_ROWS, HEAD_DIM), F32)
    return pl.pallas_call(
        _proj_states_kernel,
        grid=(m // tm, 5),
        in_specs=[pl.BlockSpec((tm, k), lambda i, j: (i, 0)), pl.BlockSpec((k, kvw), lambda i, j: (0, j))],
        out_specs=[pl.BlockSpec((tm * KV_ROWS, HEAD_DIM), lambda i, j: (i, 0))] * 4
        + [pl.BlockSpec((tm, kvw), lambda i, j: (i, 0))],
        out_shape=[state] * 4 + [jax.ShapeDtypeStruct((m, kvw), F32)],
        compiler_params=_params(2), name='project_states',
    )(x, w_f)


def _rel_bucket(dist):
    d = jnp.maximum(dist, 0)
    exact = N_BUCKETS // 2
    df = jnp.maximum(d, 1).astype(F32)
    large = exact + (jnp.log(df / exact) / math.log(MAX_DISTANCE / exact) * (N_BUCKETS - exact)).astype(I32)
    return jnp.where(d < exact, d, jnp.minimum(large, N_BUCKETS - 1))


def _num_near_tiles():
    exact = N_BUCKETS // 2
    d = np.arange(1, 4 * MAX_DISTANCE, dtype=np.float64)
    large = exact + np.floor(np.log(d / exact) / math.log(MAX_DISTANCE / exact) * (N_BUCKETS - exact))
    bucket = np.where(d < exact, d, np.minimum(large, N_BUCKETS - 1))
    d_const = int(d[np.argmax(bucket == N_BUCKETS - 1)])
    return -(-(d_const + TILE // 2 + TILE - 1) // TILE)


def _bias_kernel(u_ref, o_ref, *, n_tiles, d_top):
    for dt in range(n_tiles):
        start = d_top - dt * TILE - (TILE - 1)
        row = u_ref[0, :, start:start + 2 * TILE]
        x = jnp.broadcast_to(row, (TILE, 2 * TILE))
        x = pltpu.roll(x, TILE + 1, 1, stride=1, stride_axis=0)
        o_ref[dt] = x[:, :TILE]


def _bias_tiles(rel_table, nd):
    n_tiles = nd + 1
    d_top = n_tiles * TILE
    ul = d_top + 2 * TILE
    n_heads = rel_table.shape[1]
    dist = d_top - jnp.arange(ul)
    u = (rel_table[_rel_bucket(dist)] * LOG2_E).T.reshape(n_heads, 1, ul)
    out = pl.pallas_call(
        functools.partial(_bias_kernel, n_tiles=n_tiles, d_top=d_top),
        grid=(n_heads,),
        in_specs=[pl.BlockSpec((1, 1, ul), lambda h: (h, 0, 0))],
        out_specs=pl.BlockSpec((None, n_tiles, None, None, TILE, TILE),
                               lambda h: (h // 8, 0, (h % 8) // 4, h % 4, 0, 0)),
        out_shape=jax.ShapeDtypeStruct((2, n_tiles, 2, 4, TILE, TILE), F32),
        compiler_params=_params(1), name='bias_tiles',
    )(u)
    return out.reshape(2, n_tiles, 2, 4 * TILE, TILE)


def _dot_nt(a, b):
    return lax.dot_general(a, b, (((1,), (1,)), ((), ())), preferred_element_type=F32)


def _flash_step(s, m, l, acc, v):
    m_new = jnp.maximum(m, jnp.max(s, axis=1, keepdims=True))
    alpha = jnp.exp2(m - m_new)
    p = jnp.exp2(s - m_new)
    l = alpha * l + jnp.sum(p, axis=1, keepdims=True)
    acc = alpha * acc + jnp.dot(p.astype(BF16), v, preferred_element_type=F32)
    return m_new, l, acc


def _flash_init(rows, width):
    return (jnp.full((rows, 1), NEG, F32), jnp.zeros((rows, 1), F32), jnp.zeros((rows, width), F32))


def _flash_out(m, l, acc):
    return jnp.where(m > 0.5 * NEG, acc / jnp.maximum(l, 1e-30), 0.0)


def _sortable_key(x):
    bits = pltpu.bitcast(x, I32)
    bits = jnp.where(bits == INT_MIN, 0, bits)
    return jnp.where(bits < 0, bits ^ 0x7FFFFFFF, bits)


def _kth_largest_key(count_ge, k, shape):
    def bit_body(i, t):
        cand = t + lax.shift_left(jnp.int32(1), 31 - i)
        return jnp.where(count_ge(cand) >= k, cand, t)
    return lax.fori_loop(0, 32, bit_body, jnp.full(shape, INT_MIN, I32))


def _attn_update(q_g, kk, vv, bias_of, madd, m_ref, l_ref, acc_ref, h0, n_heads):
    tk = kk.shape[0]
    s_all = _dot_nt(q_g, kk)
    ps, alphas = [], []
    for r in range(n_heads):
        s = s_all[r * TILE:(r + 1) * TILE] + (bias_of(r) + madd)
        m_prev = m_ref[h0 + r]
        m_next = jnp.maximum(m_prev, jnp.max(s, axis=1, keepdims=True))
        alpha = jnp.exp2(m_prev - m_next)
        p = jnp.exp2(s - jnp.concatenate([m_next] * (tk // LANES), axis=1))
        l_ref[h0 + r] = alpha * l_ref[h0 + r] + jnp.sum(p, axis=1, keepdims=True)
        m_ref[h0 + r] = m_next
        ps.append(p.astype(BF16))
        alphas.append(alpha)
    pv = jnp.dot(jnp.concatenate(ps, axis=0), vv, preferred_element_type=F32)
    for r in range(n_heads):
        acc_ref[h0 + r] = alphas[r] * acc_ref[h0 + r] + pv[r * TILE:(r + 1) * TILE]


def _attn_reset(m_ref, l_ref, acc_ref):
    m_ref[...] = jnp.full(m_ref.shape, NEG, F32)
    l_ref[...] = jnp.zeros(l_ref.shape, F32)
    acc_ref[...] = jnp.zeros(acc_ref.shape, F32)


def _attn_out(m_ref, l_ref, acc_ref, h):
    return jnp.where(m_ref[h] > 0.5 * NEG, acc_ref[h] / jnp.maximum(l_ref[h], 1e-30), 0.0)


def _topn_mask(scores, n):
    colf = lax.broadcasted_iota(I32, scores[0].shape, 1).astype(F32)

    def body(_, carry):
        out = []
        for sc, selm in carry:
            mx = jnp.max(sc, axis=1, keepdims=True)
            first = jnp.min(jnp.where(sc == mx, colf, 1e9), axis=1, keepdims=True)
            hit = colf == first
            selm = jnp.maximum(selm, jnp.where(hit, jnp.where(mx > -jnp.inf, 1.0, 0.0), 0.0))
            out.append((jnp.where(hit, -jnp.inf, sc), selm))
        return tuple(out)

    init = tuple((sc, jnp.zeros(sc.shape, F32)) for sc in scores)
    return [c[1] for c in lax.fori_loop(0, n, body, init)]


def _topn_mask_columns(scores_t, n):
    n_cand, n_col = scores_t[0].shape
    keys = [jnp.where(s > -jnp.inf, _sortable_key(s), INT_MIN) for s in scores_t]
    idx = lax.broadcasted_iota(I32, (n_cand, n_col), 0)

    def count(pred):
        v = jnp.where(pred, 1.0, 0.0)
        return jnp.sum(jnp.sum(v.reshape(4, n_cand // 4, n_col), axis=0), axis=0, keepdims=True)

    def bit_body(i, ts):
        step = lax.shift_left(jnp.int32(1), 31 - i)
        return tuple(jnp.where(count(k >= t + step) >= n, t + step, t) for k, t in zip(keys, ts))

    ts = lax.fori_loop(0, 32, bit_body, tuple(jnp.full((1, n_col), INT_MIN, I32) for _ in keys))
    thrs = [jnp.maximum(t, INT_MIN + 1) for t in ts]
    needs = [n - count(k > thr) for k, thr in zip(keys, thrs)]
    eq_idx = [jnp.where(k == thr, idx, jnp.int32(2 ** 30)) for k, thr in zip(keys, thrs)]
    n_bits = int(n_cand).bit_length()

    def idx_body(i, ms):
        step = lax.shift_left(jnp.int32(1), n_bits - 1 - i)
        return tuple(jnp.where(count(e < m + step) < need, m + step, m) for e, m, need in zip(eq_idx, ms, needs))

    lasts = lax.fori_loop(0, n_bits, idx_body, tuple(jnp.zeros((1, n_col), I32) for _ in keys))
    return [jnp.where(k > thr, 1.0, jnp.where(e <= last, 1.0, 0.0))
            for k, thr, e, last in zip(keys, thrs, eq_idx, lasts)]


def _softmax_rows(s, ok):
    s = jnp.where(ok, s, NEG)
    m = jnp.max(s, axis=1, keepdims=True)
    e = jnp.where(ok, jnp.exp2(s - m), 0.0)
    return e / jnp.maximum(jnp.sum(e, axis=1, keepdims=True), 1e-30)


def _block_expand(blk0, n_blk, width=TILE):
    rb = lax.broadcasted_iota(I32, (n_blk, width), 0)
    cj = lax.broadcasted_iota(I32, (n_blk, width), 1)
    target = blk0 + lax.shift_right_logical(cj, int(math.log2(SEL_BLOCK)))
    return jnp.where(rb == target, 1.0, 0.0).astype(BF16)


def _gelu(x):
    return 0.5 * x * (1.0 + jnp.tanh(math.sqrt(2.0 / math.pi) * (x + 0.044715 * (x * x * x))))


def _layer_norm(x, g, b):
    xc = x - jnp.mean(x, axis=1, keepdims=True)
    var = jnp.mean(xc * xc, axis=1, keepdims=True)
    return xc * lax.rsqrt(var + LN_EPS) * g + b


def _cmpz_compute(load, w_ref, o_ref):
    rows = o_ref.shape[0]
    for kv in range(2):
        xs = [jnp.concatenate([load(p, g, kv) for p in range(CMP_STRIDE)], axis=1) for g in range(B_KV_HEADS)]
        z = jnp.dot(jnp.concatenate(xs, axis=0).astype(BF16), w_ref[kv], preferred_element_type=F32)
        for g in range(B_KV_HEADS):
            c = (g * 2 + kv) * CMP_RATIO * CMP_HID
            o_ref[:, c:c + CMP_RATIO * CMP_HID] = z[g * rows:(g + 1) * rows]


def _cmpz_kernel(x_ref, w_ref, o_ref):
    rows = o_ref.shape[0]

    def load(p, g, kv):
        return x_ref[pl.ds(p * KV_ROWS + g * 2 + kv, rows, stride=CMP_STRIDE * KV_ROWS), :]

    _cmpz_compute(load, w_ref, o_ref)


def _cmpz_dense(state_rows, w1cat, tc=256):
    chunk_rows = CMP_STRIDE * KV_ROWS
    n = state_rows.shape[0] // chunk_rows
    tc = min(tc, n)
    assert n % tc == 0
    return pl.pallas_call(
        _cmpz_kernel,
        grid=(n // tc,),
        in_specs=[pl.BlockSpec((tc * chunk_rows, HEAD_DIM), lambda i: (i, 0)),
                  _resident(w1cat.shape, lambda i: (0, 0, 0))],
        out_specs=pl.BlockSpec((tc, 4 * CMP_RATIO * CMP_HID), lambda i: (i, 0)),
        out_shape=jax.ShapeDtypeStruct((n, 4 * CMP_RATIO * CMP_HID), F32),
        compiler_params=_params(1), name='cmpz_dense',
    )(state_rows, w1cat)


def _page_copy(pool_ref, buf_ref, sem_ref, pid, slot, k):
    rows = pool_ref.shape[1]
    return pltpu.make_async_copy(pool_ref.at[pid], buf_ref.at[slot, pl.ds(k * rows, rows)], sem_ref.at[slot])


def _pages_per_step(pool_ref, buf_ref):
    return buf_ref.shape[1] // pool_ref.shape[1]


def _page_fetch(pt_ref, pool_ref, buf_ref, sem_ref, step, slot):
    n = _pages_per_step(pool_ref, buf_ref)
    for k in range(n):
        _page_copy(pool_ref, buf_ref, sem_ref, pt_ref[step * n + k], slot, k).start()


def _page_wait(pool_ref, buf_ref, sem_ref, slot):
    for k in range(_pages_per_step(pool_ref, buf_ref)):
        _page_copy(pool_ref, buf_ref, sem_ref, 0, slot, k).wait()


def _page_pipeline(pt_ref, pool_ref, buf_ref, sem_ref):
    step = pl.program_id(0) * pl.num_programs(1) + pl.program_id(1)
    total = pl.num_programs(0) * pl.num_programs(1)
    slot = lax.rem(step, 2)

    @pl.when(step == 0)
    def _():
        _page_fetch(pt_ref, pool_ref, buf_ref, sem_ref, step, slot)

    @pl.when(step + 1 < total)
    def _():
        _page_fetch(pt_ref, pool_ref, buf_ref, sem_ref, step + 1, 1 - slot)

    _page_wait(pool_ref, buf_ref, sem_ref, slot)
    return slot


def _cmpz_paged_kernel(pt_ref, pool_ref, w_ref, o_ref, buf_ref, sem_ref, split_ref):
    slot = _page_pipeline(pt_ref, pool_ref, buf_ref, sem_ref)
    rows = o_ref.shape[0]
    n_pos = split_ref.shape[1]
    for j in range(KV_ROWS):
        split_ref[j] = buf_ref[slot, pl.ds(j, n_pos, stride=KV_ROWS), :]

    def load(p, g, kv):
        return split_ref[g * 2 + kv, pl.ds(p, rows, stride=CMP_STRIDE), :]

    _cmpz_compute(load, w_ref, o_ref)


def _cmpz_paged(pool, page_table, w1cat):
    db, n_pages = page_table.shape
    chunks = PAGE_SIZE // CMP_STRIDE
    npg = n_pages // PAGES_PER_STEP
    rows = PAGES_PER_STEP * chunks
    return pl.pallas_call(
        _cmpz_paged_kernel,
        grid_spec=pltpu.PrefetchScalarGridSpec(
            num_scalar_prefetch=1,
            grid=(db, npg),
            in_specs=[pl.BlockSpec(memory_space=pl.ANY),
                      _resident(w1cat.shape, lambda b, g, pt: (0, 0, 0))],
            out_specs=pl.BlockSpec((rows, 4 * CMP_RATIO * CMP_HID), lambda b, g, pt: (b * npg + g, 0)),
            scratch_shapes=[pltpu.VMEM((2, PAGES_PER_STEP * pool.shape[1], pool.shape[2]), F32),
                            pltpu.SemaphoreType.DMA((2,)),
                            pltpu.VMEM((KV_ROWS, PAGES_PER_STEP * PAGE_SIZE, pool.shape[2]), F32)]),
        out_shape=jax.ShapeDtypeStruct((db * n_pages * chunks, 4 * CMP_RATIO * CMP_HID), F32),
        compiler_params=_params(2), name='cmpz_paged',
    )(page_table.reshape(-1), pool, w1cat)


def _cmp_finish_kernel(z_ref, pe_ref, w1_ref, w2_ref, k_ref, v_ref):
    n = z_ref.shape[1]
    for kv, o_ref in ((0, k_ref), (1, v_ref)):
        pew = jnp.dot(pe_ref[kv], w1_ref[kv], preferred_element_type=F32)[0:1]
        for g in range(B_KV_HEADS):
            c = (g * 2 + kv) * CMP_RATIO * CMP_HID
            z0 = z_ref[0, :, c:c + CMP_HID]
            z1 = z_ref[0, :, c + CMP_HID:c + 2 * CMP_HID]
            pre = z0 + pltpu.roll(z1, n - 1, 0) + pew
            out = jnp.dot(_gelu(pre).astype(BF16), w2_ref[kv], preferred_element_type=F32)
            o_ref[0, :, g * HEAD_DIM:(g + 1) * HEAD_DIM] = out.astype(o_ref.dtype)


def _cmp_finish(z3, pe8, w1, w2):
    nb, n, zc = z3.shape
    out = jax.ShapeDtypeStruct((nb, n, B_KV_HEADS * HEAD_DIM), BF16)
    return pl.pallas_call(
        _cmp_finish_kernel,
        grid=(nb,),
        in_specs=[pl.BlockSpec((1, n, zc), lambda b: (b, 0, 0)),
                  _resident(pe8.shape, lambda b: (0, 0, 0)),
                  _resident(w1.shape, lambda b: (0, 0, 0)),
                  _resident(w2.shape, lambda b: (0, 0, 0))],
        out_specs=[pl.BlockSpec((1, n, B_KV_HEADS * HEAD_DIM), lambda b: (b, 0, 0))] * 2,
        out_shape=[out, out],
        compiler_params=_params(1), name='cmp_finish',
    )(z3, pe8, w1, w2)


def _ka_kernel(iq_ref, ikw_ref, ikwq_ref, aq_ref, akv_ref, bias_ref, o_ref, keys_ref, m_ref, l_ref, acc_ref,
               *, topk, nd):
    qt = pl.program_id(1)
    last_pair = lax.shift_right_logical(qt, 1)
    key_row = lax.broadcasted_iota(I32, (PAIR, TILE), 0)
    q_pos = qt * TILE + lax.broadcasted_iota(I32, (PAIR, TILE), 1)
    w_t = ikwq_ref[...].T[IDX_DIM:IDX_DIM + IDX_HEADS] * (IDX_HEADS ** -0.5 * IDX_DIM ** -0.5)
    iq_all = jnp.concatenate([iq_ref[:, h * LANES:(h + 1) * LANES] for h in range(IDX_HEADS)], axis=0)

    def score_pair(kp, masked):
        kb = ikw_ref[pl.ds(pl.multiple_of(kp * PAIR, PAIR), PAIR), :].astype(BF16)
        s = _dot_nt(kb, iq_all)
        acc = jnp.zeros((PAIR, TILE), F32)
        for h in range(IDX_HEADS):
            acc = acc + jnp.maximum(s[:, h * TILE:(h + 1) * TILE], 0.0) * w_t[h:h + 1]
        key = _sortable_key(acc)
        if masked:
            key = jnp.where(kp * PAIR + key_row <= q_pos, key, INT_MIN)
        keys_ref[kp] = key

    def score_body(kp, c):
        score_pair(kp, False)
        return c

    lax.fori_loop(0, last_pair, score_body, 0)
    score_pair(last_pair, True)

    def count_where(pred):
        def body(kp, c):
            v = jnp.where(pred(keys_ref[kp], kp), 1.0, 0.0)
            return c + jnp.sum(v.reshape(4, PAIR // 4, TILE), axis=0)
        c = lax.fori_loop(0, last_pair + 1, body, jnp.zeros((PAIR // 4, TILE), F32))
        return jnp.sum(c, axis=0, keepdims=True)

    def count_ge(cand):
        return count_where(lambda k, kp: k >= cand)

    t = _kth_largest_key(count_ge, topk, (1, TILE))
    thr = jnp.maximum(t, INT_MIN + 1)

    cnt_gt = count_ge(thr + 1)
    need = topk - cnt_gt
    cnt_eq = count_ge(thr) - cnt_gt
    tie = jnp.where(t > INT_MIN, jnp.where(cnt_eq > need, 1.0, 0.0), 0.0)

    @pl.when(jnp.max(tie) > 0.0)
    def _():
        n_bits = int(keys_ref.shape[0] * PAIR).bit_length()

        def idx_body(i, mm):
            cand = mm + lax.shift_left(jnp.int32(1), n_bits - 1 - i)
            c = count_where(lambda k, kp: jnp.where(k == thr, kp * PAIR + key_row, INT_MIN) < cand)
            c = c - count_where(lambda k, kp: k != thr)
            return jnp.where(c < need, cand, mm)

        last = lax.fori_loop(0, n_bits, idx_body, jnp.zeros((1, TILE), I32))
        last = jnp.where(tie > 0.0, last, jnp.int32(2 ** 30))

        def demote(kp, c):
            k = keys_ref[kp]
            pos = jnp.where(k == thr, kp * PAIR + key_row, INT_MIN)
            keys_ref[kp] = jnp.where(pos > last, thr - 1, k)
            return c

        lax.fori_loop(0, last_pair + 1, demote, 0)

    q = [jnp.concatenate([aq_ref[:, (g * A_GROUP + r) * HEAD_DIM:(g * A_GROUP + r + 1) * HEAD_DIM]
                          for r in range(A_GROUP)], axis=0) for g in range(A_KV_HEADS)]
    _attn_reset(m_ref, l_ref, acc_ref)

    def att_pair(kp_raw):
        kp = jnp.minimum(kp_raw, last_pair)
        d0 = jnp.clip(qt - 2 * kp, 0, nd)
        d1 = jnp.clip(qt - 2 * kp - 1, 0, nd)
        live = jnp.where(kp_raw <= last_pair, 0.0, NEG)
        madd = jnp.where(keys_ref[kp] >= thr, live, NEG).T
        r0 = pl.multiple_of(kp * (PAIR * KV_ROWS), PAIR * KV_ROWS)
        for g in range(A_KV_HEADS):
            kk = akv_ref[pl.ds(r0 + 2 * g, PAIR, stride=KV_ROWS), :].astype(BF16)
            vv = akv_ref[pl.ds(r0 + 2 * g + 1, PAIR, stride=KV_ROWS), :].astype(BF16)

            def bias_of(r, g=g):
                rs = slice(r * TILE, (r + 1) * TILE)
                return jnp.concatenate([bias_ref[d0, g, rs, :], bias_ref[d1, g, rs, :]], axis=1)

            _attn_update(q[g], kk, vv, bias_of, madd, m_ref, l_ref, acc_ref, g * A_GROUP, A_GROUP)

    def att_body(j, c):
        for u in range(ATTN_UNROLL):
            att_pair(j * ATTN_UNROLL + u)
        return c

    lax.fori_loop(0, (last_pair + ATTN_UNROLL) // ATTN_UNROLL, att_body, 0)
    for h in range(A_HEADS):
        o_ref[:, h * HEAD_DIM:(h + 1) * HEAD_DIM] = _attn_out(m_ref, l_ref, acc_ref, h).astype(o_ref.dtype)


def _prompt_mixer_a(qmat, small, a_kv_rows, bias_a, bsz, t, cols, nd):
    nt = t // TILE
    assert nt % 2 == 0
    topk = min(TOPK_MAX, t // 4)
    qc = cols['q']
    return pl.pallas_call(
        functools.partial(_ka_kernel, topk=topk, nd=nd),
        grid=(bsz, nt),
        in_specs=[
            pl.BlockSpec((TILE, IDX_HEADS * LANES), lambda b, i: (b * nt + i, qc['i_q'] // (IDX_HEADS * LANES))),
            _resident((t, LANES), lambda b, i: (b, 0)),
            pl.BlockSpec((TILE, LANES), lambda b, i: (b * nt + i, 0)),
            pl.BlockSpec((TILE, A_HEADS * HEAD_DIM), lambda b, i: (b * nt + i, qc['a_q'] // (A_HEADS * HEAD_DIM))),
            _resident((t * KV_ROWS, HEAD_DIM), lambda b, i: (b, 0)),
            _resident(bias_a.shape, lambda b, i: (0, 0, 0, 0)),
        ],
        out_specs=pl.BlockSpec((TILE, A_HEADS * HEAD_DIM), lambda b, i: (b * nt + i, 0)),
        out_shape=jax.ShapeDtypeStruct((bsz * t, A_HEADS * HEAD_DIM), BF16),
        scratch_shapes=[pltpu.VMEM((nt // 2, PAIR, TILE), I32)] + [pltpu.VMEM((A_HEADS, TILE, LANES), F32)] * 3,
        compiler_params=_params(2), name='prompt_mixer_a',
    )(qmat, small, small, qmat, a_kv_rows, bias_a)


def _kb_kernel(bq_ref, gate_ref, ck_ref, cv_ref, ovt_ref, sel_ref, w0_ref, w1_ref, w2_ref, w3_ref, w4_ref,
               bias_ref, o_ref, m_ref, l_ref, acc_ref, ocmp_ref, osel_ref, *, nd, n_sel):
    qt = pl.program_id(1)
    last_pair = lax.shift_right_logical(qt, 1)
    rows = B_GROUP * TILE
    row = lax.broadcasted_iota(I32, (TILE, TILE), 0)
    col = lax.broadcasted_iota(I32, (TILE, TILE), 1)
    colp = lax.broadcasted_iota(I32, (TILE, PAIR), 1)
    q_pos_p = qt * TILE + lax.broadcasted_iota(I32, (TILE, PAIR), 0)
    q = [jnp.concatenate([bq_ref[:, (g * B_GROUP + r) * HEAD_DIM:(g * B_GROUP + r + 1) * HEAD_DIM]
                          for r in range(B_GROUP)], axis=0) for g in range(B_KV_HEADS)]

    ncp = ck_ref.shape[1]
    q_pos = qt * TILE + lax.rem(lax.broadcasted_iota(I32, (rows, ncp), 0), TILE)
    cmp_end = lax.broadcasted_iota(I32, (rows, ncp), 1) * CMP_STRIDE + (CMP_BLOCK - 1)
    cmp_ok = cmp_end <= q_pos
    blk = row
    cur = 2 * qt + jnp.where(col >= SEL_BLOCK, 1, 0)
    scores_t = []
    for g in range(B_KV_HEADS):
        p = _softmax_rows(_dot_nt(q[g], ck_ref[0, :, g * HEAD_DIM:(g + 1) * HEAD_DIM]), cmp_ok)
        o_cmp = jnp.dot(p.astype(BF16), cv_ref[0, :, g * HEAD_DIM:(g + 1) * HEAD_DIM], preferred_element_type=F32)
        psum = p[0:TILE]
        ocmp_ref[g * B_GROUP] = o_cmp[0:TILE]
        for r in range(1, B_GROUP):
            psum = psum + p[r * TILE:(r + 1) * TILE]
            ocmp_ref[g * B_GROUP + r] = o_cmp[r * TILE:(r + 1) * TILE]
        imp_t = lax.dot_general(ovt_ref[...], psum, (((1,), (1,)), ((), ())), preferred_element_type=F32,
                                precision=lax.Precision.HIGHEST)
        forced = jnp.where(blk == 0, jnp.inf, jnp.where(blk >= cur - 1, jnp.inf, imp_t))
        scores_t.append(jnp.where(blk <= cur, forced, -jnp.inf))
    selm = [m.T.astype(BF16) for m in _topn_mask_columns(scores_t, n_sel)]

    _attn_reset(m_ref, l_ref, acc_ref)

    def sel_pair(kp_raw, diagonal):
        if diagonal:
            kp = kp_raw
        else:
            kp = jnp.minimum(kp_raw, jnp.maximum(last_pair - 1, 0))
            dead = jnp.where(kp_raw < last_pair, 0.0, NEG)
        d0 = jnp.clip(qt - 2 * kp, 0, nd)
        d1 = jnp.clip(qt - 2 * kp - 1, 0, nd)
        r0 = pl.multiple_of(kp * (PAIR * KV_ROWS), PAIR * KV_ROWS)
        expand = _block_expand((PAIR // SEL_BLOCK) * kp, selm[0].shape[1], PAIR)
        for g in range(B_KV_HEADS):
            madd = (jnp.dot(selm[g], expand, preferred_element_type=F32) - 1.0) * (-NEG)
            if diagonal:
                madd = jnp.where(kp * PAIR + colp <= q_pos_p, madd, NEG)
            else:
                madd = madd + dead
            kk = sel_ref[pl.ds(r0 + 2 * g, PAIR, stride=KV_ROWS), :].astype(BF16)
            vv = sel_ref[pl.ds(r0 + 2 * g + 1, PAIR, stride=KV_ROWS), :].astype(BF16)

            def bias_of(r, g=g):
                rs = slice(r * TILE, (r + 1) * TILE)
                return jnp.concatenate([bias_ref[d0, g, rs, :], bias_ref[d1, g, rs, :]], axis=1)

            _attn_update(q[g], kk, vv, bias_of, madd, m_ref, l_ref, acc_ref, g * B_GROUP, B_GROUP)

    def sel_body(j, c):
        for u in range(ATTN_UNROLL):
            sel_pair(j * ATTN_UNROLL + u, False)
        return c

    lax.fori_loop(0, (last_pair + ATTN_UNROLL - 1) // ATTN_UNROLL, sel_body, 0)
    sel_pair(last_pair, True)
    for h in range(B_HEADS):
        osel_ref[h] = _attn_out(m_ref, l_ref, acc_ref, h)

    _attn_reset(m_ref, l_ref, acc_ref)
    w_refs = (w0_ref, w1_ref, w2_ref, w3_ref, w4_ref)

    def win_mask(k):
        if k == 0:
            ok = col <= row
        elif k == WINDOW // TILE:
            ok = row <= col
        else:
            ok = col >= 0
        return jnp.where(ok, jnp.where(qt >= k, 0.0, NEG), NEG)

    for ks in ((0, 1), (2, 3), (4,)):
        madd = jnp.concatenate([win_mask(k) for k in ks], axis=1)
        for g in range(B_KV_HEADS):
            kk = jnp.concatenate([w_refs[k][pl.ds(2 * g, TILE, stride=KV_ROWS), :] for k in ks], axis=0).astype(BF16)
            vv = jnp.concatenate([w_refs[k][pl.ds(2 * g + 1, TILE, stride=KV_ROWS), :] for k in ks],
                                 axis=0).astype(BF16)

            def bias_of(r, g=g, ks=ks):
                return jnp.concatenate([bias_ref[k, g, r * TILE:(r + 1) * TILE, :] for k in ks], axis=1)

            _attn_update(q[g], kk, vv, bias_of, madd, m_ref, l_ref, acc_ref, g * B_GROUP, B_GROUP)

    gate = jax.nn.sigmoid(gate_ref[...])
    for h in range(B_HEADS):
        o = (gate[:, 3 * h:3 * h + 1] * ocmp_ref[h] + gate[:, 3 * h + 1:3 * h + 2] * osel_ref[h]
             + gate[:, 3 * h + 2:3 * h + 3] * _attn_out(m_ref, l_ref, acc_ref, h))
        o_ref[:, h * HEAD_DIM:(h + 1) * HEAD_DIM] = o.astype(o_ref.dtype)


def _overlap_matrix(length, n_rows, n_cols):
    nc = (length - CMP_BLOCK) // CMP_STRIDE + 1
    ns = -(-length // SEL_BLOCK)
    cs = np.arange(nc) * CMP_STRIDE
    ss = np.arange(ns) * SEL_BLOCK
    ov = np.minimum(cs[:, None] + CMP_BLOCK, ss[None, :] + SEL_BLOCK) - np.maximum(cs[:, None], ss[None, :])
    out = np.zeros((n_rows, n_cols), np.float32)
    out[:nc, :ns] = np.clip(ov, 0, None).astype(np.float32) / CMP_BLOCK
    return jnp.asarray(out)


def _prompt_mixer_b(qmat, small, sel_rows, win_rows, cmp_k, cmp_v, bias_b, bsz, t, cols, nd):
    nt = t // TILE
    ns = -(-t // SEL_BLOCK)
    assert ns <= LANES and WINDOW // TILE == 4 and nd >= WINDOW // TILE
    qc = cols['q']
    ncp = cmp_k.shape[1]
    ov = _overlap_matrix(t, ncp, LANES).T

    def win_spec(k):
        return pl.BlockSpec((TILE * KV_ROWS, HEAD_DIM), lambda b, i: (b * nt + jnp.maximum(i - k, 0), 0))

    return pl.pallas_call(
        functools.partial(_kb_kernel, nd=nd, n_sel=min(SEL_TOPN, ns)),
        grid=(bsz, nt),
        in_specs=[
            pl.BlockSpec((TILE, B_HEADS * HEAD_DIM), lambda b, i: (b * nt + i, qc['b_q'] // (B_HEADS * HEAD_DIM))),
            pl.BlockSpec((TILE, LANES), lambda b, i: (b * nt + i, 1)),
            _resident((1, ncp, B_KV_HEADS * HEAD_DIM), lambda b, i: (b, 0, 0)),
            _resident((1, ncp, B_KV_HEADS * HEAD_DIM), lambda b, i: (b, 0, 0)),
            _resident(ov.shape, lambda b, i: (0, 0)),
            _resident((t * KV_ROWS, HEAD_DIM), lambda b, i: (b, 0)),
            win_spec(0), win_spec(1), win_spec(2), win_spec(3), win_spec(4),
            _resident(bias_b.shape, lambda b, i: (0, 0, 0, 0)),
        ],
        out_specs=pl.BlockSpec((TILE, B_HEADS * HEAD_DIM), lambda b, i: (b * nt + i, 0)),
        out_shape=jax.ShapeDtypeStruct((bsz * t, B_HEADS * HEAD_DIM), BF16),
        scratch_shapes=[pltpu.VMEM((B_HEADS, TILE, LANES), F32)] * 5,
        compiler_params=_params(2), name='prompt_mixer_b',
    )(qmat, small, cmp_k, cmp_v, ov, sel_rows, win_rows, win_rows, win_rows, win_rows, win_rows, bias_b)


def _mem_kernel(q_ref, kv_ref, o_ref):
    scale = MEM_HEAD_DIM ** -0.5
    for h in range(MEM_HEADS):
        c = h * 2 * MEM_HEAD_DIM
        kk = kv_ref[:, c:c + MEM_HEAD_DIM].astype(BF16)
        vv = kv_ref[:, c + MEM_HEAD_DIM:c + 2 * MEM_HEAD_DIM].astype(BF16)
        s = _dot_nt(q_ref[:, h * MEM_HEAD_DIM:(h + 1) * MEM_HEAD_DIM], kk) * scale
        e = jnp.exp(s - jnp.max(s, axis=1, keepdims=True))
        p = e / jnp.sum(e, axis=1, keepdims=True)
        o = jnp.dot(p.astype(BF16), vv, preferred_element_type=F32)
        o_ref[:, h * MEM_HEAD_DIM:(h + 1) * MEM_HEAD_DIM] = o.astype(o_ref.dtype)


def _mem_attend(qmat, q_col, mem_kv2d, n_batch, rows_per_batch, n_mem, tq):
    width = MEM_HEADS * MEM_HEAD_DIM
    tq = min(tq, rows_per_batch)
    nq = rows_per_batch // tq
    return pl.pallas_call(
        _mem_kernel,
        grid=(n_batch, nq),
        in_specs=[pl.BlockSpec((tq, width), lambda b, i: (b * nq + i, q_col // width)),
                  pl.BlockSpec((n_mem, 2 * width), lambda b, i: (b, 0))],
        out_specs=pl.BlockSpec((tq, width), lambda b, i: (b * nq + i, 0)),
        out_shape=jax.ShapeDtypeStruct((n_batch * rows_per_batch, width), BF16),
        compiler_params=_params(2), name='mem_attend',
    )(qmat, mem_kv2d)


def _gated_proj_kernel(ga_ref, gb_ref, gm_ref, oa_ref, ob_ref, om_ref, wpa_ref, wpb_ref, wpm_ref, o_ref):
    merged = jax.nn.sigmoid(ga_ref[...]) * jnp.dot(oa_ref[...], wpa_ref[...], preferred_element_type=F32)
    merged = merged + jax.nn.sigmoid(gb_ref[...]) * jnp.dot(ob_ref[...], wpb_ref[...], preferred_element_type=F32)
    merged = merged + jax.nn.sigmoid(gm_ref[...]) * jnp.dot(om_ref[...], wpm_ref[...], preferred_element_type=F32)
    o_ref[...] = merged.astype(o_ref.dtype)


def _out_proj_kernel(x_ref, mg_ref, wo_ref, lg_ref, lb_ref, o_ref, *, alpha):
    y = alpha * x_ref[...] + jnp.dot(mg_ref[...], wo_ref[...], preferred_element_type=F32)
    o_ref[...] = _layer_norm(y, lg_ref[...], lb_ref[...])


def _merge(x2d, gmat, oa, ob, om, wpa, wpb, wpm, wo, ln_g, ln_b, alpha, tm=512, tn=1024):
    m, d = x2d.shape
    tm = min(tm, m)
    assert m % tm == 0 and d % tn == 0
    nj = d // tn
    row = lambda i, j: (i, 0)
    wcol = lambda i, j: (0, j)
    merged = pl.pallas_call(
        _gated_proj_kernel,
        grid=(m // tm, nj),
        in_specs=[pl.BlockSpec((tm, tn), lambda i, j: (i, j)),
                  pl.BlockSpec((tm, tn), lambda i, j: (i, nj + j)),
                  pl.BlockSpec((tm, tn), lambda i, j: (i, 2 * nj + j)),
                  pl.BlockSpec((tm, oa.shape[1]), row), pl.BlockSpec((tm, ob.shape[1]), row),
                  pl.BlockSpec((tm, om.shape[1]), row),
                  pl.BlockSpec((wpa.shape[0], tn), wcol), pl.BlockSpec((wpb.shape[0], tn), wcol),
                  pl.BlockSpec((wpm.shape[0], tn), wcol)],
        out_specs=pl.BlockSpec((tm, tn), lambda i, j: (i, j)),
        out_shape=jax.ShapeDtypeStruct((m, d), BF16),
        compiler_params=_params(2), name='gated_proj',
    )(gmat, gmat, gmat, oa, ob, om, wpa, wpb, wpm)
    fixed = lambda i: (0, 0)
    return pl.pallas_call(
        functools.partial(_out_proj_kernel, alpha=alpha),
        grid=(m // tm,),
        in_specs=[pl.BlockSpec((tm, d), lambda i: (i, 0)), pl.BlockSpec((tm, d), lambda i: (i, 0)),
                  _resident(wo.shape, fixed), _resident((1, d), fixed), _resident((1, d), fixed)],
        out_specs=pl.BlockSpec((tm, d), lambda i: (i, 0)),
        out_shape=jax.ShapeDtypeStruct((m, d), F32),
        compiler_params=_params(1), name='out_proj_ln',
    )(x2d, merged, wo, ln_g, ln_b)


def _ffn_kernel(x_ref, wu_ref, bu_ref, wd_ref, bd_ref, lg_ref, lb_ref, o_ref, acc_ref, *, alpha):
    j = pl.program_id(1)

    @pl.when(j == 0)
    def _():
        acc_ref[...] = jnp.zeros_like(acc_ref)

    u = jnp.dot(x_ref[...].astype(BF16), wu_ref[...], preferred_element_type=F32) + bu_ref[...]
    u = jnp.square(jnp.maximum(u, 0.0))
    acc_ref[...] += jnp.dot(u.astype(BF16), wd_ref[...], preferred_element_type=F32)

    @pl.when(j == pl.num_programs(1) - 1)
    def _():
        y = alpha * x_ref[...] + acc_ref[...] + bd_ref[...]
        o_ref[...] = _layer_norm(y, lg_ref[...], lb_ref[...])


def _ffn(x2d, wu, bu, wd, bd, ln_g, ln_b, alpha, tm=512, tf=1024):
    m, d = x2d.shape
    dff = wu.shape[1]
    tm = min(tm, m)
    assert m % tm == 0 and dff % tf == 0
    return pl.pallas_call(
        functools.partial(_ffn_kernel, alpha=alpha),
        grid=(m // tm, dff // tf),
        in_specs=[pl.BlockSpec((tm, d), lambda i, j: (i, 0)),
                  pl.BlockSpec((d, tf), lambda i, j: (0, j)), pl.BlockSpec((1, tf), lambda i, j: (0, j)),
                  pl.BlockSpec((tf, d), lambda i, j: (j, 0)),
                  _resident((1, d), lambda i, j: (0, 0)), _resident((1, d), lambda i, j: (0, 0)),
                  _resident((1, d), lambda i, j: (0, 0))],
        out_specs=pl.BlockSpec((tm, d), lambda i, j: (i, 0)),
        out_shape=jax.ShapeDtypeStruct((m, d), F32),
        scratch_shapes=[pltpu.VMEM((tm, d), F32)],
        compiler_params=_params(2), name='ffn_ln',
    )(x2d, wu, bu, wd, bd, ln_g, ln_b)


def _sidx_kernel(pt_ref, iq_ref, w_ref, knew_ref, pool_ref, o_ref, buf_ref, sem_ref, keys_ref, *, topk, n_q, past):
    g = pl.program_id(1)
    slot = _page_pipeline(pt_ref, pool_ref, buf_ref, sem_ref)
    pages = _pages_per_step(pool_ref, buf_ref)
    group = 8

    def scores(k_t):
        s = jnp.dot(iq_ref[0], k_t.astype(BF16), preferred_element_type=F32)
        acc = jnp.zeros((n_q, k_t.shape[1]), F32)
        for h in range(IDX_HEADS):
            acc = acc + jnp.maximum(s[h * n_q:(h + 1) * n_q], 0.0) * w_ref[0, h * n_q:(h + 1) * n_q, 0:1]
        return _sortable_key(acc)

    def page_body(j, c):
        k_t = jnp.concatenate(
            [buf_ref[slot, pl.ds(pl.multiple_of((j * group + u) * IDX_DIM, IDX_DIM), IDX_DIM), :]
             for u in range(group)], axis=1)
        width = group * PAGE_SIZE
        keys_ref[:, pl.ds(pl.multiple_of((g * pages + j * group) * PAGE_SIZE, width), width)] = scores(k_t)
        return c

    lax.fori_loop(0, pages // group, page_body, 0)

    @pl.when(g == pl.num_programs(1) - 1)
    def _():
        lp = keys_ref.shape[1]
        rown = lax.broadcasted_iota(I32, (n_q, TILE), 0)
        coln = lax.broadcasted_iota(I32, (n_q, TILE), 1)
        keys_ref[:, past:lp] = jnp.where(coln <= rown, scores(knew_ref[0]), INT_MIN)
        keys = keys_ref[...]
        pos = lax.broadcasted_iota(I32, (n_q, lp), 1)

        def count(pred):
            v = jnp.where(pred, 1.0, 0.0)
            n_part = 16
            part = (lp // (n_part * LANES)) * LANES
            acc = v[:, 0:part]
            for i in range(1, n_part):
                acc = acc + v[:, i * part:(i + 1) * part]
            return jnp.sum(acc, axis=1, keepdims=True) + jnp.sum(v[:, n_part * part:], axis=1, keepdims=True)

        t = _kth_largest_key(lambda cand: count(keys >= cand), topk, (n_q, 1))
        thr = jnp.maximum(t, INT_MIN + 1)
        cnt_gt = count(keys >= thr + 1)
        need = topk - cnt_gt
        cnt_eq = count(keys >= thr) - cnt_gt
        tie = jnp.where(t > INT_MIN, jnp.where(cnt_eq > need, 1.0, 0.0), 0.0)
        eq_pos = jnp.where(keys == thr, pos, jnp.int32(2 ** 30))
        n_bits = int(lp).bit_length()

        def idx_body(i, mm):
            cand = mm + lax.shift_left(jnp.int32(1), n_bits - 1 - i)
            return jnp.where(count(eq_pos < cand) < need, cand, mm)

        last = lax.fori_loop(0, n_bits, idx_body, jnp.zeros((n_q, 1), I32))
        last = jnp.where(tie > 0.0, last, jnp.int32(2 ** 30))
        sel = jnp.where(keys > thr, 1.0, jnp.where(keys == thr, jnp.where(pos <= last, 1.0, 0.0), 0.0))
        o_ref[0, 0] = sel


def _sample_index_mask(page_table, iq_s, w_s, knew, pool_idx, topk):
    db, n_pages = page_table.shape
    n_q = iq_s.shape[1] // IDX_HEADS
    pages = math.gcd(n_pages, INDEX_PAGES_PER_STEP)
    assert pages % 8 == 0 and pool_idx.shape[1:] == (IDX_DIM, PAGE_SIZE)
    npg = n_pages // pages
    lp = n_pages * PAGE_SIZE + TILE
    return pl.pallas_call(
        functools.partial(_sidx_kernel, topk=topk, n_q=n_q, past=n_pages * PAGE_SIZE),
        grid_spec=pltpu.PrefetchScalarGridSpec(
            num_scalar_prefetch=1,
            grid=(db, npg),
            in_specs=[pl.BlockSpec((1,) + iq_s.shape[1:], lambda b, g, pt: (b, 0, 0)),
                      pl.BlockSpec((1,) + w_s.shape[1:], lambda b, g, pt: (b, 0, 0)),
                      pl.BlockSpec((1,) + knew.shape[1:], lambda b, g, pt: (b, 0, 0)),
                      pl.BlockSpec(memory_space=pl.ANY)],
            out_specs=pl.BlockSpec((1, 1, n_q, lp), lambda b, g, pt: (b, 0, 0, 0)),
            scratch_shapes=[pltpu.VMEM((2, pages * pool_idx.shape[1], pool_idx.shape[2]), F32),
                            pltpu.SemaphoreType.DMA((2,)),
                            pltpu.VMEM((n_q, lp), I32)]),
        out_shape=jax.ShapeDtypeStruct((db, 1, n_q, lp), F32),
        compiler_params=_params(2), name='sample_index_mask',
    )(page_table.reshape(-1), iq_s, w_s, knew, pool_idx)


def _pattn_kernel(pt_ref, q_ref, mask_ref, new_ref, bias_ref, pool_ref, o_ref, buf_ref, sem_ref,
                  m_ref, l_ref, acc_ref, *, nd, n_q, n_pages):
    g = pl.program_id(1)
    npg = pl.num_programs(1)
    slot = _page_pipeline(pt_ref, pool_ref, buf_ref, sem_ref)
    rows = q_ref.shape[1]
    half = rows // 2
    mask_groups = mask_ref.shape[1]

    @pl.when(g == 0)
    def _():
        m0, l0, a0 = _flash_init(rows, 2 * HEAD_DIM)
        m_ref[...] = m0
        l_ref[...] = l0
        acc_ref[...] = a0

    def block_update(rows_of, p0, n_tiles, carry):
        kk = jnp.concatenate([rows_of(0), rows_of(2)], axis=1).astype(BF16)
        vv = jnp.concatenate([rows_of(1), rows_of(3)], axis=1).astype(BF16)
        width = n_tiles * TILE
        mk = mask_ref[0, :, :, pl.ds(pl.multiple_of(p0 * TILE, TILE), width)]
        madd = (mk - 1.0) * (-NEG)
        reps = rows // (mask_groups * n_q)
        madd = jnp.concatenate([madd[i] for i in range(mask_groups) for _ in range(reps)], axis=0)
        bias = jnp.concatenate([bias_ref[jnp.clip(n_pages - (p0 + i), 0, nd)] for i in range(n_tiles)], axis=1)
        s = _dot_nt(q_ref[0], kk) + (bias + madd)
        return _flash_step(s, *carry, vv)

    carry = (m_ref[...], l_ref[...], acc_ref[...])
    keys = PAGES_PER_BLOCK * PAGE_SIZE
    for blk in range(PAGES_PER_STEP // PAGES_PER_BLOCK):
        def rows_of(j, blk=blk):
            return buf_ref[slot, pl.ds(blk * keys * KV_ROWS + j, keys, stride=KV_ROWS), :]
        carry = block_update(rows_of, g * PAGES_PER_STEP + blk * PAGES_PER_BLOCK, PAGES_PER_BLOCK, carry)
    m_ref[...], l_ref[...], acc_ref[...] = carry

    @pl.when(g == npg - 1)
    def _():
        def new_rows(j):
            return new_ref[0, :, j * HEAD_DIM:(j + 1) * HEAD_DIM]
        o = _flash_out(*block_update(new_rows, n_pages, 1, carry))
        o_ref[0, 0:half] = o[0:half, 0:HEAD_DIM]
        o_ref[0, half:rows] = o[half:rows, HEAD_DIM:2 * HEAD_DIM]


def _paged_attention(page_table, qblk, mask, new_kv, bias_s, pool, nd, n_q):
    db, n_pages = page_table.shape
    npg = n_pages // PAGES_PER_STEP
    rows = qblk.shape[1]
    return pl.pallas_call(
        functools.partial(_pattn_kernel, nd=nd, n_q=n_q, n_pages=n_pages),
        grid_spec=pltpu.PrefetchScalarGridSpec(
            num_scalar_prefetch=1,
            grid=(db, npg),
            in_specs=[pl.BlockSpec((1,) + qblk.shape[1:], lambda b, g, pt: (b, 0, 0)),
                      pl.BlockSpec((1,) + mask.shape[1:], lambda b, g, pt: (b, 0, 0, 0)),
                      pl.BlockSpec((1,) + new_kv.shape[1:], lambda b, g, pt: (b, 0, 0)),
                      _resident(bias_s.shape, lambda b, g, pt: (0, 0, 0)),
                      pl.BlockSpec(memory_space=pl.ANY)],
            out_specs=pl.BlockSpec((1, rows, HEAD_DIM), lambda b, g, pt: (b, 0, 0)),
            scratch_shapes=[pltpu.VMEM((2, PAGES_PER_STEP * pool.shape[1], pool.shape[2]), F32),
                            pltpu.SemaphoreType.DMA((2,)),
                            pltpu.VMEM((rows, 1), F32), pltpu.VMEM((rows, 1), F32),
                            pltpu.VMEM((rows, 2 * HEAD_DIM), F32)]),
        out_shape=jax.ShapeDtypeStruct((db, rows, HEAD_DIM), F32),
        compiler_params=_params(2), name='paged_attention',
    )(page_table.reshape(-1), qblk, mask, new_kv, bias_s, pool)


def _scmp_kernel(q_ref, ck_ref, cv_ref, ov_ref, ocmp_ref, mask_ref, selm_ref, *, past, n_q, n_sel):
    rows = q_ref.shape[2]
    ncp = ck_ref.shape[1]
    nsp = ov_ref.shape[1]
    qi =lax.rem(lax.broadcasted_iota(I32, (rows, ncp), 0), n_q)
    cmp_end = lax.broadcasted_iota(I32, (rows, ncp), 1) * CMP_STRIDE + (CMP_BLOCK - 1)
    cmp_ok = cmp_end <= past + qi
    blk = lax.broadcasted_iota(I32, (n_q, nsp), 1)
    cur = lax.shift_right_logical(past + lax.broadcasted_iota(I32, (n_q, nsp), 0), int(math.log2(SEL_BLOCK)))
    q_pos = past + lax.broadcasted_iota(I32, (n_q, TILE), 0)
    coln = lax.broadcasted_iota(I32, (n_q, TILE), 1)
    scores = []
    for g in range(B_KV_HEADS):
        p = _softmax_rows(_dot_nt(q_ref[0, g], ck_ref[0, :, g * HEAD_DIM:(g + 1) * HEAD_DIM]), cmp_ok)
        ocmp_ref[0, g] = jnp.dot(p.astype(BF16), cv_ref[0, :, g * HEAD_DIM:(g + 1) * HEAD_DIM],
                                 preferred_element_type=F32)
        psum = p[0:n_q]
        for r in range(1, rows // n_q):
            psum = psum + p[r * n_q:(r + 1) * n_q]
        imp = jnp.dot(psum, ov_ref[...], preferred_element_type=F32, precision=lax.Precision.HIGHEST)
        forced = jnp.where(blk == 0, jnp.inf, jnp.where(blk >= cur - 1, jnp.inf, imp))
        scores.append(jnp.where(blk <= cur, forced, -jnp.inf))
    for g, selm in enumerate(_topn_mask(scores, n_sel)):
        selm_ref[g] = selm

    blocks_per_tile = TILE // SEL_BLOCK
    half = lax.shift_right_logical(coln, int(math.log2(SEL_BLOCK)))

    def expand_tile(kt):
        blk0 = blocks_per_tile * kt
        win0 = pl.multiple_of(lax.shift_right_logical(blk0, int(math.log2(LANES))) * LANES, LANES)
        idx = (blk0 - win0) + half
        k0 = pl.multiple_of(kt * TILE, TILE)
        for g in range(B_KV_HEADS):
            e = jnp.take_along_axis(selm_ref[g, :, pl.ds(win0, LANES)], idx, axis=1)
            mask_ref[0, g, :, pl.ds(k0, TILE)] = jnp.where(kt * TILE + coln <= q_pos, e, 0.0)

    unroll = 4
    n_tiles = mask_ref.shape[3] // TILE

    def expand_body(j, c):
        for u in range(unroll):
            expand_tile(j * unroll + u)
        return c

    lax.fori_loop(0, n_tiles // unroll, expand_body, 0)
    for kt in range(n_tiles - n_tiles % unroll, n_tiles):
        expand_tile(jnp.int32(kt))


def _sample_cmp_select(bq_s, cmp_k, cmp_v, length, past, n_q):
    db = bq_s.shape[0]
    ncp = cmp_k.shape[1]
    ns = -(-length // SEL_BLOCK)
    nsp = -(-ns // LANES) * LANES
    ov = _overlap_matrix(length, ncp, nsp)
    lp = past + TILE
    return pl.pallas_call(
        functools.partial(_scmp_kernel, past=past, n_q=n_q, n_sel=min(SEL_TOPN, ns)),
        grid=(db,),
        in_specs=[pl.BlockSpec((1,) + bq_s.shape[1:], lambda b: (b, 0, 0, 0)),
                  pl.BlockSpec((1, ncp, B_KV_HEADS * HEAD_DIM), lambda b: (b, 0, 0)),
                  pl.BlockSpec((1, ncp, B_KV_HEADS * HEAD_DIM), lambda b: (b, 0, 0)),
                  _resident(ov.shape, lambda b: (0, 0))],
        out_specs=[pl.BlockSpec((1,) + bq_s.shape[1:], lambda b: (b, 0, 0, 0)),
                   pl.BlockSpec((1, B_KV_HEADS, n_q, lp), lambda b: (b, 0, 0, 0))],
        out_shape=[jax.ShapeDtypeStruct(bq_s.shape, F32),
                   jax.ShapeDtypeStruct((db, B_KV_HEADS, n_q, lp), F32)],
        scratch_shapes=[pltpu.VMEM((B_KV_HEADS, n_q, nsp), F32)],
        compiler_params=_params(1), name='sample_cmp_select',
    )(bq_s, cmp_k, cmp_v, ov)


def _swin_kernel(q_ref, win_ref, new_ref, bias_ref, gate_ref, ocmp_ref, osel_ref, o_ref, *, n_q):
    rows = q_ref.shape[1]
    half = rows // 2
    wb = win_ref.shape[1]
    qi =lax.rem(lax.broadcasted_iota(I32, (rows, TILE), 0), n_q)
    col = lax.broadcasted_iota(I32, (rows, TILE), 1)
    carry = _flash_init(rows, 2 * HEAD_DIM)

    def tile_update(kv, dlt, ok, carry):
        kk = jnp.concatenate([kv[:, 0:HEAD_DIM], kv[:, 2 * HEAD_DIM:3 * HEAD_DIM]], axis=1).astype(BF16)
        vv = jnp.concatenate([kv[:, HEAD_DIM:2 * HEAD_DIM], kv[:, 3 * HEAD_DIM:4 * HEAD_DIM]], axis=1).astype(BF16)
        s = _dot_nt(q_ref[0], kk) + (bias_ref[dlt] + jnp.where(ok, 0.0, NEG))
        return _flash_step(s, *carry, vv)

    for kt in range(wb // TILE):
        ok = col + kt * TILE >= qi + (wb - WINDOW)
        carry = tile_update(win_ref[0, kt * TILE:(kt + 1) * TILE, :], wb // TILE - kt, ok, carry)
    carry = tile_update(new_ref[0], 0, col <= qi, carry)
    o = _flash_out(*carry)
    gate = jax.nn.sigmoid(gate_ref[0])
    for h in range(B_HEADS):
        sl = slice(h * n_q, (h + 1) * n_q)
        ow = o[sl, 0:HEAD_DIM] if h < B_GROUP else o[sl, HEAD_DIM:2 * HEAD_DIM]
        o_ref[0, :, h * HEAD_DIM:(h + 1) * HEAD_DIM] = (
            gate[:, 3 * h:3 * h + 1] * ocmp_ref[0, sl] + gate[:, 3 * h + 1:3 * h + 2] * osel_ref[0, sl]
            + gate[:, 3 * h + 2:3 * h + 3] * ow).astype(o_ref.dtype)


def _sample_window_combine(qblk, win_state, new_win, bias_s, gates, o_cmp, o_sel, n_q):
    db, rows, _ = qblk.shape
    return pl.pallas_call(
        functools.partial(_swin_kernel, n_q=n_q),
        grid=(db,),
        in_specs=[pl.BlockSpec((1,) + qblk.shape[1:], lambda b: (b, 0, 0)),
                  pl.BlockSpec((1,) + win_state.shape[1:], lambda b: (b, 0, 0)),
                  pl.BlockSpec((1,) + new_win.shape[1:], lambda b: (b, 0, 0)),
                  _resident(bias_s.shape, lambda b: (0, 0, 0)),
                  pl.BlockSpec((1,) + gates.shape[1:], lambda b: (b, 0, 0)),
                  pl.BlockSpec((1, rows, HEAD_DIM), lambda b: (b, 0, 0)),
                  pl.BlockSpec((1, rows, HEAD_DIM), lambda b: (b, 0, 0))],
        out_specs=pl.BlockSpec((1, n_q, B_HEADS * HEAD_DIM), lambda b: (b, 0, 0)),
        out_shape=jax.ShapeDtypeStruct((db, n_q, B_HEADS * HEAD_DIM), BF16),
        compiler_params=_params(1), name='sample_window_combine',
    )(qblk, win_state, new_win, bias_s, gates, o_cmp, o_sel)


def _pack_weights(w_in):
    d = w_in.shape[0]
    sizes = _split_sizes(d)
    off, o = {}, 0
    for name in _GROUPS:
        off[name] = o
        o += sizes[name]
    take = lambda name: w_in[:, off[name]:off[name] + sizes[name]]
    zeros = lambda n: jnp.zeros((d, n), w_in.dtype)
    iq = take('i_q').reshape(d, IDX_HEADS, IDX_DIM)
    iq = jnp.concatenate([iq, jnp.zeros_like(iq)], axis=2).reshape(d, IDX_HEADS * LANES)
    w_f = jnp.concatenate([take('a_kv'), take('b_cmp'), take('b_sel'), take('b_win'),
                           take('i_k'), take('i_w'), zeros(LANES - IDX_DIM - IDX_HEADS),
                           take('b_gate'), zeros(LANES - B_HEADS * 3), zeros(2 * LANES)], axis=1).astype(BF16)
    w_q = jnp.concatenate([take('a_q') * QK_SCALE, take('b_q') * QK_SCALE, take('m_q'), iq], axis=1).astype(BF16)
    w_g = take('g_merge').astype(BF16)
    kvw = 4 * HEAD_DIM
    cols = dict(f=dict(a_kv=0, b_cmp=kvw, b_sel=2 * kvw, b_win=3 * kvw, i_kw=4 * kvw, b_gate=4 * kvw + LANES),
                q=dict(a_q=0, b_q=1024, m_q=2048, i_q=3072))
    return w_f, w_q, w_g, cols


def kernel(x_prompt, x_sample, mem_prompt, cache_a_kv, cache_a_idx, cache_b_cmp, cache_b_sel, state_b_win,
           cache_mem, page_table, rel_table, w_in, w_mem_kv, cmp_pe_k, cmp_w1_k, cmp_w2_k, cmp_pe_v, cmp_w1_v,
           cmp_w2_v, w_pa, w_pb, w_pm, w_o, ln1_g, ln1_b, w_up, b_up, w_down, b_down, ln2_g, ln2_b):
    depth = w_in.shape[0]
    assert depth == 1
    bsz, t, d = x_prompt.shape
    db, ds, _ = x_sample.shape
    n_mem = mem_prompt.shape[1]
    n_pool = cache_a_kv.shape[1]
    n_pages = page_table.shape[1]
    past = n_pages * PAGE_SIZE
    wb = state_b_win.shape[2]
    alpha = (2 * depth) ** 0.25
    kvw = 4 * HEAD_DIM
    assert t % TILE == 0 and ds == 8 and wb % TILE == 0 and n_pages % PAGES_PER_STEP == 0

    w_f, w_q, w_g, cols = _pack_weights(w_in[0])
    fc, qc = cols['f'], cols['q']
    w1cat = jnp.stack([jnp.concatenate([w[0][:CMP_STRIDE * HEAD_DIM], w[0][CMP_STRIDE * HEAD_DIM:]], axis=1)
                       for w in (cmp_w1_k, cmp_w1_v)]).astype(BF16)
    w1 = jnp.stack([cmp_w1_k[0], cmp_w1_v[0]]).astype(BF16)
    w2 = jnp.stack([cmp_w2_k[0], cmp_w2_v[0]]).astype(BF16)
    pe8 = jnp.broadcast_to(jnp.stack([cmp_pe_k[0].reshape(1, -1), cmp_pe_v[0].reshape(1, -1)]),
                           (2, 8, CMP_BLOCK * HEAD_DIM)).astype(BF16)
    wpa, wpb, wpm, wo = (w[0].astype(BF16) for w in (w_pa, w_pb, w_pm, w_o))
    wu, wd = w_up[0].astype(BF16), w_down[0].astype(BF16)
    nd = _num_near_tiles()
    bias = _bias_tiles(rel_table, nd)

    def dense_tail(x2d, gmat, oa, ob, om):
        x1 = _merge(x2d, gmat, oa, ob, om, wpa, wpb, wpm, wo, ln1_g, ln1_b, alpha)
        return _ffn(x1, wu, b_up, wd, b_down, ln2_g, ln2_b, alpha)

    xp = x_prompt.reshape(bsz * t, d)
    p_a_kv, p_b_cmp, p_b_sel, p_b_win, small_p = _project_states(xp, w_f)
    qp = _matmul(xp, w_q, BF16, tn=1024)
    gp = _matmul(xp, w_g, F32, tn=1024)
    p_a_idx = small_p[:, :IDX_DIM]

    zp = _cmpz_dense(p_b_cmp, w1cat)
    cmp_k, cmp_v = _cmp_finish(zp.reshape(bsz, t // CMP_STRIDE, -1), pe8, w1, w2)
    o_a = _prompt_mixer_a(qp, small_p, p_a_kv, bias[0], bsz, t, cols, nd)
    o_b = _prompt_mixer_b(qp, small_p, p_b_sel, p_b_win, cmp_k, cmp_v, bias[1], bsz, t, cols, nd)
    mem_kv = _matmul(mem_prompt.reshape(bsz * n_mem, d), w_mem_kv[0].astype(BF16), F32, tn=1024)
    o_m = _mem_attend(qp, qc['m_q'], mem_kv, bsz, t, n_mem, tq=512)
    y_prompt = dense_tail(xp, gp, o_a, o_b, o_m).reshape(bsz, t, d)

    xs = x_sample.reshape(db * ds, d)
    fs = _matmul(xs, w_f, F32, tn=w_f.shape[1] // 2)
    qs = _matmul(xs, w_q, BF16, tn=1024)
    gs = _matmul(xs, w_g, F32, tn=1024)
    s_a_kv = fs[:, fc['a_kv']:fc['a_kv'] + kvw]
    s_b_cmp = fs[:, fc['b_cmp']:fc['b_cmp'] + kvw]
    s_b_sel = fs[:, fc['b_sel']:fc['b_sel'] + kvw]
    s_b_win = fs[:, fc['b_win']:fc['b_win'] + kvw]
    s_a_idx = fs[:, fc['i_kw']:fc['i_kw'] + IDX_DIM]
    length = past + ds

    def pad_new(rows2d):
        r = rows2d.reshape(db, ds, -1)
        return jnp.concatenate([r, jnp.zeros((db, TILE - ds, r.shape[2]), r.dtype)], axis=1)

    def head_major(q2d, heads):
        return q2d.reshape(db, ds, heads, -1).transpose(0, 2, 1, 3).reshape(db, heads * ds, -1)

    def block_q(q2d):
        qh = head_major(q2d, A_HEADS).reshape(db, A_KV_HEADS, A_GROUP * ds, HEAD_DIM)
        z = jnp.zeros_like(qh[:, 0])
        return jnp.concatenate([jnp.concatenate([qh[:, 0], z], axis=2),
                                jnp.concatenate([z, qh[:, 1]], axis=2)], axis=1)

    def sample_bias(tiles):
        n = tiles.shape[0]
        return tiles.reshape(n, A_KV_HEADS, A_GROUP, TILE, TILE)[:, :, :, :ds].reshape(n, A_HEADS * ds, TILE)

    iq_s = head_major(qs[:, qc['i_q']:qc['i_q'] + IDX_HEADS * LANES], IDX_HEADS)[:, :, :IDX_DIM]
    w_s = fs[:, fc['i_kw'] + IDX_DIM:fc['i_kw'] + IDX_DIM + IDX_HEADS] * (IDX_HEADS ** -0.5 * IDX_DIM ** -0.5)
    w_s = jnp.broadcast_to(head_major(w_s, IDX_HEADS), (db, IDX_HEADS * ds, LANES))
    mask_a = _sample_index_mask(page_table, iq_s, w_s, pad_new(s_a_idx).transpose(0, 2, 1),
                                cache_a_idx.reshape(n_pool, PAGE_SIZE, IDX_DIM).transpose(0, 2, 1),
                                min(TOPK_MAX, length // 4))
    qa_blk = block_q(qs[:, qc['a_q']:qc['a_q'] + A_HEADS * HEAD_DIM])
    o_a_s = _paged_attention(page_table, qa_blk, mask_a, pad_new(s_a_kv), sample_bias(bias[0]),
                             cache_a_kv.reshape(n_pool, PAGE_SIZE * KV_ROWS, HEAD_DIM), nd, ds)

    zs = _cmpz_paged(cache_b_cmp.reshape(n_pool, PAGE_SIZE * KV_ROWS, HEAD_DIM), page_table, w1cat)
    cmp_k_s, cmp_v_s = _cmp_finish(zs.reshape(db, past // CMP_STRIDE, -1), pe8, w1, w2)
    bq2d = qs[:, qc['b_q']:qc['b_q'] + B_HEADS * HEAD_DIM]
    bq_s = head_major(bq2d, B_HEADS).reshape(db, B_KV_HEADS, B_GROUP * ds, HEAD_DIM)
    o_cmp_s, mask_b = _sample_cmp_select(bq_s, cmp_k_s, cmp_v_s, length, past, ds)
    qb_blk = block_q(bq2d)
    bias_sb = sample_bias(bias[1])
    o_sel_s = _paged_attention(page_table, qb_blk, mask_b, pad_new(s_b_sel), bias_sb,
                               cache_b_sel.reshape(n_pool, PAGE_SIZE * KV_ROWS, HEAD_DIM), nd, ds)
    gates_s = fs[:, fc['b_gate']:fc['b_gate'] + LANES].reshape(db, ds, LANES)
    o_b_s = _sample_window_combine(qb_blk, state_b_win.reshape(db, wb, kvw), pad_new(s_b_win), bias_sb,
                                   gates_s, o_cmp_s.reshape(db, B_HEADS * ds, HEAD_DIM), o_sel_s, ds)

    o_a_s = o_a_s.reshape(db, A_HEADS, ds, HEAD_DIM).transpose(0, 2, 1, 3).reshape(db * ds, -1).astype(BF16)
    o_m_s = _mem_attend(qs, qc['m_q'], cache_mem.reshape(db * n_mem, -1), db, ds, n_mem, tq=ds)
    y_sample = dense_tail(xs, gs, o_a_s, o_b_s.reshape(db * ds, -1), o_m_s).reshape(db, ds, d)

    kv6 = lambda a, n, rows: a.reshape(1, n, rows, 2, 2, HEAD_DIM)
    wp = min(WINDOW, t)
    new_win = jnp.concatenate([state_b_win.reshape(db, wb, 2, 2, HEAD_DIM)[:, ds:],
                               s_b_win.reshape(db, ds, 2, 2, HEAD_DIM)], axis=1)
    return (y_prompt, y_sample,
            kv6(p_a_kv, bsz, t), p_a_idx.reshape(1, bsz, t, IDX_DIM), kv6(p_b_cmp, bsz, t), kv6(p_b_sel, bsz, t),
            kv6(p_b_win, bsz, t)[:, :, t - wp:],
            mem_kv.reshape(1, bsz, n_mem, MEM_HEADS, 2, MEM_HEAD_DIM),
            kv6(s_a_kv, db, ds), s_a_idx.reshape(1, db, ds, IDX_DIM), kv6(s_b_cmp, db, ds), kv6(s_b_sel, db, ds),
            new_win[None])
```

```python
import functools
import math

import numpy as np
import jax
import jax.numpy as jnp
from jax import lax
from jax.experimental import pallas as pl
from jax.experimental.pallas import tpu as pltpu

F32 = jnp.float32
BF16 = jnp.bfloat16
I32 = jnp.int32

HEAD_DIM = 128
A_HEADS = 8
A_KV_HEADS = 2
A_GROUP = A_HEADS // A_KV_HEADS
IDX_HEADS = 8
IDX_DIM = 64
TOPK_MAX = 256
B_HEADS = 8
B_KV_HEADS = 2
B_GROUP = B_HEADS // B_KV_HEADS
CMP_BLOCK = 32
CMP_STRIDE = 16
CMP_RATIO = CMP_BLOCK // CMP_STRIDE
CMP_HID = 128
SEL_BLOCK = 64
SEL_TOPN = 16
WINDOW = 512
MEM_HEADS = 4
MEM_HEAD_DIM = 256
N_BUCKETS = 32
MAX_DISTANCE = 1024
LN_EPS = 1e-5
PAGE_SIZE = 128
KV_ROWS = 4

LANES = 128
VMEM_LIMIT = 56 * 1024 * 1024

TILE = 128
PAIR = 2 * TILE
ATTN_UNROLL = 2
NEG = -1e30
LOG2_E = math.log2(math.e)
QK_SCALE = HEAD_DIM ** -0.5 * LOG2_E
INT_MIN = -2 ** 31
PAGES_PER_STEP = 32
PAGES_PER_BLOCK = 8
INDEX_PAGES_PER_STEP = 64

_GROUPS = ('a_q', 'a_kv', 'i_q', 'i_k', 'i_w', 'b_q', 'b_cmp', 'b_sel', 'b_win', 'b_gate', 'm_q', 'g_merge')


def _split_sizes(d_model):
    return dict(
        a_q=A_HEADS * HEAD_DIM, a_kv=A_KV_HEADS * 2 * HEAD_DIM, i_q=IDX_HEADS * IDX_DIM, i_k=IDX_DIM,
        i_w=IDX_HEADS, b_q=B_HEADS * HEAD_DIM, b_cmp=B_KV_HEADS * 2 * HEAD_DIM, b_sel=B_KV_HEADS * 2 * HEAD_DIM,
        b_win=B_KV_HEADS * 2 * HEAD_DIM, b_gate=B_HEADS * 3, m_q=MEM_HEADS * MEM_HEAD_DIM, g_merge=3 * d_model)


def _params(n_grid, vmem=VMEM_LIMIT):
    return pltpu.CompilerParams(dimension_semantics=('arbitrary',) * n_grid, vmem_limit_bytes=vmem)


def _resident(block, index_map):
    return pl.BlockSpec(block, index_map, pipeline_mode=pl.Buffered(1))


def _mm_kernel(x_ref, w_ref, o_ref):
    o_ref[...] = jnp.dot(x_ref[...].astype(BF16), w_ref[...], preferred_element_type=F32).astype(o_ref.dtype)


def _matmul(x, w, out_dtype, tn, tm=1024, name='matmul'):
    m, k = x.shape
    n = w.shape[1]
    tm = min(tm, m)
    assert m % tm == 0 and n % tn == 0
    return pl.pallas_call(
        _mm_kernel,
        grid=(m // tm, n // tn),
        in_specs=[pl.BlockSpec((tm, k), lambda i, j: (i, 0)), pl.BlockSpec((k, tn), lambda i, j: (0, j))],
        out_specs=pl.BlockSpec((tm, tn), lambda i, j: (i, j)),
        out_shape=jax.ShapeDtypeStruct((m, n), out_dtype),
        compiler_params=_params(2), name=name,
    )(x, w)


def _proj_states_kernel(x_ref, w_ref, a_ref, c_ref, s_ref, n_ref, small_ref):
    j = pl.program_id(1)
    tm = x_ref.shape[0]
    res = jnp.dot(x_ref[...].astype(BF16), w_ref[...], preferred_element_type=F32)
    for k, o_ref in enumerate((a_ref, c_ref, s_ref, n_ref)):
        @pl.when(j == k)
        def _(o_ref=o_ref):
            for c in range(KV_ROWS):
                o_ref[pl.ds(c, tm, stride=KV_ROWS), :] = res[:, c * HEAD_DIM:(c + 1) * HEAD_DIM]

    @pl.when(j == 4)
    def _():
        small_ref[...] = res


def _project_states(x, w_f, tm=1024):
    m, k = x.shape
    kvw = KV_ROWS * HEAD_DIM
    assert m % tm == 0 and w_f.shape[1] == 5 * kvw
    state = jax.ShapeDtypeStruct((m * KV_ROWS, HEAD_DIM), F32)
    return pl.pallas_call(
        _proj_states_kernel,
        grid=(m // tm, 5),
        in_specs=[pl.BlockSpec((tm, k), lambda i, j: (i, 0)), pl.BlockSpec((k, kvw), lambda i, j: (0, j))],
        out_specs=[pl.BlockSpec((tm * KV_ROWS, HEAD_DIM), lambda i, j: (i, 0))] * 4
        + [pl.BlockSpec((tm, kvw), lambda i, j: (i, 0))],
        out_shape=[state] * 4 + [jax.ShapeDtypeStruct((m, kvw), F32)],
        compiler_params=_params(2), name='project_states',
    )(x, w_f)


def _rel_bucket(dist):
    d = jnp.maximum(dist, 0)
    exact = N_BUCKETS // 2
    df = jnp.maximum(d, 1).astype(F32)
    large = exact + (jnp.log(df / exact) / math.log(MAX_DISTANCE / exact) * (N_BUCKETS - exact)).astype(I32)
    return jnp.where(d < exact, d, jnp.minimum(large, N_BUCKETS - 1))


def _num_near_tiles():
    exact = N_BUCKETS // 2
    d = np.arange(1, 4 * MAX_DISTANCE, dtype=np.float64)
    large = exact + np.floor(np.log(d / exact) / math.log(MAX_DISTANCE / exact) * (N_BUCKETS - exact))
    bucket = np.where(d < exact, d, np.minimum(large, N_BUCKETS - 1))
    d_const = int(d[np.argmax(bucket == N_BUCKETS - 1)])
    return -(-(d_const + TILE // 2 + TILE - 1) // TILE)


def _bias_kernel(u_ref, o_ref, *, n_tiles, d_top):
    for dt in range(n_tiles):
        start = d_top - dt * TILE - (TILE - 1)
        row = u_ref[0, :, start:start + 2 * TILE]
        x = jnp.broadcast_to(row, (TILE, 2 * TILE))
        x = pltpu.roll(x, TILE + 1, 1, stride=1, stride_axis=0)
        o_ref[dt] = x[:, :TILE]


def _bias_tiles(rel_table, nd):
    n_tiles = nd + 1
    d_top = n_tiles * TILE
    ul = d_top + 2 * TILE
    n_heads = rel_table.shape[1]
    dist = d_top - jnp.arange(ul)
    u = (rel_table[_rel_bucket(dist)] * LOG2_E).T.reshape(n_heads, 1, ul)
    out = pl.pallas_call(
        functools.partial(_bias_kernel, n_tiles=n_tiles, d_top=d_top),
        grid=(n_heads,),
        in_specs=[pl.BlockSpec((1, 1, ul), lambda h: (h, 0, 0))],
        out_specs=pl.BlockSpec((None, n_tiles, None, None, TILE, TILE),
                               lambda h: (h // 8, 0, (h % 8) // 4, h % 4, 0, 0)),
        out_shape=jax.ShapeDtypeStruct((2, n_tiles, 2, 4, TILE, TILE), F32),
        compiler_params=_params(1), name='bias_tiles',
    )(u)
    return out.reshape(2, n_tiles, 2, 4 * TILE, TILE)


def _dot_nt(a, b):
    return lax.dot_general(a, b, (((1,), (1,)), ((), ())), preferred_element_type=F32)


def _flash_step(s, m, l, acc, v):
    m_new = jnp.maximum(m, jnp.max(s, axis=1, keepdims=True))
    alpha = jnp.exp2(m - m_new)
    p = jnp.exp2(s - m_new)
    l = alpha * l + jnp.sum(p, axis=1, keepdims=True)
    acc = alpha * acc + jnp.dot(p.astype(BF16), v, preferred_element_type=F32)
    return m_new, l, acc


def _flash_init(rows, width):
    return (jnp.full((rows, 1), NEG, F32), jnp.zeros((rows, 1), F32), jnp.zeros((rows, width), F32))


def _flash_out(m, l, acc):
    return jnp.where(m > 0.5 * NEG, acc / jnp.maximum(l, 1e-30), 0.0)


def _sortable_key(x):
    bits = pltpu.bitcast(x, I32)
    bits = jnp.where(bits == INT_MIN, 0, bits)
    return jnp.where(bits < 0, bits ^ 0x7FFFFFFF, bits)


def _kth_largest_key(count_ge, k, shape):
    def bit_body(i, t):
        cand = t + lax.shift_left(jnp.int32(1), 31 - i)
        return jnp.where(count_ge(cand) >= k, cand, t)
    return lax.fori_loop(0, 32, bit_body, jnp.full(shape, INT_MIN, I32))


def _attn_update(q_g, kk, vv, bias_of, madd, m_ref, l_ref, acc_ref, h0, n_heads):
    tk = kk.shape[0]
    s_all = _dot_nt(q_g, kk)
    ps, alphas = [], []
    for r in range(n_heads):
        s = s_all[r * TILE:(r + 1) * TILE] + (bias_of(r) + madd)
        m_prev = m_ref[h0 + r]
        m_next = jnp.maximum(m_prev, jnp.max(s, axis=1, keepdims=True))
        alpha = jnp.exp2(m_prev - m_next)
        p = jnp.exp2(s - jnp.concatenate([m_next] * (tk // LANES), axis=1))
        l_ref[h0 + r] = alpha * l_ref[h0 + r] + jnp.sum(p, axis=1, keepdims=True)
        m_ref[h0 + r] = m_next
        ps.append(p.astype(BF16))
        alphas.append(alpha)
    pv = jnp.dot(jnp.concatenate(ps, axis=0), vv, preferred_element_type=F32)
    for r in range(n_heads):
        acc_ref[h0 + r] = alphas[r] * acc_ref[h0 + r] + pv[r * TILE:(r + 1) * TILE]


def _attn_reset(m_ref, l_ref, acc_ref):
    m_ref[...] = jnp.full(m_ref.shape, NEG, F32)
    l_ref[...] = jnp.zeros(l_ref.shape, F32)
    acc_ref[...] = jnp.zeros(acc_ref.shape, F32)


def _attn_out(m_ref, l_ref, acc_ref, h):
    return jnp.where(m_ref[h] > 0.5 * NEG, acc_ref[h] / jnp.maximum(l_ref[h], 1e-30), 0.0)


def _topn_mask(scores, n):
    colf = lax.broadcasted_iota(I32, scores[0].shape, 1).astype(F32)

    def body(_, carry):
        out = []
        for sc, selm in carry:
            mx = jnp.max(sc, axis=1, keepdims=True)
            first = jnp.min(jnp.where(sc == mx, colf, 1e9), axis=1, keepdims=True)
            hit = colf == first
            selm = jnp.maximum(selm, jnp.where(hit, jnp.where(mx > -jnp.inf, 1.0, 0.0), 0.0))
            out.append((jnp.where(hit, -jnp.inf, sc), selm))
        return tuple(out)

    init = tuple((sc, jnp.zeros(sc.shape, F32)) for sc in scores)
    return [c[1] for c in lax.fori_loop(0, n, body, init)]


def _topn_mask_columns(scores_t, n):
    n_cand, n_col = scores_t[0].shape
    keys = [jnp.where(s > -jnp.inf, _sortable_key(s), INT_MIN) for s in scores_t]
    idx = lax.broadcasted_iota(I32, (n_cand, n_col), 0)

    def count(pred):
        v = jnp.where(pred, 1.0, 0.0)
        return jnp.sum(jnp.sum(v.reshape(4, n_cand // 4, n_col), axis=0), axis=0, keepdims=True)

    def bit_body(i, ts):
        step = lax.shift_left(jnp.int32(1), 31 - i)
        return tuple(jnp.where(count(k >= t + step) >= n, t + step, t) for k, t in zip(keys, ts))

    ts = lax.fori_loop(0, 32, bit_body, tuple(jnp.full((1, n_col), INT_MIN, I32) for _ in keys))
    thrs = [jnp.maximum(t, INT_MIN + 1) for t in ts]
    needs = [n - count(k > thr) for k, thr in zip(keys, thrs)]
    eq_idx = [jnp.where(k == thr, idx, jnp.int32(2 ** 30)) for k, thr in zip(keys, thrs)]
    n_bits = int(n_cand).bit_length()

    def idx_body(i, ms):
        step = lax.shift_left(jnp.int32(1), n_bits - 1 - i)
        return tuple(jnp.where(count(e < m + step) < need, m + step, m) for e, m, need in zip(eq_idx, ms, needs))

    lasts = lax.fori_loop(0, n_bits, idx_body, tuple(jnp.zeros((1, n_col), I32) for _ in keys))
    return [jnp.where(k > thr, 1.0, jnp.where(e <= last, 1.0, 0.0))
            for k, thr, e, last in zip(keys, thrs, eq_idx, lasts)]


def _softmax_rows(s, ok):
    s = jnp.where(ok, s, NEG)
    m = jnp.max(s, axis=1, keepdims=True)
    e = jnp.where(ok, jnp.exp2(s - m), 0.0)
    return e / jnp.maximum(jnp.sum(e, axis=1, keepdims=True), 1e-30)


def _block_expand(blk0, n_blk, width=TILE):
    rb = lax.broadcasted_iota(I32, (n_blk, width), 0)
    cj = lax.broadcasted_iota(I32, (n_blk, width), 1)
    target = blk0 + lax.shift_right_logical(cj, int(math.log2(SEL_BLOCK)))
    return jnp.where(rb == target, 1.0, 0.0).astype(BF16)


def _gelu(x):
    return 0.5 * x * (1.0 + jnp.tanh(math.sqrt(2.0 / math.pi) * (x + 0.044715 * (x * x * x))))


def _layer_norm(x, g, b):
    xc = x - jnp.mean(x, axis=1, keepdims=True)
    var = jnp.mean(xc * xc, axis=1, keepdims=True)
    return xc * lax.rsqrt(var + LN_EPS) * g + b


def _cmpz_compute(load, w_ref, o_ref):
    rows = o_ref.shape[0]
    for kv in range(2):
        xs = [jnp.concatenate([load(p, g, kv) for p in range(CMP_STRIDE)], axis=1) for g in range(B_KV_HEADS)]
        z = jnp.dot(jnp.concatenate(xs, axis=0).astype(BF16), w_ref[kv], preferred_element_type=F32)
        for g in range(B_KV_HEADS):
            c = (g * 2 + kv) * CMP_RATIO * CMP_HID
            o_ref[:, c:c + CMP_RATIO * CMP_HID] = z[g * rows:(g + 1) * rows]


def _cmpz_kernel(x_ref, w_ref, o_ref):
    rows = o_ref.shape[0]

    def load(p, g, kv):
        return x_ref[pl.ds(p * KV_ROWS + g * 2 + kv, rows, stride=CMP_STRIDE * KV_ROWS), :]

    _cmpz_compute(load, w_ref, o_ref)


def _cmpz_dense(state_rows, w1cat, tc=256):
    chunk_rows = CMP_STRIDE * KV_ROWS
    n = state_rows.shape[0] // chunk_rows
    tc = min(tc, n)
    assert n % tc == 0
    return pl.pallas_call(
        _cmpz_kernel,
        grid=(n // tc,),
        in_specs=[pl.BlockSpec((tc * chunk_rows, HEAD_DIM), lambda i: (i, 0)),
                  _resident(w1cat.shape, lambda i: (0, 0, 0))],
        out_specs=pl.BlockSpec((tc, 4 * CMP_RATIO * CMP_HID), lambda i: (i, 0)),
        out_shape=jax.ShapeDtypeStruct((n, 4 * CMP_RATIO * CMP_HID), F32),
        compiler_params=_params(1), name='cmpz_dense',
    )(state_rows, w1cat)


def _page_copy(pool_ref, buf_ref, sem_ref, pid, slot, k):
    rows = pool_ref.shape[1]
    return pltpu.make_async_copy(pool_ref.at[pid], buf_ref.at[slot, pl.ds(k * rows, rows)], sem_ref.at[slot])


def _pages_per_step(pool_ref, buf_ref):
    return buf_ref.shape[1] // pool_ref.shape[1]


def _page_fetch(pt_ref, pool_ref, buf_ref, sem_ref, step, slot):
    n = _pages_per_step(pool_ref, buf_ref)
    for k in range(n):
        _page_copy(pool_ref, buf_ref, sem_ref, pt_ref[step * n + k], slot, k).start()


def _page_wait(pool_ref, buf_ref, sem_ref, slot):
    for k in range(_pages_per_step(pool_ref, buf_ref)):
        _page_copy(pool_ref, buf_ref, sem_ref, 0, slot, k).wait()


def _page_pipeline(pt_ref, pool_ref, buf_ref, sem_ref):
    step = pl.program_id(0) * pl.num_programs(1) + pl.program_id(1)
    total = pl.num_programs(0) * pl.num_programs(1)
    slot = lax.rem(step, 2)

    @pl.when(step == 0)
    def _():
        _page_fetch(pt_ref, pool_ref, buf_ref, sem_ref, step, slot)

    @pl.when(step + 1 < total)
    def _():
        _page_fetch(pt_ref, pool_ref, buf_ref, sem_ref, step + 1, 1 - slot)

    _page_wait(pool_ref, buf_ref, sem_ref, slot)
    return slot


def _cmpz_paged_kernel(pt_ref, pool_ref, w_ref, o_ref, buf_ref, sem_ref, split_ref):
    slot = _page_pipeline(pt_ref, pool_ref, buf_ref, sem_ref)
    rows = o_ref.shape[0]
    n_pos = split_ref.shape[1]
    for j in range(KV_ROWS):
        split_ref[j] = buf_ref[slot, pl.ds(j, n_pos, stride=KV_ROWS), :]

    def load(p, g, kv):
        return split_ref[g * 2 + kv, pl.ds(p, rows, stride=CMP_STRIDE), :]

    _cmpz_compute(load, w_ref, o_ref)


def _cmpz_paged(pool, page_table, w1cat):
    db, n_pages = page_table.shape
    chunks = PAGE_SIZE // CMP_STRIDE
    npg = n_pages // PAGES_PER_STEP
    rows = PAGES_PER_STEP * chunks
    return pl.pallas_call(
        _cmpz_paged_kernel,
        grid_spec=pltpu.PrefetchScalarGridSpec(
            num_scalar_prefetch=1,
            grid=(db, npg),
            in_specs=[pl.BlockSpec(memory_space=pl.ANY),
                      _resident(w1cat.shape, lambda b, g, pt: (0, 0, 0))],
            out_specs=pl.BlockSpec((rows, 4 * CMP_RATIO * CMP_HID), lambda b, g, pt: (b * npg + g, 0)),
            scratch_shapes=[pltpu.VMEM((2, PAGES_PER_STEP * pool.shape[1], pool.shape[2]), F32),
                            pltpu.SemaphoreType.DMA((2,)),
                            pltpu.VMEM((KV_ROWS, PAGES_PER_STEP * PAGE_SIZE, pool.shape[2]), F32)]),
        out_shape=jax.ShapeDtypeStruct((db * n_pages * chunks, 4 * CMP_RATIO * CMP_HID), F32),
        compiler_params=_params(2), name='cmpz_paged',
    )(page_table.reshape(-1), pool, w1cat)


def _cmp_finish_kernel(z_ref, pe_ref, w1_ref, w2_ref, k_ref, v_ref):
    n = z_ref.shape[1]
    for kv, o_ref in ((0, k_ref), (1, v_ref)):
        pew = jnp.dot(pe_ref[kv], w1_ref[kv], preferred_element_type=F32)[0:1]
        for g in range(B_KV_HEADS):
            c = (g * 2 + kv) * CMP_RATIO * CMP_HID
            z0 = z_ref[0, :, c:c + CMP_HID]
            z1 = z_ref[0, :, c + CMP_HID:c + 2 * CMP_HID]
            pre = z0 + pltpu.roll(z1, n - 1, 0) + pew
            out = jnp.dot(_gelu(pre).astype(BF16), w2_ref[kv], preferred_element_type=F32)
            o_ref[0, :, g * HEAD_DIM:(g + 1) * HEAD_DIM] = out.astype(o_ref.dtype)


def _cmp_finish(z3, pe8, w1, w2):
    nb, n, zc = z3.shape
    out = jax.ShapeDtypeStruct((nb, n, B_KV_HEADS * HEAD_DIM), BF16)
    return pl.pallas_call(
        _cmp_finish_kernel,
        grid=(nb,),
        in_specs=[pl.BlockSpec((1, n, zc), lambda b: (b, 0, 0)),
                  _resident(pe8.shape, lambda b: (0, 0, 0)),
                  _resident(w1.shape, lambda b: (0, 0, 0)),
                  _resident(w2.shape, lambda b: (0, 0, 0))],
        out_specs=[pl.BlockSpec((1, n, B_KV_HEADS * HEAD_DIM), lambda b: (b, 0, 0))] * 2,
        out_shape=[out, out],
        compiler_params=_params(1), name='cmp_finish',
    )(z3, pe8, w1, w2)


def _ka_kernel(iq_ref, ikw_ref, ikwq_ref, aq_ref, akv_ref, bias_ref, o_ref, keys_ref, m_ref, l_ref, acc_ref,
               *, topk, nd):
    qt = pl.program_id(1)
    last_pair = lax.shift_right_logical(qt, 1)
    key_row = lax.broadcasted_iota(I32, (PAIR, TILE), 0)
    q_pos = qt * TILE + lax.broadcasted_iota(I32, (PAIR, TILE), 1)
    w_t = ikwq_ref[...].T[IDX_DIM:IDX_DIM + IDX_HEADS] * (IDX_HEADS ** -0.5 * IDX_DIM ** -0.5)
    iq_all = jnp.concatenate([iq_ref[:, h * LANES:(h + 1) * LANES] for h in range(IDX_HEADS)], axis=0)

    def score_pair(kp, masked):
        kb = ikw_ref[pl.ds(pl.multiple_of(kp * PAIR, PAIR), PAIR), :].astype(BF16)
        s = _dot_nt(kb, iq_all)
        acc = jnp.zeros((PAIR, TILE), F32)
        for h in range(IDX_HEADS):
            acc = acc + jnp.maximum(s[:, h * TILE:(h + 1) * TILE], 0.0) * w_t[h:h + 1]
        key = _sortable_key(acc)
        if masked:
            key = jnp.where(kp * PAIR + key_row <= q_pos, key, INT_MIN)
        keys_ref[kp] = key

    def score_body(j, c):
        for u in range(ATTN_UNROLL):
            score_pair(jnp.minimum(j * ATTN_UNROLL + u, jnp.maximum(last_pair - 1, 0)), False)
        return c

    lax.fori_loop(0, (last_pair + ATTN_UNROLL - 1) // ATTN_UNROLL, score_body, 0)
    score_pair(last_pair, True)

    def count_where(pred):
        def body(kp, c):
            v = jnp.where(pred(keys_ref[kp], kp), 1.0, 0.0)
            return c + jnp.sum(v.reshape(4, PAIR // 4, TILE), axis=0)
        c = lax.fori_loop(0, last_pair + 1, body, jnp.zeros((PAIR // 4, TILE), F32))
        return jnp.sum(c, axis=0, keepdims=True)

    def count_ge(cand):
        return count_where(lambda k, kp: k >= cand)

    t = _kth_largest_key(count_ge, topk, (1, TILE))
    thr = jnp.maximum(t, INT_MIN + 1)

    cnt_gt = count_ge(thr + 1)
    need = topk - cnt_gt
    cnt_eq = count_ge(thr) - cnt_gt
    tie = jnp.where(t > INT_MIN, jnp.where(cnt_eq > need, 1.0, 0.0), 0.0)

    @pl.when(jnp.max(tie) > 0.0)
    def _():
        n_bits = int(keys_ref.shape[0] * PAIR).bit_length()

        def idx_body(i, mm):
            cand = mm + lax.shift_left(jnp.int32(1), n_bits - 1 - i)
            c = count_where(lambda k, kp: jnp.where(k == thr, kp * PAIR + key_row, INT_MIN) < cand)
            c = c - count_where(lambda k, kp: k != thr)
            return jnp.where(c < need, cand, mm)

        last = lax.fori_loop(0, n_bits, idx_body, jnp.zeros((1, TILE), I32))
        last = jnp.where(tie > 0.0, last, jnp.int32(2 ** 30))

        def demote(kp, c):
            k = keys_ref[kp]
            pos = jnp.where(k == thr, kp * PAIR + key_row, INT_MIN)
            keys_ref[kp] = jnp.where(pos > last, thr - 1, k)
            return c

        lax.fori_loop(0, last_pair + 1, demote, 0)

    q = [jnp.concatenate([aq_ref[:, (g * A_GROUP + r) * HEAD_DIM:(g * A_GROUP + r + 1) * HEAD_DIM]
                          for r in range(A_GROUP)], axis=0) for g in range(A_KV_HEADS)]
    _attn_reset(m_ref, l_ref, acc_ref)

    def att_pair(kp_raw):
        kp = jnp.minimum(kp_raw, last_pair)
        d0 = jnp.clip(qt - 2 * kp, 0, nd)
        d1 = jnp.clip(qt - 2 * kp - 1, 0, nd)
        live = jnp.where(kp_raw <= last_pair, 0.0, NEG)
        madd = jnp.where(keys_ref[kp] >= thr, live, NEG).T
        r0 = pl.multiple_of(kp * (PAIR * KV_ROWS), PAIR * KV_ROWS)
        for g in range(A_KV_HEADS):
            kk = akv_ref[pl.ds(r0 + 2 * g, PAIR, stride=KV_ROWS), :].astype(BF16)
            vv = akv_ref[pl.ds(r0 + 2 * g + 1, PAIR, stride=KV_ROWS), :].astype(BF16)

            def bias_of(r, g=g):
                rs = slice(r * TILE, (r + 1) * TILE)
                return jnp.concatenate([bias_ref[d0, g, rs, :], bias_ref[d1, g, rs, :]], axis=1)

            _attn_update(q[g], kk, vv, bias_of, madd, m_ref, l_ref, acc_ref, g * A_GROUP, A_GROUP)

    def att_body(j, c):
        for u in range(ATTN_UNROLL):
            att_pair(j * ATTN_UNROLL + u)
        return c

    lax.fori_loop(0, (last_pair + ATTN_UNROLL) // ATTN_UNROLL, att_body, 0)
    for h in range(A_HEADS):
        o_ref[:, h * HEAD_DIM:(h + 1) * HEAD_DIM] = _attn_out(m_ref, l_ref, acc_ref, h).astype(o_ref.dtype)


def _prompt_mixer_a(qmat, small, a_kv_rows, bias_a, bsz, t, cols, nd):
    nt = t // TILE
    assert nt % 2 == 0
    topk = min(TOPK_MAX, t // 4)
    qc = cols['q']
    return pl.pallas_call(
        functools.partial(_ka_kernel, topk=topk, nd=nd),
        grid=(bsz, nt),
        in_specs=[
            pl.BlockSpec((TILE, IDX_HEADS * LANES), lambda b, i: (b * nt + i, qc['i_q'] // (IDX_HEADS * LANES))),
            _resident((t, LANES), lambda b, i: (b, 0)),
            pl.BlockSpec((TILE, LANES), lambda b, i: (b * nt + i, 0)),
            pl.BlockSpec((TILE, A_HEADS * HEAD_DIM), lambda b, i: (b * nt + i, qc['a_q'] // (A_HEADS * HEAD_DIM))),
            _resident((t * KV_ROWS, HEAD_DIM), lambda b, i: (b, 0)),
            _resident(bias_a.shape, lambda b, i: (0, 0, 0, 0)),
        ],
        out_specs=pl.BlockSpec((TILE, A_HEADS * HEAD_DIM), lambda b, i: (b * nt + i, 0)),
        out_shape=jax.ShapeDtypeStruct((bsz * t, A_HEADS * HEAD_DIM), BF16),
        scratch_shapes=[pltpu.VMEM((nt // 2, PAIR, TILE), I32)] + [pltpu.VMEM((A_HEADS, TILE, LANES), F32)] * 3,
        compiler_params=_params(2), name='prompt_mixer_a',
    )(qmat, small, small, qmat, a_kv_rows, bias_a)


def _kb_kernel(bq_ref, gate_ref, ck_ref, cv_ref, ovt_ref, sel_ref, w0_ref, w1_ref, w2_ref, w3_ref, w4_ref,
               bias_ref, o_ref, m_ref, l_ref, acc_ref, ocmp_ref, osel_ref, *, nd, n_sel):
    qt = pl.program_id(1)
    last_pair = lax.shift_right_logical(qt, 1)
    rows = B_GROUP * TILE
    row = lax.broadcasted_iota(I32, (TILE, TILE), 0)
    col = lax.broadcasted_iota(I32, (TILE, TILE), 1)
    colp = lax.broadcasted_iota(I32, (TILE, PAIR), 1)
    q_pos_p = qt * TILE + lax.broadcasted_iota(I32, (TILE, PAIR), 0)
    q = [jnp.concatenate([bq_ref[:, (g * B_GROUP + r) * HEAD_DIM:(g * B_GROUP + r + 1) * HEAD_DIM]
                          for r in range(B_GROUP)], axis=0) for g in range(B_KV_HEADS)]

    ncp = ck_ref.shape[1]
    q_pos = qt * TILE + lax.rem(lax.broadcasted_iota(I32, (rows, ncp), 0), TILE)
    cmp_end = lax.broadcasted_iota(I32, (rows, ncp), 1) * CMP_STRIDE + (CMP_BLOCK - 1)
    cmp_ok = cmp_end <= q_pos
    blk = row
    cur = 2 * qt + jnp.where(col >= SEL_BLOCK, 1, 0)
    scores_t = []
    for g in range(B_KV_HEADS):
        p = _softmax_rows(_dot_nt(q[g], ck_ref[0, :, g * HEAD_DIM:(g + 1) * HEAD_DIM]), cmp_ok)
        o_cmp = jnp.dot(p.astype(BF16), cv_ref[0, :, g * HEAD_DIM:(g + 1) * HEAD_DIM], preferred_element_type=F32)
        psum = p[0:TILE]
        ocmp_ref[g * B_GROUP] = o_cmp[0:TILE]
        for r in range(1, B_GROUP):
            psum = psum + p[r * TILE:(r + 1) * TILE]
            ocmp_ref[g * B_GROUP + r] = o_cmp[r * TILE:(r + 1) * TILE]
        imp_t = lax.dot_general(ovt_ref[...], psum, (((1,), (1,)), ((), ())), preferred_element_type=F32,
                                precision=lax.Precision.HIGHEST)
        forced = jnp.where(blk == 0, jnp.inf, jnp.where(blk >= cur - 1, jnp.inf, imp_t))
        scores_t.append(jnp.where(blk <= cur, forced, -jnp.inf))
    selm = [m.T.astype(BF16) for m in _topn_mask_columns(scores_t, n_sel)]

    _attn_reset(m_ref, l_ref, acc_ref)

    def sel_pair(kp_raw, diagonal):
        if diagonal:
            kp = kp_raw
        else:
            kp = jnp.minimum(kp_raw, jnp.maximum(last_pair - 1, 0))
            dead = jnp.where(kp_raw < last_pair, 0.0, NEG)
        d0 = jnp.clip(qt - 2 * kp, 0, nd)
        d1 = jnp.clip(qt - 2 * kp - 1, 0, nd)
        r0 = pl.multiple_of(kp * (PAIR * KV_ROWS), PAIR * KV_ROWS)
        expand = _block_expand((PAIR // SEL_BLOCK) * kp, selm[0].shape[1], PAIR)
        for g in range(B_KV_HEADS):
            madd = (jnp.dot(selm[g], expand, preferred_element_type=F32) - 1.0) * (-NEG)
            if diagonal:
                madd = jnp.where(kp * PAIR + colp <= q_pos_p, madd, NEG)
            else:
                madd = madd + dead
            kk = sel_ref[pl.ds(r0 + 2 * g, PAIR, stride=KV_ROWS), :].astype(BF16)
            vv = sel_ref[pl.ds(r0 + 2 * g + 1, PAIR, stride=KV_ROWS), :].astype(BF16)

            def bias_of(r, g=g):
                rs = slice(r * TILE, (r + 1) * TILE)
                return jnp.concatenate([bias_ref[d0, g, rs, :], bias_ref[d1, g, rs, :]], axis=1)

            _attn_update(q[g], kk, vv, bias_of, madd, m_ref, l_ref, acc_ref, g * B_GROUP, B_GROUP)

    def sel_body(j, c):
        for u in range(ATTN_UNROLL):
            sel_pair(j * ATTN_UNROLL + u, False)
        return c

    lax.fori_loop(0, (last_pair + ATTN_UNROLL - 1) // ATTN_UNROLL, sel_body, 0)
    sel_pair(last_pair, True)
    for h in range(B_HEADS):
        osel_ref[h] = _attn_out(m_ref, l_ref, acc_ref, h)

    _attn_reset(m_ref, l_ref, acc_ref)
    w_refs = (w0_ref, w1_ref, w2_ref, w3_ref, w4_ref)

    def win_mask(k):
        if k == 0:
            ok = col <= row
        elif k == WINDOW // TILE:
            ok = row <= col
        else:
            ok = col >= 0
        return jnp.where(ok, jnp.where(qt >= k, 0.0, NEG), NEG)

    for ks in ((0, 1), (2, 3), (4,)):
        madd = jnp.concatenate([win_mask(k) for k in ks], axis=1)
        for g in range(B_KV_HEADS):
            kk = jnp.concatenate([w_refs[k][pl.ds(2 * g, TILE, stride=KV_ROWS), :] for k in ks], axis=0).astype(BF16)
            vv = jnp.concatenate([w_refs[k][pl.ds(2 * g + 1, TILE, stride=KV_ROWS), :] for k in ks],
                                 axis=0).astype(BF16)

            def bias_of(r, g=g, ks=ks):
                return jnp.concatenate([bias_ref[k, g, r * TILE:(r + 1) * TILE, :] for k in ks], axis=1)

            _attn_update(q[g], kk, vv, bias_of, madd, m_ref, l_ref, acc_ref, g * B_GROUP, B_GROUP)

    gate = jax.nn.sigmoid(gate_ref[...])
    for h in range(B_HEADS):
        o = (gate[:, 3 * h:3 * h + 1] * ocmp_ref[h] + gate[:, 3 * h + 1:3 * h + 2] * osel_ref[h]
             + gate[:, 3 * h + 2:3 * h + 3] * _attn_out(m_ref, l_ref, acc_ref, h))
        o_ref[:, h * HEAD_DIM:(h + 1) * HEAD_DIM] = o.astype(o_ref.dtype)


def _overlap_matrix(length, n_rows, n_cols):
    nc = (length - CMP_BLOCK) // CMP_STRIDE + 1
    ns = -(-length // SEL_BLOCK)
    cs = np.arange(nc) * CMP_STRIDE
    ss = np.arange(ns) * SEL_BLOCK
    ov = np.minimum(cs[:, None] + CMP_BLOCK, ss[None, :] + SEL_BLOCK) - np.maximum(cs[:, None], ss[None, :])
    out = np.zeros((n_rows, n_cols), np.float32)
    out[:nc, :ns] = np.clip(ov, 0, None).astype(np.float32) / CMP_BLOCK
    return jnp.asarray(out)


def _prompt_mixer_b(qmat, small, sel_rows, win_rows, cmp_k, cmp_v, bias_b, bsz, t, cols, nd):
    nt = t // TILE
    ns = -(-t // SEL_BLOCK)
    assert ns <= LANES and WINDOW // TILE == 4 and nd >= WINDOW // TILE
    qc = cols['q']
    ncp = cmp_k.shape[1]
    ov = _overlap_matrix(t, ncp, LANES).T

    def win_spec(k):
        return pl.BlockSpec((TILE * KV_ROWS, HEAD_DIM), lambda b, i: (b * nt + jnp.maximum(i - k, 0), 0))

    return pl.pallas_call(
        functools.partial(_kb_kernel, nd=nd, n_sel=min(SEL_TOPN, ns)),
        grid=(bsz, nt),
        in_specs=[
            pl.BlockSpec((TILE, B_HEADS * HEAD_DIM), lambda b, i: (b * nt + i, qc['b_q'] // (B_HEADS * HEAD_DIM))),
            pl.BlockSpec((TILE, LANES), lambda b, i: (b * nt + i, 1)),
            _resident((1, ncp, B_KV_HEADS * HEAD_DIM), lambda b, i: (b, 0, 0)),
            _resident((1, ncp, B_KV_HEADS * HEAD_DIM), lambda b, i: (b, 0, 0)),
            _resident(ov.shape, lambda b, i: (0, 0)),
            _resident((t * KV_ROWS, HEAD_DIM), lambda b, i: (b, 0)),
            win_spec(0), win_spec(1), win_spec(2), win_spec(3), win_spec(4),
            _resident(bias_b.shape, lambda b, i: (0, 0, 0, 0)),
        ],
        out_specs=pl.BlockSpec((TILE, B_HEADS * HEAD_DIM), lambda b, i: (b * nt + i, 0)),
        out_shape=jax.ShapeDtypeStruct((bsz * t, B_HEADS * HEAD_DIM), BF16),
        scratch_shapes=[pltpu.VMEM((B_HEADS, TILE, LANES), F32)] * 5,
        compiler_params=_params(2), name='prompt_mixer_b',
    )(qmat, small, cmp_k, cmp_v, ov, sel_rows, win_rows, win_rows, win_rows, win_rows, win_rows, bias_b)


def _mem_kernel(q_ref, kv_ref, o_ref):
    scale = MEM_HEAD_DIM ** -0.5
    for h in range(MEM_HEADS):
        c = h * 2 * MEM_HEAD_DIM
        kk = kv_ref[:, c:c + MEM_HEAD_DIM].astype(BF16)
        vv = kv_ref[:, c + MEM_HEAD_DIM:c + 2 * MEM_HEAD_DIM].astype(BF16)
        s = _dot_nt(q_ref[:, h * MEM_HEAD_DIM:(h + 1) * MEM_HEAD_DIM], kk) * scale
        e = jnp.exp(s - jnp.max(s, axis=1, keepdims=True))
        p = e / jnp.sum(e, axis=1, keepdims=True)
        o = jnp.dot(p.astype(BF16), vv, preferred_element_type=F32)
        o_ref[:, h * MEM_HEAD_DIM:(h + 1) * MEM_HEAD_DIM] = o.astype(o_ref.dtype)


def _mem_attend(qmat, q_col, mem_kv2d, n_batch, rows_per_batch, n_mem, tq):
    width = MEM_HEADS * MEM_HEAD_DIM
    tq = min(tq, rows_per_batch)
    nq = rows_per_batch // tq
    return pl.pallas_call(
        _mem_kernel,
        grid=(n_batch, nq),
        in_specs=[pl.BlockSpec((tq, width), lambda b, i: (b * nq + i, q_col // width)),
                  pl.BlockSpec((n_mem, 2 * width), lambda b, i: (b, 0))],
        out_specs=pl.BlockSpec((tq, width), lambda b, i: (b * nq + i, 0)),
        out_shape=jax.ShapeDtypeStruct((n_batch * rows_per_batch, width), BF16),
        compiler_params=_params(2), name='mem_attend',
    )(qmat, mem_kv2d)


def _gated_proj_kernel(ga_ref, gb_ref, gm_ref, oa_ref, ob_ref, om_ref, wpa_ref, wpb_ref, wpm_ref, o_ref):
    merged = jax.nn.sigmoid(ga_ref[...]) * jnp.dot(oa_ref[...], wpa_ref[...], preferred_element_type=F32)
    merged = merged + jax.nn.sigmoid(gb_ref[...]) * jnp.dot(ob_ref[...], wpb_ref[...], preferred_element_type=F32)
    merged = merged + jax.nn.sigmoid(gm_ref[...]) * jnp.dot(om_ref[...], wpm_ref[...], preferred_element_type=F32)
    o_ref[...] = merged.astype(o_ref.dtype)


def _out_proj_kernel(x_ref, mg_ref, wo_ref, lg_ref, lb_ref, o_ref, *, alpha):
    y = alpha * x_ref[...] + jnp.dot(mg_ref[...], wo_ref[...], preferred_element_type=F32)
    o_ref[...] = _layer_norm(y, lg_ref[...], lb_ref[...])


def _merge(x2d, gmat, oa, ob, om, wpa, wpb, wpm, wo, ln_g, ln_b, alpha, tm=512, tn=1024):
    m, d = x2d.shape
    tm = min(tm, m)
    assert m % tm == 0 and d % tn == 0
    nj = d // tn
    row = lambda i, j: (i, 0)
    wcol = lambda i, j: (0, j)
    merged = pl.pallas_call(
        _gated_proj_kernel,
        grid=(m // tm, nj),
        in_specs=[pl.BlockSpec((tm, tn), lambda i, j: (i, j)),
                  pl.BlockSpec((tm, tn), lambda i, j: (i, nj + j)),
                  pl.BlockSpec((tm, tn), lambda i, j: (i, 2 * nj + j)),
                  pl.BlockSpec((tm, oa.shape[1]), row), pl.BlockSpec((tm, ob.shape[1]), row),
                  pl.BlockSpec((tm, om.shape[1]), row),
                  pl.BlockSpec((wpa.shape[0], tn), wcol), pl.BlockSpec((wpb.shape[0], tn), wcol),
                  pl.BlockSpec((wpm.shape[0], tn), wcol)],
        out_specs=pl.BlockSpec((tm, tn), lambda i, j: (i, j)),
        out_shape=jax.ShapeDtypeStruct((m, d), BF16),
        compiler_params=_params(2), name='gated_proj',
    )(gmat, gmat, gmat, oa, ob, om, wpa, wpb, wpm)
    fixed = lambda i: (0, 0)
    return pl.pallas_call(
        functools.partial(_out_proj_kernel, alpha=alpha),
        grid=(m // tm,),
        in_specs=[pl.BlockSpec((tm, d), lambda i: (i, 0)), pl.BlockSpec((tm, d), lambda i: (i, 0)),
                  _resident(wo.shape, fixed), _resident((1, d), fixed), _resident((1, d), fixed)],
        out_specs=pl.BlockSpec((tm, d), lambda i: (i, 0)),
        out_shape=jax.ShapeDtypeStruct((m, d), F32),
        compiler_params=_params(1), name='out_proj_ln',
    )(x2d, merged, wo, ln_g, ln_b)


def _ffn_kernel(x_ref, wu_ref, bu_ref, wd_ref, bd_ref, lg_ref, lb_ref, o_ref, acc_ref, *, alpha):
    j = pl.program_id(1)

    @pl.when(j == 0)
    def _():
        acc_ref[...] = jnp.zeros_like(acc_ref)

    u = jnp.dot(x_ref[...].astype(BF16), wu_ref[...], preferred_element_type=F32) + bu_ref[...]
    u = jnp.square(jnp.maximum(u, 0.0))
    acc_ref[...] += jnp.dot(u.astype(BF16), wd_ref[...], preferred_element_type=F32)

    @pl.when(j == pl.num_programs(1) - 1)
    def _():
        y = alpha * x_ref[...] + acc_ref[...] + bd_ref[...]
        o_ref[...] = _layer_norm(y, lg_ref[...], lb_ref[...])


def _ffn(x2d, wu, bu, wd, bd, ln_g, ln_b, alpha, tm=512, tf=1024):
    m, d = x2d.shape
    dff = wu.shape[1]
    tm = min(tm, m)
    assert m % tm == 0 and dff % tf == 0
    return pl.pallas_call(
        functools.partial(_ffn_kernel, alpha=alpha),
        grid=(m // tm, dff // tf),
        in_specs=[pl.BlockSpec((tm, d), lambda i, j: (i, 0)),
                  pl.BlockSpec((d, tf), lambda i, j: (0, j)), pl.BlockSpec((1, tf), lambda i, j: (0, j)),
                  pl.BlockSpec((tf, d), lambda i, j: (j, 0)),
                  _resident((1, d), lambda i, j: (0, 0)), _resident((1, d), lambda i, j: (0, 0)),
                  _resident((1, d), lambda i, j: (0, 0))],
        out_specs=pl.BlockSpec((tm, d), lambda i, j: (i, 0)),
        out_shape=jax.ShapeDtypeStruct((m, d), F32),
        scratch_shapes=[pltpu.VMEM((tm, d), F32)],
        compiler_params=_params(2), name='ffn_ln',
    )(x2d, wu, bu, wd, bd, ln_g, ln_b)


def _sidx_kernel(pt_ref, iq_ref, w_ref, knew_ref, pool_ref, o_ref, buf_ref, sem_ref, keys_ref, *, topk, n_q, past):
    g = pl.program_id(1)
    slot = _page_pipeline(pt_ref, pool_ref, buf_ref, sem_ref)
    pages = _pages_per_step(pool_ref, buf_ref)
    group = 8

    def scores(k_t):
        s = jnp.dot(iq_ref[0], k_t.astype(BF16), preferred_element_type=F32)
        acc = jnp.zeros((n_q, k_t.shape[1]), F32)
        for h in range(IDX_HEADS):
            acc = acc + jnp.maximum(s[h * n_q:(h + 1) * n_q], 0.0) * w_ref[0, h * n_q:(h + 1) * n_q, 0:1]
        return _sortable_key(acc)

    def page_body(j, c):
        k_t = jnp.concatenate(
            [buf_ref[slot, pl.ds(pl.multiple_of((j * group + u) * IDX_DIM, IDX_DIM), IDX_DIM), :]
             for u in range(group)], axis=1)
        width = group * PAGE_SIZE
        keys_ref[:, pl.ds(pl.multiple_of((g * pages + j * group) * PAGE_SIZE, width), width)] = scores(k_t)
        return c

    lax.fori_loop(0, pages // group, page_body, 0)

    @pl.when(g == pl.num_programs(1) - 1)
    def _():
        lp = keys_ref.shape[1]
        rown = lax.broadcasted_iota(I32, (n_q, TILE), 0)
        coln = lax.broadcasted_iota(I32, (n_q, TILE), 1)
        keys_ref[:, past:lp] = jnp.where(coln <= rown, scores(knew_ref[0]), INT_MIN)
        keys = keys_ref[...]
        pos = lax.broadcasted_iota(I32, (n_q, lp), 1)

        def count(pred):
            v = jnp.where(pred, 1.0, 0.0)
            n_part = 16
            part = (lp // (n_part * LANES)) * LANES
            acc = v[:, 0:part]
            for i in range(1, n_part):
                acc = acc + v[:, i * part:(i + 1) * part]
            return jnp.sum(acc, axis=1, keepdims=True) + jnp.sum(v[:, n_part * part:], axis=1, keepdims=True)

        t = _kth_largest_key(lambda cand: count(keys >= cand), topk, (n_q, 1))
        thr = jnp.maximum(t, INT_MIN + 1)
        cnt_gt = count(keys >= thr + 1)
        need = topk - cnt_gt
        cnt_eq = count(keys >= thr) - cnt_gt
        tie = jnp.where(t > INT_MIN, jnp.where(cnt_eq > need, 1.0, 0.0), 0.0)
        eq_pos = jnp.where(keys == thr, pos, jnp.int32(2 ** 30))
        n_bits = int(lp).bit_length()

        def idx_body(i, mm):
            cand = mm + lax.shift_left(jnp.int32(1), n_bits - 1 - i)
            return jnp.where(count(eq_pos < cand) < need, cand, mm)

        last = lax.fori_loop(0, n_bits, idx_body, jnp.zeros((n_q, 1), I32))
        last = jnp.where(tie > 0.0, last, jnp.int32(2 ** 30))
        sel = jnp.where(keys > thr, 1.0, jnp.where(keys == thr, jnp.where(pos <= last, 1.0, 0.0), 0.0))
        o_ref[0, 0] = sel


def _sample_index_mask(page_table, iq_s, w_s, knew, pool_idx, topk):
    db, n_pages = page_table.shape
    n_q = iq_s.shape[1] // IDX_HEADS
    pages = math.gcd(n_pages, INDEX_PAGES_PER_STEP)
    assert pages % 8 == 0 and pool_idx.shape[1:] == (IDX_DIM, PAGE_SIZE)
    npg = n_pages // pages
    lp = n_pages * PAGE_SIZE + TILE
    return pl.pallas_call(
        functools.partial(_sidx_kernel, topk=topk, n_q=n_q, past=n_pages * PAGE_SIZE),
        grid_spec=pltpu.PrefetchScalarGridSpec(
            num_scalar_prefetch=1,
            grid=(db, npg),
            in_specs=[pl.BlockSpec((1,) + iq_s.shape[1:], lambda b, g, pt: (b, 0, 0)),
                      pl.BlockSpec((1,) + w_s.shape[1:], lambda b, g, pt: (b, 0, 0)),
                      pl.BlockSpec((1,) + knew.shape[1:], lambda b, g, pt: (b, 0, 0)),
                      pl.BlockSpec(memory_space=pl.ANY)],
            out_specs=pl.BlockSpec((1, 1, n_q, lp), lambda b, g, pt: (b, 0, 0, 0)),
            scratch_shapes=[pltpu.VMEM((2, pages * pool_idx.shape[1], pool_idx.shape[2]), F32),
                            pltpu.SemaphoreType.DMA((2,)),
                            pltpu.VMEM((n_q, lp), I32)]),
        out_shape=jax.ShapeDtypeStruct((db, 1, n_q, lp), F32),
        compiler_params=_params(2), name='sample_index_mask',
    )(page_table.reshape(-1), iq_s, w_s, knew, pool_idx)


def _pattn_kernel(pt_ref, q_ref, mask_ref, new_ref, bias_ref, pool_ref, o_ref, buf_ref, sem_ref,
                  m_ref, l_ref, acc_ref, *, nd, n_q, n_pages):
    g = pl.program_id(1)
    npg = pl.num_programs(1)
    slot = _page_pipeline(pt_ref, pool_ref, buf_ref, sem_ref)
    rows = q_ref.shape[1]
    half = rows // 2
    mask_groups = mask_ref.shape[1]

    @pl.when(g == 0)
    def _():
        m0, l0, a0 = _flash_init(rows, 2 * HEAD_DIM)
        m_ref[...] = m0
        l_ref[...] = l0
        acc_ref[...] = a0

    def block_update(rows_of, p0, n_tiles, carry):
        kk = jnp.concatenate([rows_of(0), rows_of(2)], axis=1).astype(BF16)
        vv = jnp.concatenate([rows_of(1), rows_of(3)], axis=1).astype(BF16)
        width = n_tiles * TILE
        mk = mask_ref[0, :, :, pl.ds(pl.multiple_of(p0 * TILE, TILE), width)]
        madd = (mk - 1.0) * (-NEG)
        reps = rows // (mask_groups * n_q)
        madd = jnp.concatenate([madd[i] for i in range(mask_groups) for _ in range(reps)], axis=0)
        bias = jnp.concatenate([bias_ref[jnp.clip(n_pages - (p0 + i), 0, nd)] for i in range(n_tiles)], axis=1)
        s = _dot_nt(q_ref[0], kk) + (bias + madd)
        return _flash_step(s, *carry, vv)

    carry = (m_ref[...], l_ref[...], acc_ref[...])
    keys = PAGES_PER_BLOCK * PAGE_SIZE
    for blk in range(PAGES_PER_STEP // PAGES_PER_BLOCK):
        def rows_of(j, blk=blk):
            return buf_ref[slot, pl.ds(blk * keys * KV_ROWS + j, keys, stride=KV_ROWS), :]
        carry = block_update(rows_of, g * PAGES_PER_STEP + blk * PAGES_PER_BLOCK, PAGES_PER_BLOCK, carry)
    m_ref[...], l_ref[...], acc_ref[...] = carry

    @pl.when(g == npg - 1)
    def _():
        def new_rows(j):
            return new_ref[0, :, j * HEAD_DIM:(j + 1) * HEAD_DIM]
        o = _flash_out(*block_update(new_rows, n_pages, 1, carry))
        o_ref[0, 0:half] = o[0:half, 0:HEAD_DIM]
        o_ref[0, half:rows] = o[half:rows, HEAD_DIM:2 * HEAD_DIM]


def _paged_attention(page_table, qblk, mask, new_kv, bias_s, pool, nd, n_q):
    db, n_pages = page_table.shape
    npg = n_pages // PAGES_PER_STEP
    rows = qblk.shape[1]
    return pl.pallas_call(
        functools.partial(_pattn_kernel, nd=nd, n_q=n_q, n_pages=n_pages),
        grid_spec=pltpu.PrefetchScalarGridSpec(
            num_scalar_prefetch=1,
            grid=(db, npg),
            in_specs=[pl.BlockSpec((1,) + qblk.shape[1:], lambda b, g, pt: (b, 0, 0)),
                      pl.BlockSpec((1,) + mask.shape[1:], lambda b, g, pt: (b, 0, 0, 0)),
                      pl.BlockSpec((1,) + new_kv.shape[1:], lambda b, g, pt: (b, 0, 0)),
                      _resident(bias_s.shape, lambda b, g, pt: (0, 0, 0)),
                      pl.BlockSpec(memory_space=pl.ANY)],
            out_specs=pl.BlockSpec((1, rows, HEAD_DIM), lambda b, g, pt: (b, 0, 0)),
            scratch_shapes=[pltpu.VMEM((2, PAGES_PER_STEP * pool.shape[1], pool.shape[2]), F32),
                            pltpu.SemaphoreType.DMA((2,)),
                            pltpu.VMEM((rows, 1), F32), pltpu.VMEM((rows, 1), F32),
                            pltpu.VMEM((rows, 2 * HEAD_DIM), F32)]),
        out_shape=jax.ShapeDtypeStruct((db, rows, HEAD_DIM), F32),
        compiler_params=_params(2), name='paged_attention',
    )(page_table.reshape(-1), qblk, mask, new_kv, bias_s, pool)


def _scmp_kernel(q_ref, ck_ref, cv_ref, ov_ref, ocmp_ref, mask_ref, selm_ref, *, past, n_q, n_sel):
    rows = q_ref.shape[2]
    ncp = ck_ref.shape[1]
    nsp = ov_ref.shape[1]
    qi =lax.rem(lax.broadcasted_iota(I32, (rows, ncp), 0), n_q)
    cmp_end = lax.broadcasted_iota(I32, (rows, ncp), 1) * CMP_STRIDE + (CMP_BLOCK - 1)
    cmp_ok = cmp_end <= past + qi
    blk = lax.broadcasted_iota(I32, (n_q, nsp), 1)
    cur = lax.shift_right_logical(past + lax.broadcasted_iota(I32, (n_q, nsp), 0), int(math.log2(SEL_BLOCK)))
    q_pos = past + lax.broadcasted_iota(I32, (n_q, TILE), 0)
    coln = lax.broadcasted_iota(I32, (n_q, TILE), 1)
    scores = []
    for g in range(B_KV_HEADS):
        p = _softmax_rows(_dot_nt(q_ref[0, g], ck_ref[0, :, g * HEAD_DIM:(g + 1) * HEAD_DIM]), cmp_ok)
        ocmp_ref[0, g] = jnp.dot(p.astype(BF16), cv_ref[0, :, g * HEAD_DIM:(g + 1) * HEAD_DIM],
                                 preferred_element_type=F32)
        psum = p[0:n_q]
        for r in range(1, rows // n_q):
            psum = psum + p[r * n_q:(r + 1) * n_q]
        imp = jnp.dot(psum, ov_ref[...], preferred_element_type=F32, precision=lax.Precision.HIGHEST)
        forced = jnp.where(blk == 0, jnp.inf, jnp.where(blk >= cur - 1, jnp.inf, imp))
        scores.append(jnp.where(blk <= cur, forced, -jnp.inf))
    for g, selm in enumerate(_topn_mask(scores, n_sel)):
        selm_ref[g] = selm

    blocks_per_tile = TILE // SEL_BLOCK
    half = lax.shift_right_logical(coln, int(math.log2(SEL_BLOCK)))

    def expand_tile(kt):
        blk0 = blocks_per_tile * kt
        win0 = pl.multiple_of(lax.shift_right_logical(blk0, int(math.log2(LANES))) * LANES, LANES)
        idx = (blk0 - win0) + half
        k0 = pl.multiple_of(kt * TILE, TILE)
        for g in range(B_KV_HEADS):
            e = jnp.take_along_axis(selm_ref[g, :, pl.ds(win0, LANES)], idx, axis=1)
            mask_ref[0, g, :, pl.ds(k0, TILE)] = jnp.where(kt * TILE + coln <= q_pos, e, 0.0)

    unroll = 4
    n_tiles = mask_ref.shape[3] // TILE

    def expand_body(j, c):
        for u in range(unroll):
            expand_tile(j * unroll + u)
        return c

    lax.fori_loop(0, n_tiles // unroll, expand_body, 0)
    for kt in range(n_tiles - n_tiles % unroll, n_tiles):
        expand_tile(jnp.int32(kt))


def _sample_cmp_select(bq_s, cmp_k, cmp_v, length, past, n_q):
    db = bq_s.shape[0]
    ncp = cmp_k.shape[1]
    ns = -(-length // SEL_BLOCK)
    nsp = -(-ns // LANES) * LANES
    ov = _overlap_matrix(length, ncp, nsp)
    lp = past + TILE
    return pl.pallas_call(
        functools.partial(_scmp_kernel, past=past, n_q=n_q, n_sel=min(SEL_TOPN, ns)),
        grid=(db,),
        in_specs=[pl.BlockSpec((1,) + bq_s.shape[1:], lambda b: (b, 0, 0, 0)),
                  pl.BlockSpec((1, ncp, B_KV_HEADS * HEAD_DIM), lambda b: (b, 0, 0)),
                  pl.BlockSpec((1, ncp, B_KV_HEADS * HEAD_DIM), lambda b: (b, 0, 0)),
                  _resident(ov.shape, lambda b: (0, 0))],
        out_specs=[pl.BlockSpec((1,) + bq_s.shape[1:], lambda b: (b, 0, 0, 0)),
                   pl.BlockSpec((1, B_KV_HEADS, n_q, lp), lambda b: (b, 0, 0, 0))],
        out_shape=[jax.ShapeDtypeStruct(bq_s.shape, F32),
                   jax.ShapeDtypeStruct((db, B_KV_HEADS, n_q, lp), F32)],
        scratch_shapes=[pltpu.VMEM((B_KV_HEADS, n_q, nsp), F32)],
        compiler_params=_params(1), name='sample_cmp_select',
    )(bq_s, cmp_k, cmp_v, ov)


def _swin_kernel(q_ref, win_ref, new_ref, bias_ref, gate_ref, ocmp_ref, osel_ref, o_ref, *, n_q):
    rows = q_ref.shape[1]
    half = rows // 2
    wb = win_ref.shape[1]
    qi =lax.rem(lax.broadcasted_iota(I32, (rows, TILE), 0), n_q)
    col = lax.broadcasted_iota(I32, (rows, TILE), 1)
    carry = _flash_init(rows, 2 * HEAD_DIM)

    def tile_update(kv, dlt, ok, carry):
        kk = jnp.concatenate([kv[:, 0:HEAD_DIM], kv[:, 2 * HEAD_DIM:3 * HEAD_DIM]], axis=1).astype(BF16)
        vv = jnp.concatenate([kv[:, HEAD_DIM:2 * HEAD_DIM], kv[:, 3 * HEAD_DIM:4 * HEAD_DIM]], axis=1).astype(BF16)
        s = _dot_nt(q_ref[0], kk) + (bias_ref[dlt] + jnp.where(ok, 0.0, NEG))
        return _flash_step(s, *carry, vv)

    for kt in range(wb // TILE):
        ok = col + kt * TILE >= qi + (wb - WINDOW)
        carry = tile_update(win_ref[0, kt * TILE:(kt + 1) * TILE, :], wb // TILE - kt, ok, carry)
    carry = tile_update(new_ref[0], 0, col <= qi, carry)
    o = _flash_out(*carry)
    gate = jax.nn.sigmoid(gate_ref[0])
    for h in range(B_HEADS):
        sl = slice(h * n_q, (h + 1) * n_q)
        ow = o[sl, 0:HEAD_DIM] if h < B_GROUP else o[sl, HEAD_DIM:2 * HEAD_DIM]
        o_ref[0, :, h * HEAD_DIM:(h + 1) * HEAD_DIM] = (
            gate[:, 3 * h:3 * h + 1] * ocmp_ref[0, sl] + gate[:, 3 * h + 1:3 * h + 2] * osel_ref[0, sl]
            + gate[:, 3 * h + 2:3 * h + 3] * ow).astype(o_ref.dtype)


def _sample_window_combine(qblk, win_state, new_win, bias_s, gates, o_cmp, o_sel, n_q):
    db, rows, _ = qblk.shape
    return pl.pallas_call(
        functools.partial(_swin_kernel, n_q=n_q),
        grid=(db,),
        in_specs=[pl.BlockSpec((1,) + qblk.shape[1:], lambda b: (b, 0, 0)),
                  pl.BlockSpec((1,) + win_state.shape[1:], lambda b: (b, 0, 0)),
                  pl.BlockSpec((1,) + new_win.shape[1:], lambda b: (b, 0, 0)),
                  _resident(bias_s.shape, lambda b: (0, 0, 0)),
                  pl.BlockSpec((1,) + gates.shape[1:], lambda b: (b, 0, 0)),
                  pl.BlockSpec((1, rows, HEAD_DIM), lambda b: (b, 0, 0)),
                  pl.BlockSpec((1, rows, HEAD_DIM), lambda b: (b, 0, 0))],
        out_specs=pl.BlockSpec((1, n_q, B_HEADS * HEAD_DIM), lambda b: (b, 0, 0)),
        out_shape=jax.ShapeDtypeStruct((db, n_q, B_HEADS * HEAD_DIM), BF16),
        compiler_params=_params(1), name='sample_window_combine',
    )(qblk, win_state, new_win, bias_s, gates, o_cmp, o_sel)


def _pack_weights(w_in):
    d = w_in.shape[0]
    sizes = _split_sizes(d)
    off, o = {}, 0
    for name in _GROUPS:
        off[name] = o
        o += sizes[name]
    take = lambda name: w_in[:, off[name]:off[name] + sizes[name]]
    zeros = lambda n: jnp.zeros((d, n), w_in.dtype)
    iq = take('i_q').reshape(d, IDX_HEADS, IDX_DIM)
    iq = jnp.concatenate([iq, jnp.zeros_like(iq)], axis=2).reshape(d, IDX_HEADS * LANES)
    w_f = jnp.concatenate([take('a_kv'), take('b_cmp'), take('b_sel'), take('b_win'),
                           take('i_k'), take('i_w'), zeros(LANES - IDX_DIM - IDX_HEADS),
                           take('b_gate'), zeros(LANES - B_HEADS * 3), zeros(2 * LANES)], axis=1).astype(BF16)
    w_q = jnp.concatenate([take('a_q') * QK_SCALE, take('b_q') * QK_SCALE, take('m_q'), iq], axis=1).astype(BF16)
    w_g = take('g_merge').astype(BF16)
    kvw = 4 * HEAD_DIM
    cols = dict(f=dict(a_kv=0, b_cmp=kvw, b_sel=2 * kvw, b_win=3 * kvw, i_kw=4 * kvw, b_gate=4 * kvw + LANES),
                q=dict(a_q=0, b_q=1024, m_q=2048, i_q=3072))
    return w_f, w_q, w_g, cols


def kernel(x_prompt, x_sample, mem_prompt, cache_a_kv, cache_a_idx, cache_b_cmp, cache_b_sel, state_b_win,
           cache_mem, page_table, rel_table, w_in, w_mem_kv, cmp_pe_k, cmp_w1_k, cmp_w2_k, cmp_pe_v, cmp_w1_v,
           cmp_w2_v, w_pa, w_pb, w_pm, w_o, ln1_g, ln1_b, w_up, b_up, w_down, b_down, ln2_g, ln2_b):
    depth = w_in.shape[0]
    assert depth == 1
    bsz, t, d = x_prompt.shape
    db, ds, _ = x_sample.shape
    n_mem = mem_prompt.shape[1]
    n_pool = cache_a_kv.shape[1]
    n_pages = page_table.shape[1]
    past = n_pages * PAGE_SIZE
    wb = state_b_win.shape[2]
    alpha = (2 * depth) ** 0.25
    kvw = 4 * HEAD_DIM
    assert t % TILE == 0 and ds == 8 and wb % TILE == 0 and n_pages % PAGES_PER_STEP == 0

    w_f, w_q, w_g, cols = _pack_weights(w_in[0])
    fc, qc = cols['f'], cols['q']
    w1cat = jnp.stack([jnp.concatenate([w[0][:CMP_STRIDE * HEAD_DIM], w[0][CMP_STRIDE * HEAD_DIM:]], axis=1)
                       for w in (cmp_w1_k, cmp_w1_v)]).astype(BF16)
    w1 = jnp.stack([cmp_w1_k[0], cmp_w1_v[0]]).astype(BF16)
    w2 = jnp.stack([cmp_w2_k[0], cmp_w2_v[0]]).astype(BF16)
    pe8 = jnp.broadcast_to(jnp.stack([cmp_pe_k[0].reshape(1, -1), cmp_pe_v[0].reshape(1, -1)]),
                           (2, 8, CMP_BLOCK * HEAD_DIM)).astype(BF16)
    wpa, wpb, wpm, wo = (w[0].astype(BF16) for w in (w_pa, w_pb, w_pm, w_o))
    wu, wd = w_up[0].astype(BF16), w_down[0].astype(BF16)
    nd = _num_near_tiles()
    bias = _bias_tiles(rel_table, nd)

    def dense_tail(x2d, gmat, oa, ob, om):
        x1 = _merge(x2d, gmat, oa, ob, om, wpa, wpb, wpm, wo, ln1_g, ln1_b, alpha)
        return _ffn(x1, wu, b_up, wd, b_down, ln2_g, ln2_b, alpha)

    xp = x_prompt.reshape(bsz * t, d)
    p_a_kv, p_b_cmp, p_b_sel, p_b_win, small_p = _project_states(xp, w_f)
    qp = _matmul(xp, w_q, BF16, tn=1024)
    gp = _matmul(xp, w_g, F32, tn=1024)
    p_a_idx = small_p[:, :IDX_DIM]

    zp = _cmpz_dense(p_b_cmp, w1cat)
    cmp_k, cmp_v = _cmp_finish(zp.reshape(bsz, t // CMP_STRIDE, -1), pe8, w1, w2)
    o_a = _prompt_mixer_a(qp, small_p, p_a_kv, bias[0], bsz, t, cols, nd)
    o_b = _prompt_mixer_b(qp, small_p, p_b_sel, p_b_win, cmp_k, cmp_v, bias[1], bsz, t, cols, nd)
    mem_kv = _matmul(mem_prompt.reshape(bsz * n_mem, d), w_mem_kv[0].astype(BF16), F32, tn=1024)
    o_m = _mem_attend(qp, qc['m_q'], mem_kv, bsz, t, n_mem, tq=512)
    y_prompt = dense_tail(xp, gp, o_a, o_b, o_m).reshape(bsz, t, d)

    xs = x_sample.reshape(db * ds, d)
    fs = _matmul(xs, w_f, F32, tn=w_f.shape[1] // 2)
    qs = _matmul(xs, w_q, BF16, tn=1024)
    gs = _matmul(xs, w_g, F32, tn=1024)
    s_a_kv = fs[:, fc['a_kv']:fc['a_kv'] + kvw]
    s_b_cmp = fs[:, fc['b_cmp']:fc['b_cmp'] + kvw]
    s_b_sel = fs[:, fc['b_sel']:fc['b_sel'] + kvw]
    s_b_win = fs[:, fc['b_win']:fc['b_win'] + kvw]
    s_a_idx = fs[:, fc['i_kw']:fc['i_kw'] + IDX_DIM]
    length = past + ds

    def pad_new(rows2d):
        r = rows2d.reshape(db, ds, -1)
        return jnp.concatenate([r, jnp.zeros((db, TILE - ds, r.shape[2]), r.dtype)], axis=1)

    def head_major(q2d, heads):
        return q2d.reshape(db, ds, heads, -1).transpose(0, 2, 1, 3).reshape(db, heads * ds, -1)

    def block_q(q2d):
        qh = head_major(q2d, A_HEADS).reshape(db, A_KV_HEADS, A_GROUP * ds, HEAD_DIM)
        z = jnp.zeros_like(qh[:, 0])
        return jnp.concatenate([jnp.concatenate([qh[:, 0], z], axis=2),
                                jnp.concatenate([z, qh[:, 1]], axis=2)], axis=1)

    def sample_bias(tiles):
        n = tiles.shape[0]
        return tiles.reshape(n, A_KV_HEADS, A_GROUP, TILE, TILE)[:, :, :, :ds].reshape(n, A_HEADS * ds, TILE)

    iq_s = head_major(qs[:, qc['i_q']:qc['i_q'] + IDX_HEADS * LANES], IDX_HEADS)[:, :, :IDX_DIM]
    w_s = fs[:, fc['i_kw'] + IDX_DIM:fc['i_kw'] + IDX_DIM + IDX_HEADS] * (IDX_HEADS ** -0.5 * IDX_DIM ** -0.5)
    w_s = jnp.broadcast_to(head_major(w_s, IDX_HEADS), (db, IDX_HEADS * ds, LANES))
    mask_a = _sample_index_mask(page_table, iq_s, w_s, pad_new(s_a_idx).transpose(0, 2, 1),
                                cache_a_idx.reshape(n_pool, PAGE_SIZE, IDX_DIM).transpose(0, 2, 1),
                                min(TOPK_MAX, length // 4))
    qa_blk = block_q(qs[:, qc['a_q']:qc['a_q'] + A_HEADS * HEAD_DIM])
    o_a_s = _paged_attention(page_table, qa_blk, mask_a, pad_new(s_a_kv), sample_bias(bias[0]),
                             cache_a_kv.reshape(n_pool, PAGE_SIZE * KV_ROWS, HEAD_DIM), nd, ds)

    zs = _cmpz_paged(cache_b_cmp.reshape(n_pool, PAGE_SIZE * KV_ROWS, HEAD_DIM), page_table, w1cat)
    cmp_k_s, cmp_v_s = _cmp_finish(zs.reshape(db, past // CMP_STRIDE, -1), pe8, w1, w2)
    bq2d = qs[:, qc['b_q']:qc['b_q'] + B_HEADS * HEAD_DIM]
    bq_s = head_major(bq2d, B_HEADS).reshape(db, B_KV_HEADS, B_GROUP * ds, HEAD_DIM)
    o_cmp_s, mask_b = _sample_cmp_select(bq_s, cmp_k_s, cmp_v_s, length, past, ds)
    qb_blk = block_q(bq2d)
    bias_sb = sample_bias(bias[1])
    o_sel_s = _paged_attention(page_table, qb_blk, mask_b, pad_new(s_b_sel), bias_sb,
                               cache_b_sel.reshape(n_pool, PAGE_SIZE * KV_ROWS, HEAD_DIM), nd, ds)
    gates_s = fs[:, fc['b_gate']:fc['b_gate'] + LANES].reshape(db, ds, LANES)
    o_b_s = _sample_window_combine(qb_blk, state_b_win.reshape(db, wb, kvw), pad_new(s_b_win), bias_sb,
                                   gates_s, o_cmp_s.reshape(db, B_HEADS * ds, HEAD_DIM), o_sel_s, ds)

    o_a_s = o_a_s.reshape(db, A_HEADS, ds, HEAD_DIM).transpose(0, 2, 1, 3).reshape(db * ds, -1).astype(BF16)
    o_m_s = _mem_attend(qs, qc['m_q'], cache_mem.reshape(db * n_mem, -1), db, ds, n_mem, tq=ds)
    y_sample = dense_tail(xs, gs, o_a_s, o_b_s.reshape(db * ds, -1), o_m_s).reshape(db, ds, d)

    kv6 = lambda a, n, rows: a.reshape(1, n, rows, 2, 2, HEAD_DIM)
    wp = min(WINDOW, t)
    new_win = jnp.concatenate([state_b_win.reshape(db, wb, 2, 2, HEAD_DIM)[:, ds:],
                               s_b_win.reshape(db, ds, 2, 2, HEAD_DIM)], axis=1)
    return (y_prompt, y_sample,
            kv6(p_a_kv, bsz, t), p_a_idx.reshape(1, bsz, t, IDX_DIM), kv6(p_b_cmp, bsz, t), kv6(p_b_sel, bsz, t),
            kv6(p_b_win, bsz, t)[:, :, t - wp:],
            mem_kv.reshape(1, bsz, n_mem, MEM_HEADS, 2, MEM_HEAD_DIM),
            kv6(s_a_kv, db, ds), s_a_idx.reshape(1, db, ds, IDX_DIM), kv6(s_b_cmp, db, ds), kv6(s_b_sel, db, ds),
            new_win[None])
```

```python
import functools
import math

import numpy as np
import jax
import jax.numpy as jnp
from jax import lax
from jax.experimental import pallas as pl
from jax.experimental.pallas import tpu as pltpu

F32 = jnp.float32
BF16 = jnp.bfloat16
I32 = jnp.int32

HEAD_DIM = 128
A_HEADS = 8
A_KV_HEADS = 2
A_GROUP = A_HEADS // A_KV_HEADS
IDX_HEADS = 8
IDX_DIM = 64
TOPK_MAX = 256
B_HEADS = 8
B_KV_HEADS = 2
B_GROUP = B_HEADS // B_KV_HEADS
CMP_BLOCK = 32
CMP_STRIDE = 16
CMP_RATIO = CMP_BLOCK // CMP_STRIDE
CMP_HID = 128
SEL_BLOCK = 64
SEL_TOPN = 16
WINDOW = 512
MEM_HEADS = 4
MEM_HEAD_DIM = 256
N_BUCKETS = 32
MAX_DISTANCE = 1024
LN_EPS = 1e-5
PAGE_SIZE = 128
KV_ROWS = 4

LANES = 128
VMEM_LIMIT = 56 * 1024 * 1024

TILE = 128
PAIR = 2 * TILE
ATTN_UNROLL = 2
NEG = -1e30
LOG2_E = math.log2(math.e)
QK_SCALE = HEAD_DIM ** -0.5 * LOG2_E
INT_MIN = -2 ** 31
PAGES_PER_STEP = 32
PAGES_PER_BLOCK = 8
INDEX_PAGES_PER_STEP = 64

_GROUPS = ('a_q', 'a_kv', 'i_q', 'i_k', 'i_w', 'b_q', 'b_cmp', 'b_sel', 'b_win', 'b_gate', 'm_q', 'g_merge')


def _split_sizes(d_model):
    return dict(
        a_q=A_HEADS * HEAD_DIM, a_kv=A_KV_HEADS * 2 * HEAD_DIM, i_q=IDX_HEADS * IDX_DIM, i_k=IDX_DIM,
        i_w=IDX_HEADS, b_q=B_HEADS * HEAD_DIM, b_cmp=B_KV_HEADS * 2 * HEAD_DIM, b_sel=B_KV_HEADS * 2 * HEAD_DIM,
        b_win=B_KV_HEADS * 2 * HEAD_DIM, b_gate=B_HEADS * 3, m_q=MEM_HEADS * MEM_HEAD_DIM, g_merge=3 * d_model)


def _params(n_grid, vmem=VMEM_LIMIT):
    return pltpu.CompilerParams(dimension_semantics=('arbitrary',) * n_grid, vmem_limit_bytes=vmem)


def _resident(block, index_map):
    return pl.BlockSpec(block, index_map, pipeline_mode=pl.Buffered(1))


def _mm_kernel(x_ref, w_ref, o_ref):
    o_ref[...] = jnp.dot(x_ref[...].astype(BF16), w_ref[...], preferred_element_type=F32).astype(o_ref.dtype)


def _matmul(x, w, out_dtype, tn, tm=1024, name='matmul'):
    m, k = x.shape
    n = w.shape[1]
    tm = min(tm, m)
    assert m % tm == 0 and n % tn == 0
    return pl.pallas_call(
        _mm_kernel,
        grid=(m // tm, n // tn),
        in_specs=[pl.BlockSpec((tm, k), lambda i, j: (i, 0)), pl.BlockSpec((k, tn), lambda i, j: (0, j))],
        out_specs=pl.BlockSpec((tm, tn), lambda i, j: (i, j)),
        out_shape=jax.ShapeDtypeStruct((m, n), out_dtype),
        compiler_params=_params(2), name=name,
    )(x, w)


def _proj_states_kernel(x_ref, w_ref, a_ref, c_ref, s_ref, n_ref, small_ref):
    j = pl.program_id(1)
    tm = x_ref.shape[0]
    res = jnp.dot(x_ref[...].astype(BF16), w_ref[...], preferred_element_type=F32)
    for k, o_ref in enumerate((a_ref, c_ref, s_ref, n_ref)):
        @pl.when(j == k)
        def _(o_ref=o_ref):
            for c in range(KV_ROWS):
                o_ref[pl.ds(c, tm, stride=KV_ROWS), :] = res[:, c * HEAD_DIM:(c + 1) * HEAD_DIM]

    @pl.when(j == 4)
    def _():
        small_ref[...] = res


def _project_states(x, w_f, tm=1024):
    m, k = x.shape
    kvw = KV_ROWS * HEAD_DIM
    assert m % tm == 0 and w_f.shape[1] == 5 * kvw
    state = jax.ShapeDtypeStruct((m * KV_ROWS, HEAD_DIM), F32)
    return pl.pallas_call(
        _proj_states_kernel,
        grid=(m // tm, 5),
        in_specs=[pl.BlockSpec((tm, k), lambda i, j: (i, 0)), pl.BlockSpec((k, kvw), lambda i, j: (0, j))],
        out_specs=[pl.BlockSpec((tm * KV_ROWS, HEAD_DIM), lambda i, j: (i, 0))] * 4
        + [pl.BlockSpec((tm, kvw), lambda i, j: (i, 0))],
        out_shape=[state] * 4 + [jax.ShapeDtypeStruct((m, kvw), F32)],
        compiler_params=_params(2), name='project_states',
    )(x, w_f)


def _rel_bucket(dist):
    d = jnp.maximum(dist, 0)
    exact = N_BUCKETS // 2
    df = jnp.maximum(d, 1).astype(F32)
    large = exact + (jnp.log(df / exact) / math.log(MAX_DISTANCE / exact) * (N_BUCKETS - exact)).astype(I32)
    return jnp.where(d < exact, d, jnp.minimum(large, N_BUCKETS - 1))


def _num_near_tiles():
    exact = N_BUCKETS // 2
    d = np.arange(1, 4 * MAX_DISTANCE, dtype=np.float64)
    large = exact + np.floor(np.log(d / exact) / math.log(MAX_DISTANCE / exact) * (N_BUCKETS - exact))
    bucket = np.where(d < exact, d, np.minimum(large, N_BUCKETS - 1))
    d_const = int(d[np.argmax(bucket == N_BUCKETS - 1)])
    return -(-(d_const + TILE // 2 + TILE - 1) // TILE)


def _bias_kernel(u_ref, o_ref, *, n_tiles, d_top):
    for dt in range(n_tiles):
        start = d_top - dt * TILE - (TILE - 1)
        row = u_ref[0, :, start:start + 2 * TILE]
        x = jnp.broadcast_to(row, (TILE, 2 * TILE))
        x = pltpu.roll(x, TILE + 1, 1, stride=1, stride_axis=0)
        o_ref[dt] = x[:, :TILE]


def _bias_tiles(rel_table, nd):
    n_tiles = nd + 1
    d_top = n_tiles * TILE
    ul = d_top + 2 * TILE
    n_heads = rel_table.shape[1]
    dist = d_top - jnp.arange(ul)
    u = (rel_table[_rel_bucket(dist)] * LOG2_E).T.reshape(n_heads, 1, ul)
    out = pl.pallas_call(
        functools.partial(_bias_kernel, n_tiles=n_tiles, d_top=d_top),
        grid=(n_heads,),
        in_specs=[pl.BlockSpec((1, 1, ul), lambda h: (h, 0, 0))],
        out_specs=pl.BlockSpec((None, n_tiles, None, None, TILE, TILE),
                               lambda h: (h // 8, 0, (h % 8) // 4, h % 4, 0, 0)),
        out_shape=jax.ShapeDtypeStruct((2, n_tiles, 2, 4, TILE, TILE), F32),
        compiler_params=_params(1), name='bias_tiles',
    )(u)
    return out.reshape(2, n_tiles, 2, 4 * TILE, TILE)


def _dot_nt(a, b):
    return lax.dot_general(a, b, (((1,), (1,)), ((), ())), preferred_element_type=F32)


def _flash_step(s, m, l, acc, v):
    m_new = jnp.maximum(m, jnp.max(s, axis=1, keepdims=True))
    alpha = jnp.exp2(m - m_new)
    p = jnp.exp2(s - m_new)
    l = alpha * l + jnp.sum(p, axis=1, keepdims=True)
    acc = alpha * acc + jnp.dot(p.astype(BF16), v, preferred_element_type=F32)
    return m_new, l, acc


def _flash_init(rows, width):
    return (jnp.full((rows, 1), NEG, F32), jnp.zeros((rows, 1), F32), jnp.zeros((rows, width), F32))


def _flash_out(m, l, acc):
    return jnp.where(m > 0.5 * NEG, acc / jnp.maximum(l, 1e-30), 0.0)


def _sortable_key(x):
    bits = pltpu.bitcast(x, I32)
    bits = jnp.where(bits == INT_MIN, 0, bits)
    return jnp.where(bits < 0, bits ^ 0x7FFFFFFF, bits)


def _kth_largest_key(count_ge, k, shape):
    def bit_body(i, t):
        cand = t + lax.shift_left(jnp.int32(1), 31 - i)
        return jnp.where(count_ge(cand) >= k, cand, t)
    return lax.fori_loop(0, 32, bit_body, jnp.full(shape, INT_MIN, I32))


def _attn_update(q_g, kk, vv, bias_of, madd, m_ref, l_ref, acc_ref, h0, n_heads):
    tk = kk.shape[0]
    s_all = _dot_nt(q_g, kk)
    ps, alphas = [], []
    for r in range(n_heads):
        s = s_all[r * TILE:(r + 1) * TILE] + (bias_of(r) + madd)
        m_prev = m_ref[h0 + r]
        m_next = jnp.maximum(m_prev, jnp.max(s, axis=1, keepdims=True))
        alpha = jnp.exp2(m_prev - m_next)
        p = jnp.exp2(s - jnp.concatenate([m_next] * (tk // LANES), axis=1))
        l_ref[h0 + r] = alpha * l_ref[h0 + r] + jnp.sum(p, axis=1, keepdims=True)
        m_ref[h0 + r] = m_next
        ps.append(p.astype(BF16))
        alphas.append(alpha)
    pv = jnp.dot(jnp.concatenate(ps, axis=0), vv, preferred_element_type=F32)
    for r in range(n_heads):
        acc_ref[h0 + r] = alphas[r] * acc_ref[h0 + r] + pv[r * TILE:(r + 1) * TILE]


def _attn_reset(m_ref, l_ref, acc_ref):
    m_ref[...] = jnp.full(m_ref.shape, NEG, F32)
    l_ref[...] = jnp.zeros(l_ref.shape, F32)
    acc_ref[...] = jnp.zeros(acc_ref.shape, F32)


def _attn_out(m_ref, l_ref, acc_ref, h):
    return jnp.where(m_ref[h] > 0.5 * NEG, acc_ref[h] / jnp.maximum(l_ref[h], 1e-30), 0.0)


def _topn_mask(scores, n):
    colf = lax.broadcasted_iota(I32, scores[0].shape, 1).astype(F32)

    def body(_, carry):
        out = []
        for sc, selm in carry:
            mx = jnp.max(sc, axis=1, keepdims=True)
            first = jnp.min(jnp.where(sc == mx, colf, 1e9), axis=1, keepdims=True)
            hit = colf == first
            selm = jnp.maximum(selm, jnp.where(hit, jnp.where(mx > -jnp.inf, 1.0, 0.0), 0.0))
            out.append((jnp.where(hit, -jnp.inf, sc), selm))
        return tuple(out)

    init = tuple((sc, jnp.zeros(sc.shape, F32)) for sc in scores)
    return [c[1] for c in lax.fori_loop(0, n, body, init)]


def _topn_mask_columns(scores_t, n):
    n_cand, n_col = scores_t[0].shape
    keys = [jnp.where(s > -jnp.inf, _sortable_key(s), INT_MIN) for s in scores_t]
    idx = lax.broadcasted_iota(I32, (n_cand, n_col), 0)

    def count(pred):
        v = jnp.where(pred, 1.0, 0.0)
        return jnp.sum(jnp.sum(v.reshape(4, n_cand // 4, n_col), axis=0), axis=0, keepdims=True)

    def bit_body(i, ts):
        step = lax.shift_left(jnp.int32(1), 31 - i)
        return tuple(jnp.where(count(k >= t + step) >= n, t + step, t) for k, t in zip(keys, ts))

    ts = lax.fori_loop(0, 32, bit_body, tuple(jnp.full((1, n_col), INT_MIN, I32) for _ in keys))
    thrs = [jnp.maximum(t, INT_MIN + 1) for t in ts]
    needs = [n - count(k > thr) for k, thr in zip(keys, thrs)]
    eq_idx = [jnp.where(k == thr, idx, jnp.int32(2 ** 30)) for k, thr in zip(keys, thrs)]
    n_bits = int(n_cand).bit_length()

    def idx_body(i, ms):
        step = lax.shift_left(jnp.int32(1), n_bits - 1 - i)
        return tuple(jnp.where(count(e < m + step) < need, m + step, m) for e, m, need in zip(eq_idx, ms, needs))

    lasts = lax.fori_loop(0, n_bits, idx_body, tuple(jnp.zeros((1, n_col), I32) for _ in keys))
    return [jnp.where(k > thr, 1.0, jnp.where(e <= last, 1.0, 0.0))
            for k, thr, e, last in zip(keys, thrs, eq_idx, lasts)]


def _softmax_rows(s, ok):
    s = jnp.where(ok, s, NEG)
    m = jnp.max(s, axis=1, keepdims=True)
    e = jnp.where(ok, jnp.exp2(s - m), 0.0)
    return e / jnp.maximum(jnp.sum(e, axis=1, keepdims=True), 1e-30)


def _block_expand(blk0, n_blk, width=TILE):
    rb = lax.broadcasted_iota(I32, (n_blk, width), 0)
    cj = lax.broadcasted_iota(I32, (n_blk, width), 1)
    target = blk0 + lax.shift_right_logical(cj, int(math.log2(SEL_BLOCK)))
    return jnp.where(rb == target, 1.0, 0.0).astype(BF16)


def _gelu(x):
    return 0.5 * x * (1.0 + jnp.tanh(math.sqrt(2.0 / math.pi) * (x + 0.044715 * (x * x * x))))


def _layer_norm(x, g, b):
    xc = x - jnp.mean(x, axis=1, keepdims=True)
    var = jnp.mean(xc * xc, axis=1, keepdims=True)
    return xc * lax.rsqrt(var + LN_EPS) * g + b


def _cmpz_compute(load, w_ref, o_ref):
    rows = o_ref.shape[0]
    for kv in range(2):
        xs = [jnp.concatenate([load(p, g, kv) for p in range(CMP_STRIDE)], axis=1) for g in range(B_KV_HEADS)]
        z = jnp.dot(jnp.concatenate(xs, axis=0).astype(BF16), w_ref[kv], preferred_element_type=F32)
        for g in range(B_KV_HEADS):
            c = (g * 2 + kv) * CMP_RATIO * CMP_HID
            o_ref[:, c:c + CMP_RATIO * CMP_HID] = z[g * rows:(g + 1) * rows]


def _cmpz_kernel(x_ref, w_ref, o_ref):
    rows = o_ref.shape[0]

    def load(p, g, kv):
        return x_ref[pl.ds(p * KV_ROWS + g * 2 + kv, rows, stride=CMP_STRIDE * KV_ROWS), :]

    _cmpz_compute(load, w_ref, o_ref)


def _cmpz_dense(state_rows, w1cat, tc=256):
    chunk_rows = CMP_STRIDE * KV_ROWS
    n = state_rows.shape[0] // chunk_rows
    tc = min(tc, n)
    assert n % tc == 0
    return pl.pallas_call(
        _cmpz_kernel,
        grid=(n // tc,),
        in_specs=[pl.BlockSpec((tc * chunk_rows, HEAD_DIM), lambda i: (i, 0)),
                  _resident(w1cat.shape, lambda i: (0, 0, 0))],
        out_specs=pl.BlockSpec((tc, 4 * CMP_RATIO * CMP_HID), lambda i: (i, 0)),
        out_shape=jax.ShapeDtypeStruct((n, 4 * CMP_RATIO * CMP_HID), F32),
        compiler_params=_params(1), name='cmpz_dense',
    )(state_rows, w1cat)


def _page_copy(pool_ref, buf_ref, sem_ref, pid, slot, k):
    rows = pool_ref.shape[1]
    return pltpu.make_async_copy(pool_ref.at[pid], buf_ref.at[slot, pl.ds(k * rows, rows)], sem_ref.at[slot])


def _pages_per_step(pool_ref, buf_ref):
    return buf_ref.shape[1] // pool_ref.shape[1]


def _page_fetch(pt_ref, pool_ref, buf_ref, sem_ref, step, slot):
    n = _pages_per_step(pool_ref, buf_ref)
    for k in range(n):
        _page_copy(pool_ref, buf_ref, sem_ref, pt_ref[step * n + k], slot, k).start(priority=k % 2)


def _page_wait(pool_ref, buf_ref, sem_ref, slot):
    for k in range(_pages_per_step(pool_ref, buf_ref)):
        _page_copy(pool_ref, buf_ref, sem_ref, 0, slot, k).wait()


def _page_pipeline(pt_ref, pool_ref, buf_ref, sem_ref):
    step = pl.program_id(0) * pl.num_programs(1) + pl.program_id(1)
    total = pl.num_programs(0) * pl.num_programs(1)
    slot = lax.rem(step, 2)

    @pl.when(step == 0)
    def _():
        _page_fetch(pt_ref, pool_ref, buf_ref, sem_ref, step, slot)

    @pl.when(step + 1 < total)
    def _():
        _page_fetch(pt_ref, pool_ref, buf_ref, sem_ref, step + 1, 1 - slot)

    _page_wait(pool_ref, buf_ref, sem_ref, slot)
    return slot


def _cmpz_paged_kernel(pt_ref, pool_ref, w_ref, o_ref, buf_ref, sem_ref, split_ref):
    slot = _page_pipeline(pt_ref, pool_ref, buf_ref, sem_ref)
    rows = o_ref.shape[0]
    n_pos = split_ref.shape[1]
    for j in range(KV_ROWS):
        split_ref[j] = buf_ref[slot, pl.ds(j, n_pos, stride=KV_ROWS), :]

    def load(p, g, kv):
        return split_ref[g * 2 + kv, pl.ds(p, rows, stride=CMP_STRIDE), :]

    _cmpz_compute(load, w_ref, o_ref)


def _cmpz_paged(pool, page_table, w1cat):
    db, n_pages = page_table.shape
    chunks = PAGE_SIZE // CMP_STRIDE
    npg = n_pages // PAGES_PER_STEP
    rows = PAGES_PER_STEP * chunks
    return pl.pallas_call(
        _cmpz_paged_kernel,
        grid_spec=pltpu.PrefetchScalarGridSpec(
            num_scalar_prefetch=1,
            grid=(db, npg),
            in_specs=[pl.BlockSpec(memory_space=pl.ANY),
                      _resident(w1cat.shape, lambda b, g, pt: (0, 0, 0))],
            out_specs=pl.BlockSpec((rows, 4 * CMP_RATIO * CMP_HID), lambda b, g, pt: (b * npg + g, 0)),
            scratch_shapes=[pltpu.VMEM((2, PAGES_PER_STEP * pool.shape[1], pool.shape[2]), F32),
                            pltpu.SemaphoreType.DMA((2,)),
                            pltpu.VMEM((KV_ROWS, PAGES_PER_STEP * PAGE_SIZE, pool.shape[2]), F32)]),
        out_shape=jax.ShapeDtypeStruct((db * n_pages * chunks, 4 * CMP_RATIO * CMP_HID), F32),
        compiler_params=_params(2), name='cmpz_paged',
    )(page_table.reshape(-1), pool, w1cat)


def _cmp_finish_kernel(z_ref, pe_ref, w1_ref, w2_ref, k_ref, v_ref):
    n = z_ref.shape[1]
    for kv, o_ref in ((0, k_ref), (1, v_ref)):
        pew = jnp.dot(pe_ref[kv], w1_ref[kv], preferred_element_type=F32)[0:1]
        for g in range(B_KV_HEADS):
            c = (g * 2 + kv) * CMP_RATIO * CMP_HID
            z0 = z_ref[0, :, c:c + CMP_HID]
            z1 = z_ref[0, :, c + CMP_HID:c + 2 * CMP_HID]
            pre = z0 + pltpu.roll(z1, n - 1, 0) + pew
            out = jnp.dot(_gelu(pre).astype(BF16), w2_ref[kv], preferred_element_type=F32)
            o_ref[0, :, g * HEAD_DIM:(g + 1) * HEAD_DIM] = out.astype(o_ref.dtype)


def _cmp_finish(z3, pe8, w1, w2):
    nb, n, zc = z3.shape
    out = jax.ShapeDtypeStruct((nb, n, B_KV_HEADS * HEAD_DIM), BF16)
    return pl.pallas_call(
        _cmp_finish_kernel,
        grid=(nb,),
        in_specs=[pl.BlockSpec((1, n, zc), lambda b: (b, 0, 0)),
                  _resident(pe8.shape, lambda b: (0, 0, 0)),
                  _resident(w1.shape, lambda b: (0, 0, 0)),
                  _resident(w2.shape, lambda b: (0, 0, 0))],
        out_specs=[pl.BlockSpec((1, n, B_KV_HEADS * HEAD_DIM), lambda b: (b, 0, 0))] * 2,
        out_shape=[out, out],
        compiler_params=_params(1), name='cmp_finish',
    )(z3, pe8, w1, w2)


def _ka_kernel(iq_ref, ikw_ref, ikwq_ref, aq_ref, akv_ref, bias_ref, o_ref, keys_ref, m_ref, l_ref, acc_ref,
               *, topk, nd):
    qt = pl.program_id(1)
    last_pair = lax.shift_right_logical(qt, 1)
    key_row = lax.broadcasted_iota(I32, (PAIR, TILE), 0)
    q_pos = qt * TILE + lax.broadcasted_iota(I32, (PAIR, TILE), 1)
    w_t = ikwq_ref[...].T[IDX_DIM:IDX_DIM + IDX_HEADS] * (IDX_HEADS ** -0.5 * IDX_DIM ** -0.5)
    iq_all = jnp.concatenate([iq_ref[:, h * LANES:(h + 1) * LANES] for h in range(IDX_HEADS)], axis=0)

    def score_pair(kp, masked):
        kb = ikw_ref[pl.ds(pl.multiple_of(kp * PAIR, PAIR), PAIR), :].astype(BF16)
        s = _dot_nt(kb, iq_all)
        acc = jnp.zeros((PAIR, TILE), F32)
        for h in range(IDX_HEADS):
            acc = acc + jnp.maximum(s[:, h * TILE:(h + 1) * TILE], 0.0) * w_t[h:h + 1]
        key = _sortable_key(acc)
        if masked:
            key = jnp.where(kp * PAIR + key_row <= q_pos, key, INT_MIN)
        keys_ref[kp] = key

    def score_body(j, c):
        for u in range(ATTN_UNROLL):
            score_pair(jnp.minimum(j * ATTN_UNROLL + u, jnp.maximum(last_pair - 1, 0)), False)
        return c

    lax.fori_loop(0, (last_pair + ATTN_UNROLL - 1) // ATTN_UNROLL, score_body, 0)
    score_pair(last_pair, True)

    def count_where(pred):
        def body(kp, c):
            v = jnp.where(pred(keys_ref[kp], kp), 1.0, 0.0)
            return c + jnp.sum(v.reshape(4, PAIR // 4, TILE), axis=0)
        c = lax.fori_loop(0, last_pair + 1, body, jnp.zeros((PAIR // 4, TILE), F32))
        return jnp.sum(c, axis=0, keepdims=True)

    def count_ge(cand):
        return count_where(lambda k, kp: k >= cand)

    t = _kth_largest_key(count_ge, topk, (1, TILE))
    thr = jnp.maximum(t, INT_MIN + 1)

    cnt_gt = count_ge(thr + 1)
    need = topk - cnt_gt
    cnt_eq = count_ge(thr) - cnt_gt
    tie = jnp.where(t > INT_MIN, jnp.where(cnt_eq > need, 1.0, 0.0), 0.0)

    @pl.when(jnp.max(tie) > 0.0)
    def _():
        n_bits = int(keys_ref.shape[0] * PAIR).bit_length()

        def idx_body(i, mm):
            cand = mm + lax.shift_left(jnp.int32(1), n_bits - 1 - i)
            c = count_where(lambda k, kp: jnp.where(k == thr, kp * PAIR + key_row, INT_MIN) < cand)
            c = c - count_where(lambda k, kp: k != thr)
            return jnp.where(c < need, cand, mm)

        last = lax.fori_loop(0, n_bits, idx_body, jnp.zeros((1, TILE), I32))
        last = jnp.where(tie > 0.0, last, jnp.int32(2 ** 30))

        def demote(kp, c):
            k = keys_ref[kp]
            pos = jnp.where(k == thr, kp * PAIR + key_row, INT_MIN)
            keys_ref[kp] = jnp.where(pos > last, thr - 1, k)
            return c

        lax.fori_loop(0, last_pair + 1, demote, 0)

    q = [jnp.concatenate([aq_ref[:, (g * A_GROUP + r) * HEAD_DIM:(g * A_GROUP + r + 1) * HEAD_DIM]
                          for r in range(A_GROUP)], axis=0) for g in range(A_KV_HEADS)]
    _attn_reset(m_ref, l_ref, acc_ref)

    def att_pair(kp_raw):
        kp = jnp.minimum(kp_raw, last_pair)
        d0 = jnp.clip(qt - 2 * kp, 0, nd)
        d1 = jnp.clip(qt - 2 * kp - 1, 0, nd)
        live = jnp.where(kp_raw <= last_pair, 0.0, NEG)
        madd = jnp.where(keys_ref[kp] >= thr, live, NEG).T
        r0 = pl.multiple_of(kp * (PAIR * KV_ROWS), PAIR * KV_ROWS)
        for g in range(A_KV_HEADS):
            kk = akv_ref[pl.ds(r0 + 2 * g, PAIR, stride=KV_ROWS), :].astype(BF16)
            vv = akv_ref[pl.ds(r0 + 2 * g + 1, PAIR, stride=KV_ROWS), :].astype(BF16)

            def bias_of(r, g=g):
                rs = slice(r * TILE, (r + 1) * TILE)
                return jnp.concatenate([bias_ref[d0, g, rs, :], bias_ref[d1, g, rs, :]], axis=1)

            _attn_update(q[g], kk, vv, bias_of, madd, m_ref, l_ref, acc_ref, g * A_GROUP, A_GROUP)

    def att_body(j, c):
        for u in range(ATTN_UNROLL):
            att_pair(j * ATTN_UNROLL + u)
        return c

    lax.fori_loop(0, (last_pair + ATTN_UNROLL) // ATTN_UNROLL, att_body, 0)
    for h in range(A_HEADS):
        o_ref[:, h * HEAD_DIM:(h + 1) * HEAD_DIM] = _attn_out(m_ref, l_ref, acc_ref, h).astype(o_ref.dtype)


def _prompt_mixer_a(qmat, small, a_kv_rows, bias_a, bsz, t, cols, nd):
    nt = t // TILE
    assert nt % 2 == 0
    topk = min(TOPK_MAX, t // 4)
    qc = cols['q']
    return pl.pallas_call(
        functools.partial(_ka_kernel, topk=topk, nd=nd),
        grid=(bsz, nt),
        in_specs=[
            pl.BlockSpec((TILE, IDX_HEADS * LANES), lambda b, i: (b * nt + i, qc['i_q'] // (IDX_HEADS * LANES))),
            _resident((t, LANES), lambda b, i: (b, 0)),
            pl.BlockSpec((TILE, LANES), lambda b, i: (b * nt + i, 0)),
            pl.BlockSpec((TILE, A_HEADS * HEAD_DIM), lambda b, i: (b * nt + i, qc['a_q'] // (A_HEADS * HEAD_DIM))),
            _resident((t * KV_ROWS, HEAD_DIM), lambda b, i: (b, 0)),
            _resident(bias_a.shape, lambda b, i: (0, 0, 0, 0)),
        ],
        out_specs=pl.BlockSpec((TILE, A_HEADS * HEAD_DIM), lambda b, i: (b * nt + i, 0)),
        out_shape=jax.ShapeDtypeStruct((bsz * t, A_HEADS * HEAD_DIM), BF16),
        scratch_shapes=[pltpu.VMEM((nt // 2, PAIR, TILE), I32)] + [pltpu.VMEM((A_HEADS, TILE, LANES), F32)] * 3,
        compiler_params=_params(2), name='prompt_mixer_a',
    )(qmat, small, small, qmat, a_kv_rows, bias_a)


def _kb_kernel(bq_ref, gate_ref, ck_ref, cv_ref, ovt_ref, sel_ref, w0_ref, w1_ref, w2_ref, w3_ref, w4_ref,
               bias_ref, o_ref, m_ref, l_ref, acc_ref, ocmp_ref, osel_ref, *, nd, n_sel):
    qt = pl.program_id(1)
    last_pair = lax.shift_right_logical(qt, 1)
    rows = B_GROUP * TILE
    row = lax.broadcasted_iota(I32, (TILE, TILE), 0)
    col = lax.broadcasted_iota(I32, (TILE, TILE), 1)
    colp = lax.broadcasted_iota(I32, (TILE, PAIR), 1)
    q_pos_p = qt * TILE + lax.broadcasted_iota(I32, (TILE, PAIR), 0)
    q = [jnp.concatenate([bq_ref[:, (g * B_GROUP + r) * HEAD_DIM:(g * B_GROUP + r + 1) * HEAD_DIM]
                          for r in range(B_GROUP)], axis=0) for g in range(B_KV_HEADS)]

    ncp = ck_ref.shape[1]
    q_pos = qt * TILE + lax.rem(lax.broadcasted_iota(I32, (rows, ncp), 0), TILE)
    cmp_end = lax.broadcasted_iota(I32, (rows, ncp), 1) * CMP_STRIDE + (CMP_BLOCK - 1)
    cmp_ok = cmp_end <= q_pos
    blk = row
    cur = 2 * qt + jnp.where(col >= SEL_BLOCK, 1, 0)
    scores_t = []
    for g in range(B_KV_HEADS):
        p = _softmax_rows(_dot_nt(q[g], ck_ref[0, :, g * HEAD_DIM:(g + 1) * HEAD_DIM]), cmp_ok)
        o_cmp = jnp.dot(p.astype(BF16), cv_ref[0, :, g * HEAD_DIM:(g + 1) * HEAD_DIM], preferred_element_type=F32)
        psum = p[0:TILE]
        ocmp_ref[g * B_GROUP] = o_cmp[0:TILE]
        for r in range(1, B_GROUP):
            psum = psum + p[r * TILE:(r + 1) * TILE]
            ocmp_ref[g * B_GROUP + r] = o_cmp[r * TILE:(r + 1) * TILE]
        imp_t = lax.dot_general(ovt_ref[...], psum, (((1,), (1,)), ((), ())), preferred_element_type=F32,
                                precision=lax.Precision.HIGHEST)
        forced = jnp.where(blk == 0, jnp.inf, jnp.where(blk >= cur - 1, jnp.inf, imp_t))
        scores_t.append(jnp.where(blk <= cur, forced, -jnp.inf))
    selm = [m.T.astype(BF16) for m in _topn_mask_columns(scores_t, n_sel)]

    _attn_reset(m_ref, l_ref, acc_ref)

    def sel_pair(kp_raw, diagonal):
        if diagonal:
            kp = kp_raw
        else:
            kp = jnp.minimum(kp_raw, jnp.maximum(last_pair - 1, 0))
            dead = jnp.where(kp_raw < last_pair, 0.0, NEG)
        d0 = jnp.clip(qt - 2 * kp, 0, nd)
        d1 = jnp.clip(qt - 2 * kp - 1, 0, nd)
        r0 = pl.multiple_of(kp * (PAIR * KV_ROWS), PAIR * KV_ROWS)
        expand = _block_expand((PAIR // SEL_BLOCK) * kp, selm[0].shape[1], PAIR)
        for g in range(B_KV_HEADS):
            madd = (jnp.dot(selm[g], expand, preferred_element_type=F32) - 1.0) * (-NEG)
            if diagonal:
                madd = jnp.where(kp * PAIR + colp <= q_pos_p, madd, NEG)
            else:
                madd = madd + dead
            kk = sel_ref[pl.ds(r0 + 2 * g, PAIR, stride=KV_ROWS), :].astype(BF16)
            vv = sel_ref[pl.ds(r0 + 2 * g + 1, PAIR, stride=KV_ROWS), :].astype(BF16)

            def bias_of(r, g=g):
                rs = slice(r * TILE, (r + 1) * TILE)
                return jnp.concatenate([bias_ref[d0, g, rs, :], bias_ref[d1, g, rs, :]], axis=1)

            _attn_update(q[g], kk, vv, bias_of, madd, m_ref, l_ref, acc_ref, g * B_GROUP, B_GROUP)

    def sel_body(j, c):
        for u in range(ATTN_UNROLL):
            sel_pair(j * ATTN_UNROLL + u, False)
        return c

    lax.fori_loop(0, (last_pair + ATTN_UNROLL - 1) // ATTN_UNROLL, sel_body, 0)
    sel_pair(last_pair, True)
    for h in range(B_HEADS):
        osel_ref[h] = _attn_out(m_ref, l_ref, acc_ref, h)

    _attn_reset(m_ref, l_ref, acc_ref)
    w_refs = (w0_ref, w1_ref, w2_ref, w3_ref, w4_ref)

    def win_mask(k):
        if k == 0:
            ok = col <= row
        elif k == WINDOW // TILE:
            ok = row <= col
        else:
            ok = col >= 0
        return jnp.where(ok, jnp.where(qt >= k, 0.0, NEG), NEG)

    for ks in ((0, 1), (2, 3), (4,)):
        madd = jnp.concatenate([win_mask(k) for k in ks], axis=1)
        for g in range(B_KV_HEADS):
            kk = jnp.concatenate([w_refs[k][pl.ds(2 * g, TILE, stride=KV_ROWS), :] for k in ks], axis=0).astype(BF16)
            vv = jnp.concatenate([w_refs[k][pl.ds(2 * g + 1, TILE, stride=KV_ROWS), :] for k in ks],
                                 axis=0).astype(BF16)

            def bias_of(r, g=g, ks=ks):
                return jnp.concatenate([bias_ref[k, g, r * TILE:(r + 1) * TILE, :] for k in ks], axis=1)

            _attn_update(q[g], kk, vv, bias_of, madd, m_ref, l_ref, acc_ref, g * B_GROUP, B_GROUP)

    gate = jax.nn.sigmoid(gate_ref[...])
    for h in range(B_HEADS):
        o = (gate[:, 3 * h:3 * h + 1] * ocmp_ref[h] + gate[:, 3 * h + 1:3 * h + 2] * osel_ref[h]
             + gate[:, 3 * h + 2:3 * h + 3] * _attn_out(m_ref, l_ref, acc_ref, h))
        o_ref[:, h * HEAD_DIM:(h + 1) * HEAD_DIM] = o.astype(o_ref.dtype)


def _overlap_matrix(length, n_rows, n_cols):
    nc = (length - CMP_BLOCK) // CMP_STRIDE + 1
    ns = -(-length // SEL_BLOCK)
    cs = np.arange(nc) * CMP_STRIDE
    ss = np.arange(ns) * SEL_BLOCK
    ov = np.minimum(cs[:, None] + CMP_BLOCK, ss[None, :] + SEL_BLOCK) - np.maximum(cs[:, None], ss[None, :])
    out = np.zeros((n_rows, n_cols), np.float32)
    out[:nc, :ns] = np.clip(ov, 0, None).astype(np.float32) / CMP_BLOCK
    return jnp.asarray(out)


def _prompt_mixer_b(qmat, small, sel_rows, win_rows, cmp_k, cmp_v, bias_b, bsz, t, cols, nd):
    nt = t // TILE
    ns = -(-t // SEL_BLOCK)
    assert ns <= LANES and WINDOW // TILE == 4 and nd >= WINDOW // TILE
    qc = cols['q']
    ncp = cmp_k.shape[1]
    ov = _overlap_matrix(t, ncp, LANES).T

    def win_spec(k):
        return pl.BlockSpec((TILE * KV_ROWS, HEAD_DIM), lambda b, i: (b * nt + jnp.maximum(i - k, 0), 0))

    return pl.pallas_call(
        functools.partial(_kb_kernel, nd=nd, n_sel=min(SEL_TOPN, ns)),
        grid=(bsz, nt),
        in_specs=[
            pl.BlockSpec((TILE, B_HEADS * HEAD_DIM), lambda b, i: (b * nt + i, qc['b_q'] // (B_HEADS * HEAD_DIM))),
            pl.BlockSpec((TILE, LANES), lambda b, i: (b * nt + i, 1)),
            _resident((1, ncp, B_KV_HEADS * HEAD_DIM), lambda b, i: (b, 0, 0)),
            _resident((1, ncp, B_KV_HEADS * HEAD_DIM), lambda b, i: (b, 0, 0)),
            _resident(ov.shape, lambda b, i: (0, 0)),
            _resident((t * KV_ROWS, HEAD_DIM), lambda b, i: (b, 0)),
            win_spec(0), win_spec(1), win_spec(2), win_spec(3), win_spec(4),
            _resident(bias_b.shape, lambda b, i: (0, 0, 0, 0)),
        ],
        out_specs=pl.BlockSpec((TILE, B_HEADS * HEAD_DIM), lambda b, i: (b * nt + i, 0)),
        out_shape=jax.ShapeDtypeStruct((bsz * t, B_HEADS * HEAD_DIM), BF16),
        scratch_shapes=[pltpu.VMEM((B_HEADS, TILE, LANES), F32)] * 5,
        compiler_params=_params(2), name='prompt_mixer_b',
    )(qmat, small, cmp_k, cmp_v, ov, sel_rows, win_rows, win_rows, win_rows, win_rows, win_rows, bias_b)


def _mem_kernel(q_ref, kv_ref, o_ref):
    scale = MEM_HEAD_DIM ** -0.5
    for h in range(MEM_HEADS):
        c = h * 2 * MEM_HEAD_DIM
        kk = kv_ref[:, c:c + MEM_HEAD_DIM].astype(BF16)
        vv = kv_ref[:, c + MEM_HEAD_DIM:c + 2 * MEM_HEAD_DIM].astype(BF16)
        s = _dot_nt(q_ref[:, h * MEM_HEAD_DIM:(h + 1) * MEM_HEAD_DIM], kk) * scale
        e = jnp.exp(s - jnp.max(s, axis=1, keepdims=True))
        p = e / jnp.sum(e, axis=1, keepdims=True)
        o = jnp.dot(p.astype(BF16), vv, preferred_element_type=F32)
        o_ref[:, h * MEM_HEAD_DIM:(h + 1) * MEM_HEAD_DIM] = o.astype(o_ref.dtype)


def _mem_attend(qmat, q_col, mem_kv2d, n_batch, rows_per_batch, n_mem, tq):
    width = MEM_HEADS * MEM_HEAD_DIM
    tq = min(tq, rows_per_batch)
    nq = rows_per_batch // tq
    return pl.pallas_call(
        _mem_kernel,
        grid=(n_batch, nq),
        in_specs=[pl.BlockSpec((tq, width), lambda b, i: (b * nq + i, q_col // width)),
                  pl.BlockSpec((n_mem, 2 * width), lambda b, i: (b, 0))],
        out_specs=pl.BlockSpec((tq, width), lambda b, i: (b * nq + i, 0)),
        out_shape=jax.ShapeDtypeStruct((n_batch * rows_per_batch, width), BF16),
        compiler_params=_params(2), name='mem_attend',
    )(qmat, mem_kv2d)


def _gated_proj_kernel(ga_ref, gb_ref, gm_ref, oa_ref, ob_ref, om_ref, wpa_ref, wpb_ref, wpm_ref, o_ref):
    merged = jax.nn.sigmoid(ga_ref[...]) * jnp.dot(oa_ref[...], wpa_ref[...], preferred_element_type=F32)
    merged = merged + jax.nn.sigmoid(gb_ref[...]) * jnp.dot(ob_ref[...], wpb_ref[...], preferred_element_type=F32)
    merged = merged + jax.nn.sigmoid(gm_ref[...]) * jnp.dot(om_ref[...], wpm_ref[...], preferred_element_type=F32)
    o_ref[...] = merged.astype(o_ref.dtype)


def _out_proj_kernel(x_ref, mg_ref, wo_ref, lg_ref, lb_ref, o_ref, *, alpha):
    y = alpha * x_ref[...] + jnp.dot(mg_ref[...], wo_ref[...], preferred_element_type=F32)
    o_ref[...] = _layer_norm(y, lg_ref[...], lb_ref[...])


def _merge(x2d, gmat, oa, ob, om, wpa, wpb, wpm, wo, ln_g, ln_b, alpha, tm=512, tn=1024):
    m, d = x2d.shape
    tm = min(tm, m)
    assert m % tm == 0 and d % tn == 0
    nj = d // tn
    row = lambda i, j: (i, 0)
    wcol = lambda i, j: (0, j)
    merged = pl.pallas_call(
        _gated_proj_kernel,
        grid=(m // tm, nj),
        in_specs=[pl.BlockSpec((tm, tn), lambda i, j: (i, j)),
                  pl.BlockSpec((tm, tn), lambda i, j: (i, nj + j)),
                  pl.BlockSpec((tm, tn), lambda i, j: (i, 2 * nj + j)),
                  pl.BlockSpec((tm, oa.shape[1]), row), pl.BlockSpec((tm, ob.shape[1]), row),
                  pl.BlockSpec((tm, om.shape[1]), row),
                  pl.BlockSpec((wpa.shape[0], tn), wcol), pl.BlockSpec((wpb.shape[0], tn), wcol),
                  pl.BlockSpec((wpm.shape[0], tn), wcol)],
        out_specs=pl.BlockSpec((tm, tn), lambda i, j: (i, j)),
        out_shape=jax.ShapeDtypeStruct((m, d), BF16),
        compiler_params=_params(2), name='gated_proj',
    )(gmat, gmat, gmat, oa, ob, om, wpa, wpb, wpm)
    fixed = lambda i: (0, 0)
    return pl.pallas_call(
        functools.partial(_out_proj_kernel, alpha=alpha),
        grid=(m // tm,),
        in_specs=[pl.BlockSpec((tm, d), lambda i: (i, 0)), pl.BlockSpec((tm, d), lambda i: (i, 0)),
                  _resident(wo.shape, fixed), _resident((1, d), fixed), _resident((1, d), fixed)],
        out_specs=pl.BlockSpec((tm, d), lambda i: (i, 0)),
        out_shape=jax.ShapeDtypeStruct((m, d), F32),
        compiler_params=_params(1), name='out_proj_ln',
    )(x2d, merged, wo, ln_g, ln_b)


def _ffn_kernel(x_ref, wu_ref, bu_ref, wd_ref, bd_ref, lg_ref, lb_ref, o_ref, acc_ref, *, alpha):
    j = pl.program_id(1)

    @pl.when(j == 0)
    def _():
        acc_ref[...] = jnp.zeros_like(acc_ref)

    u = jnp.dot(x_ref[...].astype(BF16), wu_ref[...], preferred_element_type=F32) + bu_ref[...]
    u = jnp.square(jnp.maximum(u, 0.0))
    acc_ref[...] += jnp.dot(u.astype(BF16), wd_ref[...], preferred_element_type=F32)

    @pl.when(j == pl.num_programs(1) - 1)
    def _():
        y = alpha * x_ref[...] + acc_ref[...] + bd_ref[...]
        o_ref[...] = _layer_norm(y, lg_ref[...], lb_ref[...])


def _ffn(x2d, wu, bu, wd, bd, ln_g, ln_b, alpha, tm=512, tf=1024):
    m, d = x2d.shape
    dff = wu.shape[1]
    tm = min(tm, m)
    assert m % tm == 0 and dff % tf == 0
    return pl.pallas_call(
        functools.partial(_ffn_kernel, alpha=alpha),
        grid=(m // tm, dff // tf),
        in_specs=[pl.BlockSpec((tm, d), lambda i, j: (i, 0)),
                  pl.BlockSpec((d, tf), lambda i, j: (0, j)), pl.BlockSpec((1, tf), lambda i, j: (0, j)),
                  pl.BlockSpec((tf, d), lambda i, j: (j, 0)),
                  _resident((1, d), lambda i, j: (0, 0)), _resident((1, d), lambda i, j: (0, 0)),
                  _resident((1, d), lambda i, j: (0, 0))],
        out_specs=pl.BlockSpec((tm, d), lambda i, j: (i, 0)),
        out_shape=jax.ShapeDtypeStruct((m, d), F32),
        scratch_shapes=[pltpu.VMEM((tm, d), F32)],
        compiler_params=_params(2), name='ffn_ln',
    )(x2d, wu, bu, wd, bd, ln_g, ln_b)


def _sidx_kernel(pt_ref, iq_ref, w_ref, knew_ref, pool_ref, o_ref, buf_ref, sem_ref, keys_ref, *, topk, n_q, past):
    g = pl.program_id(1)
    slot = _page_pipeline(pt_ref, pool_ref, buf_ref, sem_ref)
    pages = _pages_per_step(pool_ref, buf_ref)
    group = 8

    def scores(k_t):
        s = jnp.dot(iq_ref[0], k_t.astype(BF16), preferred_element_type=F32)
        acc = jnp.zeros((n_q, k_t.shape[1]), F32)
        for h in range(IDX_HEADS):
            acc = acc + jnp.maximum(s[h * n_q:(h + 1) * n_q], 0.0) * w_ref[0, h * n_q:(h + 1) * n_q, 0:1]
        return _sortable_key(acc)

    def page_body(j, c):
        k_t = jnp.concatenate(
            [buf_ref[slot, pl.ds(pl.multiple_of((j * group + u) * IDX_DIM, IDX_DIM), IDX_DIM), :]
             for u in range(group)], axis=1)
        width = group * PAGE_SIZE
        keys_ref[:, pl.ds(pl.multiple_of((g * pages + j * group) * PAGE_SIZE, width), width)] = scores(k_t)
        return c

    lax.fori_loop(0, pages // group, page_body, 0)

    @pl.when(g == pl.num_programs(1) - 1)
    def _():
        lp = keys_ref.shape[1]
        rown = lax.broadcasted_iota(I32, (n_q, TILE), 0)
        coln = lax.broadcasted_iota(I32, (n_q, TILE), 1)
        keys_ref[:, past:lp] = jnp.where(coln <= rown, scores(knew_ref[0]), INT_MIN)
        keys = keys_ref[...]
        pos = lax.broadcasted_iota(I32, (n_q, lp), 1)

        def count(pred):
            v = jnp.where(pred, 1.0, 0.0)
            n_part = 16
            part = (lp // (n_part * LANES)) * LANES
            acc = v[:, 0:part]
            for i in range(1, n_part):
                acc = acc + v[:, i * part:(i + 1) * part]
            return jnp.sum(acc, axis=1, keepdims=True) + jnp.sum(v[:, n_part * part:], axis=1, keepdims=True)

        t = _kth_largest_key(lambda cand: count(keys >= cand), topk, (n_q, 1))
        thr = jnp.maximum(t, INT_MIN + 1)
        cnt_gt = count(keys >= thr + 1)
        need = topk - cnt_gt
        cnt_eq = count(keys >= thr) - cnt_gt
        tie = jnp.where(t > INT_MIN, jnp.where(cnt_eq > need, 1.0, 0.0), 0.0)
        eq_pos = jnp.where(keys == thr, pos, jnp.int32(2 ** 30))
        n_bits = int(lp).bit_length()

        def idx_body(i, mm):
            cand = mm + lax.shift_left(jnp.int32(1), n_bits - 1 - i)
            return jnp.where(count(eq_pos < cand) < need, cand, mm)

        last = lax.fori_loop(0, n_bits, idx_body, jnp.zeros((n_q, 1), I32))
        last = jnp.where(tie > 0.0, last, jnp.int32(2 ** 30))
        sel = jnp.where(keys > thr, 1.0, jnp.where(keys == thr, jnp.where(pos <= last, 1.0, 0.0), 0.0))
        o_ref[0, 0] = sel


def _sample_index_mask(page_table, iq_s, w_s, knew, pool_idx, topk):
    db, n_pages = page_table.shape
    n_q = iq_s.shape[1] // IDX_HEADS
    pages = math.gcd(n_pages, INDEX_PAGES_PER_STEP)
    assert pages % 8 == 0 and pool_idx.shape[1:] == (IDX_DIM, PAGE_SIZE)
    npg = n_pages // pages
    lp = n_pages * PAGE_SIZE + TILE
    return pl.pallas_call(
        functools.partial(_sidx_kernel, topk=topk, n_q=n_q, past=n_pages * PAGE_SIZE),
        grid_spec=pltpu.PrefetchScalarGridSpec(
            num_scalar_prefetch=1,
            grid=(db, npg),
            in_specs=[pl.BlockSpec((1,) + iq_s.shape[1:], lambda b, g, pt: (b, 0, 0)),
                      pl.BlockSpec((1,) + w_s.shape[1:], lambda b, g, pt: (b, 0, 0)),
                      pl.BlockSpec((1,) + knew.shape[1:], lambda b, g, pt: (b, 0, 0)),
                      pl.BlockSpec(memory_space=pl.ANY)],
            out_specs=pl.BlockSpec((1, 1, n_q, lp), lambda b, g, pt: (b, 0, 0, 0)),
            scratch_shapes=[pltpu.VMEM((2, pages * pool_idx.shape[1], pool_idx.shape[2]), F32),
                            pltpu.SemaphoreType.DMA((2,)),
                            pltpu.VMEM((n_q, lp), I32)]),
        out_shape=jax.ShapeDtypeStruct((db, 1, n_q, lp), F32),
        compiler_params=_params(2), name='sample_index_mask',
    )(page_table.reshape(-1), iq_s, w_s, knew, pool_idx)


def _pattn_kernel(pt_ref, q_ref, mask_ref, new_ref, bias_ref, pool_ref, o_ref, buf_ref, sem_ref,
                  m_ref, l_ref, acc_ref, *, nd, n_q, n_pages):
    g = pl.program_id(1)
    npg = pl.num_programs(1)
    slot = _page_pipeline(pt_ref, pool_ref, buf_ref, sem_ref)
    rows = q_ref.shape[1]
    half = rows // 2
    mask_groups = mask_ref.shape[1]

    @pl.when(g == 0)
    def _():
        m0, l0, a0 = _flash_init(rows, 2 * HEAD_DIM)
        m_ref[...] = m0
        l_ref[...] = l0
        acc_ref[...] = a0

    def block_update(rows_of, p0, n_tiles, carry):
        kk = jnp.concatenate([rows_of(0), rows_of(2)], axis=1).astype(BF16)
        vv = jnp.concatenate([rows_of(1), rows_of(3)], axis=1).astype(BF16)
        width = n_tiles * TILE
        mk = mask_ref[0, :, :, pl.ds(pl.multiple_of(p0 * TILE, TILE), width)]
        madd = (mk - 1.0) * (-NEG)
        reps = rows // (mask_groups * n_q)
        madd = jnp.concatenate([madd[i] for i in range(mask_groups) for _ in range(reps)], axis=0)
        bias = jnp.concatenate([bias_ref[jnp.clip(n_pages - (p0 + i), 0, nd)] for i in range(n_tiles)], axis=1)
        s = _dot_nt(q_ref[0], kk) + (bias + madd)
        return _flash_step(s, *carry, vv)

    carry = (m_ref[...], l_ref[...], acc_ref[...])
    keys = PAGES_PER_BLOCK * PAGE_SIZE
    for blk in range(PAGES_PER_STEP // PAGES_PER_BLOCK):
        def rows_of(j, blk=blk):
            return buf_ref[slot, pl.ds(blk * keys * KV_ROWS + j, keys, stride=KV_ROWS), :]
        carry = block_update(rows_of, g * PAGES_PER_STEP + blk * PAGES_PER_BLOCK, PAGES_PER_BLOCK, carry)
    m_ref[...], l_ref[...], acc_ref[...] = carry

    @pl.when(g == npg - 1)
    def _():
        def new_rows(j):
            return new_ref[0, :, j * HEAD_DIM:(j + 1) * HEAD_DIM]
        o = _flash_out(*block_update(new_rows, n_pages, 1, carry))
        o_ref[0, 0:half] = o[0:half, 0:HEAD_DIM]
        o_ref[0, half:rows] = o[half:rows, HEAD_DIM:2 * HEAD_DIM]


def _paged_attention(page_table, qblk, mask, new_kv, bias_s, pool, nd, n_q):
    db, n_pages = page_table.shape
    npg = n_pages // PAGES_PER_STEP
    rows = qblk.shape[1]
    return pl.pallas_call(
        functools.partial(_pattn_kernel, nd=nd, n_q=n_q, n_pages=n_pages),
        grid_spec=pltpu.PrefetchScalarGridSpec(
            num_scalar_prefetch=1,
            grid=(db, npg),
            in_specs=[pl.BlockSpec((1,) + qblk.shape[1:], lambda b, g, pt: (b, 0, 0)),
                      pl.BlockSpec((1,) + mask.shape[1:], lambda b, g, pt: (b, 0, 0, 0)),
                      pl.BlockSpec((1,) + new_kv.shape[1:], lambda b, g, pt: (b, 0, 0)),
                      _resident(bias_s.shape, lambda b, g, pt: (0, 0, 0)),
                      pl.BlockSpec(memory_space=pl.ANY)],
            out_specs=pl.BlockSpec((1, rows, HEAD_DIM), lambda b, g, pt: (b, 0, 0)),
            scratch_shapes=[pltpu.VMEM((2, PAGES_PER_STEP * pool.shape[1], pool.shape[2]), F32),
                            pltpu.SemaphoreType.DMA((2,)),
                            pltpu.VMEM((rows, 1), F32), pltpu.VMEM((rows, 1), F32),
                            pltpu.VMEM((rows, 2 * HEAD_DIM), F32)]),
        out_shape=jax.ShapeDtypeStruct((db, rows, HEAD_DIM), F32),
        compiler_params=_params(2), name='paged_attention',
    )(page_table.reshape(-1), qblk, mask, new_kv, bias_s, pool)


def _scmp_kernel(q_ref, ck_ref, cv_ref, ov_ref, ocmp_ref, mask_ref, selm_ref, *, past, n_q, n_sel):
    rows = q_ref.shape[2]
    ncp = ck_ref.shape[1]
    nsp = ov_ref.shape[1]
    qi =lax.rem(lax.broadcasted_iota(I32, (rows, ncp), 0), n_q)
    cmp_end = lax.broadcasted_iota(I32, (rows, ncp), 1) * CMP_STRIDE + (CMP_BLOCK - 1)
    cmp_ok = cmp_end <= past + qi
    blk = lax.broadcasted_iota(I32, (n_q, nsp), 1)
    cur = lax.shift_right_logical(past + lax.broadcasted_iota(I32, (n_q, nsp), 0), int(math.log2(SEL_BLOCK)))
    q_pos = past + lax.broadcasted_iota(I32, (n_q, TILE), 0)
    coln = lax.broadcasted_iota(I32, (n_q, TILE), 1)
    scores = []
    for g in range(B_KV_HEADS):
        p = _softmax_rows(_dot_nt(q_ref[0, g], ck_ref[0, :, g * HEAD_DIM:(g + 1) * HEAD_DIM]), cmp_ok)
        ocmp_ref[0, g] = jnp.dot(p.astype(BF16), cv_ref[0, :, g * HEAD_DIM:(g + 1) * HEAD_DIM],
                                 preferred_element_type=F32)
        psum = p[0:n_q]
        for r in range(1, rows // n_q):
            psum = psum + p[r * n_q:(r + 1) * n_q]
        imp = jnp.dot(psum, ov_ref[...], preferred_element_type=F32, precision=lax.Precision.HIGHEST)
        forced = jnp.where(blk == 0, jnp.inf, jnp.where(blk >= cur - 1, jnp.inf, imp))
        scores.append(jnp.where(blk <= cur, forced, -jnp.inf))
    for g, selm in enumerate(_topn_mask(scores, n_sel)):
        selm_ref[g] = selm

    blocks_per_tile = TILE // SEL_BLOCK
    half = lax.shift_right_logical(coln, int(math.log2(SEL_BLOCK)))

    def expand_tile(kt):
        blk0 = blocks_per_tile * kt
        win0 = pl.multiple_of(lax.shift_right_logical(blk0, int(math.log2(LANES))) * LANES, LANES)
        idx = (blk0 - win0) + half
        k0 = pl.multiple_of(kt * TILE, TILE)
        for g in range(B_KV_HEADS):
            e = jnp.take_along_axis(selm_ref[g, :, pl.ds(win0, LANES)], idx, axis=1)
            mask_ref[0, g, :, pl.ds(k0, TILE)] = jnp.where(kt * TILE + coln <= q_pos, e, 0.0)

    unroll = 4
    n_tiles = mask_ref.shape[3] // TILE

    def expand_body(j, c):
        for u in range(unroll):
            expand_tile(j * unroll + u)
        return c

    lax.fori_loop(0, n_tiles // unroll, expand_body, 0)
    for kt in range(n_tiles - n_tiles % unroll, n_tiles):
        expand_tile(jnp.int32(kt))


def _sample_cmp_select(bq_s, cmp_k, cmp_v, length, past, n_q):
    db = bq_s.shape[0]
    ncp = cmp_k.shape[1]
    ns = -(-length // SEL_BLOCK)
    nsp = -(-ns // LANES) * LANES
    ov = _overlap_matrix(length, ncp, nsp)
    lp = past + TILE
    return pl.pallas_call(
        functools.partial(_scmp_kernel, past=past, n_q=n_q, n_sel=min(SEL_TOPN, ns)),
        grid=(db,),
        in_specs=[pl.BlockSpec((1,) + bq_s.shape[1:], lambda b: (b, 0, 0, 0)),
                  pl.BlockSpec((1, ncp, B_KV_HEADS * HEAD_DIM), lambda b: (b, 0, 0)),
                  pl.BlockSpec((1, ncp, B_KV_HEADS * HEAD_DIM), lambda b: (b, 0, 0)),
                  _resident(ov.shape, lambda b: (0, 0))],
        out_specs=[pl.BlockSpec((1,) + bq_s.shape[1:], lambda b: (b, 0, 0, 0)),
                   pl.BlockSpec((1, B_KV_HEADS, n_q, lp), lambda b: (b, 0, 0, 0))],
        out_shape=[jax.ShapeDtypeStruct(bq_s.shape, F32),
                   jax.ShapeDtypeStruct((db, B_KV_HEADS, n_q, lp), F32)],
        scratch_shapes=[pltpu.VMEM((B_KV_HEADS, n_q, nsp), F32)],
        compiler_params=_params(1), name='sample_cmp_select',
    )(bq_s, cmp_k, cmp_v, ov)


def _swin_kernel(q_ref, win_ref, new_ref, bias_ref, gate_ref, ocmp_ref, osel_ref, o_ref, *, n_q):
    rows = q_ref.shape[1]
    half = rows // 2
    wb = win_ref.shape[1]
    qi =lax.rem(lax.broadcasted_iota(I32, (rows, TILE), 0), n_q)
    col = lax.broadcasted_iota(I32, (rows, TILE), 1)
    carry = _flash_init(rows, 2 * HEAD_DIM)

    def tile_update(kv, dlt, ok, carry):
        kk = jnp.concatenate([kv[:, 0:HEAD_DIM], kv[:, 2 * HEAD_DIM:3 * HEAD_DIM]], axis=1).astype(BF16)
        vv = jnp.concatenate([kv[:, HEAD_DIM:2 * HEAD_DIM], kv[:, 3 * HEAD_DIM:4 * HEAD_DIM]], axis=1).astype(BF16)
        s = _dot_nt(q_ref[0], kk) + (bias_ref[dlt] + jnp.where(ok, 0.0, NEG))
        return _flash_step(s, *carry, vv)

    for kt in range(wb // TILE):
        ok = col + kt * TILE >= qi + (wb - WINDOW)
        carry = tile_update(win_ref[0, kt * TILE:(kt + 1) * TILE, :], wb // TILE - kt, ok, carry)
    carry = tile_update(new_ref[0], 0, col <= qi, carry)
    o = _flash_out(*carry)
    gate = jax.nn.sigmoid(gate_ref[0])
    for h in range(B_HEADS):
        sl = slice(h * n_q, (h + 1) * n_q)
        ow = o[sl, 0:HEAD_DIM] if h < B_GROUP else o[sl, HEAD_DIM:2 * HEAD_DIM]
        o_ref[0, :, h * HEAD_DIM:(h + 1) * HEAD_DIM] = (
            gate[:, 3 * h:3 * h + 1] * ocmp_ref[0, sl] + gate[:, 3 * h + 1:3 * h + 2] * osel_ref[0, sl]
            + gate[:, 3 * h + 2:3 * h + 3] * ow).astype(o_ref.dtype)


def _sample_window_combine(qblk, win_state, new_win, bias_s, gates, o_cmp, o_sel, n_q):
    db, rows, _ = qblk.shape
    return pl.pallas_call(
        functools.partial(_swin_kernel, n_q=n_q),
        grid=(db,),
        in_specs=[pl.BlockSpec((1,) + qblk.shape[1:], lambda b: (b, 0, 0)),
                  pl.BlockSpec((1,) + win_state.shape[1:], lambda b: (b, 0, 0)),
                  pl.BlockSpec((1,) + new_win.shape[1:], lambda b: (b, 0, 0)),
                  _resident(bias_s.shape, lambda b: (0, 0, 0)),
                  pl.BlockSpec((1,) + gates.shape[1:], lambda b: (b, 0, 0)),
                  pl.BlockSpec((1, rows, HEAD_DIM), lambda b: (b, 0, 0)),
                  pl.BlockSpec((1, rows, HEAD_DIM), lambda b: (b, 0, 0))],
        out_specs=pl.BlockSpec((1, n_q, B_HEADS * HEAD_DIM), lambda b: (b, 0, 0)),
        out_shape=jax.ShapeDtypeStruct((db, n_q, B_HEADS * HEAD_DIM), BF16),
        compiler_params=_params(1), name='sample_window_combine',
    )(qblk, win_state, new_win, bias_s, gates, o_cmp, o_sel)


def _pack_weights(w_in):
    d = w_in.shape[0]
    sizes = _split_sizes(d)
    off, o = {}, 0
    for name in _GROUPS:
        off[name] = o
        o += sizes[name]
    take = lambda name: w_in[:, off[name]:off[name] + sizes[name]]
    zeros = lambda n: jnp.zeros((d, n), w_in.dtype)
    iq = take('i_q').reshape(d, IDX_HEADS, IDX_DIM)
    iq = jnp.concatenate([iq, jnp.zeros_like(iq)], axis=2).reshape(d, IDX_HEADS * LANES)
    w_f = jnp.concatenate([take('a_kv'), take('b_cmp'), take('b_sel'), take('b_win'),
                           take('i_k'), take('i_w'), zeros(LANES - IDX_DIM - IDX_HEADS),
                           take('b_gate'), zeros(LANES - B_HEADS * 3), zeros(2 * LANES)], axis=1).astype(BF16)
    w_q = jnp.concatenate([take('a_q') * QK_SCALE, take('b_q') * QK_SCALE, take('m_q'), iq], axis=1).astype(BF16)
    w_g = take('g_merge').astype(BF16)
    kvw = 4 * HEAD_DIM
    cols = dict(f=dict(a_kv=0, b_cmp=kvw, b_sel=2 * kvw, b_win=3 * kvw, i_kw=4 * kvw, b_gate=4 * kvw + LANES),
                q=dict(a_q=0, b_q=1024, m_q=2048, i_q=3072))
    return w_f, w_q, w_g, cols


def kernel(x_prompt, x_sample, mem_prompt, cache_a_kv, cache_a_idx, cache_b_cmp, cache_b_sel, state_b_win,
           cache_mem, page_table, rel_table, w_in, w_mem_kv, cmp_pe_k, cmp_w1_k, cmp_w2_k, cmp_pe_v, cmp_w1_v,
           cmp_w2_v, w_pa, w_pb, w_pm, w_o, ln1_g, ln1_b, w_up, b_up, w_down, b_down, ln2_g, ln2_b):
    depth = w_in.shape[0]
    assert depth == 1
    bsz, t, d = x_prompt.shape
    db, ds, _ = x_sample.shape
    n_mem = mem_prompt.shape[1]
    n_pool = cache_a_kv.shape[1]
    n_pages = page_table.shape[1]
    past = n_pages * PAGE_SIZE
    wb = state_b_win.shape[2]
    alpha = (2 * depth) ** 0.25
    kvw = 4 * HEAD_DIM
    assert t % TILE == 0 and ds == 8 and wb % TILE == 0 and n_pages % PAGES_PER_STEP == 0

    w_f, w_q, w_g, cols = _pack_weights(w_in[0])
    fc, qc = cols['f'], cols['q']
    w1cat = jnp.stack([jnp.concatenate([w[0][:CMP_STRIDE * HEAD_DIM], w[0][CMP_STRIDE * HEAD_DIM:]], axis=1)
                       for w in (cmp_w1_k, cmp_w1_v)]).astype(BF16)
    w1 = jnp.stack([cmp_w1_k[0], cmp_w1_v[0]]).astype(BF16)
    w2 = jnp.stack([cmp_w2_k[0], cmp_w2_v[0]]).astype(BF16)
    pe8 = jnp.broadcast_to(jnp.stack([cmp_pe_k[0].reshape(1, -1), cmp_pe_v[0].reshape(1, -1)]),
                           (2, 8, CMP_BLOCK * HEAD_DIM)).astype(BF16)
    wpa, wpb, wpm, wo = (w[0].astype(BF16) for w in (w_pa, w_pb, w_pm, w_o))
    wu, wd = w_up[0].astype(BF16), w_down[0].astype(BF16)
    nd = _num_near_tiles()
    bias = _bias_tiles(rel_table, nd)

    def dense_tail(x2d, gmat, oa, ob, om):
        x1 = _merge(x2d, gmat, oa, ob, om, wpa, wpb, wpm, wo, ln1_g, ln1_b, alpha)
        return _ffn(x1, wu, b_up, wd, b_down, ln2_g, ln2_b, alpha)

    xp = x_prompt.reshape(bsz * t, d)
    p_a_kv, p_b_cmp, p_b_sel, p_b_win, small_p = _project_states(xp, w_f)
    qp = _matmul(xp, w_q, BF16, tn=1024)
    gp = _matmul(xp, w_g, F32, tn=1024)
    p_a_idx = small_p[:, :IDX_DIM]

    zp = _cmpz_dense(p_b_cmp, w1cat)
    cmp_k, cmp_v = _cmp_finish(zp.reshape(bsz, t // CMP_STRIDE, -1), pe8, w1, w2)
    o_a = _prompt_mixer_a(qp, small_p, p_a_kv, bias[0], bsz, t, cols, nd)
    o_b = _prompt_mixer_b(qp, small_p, p_b_sel, p_b_win, cmp_k, cmp_v, bias[1], bsz, t, cols, nd)
    mem_kv = _matmul(mem_prompt.reshape(bsz * n_mem, d), w_mem_kv[0].astype(BF16), F32, tn=1024)
    o_m = _mem_attend(qp, qc['m_q'], mem_kv, bsz, t, n_mem, tq=512)
    y_prompt = dense_tail(xp, gp, o_a, o_b, o_m).reshape(bsz, t, d)

    xs = x_sample.reshape(db * ds, d)
    fs = _matmul(xs, w_f, F32, tn=w_f.shape[1] // 2)
    qs = _matmul(xs, w_q, BF16, tn=1024)
    gs = _matmul(xs, w_g, F32, tn=1024)
    s_a_kv = fs[:, fc['a_kv']:fc['a_kv'] + kvw]
    s_b_cmp = fs[:, fc['b_cmp']:fc['b_cmp'] + kvw]
    s_b_sel = fs[:, fc['b_sel']:fc['b_sel'] + kvw]
    s_b_win = fs[:, fc['b_win']:fc['b_win'] + kvw]
    s_a_idx = fs[:, fc['i_kw']:fc['i_kw'] + IDX_DIM]
    length = past + ds

    def pad_new(rows2d):
        r = rows2d.reshape(db, ds, -1)
        return jnp.concatenate([r, jnp.zeros((db, TILE - ds, r.shape[2]), r.dtype)], axis=1)

    def head_major(q2d, heads):
        return q2d.reshape(db, ds, heads, -1).transpose(0, 2, 1, 3).reshape(db, heads * ds, -1)

    def block_q(q2d):
        qh = head_major(q2d, A_HEADS).reshape(db, A_KV_HEADS, A_GROUP * ds, HEAD_DIM)
        z = jnp.zeros_like(qh[:, 0])
        return jnp.concatenate([jnp.concatenate([qh[:, 0], z], axis=2),
                                jnp.concatenate([z, qh[:, 1]], axis=2)], axis=1)

    def sample_bias(tiles):
        n = tiles.shape[0]
        return tiles.reshape(n, A_KV_HEADS, A_GROUP, TILE, TILE)[:, :, :, :ds].reshape(n, A_HEADS * ds, TILE)

    iq_s = head_major(qs[:, qc['i_q']:qc['i_q'] + IDX_HEADS * LANES], IDX_HEADS)[:, :, :IDX_DIM]
    w_s = fs[:, fc['i_kw'] + IDX_DIM:fc['i_kw'] + IDX_DIM + IDX_HEADS] * (IDX_HEADS ** -0.5 * IDX_DIM ** -0.5)
    w_s = jnp.broadcast_to(head_major(w_s, IDX_HEADS), (db, IDX_HEADS * ds, LANES))
    mask_a = _sample_index_mask(page_table, iq_s, w_s, pad_new(s_a_idx).transpose(0, 2, 1),
                                cache_a_idx.reshape(n_pool, PAGE_SIZE, IDX_DIM).transpose(0, 2, 1),
                                min(TOPK_MAX, length // 4))
    qa_blk = block_q(qs[:, qc['a_q']:qc['a_q'] + A_HEADS * HEAD_DIM])
    o_a_s = _paged_attention(page_table, qa_blk, mask_a, pad_new(s_a_kv), sample_bias(bias[0]),
                             cache_a_kv.reshape(n_pool, PAGE_SIZE * KV_ROWS, HEAD_DIM), nd, ds)

    zs = _cmpz_paged(cache_b_cmp.reshape(n_pool, PAGE_SIZE * KV_ROWS, HEAD_DIM), page_table, w1cat)
    cmp_k_s, cmp_v_s = _cmp_finish(zs.reshape(db, past // CMP_STRIDE, -1), pe8, w1, w2)
    bq2d = qs[:, qc['b_q']:qc['b_q'] + B_HEADS * HEAD_DIM]
    bq_s = head_major(bq2d, B_HEADS).reshape(db, B_KV_HEADS, B_GROUP * ds, HEAD_DIM)
    o_cmp_s, mask_b = _sample_cmp_select(bq_s, cmp_k_s, cmp_v_s, length, past, ds)
    qb_blk = block_q(bq2d)
    bias_sb = sample_bias(bias[1])
    o_sel_s = _paged_attention(page_table, qb_blk, mask_b, pad_new(s_b_sel), bias_sb,
                               cache_b_sel.reshape(n_pool, PAGE_SIZE * KV_ROWS, HEAD_DIM), nd, ds)
    gates_s = fs[:, fc['b_gate']:fc['b_gate'] + LANES].reshape(db, ds, LANES)
    o_b_s = _sample_window_combine(qb_blk, state_b_win.reshape(db, wb, kvw), pad_new(s_b_win), bias_sb,
                                   gates_s, o_cmp_s.reshape(db, B_HEADS * ds, HEAD_DIM), o_sel_s, ds)

    o_a_s = o_a_s.reshape(db, A_HEADS, ds, HEAD_DIM).transpose(0, 2, 1, 3).reshape(db * ds, -1).astype(BF16)
    o_m_s = _mem_attend(qs, qc['m_q'], cache_mem.reshape(db * n_mem, -1), db, ds, n_mem, tq=ds)
    y_sample = dense_tail(xs, gs, o_a_s, o_b_s.reshape(db * ds, -1), o_m_s).reshape(db, ds, d)

    kv6 = lambda a, n, rows: a.reshape(1, n, rows, 2, 2, HEAD_DIM)
    wp = min(WINDOW, t)
    new_win = jnp.concatenate([state_b_win.reshape(db, wb, 2, 2, HEAD_DIM)[:, ds:],
                               s_b_win.reshape(db, ds, 2, 2, HEAD_DIM)], axis=1)
    return (y_prompt, y_sample,
            kv6(p_a_kv, bsz, t), p_a_idx.reshape(1, bsz, t, IDX_DIM), kv6(p_b_cmp, bsz, t), kv6(p_b_sel, bsz, t),
            kv6(p_b_win, bsz, t)[:, :, t - wp:],
            mem_kv.reshape(1, bsz, n_mem, MEM_HEADS, 2, MEM_HEAD_DIM),
            kv6(s_a_kv, db, ds), s_a_idx.reshape(1, db, ds, IDX_DIM), kv6(s_b_cmp, db, ds), kv6(s_b_sel, db, ds),
            new_win[None])
```
